```python
import math
import jax
import jax.numpy as jnp
from jax import lax
import numpy as np

D_MODEL = 1024
BATCH = 32
SEQ = 256
DEPTH = 4
DEC_BATCH = 8
DEC_SEQ = 2048
PAST_LEN = 512

GRID_W = 64
D_MIX = D_MODEL
GROUP_W = D_MIX // 4
N_MOD = 9
D_FF = 2816
EPS = 1e-6
ROPE_BASE = 10000.0
Q_BLOCK = 128
H_A = 4
DK_A = GROUP_W // H_A
DV_A = GROUP_W // H_A
A_WK = H_A * DK_A
A_WV = H_A * DV_A
GLA_CHUNK = 16
MAX_INPUT_KEY = 1.0 - 1e-6
H_B = 4
NOPE_B = 64
ROPE_B = 32
V_B = GROUP_W // H_B
Q_LORA = 256
KV_LORA = 128
W_C = GROUP_W
HY_ORDER = 2
HY_BANDS = 8
HY_EMB = 1 + 2 * HY_BANDS
HY_FH = 64
SHORT_K = 3
HY_MIN_DECAY = 3.07
HY_MAX_DECAY = 15.35
H_D = 4
DV_D = GROUP_W // H_D
DH_D = DV_D // 2
D_WQK = H_D * 2 * DH_D
D_WV = H_D * DV_D
IN_W = 3 * A_WK + 2 * A_WV + Q_LORA + KV_LORA + ROPE_B + 3 * W_C + 2 * D_WQK + D_WV

kernel_name = 'hybrid_dit_hgrn2_mla_hyena_diffattn_step'


def rmsnorm(x, g):
    xf = x.astype(jnp.float32)
    y = xf * lax.rsqrt(jnp.mean(xf * xf, axis=-1, keepdims=True) + EPS)
    return (y * g.astype(jnp.float32)).astype(x.dtype)


def swiglu(h, w_gu, w_down):
    a, u = jnp.split(h @ w_gu, 2, axis=-1)
    return (jax.nn.silu(a) * u) @ w_down


def axial_angles(n_tok, rot_dim):
    rows = n_tok // GRID_W
    r, col = jnp.meshgrid(jnp.arange(rows), jnp.arange(GRID_W), indexing='ij')
    n_f = rot_dim // 4
    inv = ROPE_BASE ** (-jnp.arange(n_f, dtype=jnp.float32) / n_f)
    ang_r = r.reshape(-1).astype(jnp.float32)[:, None] * inv
    ang_c = col.reshape(-1).astype(jnp.float32)[:, None] * inv
    return (ang_r, ang_c)


def rope_1d(x, ang):
    x1, x2 = jnp.split(x, 2, axis=-1)
    cos = jnp.cos(ang)[None, :, None, :].astype(x.dtype)
    sin = jnp.sin(ang)[None, :, None, :].astype(x.dtype)
    return jnp.concatenate([x1 * cos - x2 * sin, x1 * sin + x2 * cos], axis=-1)


def axial_rope(x, angs):
    xr, xc = jnp.split(x, 2, axis=-1)
    return jnp.concatenate([rope_1d(xr, angs[0]), rope_1d(xc, angs[1])], axis=-1)


def block_map(fn, q):
    b, lq = q.shape[:2]
    nb = lq // Q_BLOCK
    qb = jnp.moveaxis(q.reshape((b, nb, Q_BLOCK) + q.shape[2:]), 1, 0)
    out = lax.map(fn, qb)
    return jnp.moveaxis(out, 0, 1).reshape((b, lq) + out.shape[3:])


def gla_chunk(q, k, v, log_f, s0):
    b, n, h, dk = q.shape
    dv = v.shape[-1]
    nc = n // GLA_CHUNK
    r = lambda t: t.reshape(b, nc, GLA_CHUNK, h, t.shape[-1])
    q, k, v, g = r(q), r(k), r(v), r(log_f)
    cum = jnp.cumsum(g, axis=2)
    causal = jnp.tril(jnp.ones((GLA_CHUNK, GLA_CHUNK), dtype=bool))[None, None, :, :, None, None]
    diff = cum[:, :, :, None] - cum[:, :, None, :]
    decay = jnp.where(causal, jnp.exp(jnp.minimum(diff, 0.0)), 0.0)
    attn = jnp.einsum('bnthd,bnshd,bntshd->bnhts', q, k, decay)
    o_intra = jnp.einsum('bnhts,bnshv->bnthv', attn, v)
    last = cum[:, :, -1]
    upd = jnp.einsum('bnchk,bnchv->bnhkv', k * jnp.exp(last[:, :, None] - cum), v)

    def step(s, xs):
        a, u = xs
        return a[..., None] * s + u, s

    s_fin, s_prev = lax.scan(step, s0, (jnp.moveaxis(jnp.exp(last), 1, 0), jnp.moveaxis(upd, 1, 0)))
    o_inter = jnp.einsum('bnchk,nbhkv->bnchv', q * jnp.exp(cum), s_prev)
    return (o_intra + o_inter).reshape(b, n, h, dv), s_fin


def hgrn_forget(logit, lb):
    x = logit.astype(jnp.float32)
    lbf = lb.astype(jnp.float32).reshape(H_A, DK_A)
    k = jnp.minimum((1.0 - lbf) * jax.nn.sigmoid(-x), MAX_INPUT_KEY)
    log_f = jnp.log1p(-k)
    return log_f, k


def hgrn_mixer(pq, pi, pff, pfb, pog, lb_f, lb_b, onorm_g, s0):
    b, n, _ = pq.shape
    hv = lambda t, d: t.reshape(b, n, H_A, d)
    q = hv(jax.nn.silu(pq), DK_A).astype(jnp.float32) * DK_A ** -0.5
    v = hv(pi, DV_A).astype(jnp.float32)
    logf_f, k_f = hgrn_forget(hv(pff, DK_A), lb_f)
    logf_b, k_b = hgrn_forget(hv(pfb, DK_A), lb_b)
    s0 = s0.astype(jnp.float32)
    o_f, s_f = gla_chunk(q, k_f, v, logf_f, s0[:, 0])
    rev = lambda t: jnp.flip(t, axis=1)
    o_b, s_b = gla_chunk(rev(q), rev(k_b), rev(v), rev(logf_b), s0[:, 1])
    o = rmsnorm(o_f + rev(o_b), onorm_g) * jax.nn.silu(hv(pog, DV_A).astype(jnp.float32))
    return o.reshape(b, n, A_WV).astype(pq.dtype), jnp.stack([s_f, s_b], axis=1)


def mla_queries(c_q, w_uq, qn_g, angs):
    b, n, _ = c_q.shape
    q = rmsnorm((c_q @ w_uq).reshape(b, n, H_B, NOPE_B + ROPE_B), qn_g)
    if angs is not None:
        q = jnp.concatenate([q[..., :NOPE_B], axial_rope(q[..., NOPE_B:], angs)], axis=-1)
    return q


def mla_keys(c_kv, k_rope, w_ukv, kn_g, angs):
    b, n, _ = c_kv.shape
    kv = (c_kv @ w_ukv).reshape(b, n, H_B, NOPE_B + V_B)
    k = jnp.concatenate([kv[..., :NOPE_B], jnp.broadcast_to(k_rope[:, :, None, :], (b, n, H_B, ROPE_B))], axis=-1)
    k = rmsnorm(k, kn_g)
    if angs is not None:
        k = jnp.concatenate([k[..., :NOPE_B], axial_rope(k[..., NOPE_B:], angs)], axis=-1)
    return k, kv[..., NOPE_B:]


def mla_attend(q, k, v):
    scale = (NOPE_B + ROPE_B) ** -0.5

    def blk(qb):
        s = jnp.einsum('bqhd,bkhd->bhqk', qb, k).astype(jnp.float32) * scale
        p = jax.nn.softmax(s, axis=-1).astype(v.dtype)
        return jnp.einsum('bhqk,bkhd->bqhd', p, v)

    return block_map(blk, q)


def short_conv(x, w):
    pad = SHORT_K // 2
    n = x.shape[1]
    xp = jnp.pad(x, ((0, 0), (pad, pad), (0, 0)))
    return sum(xp[:, j:j + n] * w[j] for j in range(SHORT_K))


def hyena_filters(n, w1, b1, w2, b2, w3, log_decay):
    f32 = jnp.float32
    tn = jnp.arange(n, dtype=f32) / n
    ang = 2.0 * math.pi * tn[:, None] * jnp.arange(1, HY_BANDS + 1, dtype=f32)
    feats = jnp.concatenate([tn[:, None], jnp.cos(ang), jnp.sin(ang)], axis=-1)
    h = jnp.sin(feats @ w1.astype(f32) + b1.astype(f32))
    h = jnp.sin(h @ w2.astype(f32) + b2.astype(f32))
    h = (h @ w3.astype(f32)).reshape(n, 2, HY_ORDER, W_C)
    h = h * jnp.exp(-jnp.exp(log_decay.astype(f32))[None] * tn[:, None, None, None])
    h_f, h_b = h[:, 0], h[:, 1]
    filt = jnp.concatenate([h_f, jnp.zeros((1, HY_ORDER, W_C), f32), jnp.flip(h_b[1:], axis=0)], axis=0)
    return filt / (jnp.sum(jnp.abs(filt), axis=0, keepdims=True) + EPS)


def long_conv(u, filt, bias):
    n = u.shape[1]
    uf = jnp.fft.rfft(u, n=2 * n, axis=1)
    ff = jnp.fft.rfft(filt, n=2 * n, axis=0)
    y = jnp.fft.irfft(uf * ff[None], n=2 * n, axis=1)[:, :n]
    return y + u * bias.astype(jnp.float32)


def hyena_mixer(pv, px1, px2, w_short, w1, b1, w2, b2, w3, log_decay, bias):
    n = pv.shape[1]
    u = short_conv(jnp.concatenate([pv, px1, px2], axis=-1), w_short).astype(jnp.float32)
    v, x1, x2 = jnp.split(u, 3, axis=-1)
    filt = hyena_filters(n, w1, b1, w2, b2, w3, log_decay)
    z = x1 * long_conv(v, filt[:, 0], bias[0])
    z = x2 * long_conv(z, filt[:, 1], bias[1])
    return z.astype(pv.dtype)


def diff_heads(p, g, angs):
    b, n, _ = p.shape
    x = rmsnorm(p.reshape(b, n, H_D, 2, DH_D), g)
    if angs is not None:
        x = axial_rope(x.reshape(b, n, H_D * 2, DH_D), angs).reshape(b, n, H_D, 2, DH_D)
    return x


def diff_attend(q, k, v, lam, lam_init, sub_g):
    scale = DH_D ** -0.5

    def blk(qb):
        s = jnp.einsum('bqhmd,bkhmd->bmhqk', qb, k).astype(jnp.float32) * scale
        p = jax.nn.softmax(s, axis=-1)
        w = (p[:, 0] - lam * p[:, 1]).astype(v.dtype)
        return jnp.einsum('bhqk,bkhv->bqhv', w, v)

    return rmsnorm(block_map(blk, q), sub_g) * (1.0 - lam_init)


def mixing(h, l, P, lb, angs, ctx):
    b, n, _ = h.shape
    sizes = [A_WK, A_WV, A_WK, A_WK, A_WV, Q_LORA, KV_LORA, ROPE_B, W_C, W_C, W_C, D_WQK, D_WQK, D_WV]
    idx = np.cumsum(sizes)[:-1].tolist()
    (a_q, a_i, a_ff, a_fb, a_og, b_cq, b_ckv, b_kr,
     c_v, c_x1, c_x2, d_q, d_k, d_v) = jnp.split(h @ P['w_in'][l], idx, axis=-1)
    angs_b, angs_d = (None, None) if angs is None else angs
    s0 = jnp.zeros((b, 2, H_A, DK_A, DV_A), jnp.float32) if ctx is None else ctx['hgrn']
    o_a, s_hgrn = hgrn_mixer(a_q, a_i, a_ff, a_fb, a_og, lb[0, l], lb[1, l], P['hgrn_onorm'][l], s0)
    c_q = rmsnorm(b_cq, P['mla_q_norm'][l])
    c_kv = rmsnorm(b_ckv, P['mla_kv_norm'][l])
    q_b = mla_queries(c_q, P['mla_w_uq'][l], P['mla_qk_norm'][l, 0], angs_b)
    k_b, v_b = mla_keys(c_kv, b_kr, P['mla_w_ukv'][l], P['mla_qk_norm'][l, 1], angs_b)
    if ctx is not None:
        k_c, v_c = mla_keys(ctx['mla'][..., :KV_LORA], ctx['mla'][..., KV_LORA:], P['mla_w_ukv'][l], P['mla_qk_norm'][l, 1], None)
        k_b = jnp.concatenate([k_c, k_b], axis=1)
        v_b = jnp.concatenate([v_c, v_b], axis=1)
    o_b = mla_attend(q_b, k_b, v_b).reshape(b, n, H_B * V_B)
    o_c = hyena_mixer(c_v, c_x1, c_x2, P['hy_short'][l], P['hy_w1'][l], P['hy_b1'][l], P['hy_w2'][l],
                      P['hy_b2'][l], P['hy_w3'][l], P['hy_log_decay'][l], P['hy_bias'][l])
    lam_init = 0.8 - 0.6 * math.exp(-0.3 * l)
    dl = P['diff_lambda'][l].astype(jnp.float32)
    lam = jnp.exp(jnp.sum(dl[0] * dl[1])) - jnp.exp(jnp.sum(dl[2] * dl[3])) + lam_init
    q_d = diff_heads(d_q, P['diff_qk_norm'][l, 0], angs_d)
    k_d = diff_heads(d_k, P['diff_qk_norm'][l, 1], angs_d)
    v_d = d_v.reshape(b, n, H_D, DV_D)
    k_all, v_all = k_d, v_d
    if ctx is not None:
        k_all = jnp.concatenate([ctx['dk'], k_d], axis=1)
        v_all = jnp.concatenate([ctx['dv'], v_d], axis=1)
    o_d = diff_attend(q_d, k_all, v_all, lam, lam_init, P['diff_subln'][l]).reshape(b, n, D_WV)
    out = jnp.concatenate([o_a, o_b, o_c, o_d], axis=-1) @ P['w_out'][l]
    ctx_out = (jnp.concatenate([c_kv, b_kr], axis=-1), k_d, v_d, s_hgrn)
    return out, ctx_out


def layer(x, cvec, l, P, lb, angs, ctx):
    m = jax.nn.silu(cvec) @ P['w_mod'][l] + P['b_mod'][l]
    sh1, sc1, g1, sh2, sc2, g2, sh3, sc3, g3 = jnp.split(m, N_MOD, axis=-1)
    h = rmsnorm(x, P['norm_g'][l, 0]) * (1 + sc1) + sh1
    x = x + 0.5 * g1 * swiglu(h, P['ffn_w_gu'][l, 0], P['ffn_w_down'][l, 0])
    h = rmsnorm(x, P['norm_g'][l, 1]) * (1 + sc2) + sh2
    o, ctx_out = mixing(h, l, P, lb, angs, ctx)
    x = x + g2 * o
    h = rmsnorm(x, P['norm_g'][l, 2]) * (1 + sc3) + sh3
    x = x + 0.5 * g3 * swiglu(h, P['ffn_w_gu'][l, 1], P['ffn_w_down'][l, 1])
    return x, ctx_out


def setup_inputs(seed: int = 0) -> dict:
    key = jax.random.key(seed)
    ks = iter(jax.random.split(key, 40))
    f32 = jnp.float32
    nrm = lambda shape, s: s * jax.random.normal(next(ks), shape, f32)
    return {
        'x_prompt': nrm((BATCH, SEQ, D_MODEL), 1.0),
        'x_sample': nrm((DEC_BATCH, DEC_SEQ, D_MODEL), 1.0),
        'c': nrm((DEC_BATCH, D_MODEL), 1.0),
        'cache_mla': nrm((DEC_BATCH, DEPTH, PAST_LEN, KV_LORA + ROPE_B), 1.0),
        'cache_diff_k': nrm((DEC_BATCH, DEPTH, PAST_LEN, H_D, 2, DH_D), 1.0),
        'cache_diff_v': nrm((DEC_BATCH, DEPTH, PAST_LEN, H_D, DV_D), 1.0),
        'state_hgrn': nrm((DEC_BATCH, DEPTH, 2, H_A, DK_A, DV_A), 0.5),
        'c_ctx': nrm((D_MODEL,), 1.0),
        'w_mod': nrm((DEPTH, D_MODEL, N_MOD * D_MODEL), 0.5 * D_MODEL ** -0.5),
        'b_mod': nrm((DEPTH, N_MOD * D_MODEL), 0.02),
        'norm_g': 1.0 + nrm((DEPTH, 3, D_MODEL), 0.02),
        'ffn_w_gu': nrm((DEPTH, 2, D_MODEL, 2 * D_FF), D_MODEL ** -0.5),
        'ffn_w_down': nrm((DEPTH, 2, D_FF, D_MODEL), D_FF ** -0.5),
        'w_in': nrm((DEPTH, D_MODEL, IN_W), D_MODEL ** -0.5),
        'w_out': nrm((DEPTH, D_MIX, D_MODEL), D_MIX ** -0.5),
        'hgrn_lb_logits': nrm((2, DEPTH, A_WK), 0.1),
        'hgrn_onorm': 1.0 + nrm((DEPTH, DV_A), 0.02),
        'mla_q_norm': 1.0 + nrm((DEPTH, Q_LORA), 0.02),
        'mla_kv_norm': 1.0 + nrm((DEPTH, KV_LORA), 0.02),
        'mla_w_uq': nrm((DEPTH, Q_LORA, H_B * (NOPE_B + ROPE_B)), Q_LORA ** -0.5),
        'mla_w_ukv': nrm((DEPTH, KV_LORA, H_B * (NOPE_B + V_B)), KV_LORA ** -0.5),
        'mla_qk_norm': 1.0 + nrm((DEPTH, 2, NOPE_B + ROPE_B), 0.02),
        'hy_short': nrm((DEPTH, SHORT_K, 3 * W_C), SHORT_K ** -0.5),
        'hy_w1': nrm((DEPTH, HY_EMB, HY_FH), 1.0),
        'hy_b1': nrm((DEPTH, HY_FH), 0.1),
        'hy_w2': nrm((DEPTH, HY_FH, HY_FH), HY_FH ** -0.5),
        'hy_b2': nrm((DEPTH, HY_FH), 0.1),
        'hy_w3': nrm((DEPTH, HY_FH, 2 * HY_ORDER * W_C), HY_FH ** -0.5),
        'hy_log_decay': jnp.log(jnp.linspace(HY_MIN_DECAY, HY_MAX_DECAY, W_C, dtype=f32)) + nrm((DEPTH, 2, HY_ORDER, W_C), 0.05),
        'hy_bias': nrm((DEPTH, HY_ORDER, W_C), 0.1),
        'diff_qk_norm': 1.0 + nrm((DEPTH, 2, DH_D), 0.02),
        'diff_lambda': nrm((DEPTH, 4, DH_D), 0.1),
        'diff_subln': 1.0 + nrm((DEPTH, DV_D), 0.02),
    }


def reference(x_prompt, x_sample, c, cache_mla, cache_diff_k, cache_diff_v, state_hgrn, c_ctx,
              w_mod, b_mod, norm_g, ffn_w_gu, ffn_w_down, w_in, w_out, hgrn_lb_logits, hgrn_onorm,
              mla_q_norm, mla_kv_norm, mla_w_uq, mla_w_ukv, mla_qk_norm, hy_short, hy_w1, hy_b1,
              hy_w2, hy_b2, hy_w3, hy_log_decay, hy_bias, diff_qk_norm, diff_lambda, diff_subln):
    P = dict(w_mod=w_mod, b_mod=b_mod, norm_g=norm_g, ffn_w_gu=ffn_w_gu, ffn_w_down=ffn_w_down,
             w_in=w_in, w_out=w_out, hgrn_onorm=hgrn_onorm, mla_q_norm=mla_q_norm,
             mla_kv_norm=mla_kv_norm, mla_w_uq=mla_w_uq, mla_w_ukv=mla_w_ukv, mla_qk_norm=mla_qk_norm,
             hy_short=hy_short, hy_w1=hy_w1, hy_b1=hy_b1, hy_w2=hy_w2, hy_b2=hy_b2, hy_w3=hy_w3,
             hy_log_decay=hy_log_decay, hy_bias=hy_bias, diff_qk_norm=diff_qk_norm,
             diff_lambda=diff_lambda, diff_subln=diff_subln)
    lb_p = jax.nn.softmax(hgrn_lb_logits.astype(jnp.float32), axis=1)
    lb = jnp.cumsum(lb_p, axis=1) - lb_p[:, :1]
    n_lat = x_sample.shape[1]
    angs = (axial_angles(n_lat, ROPE_B), axial_angles(n_lat, DH_D))
    c_lat = c[:, None, :]
    x_p, x_s = x_prompt, x_sample
    mla_l, dk_l, dv_l, hg_l = [], [], [], []
    for l in range(DEPTH):
        x_p, (m_c, k_c, v_c, s_c) = layer(x_p, c_ctx, l, P, lb, None, None)
        mla_l.append(m_c.astype(x_prompt.dtype))
        dk_l.append(k_c.astype(x_prompt.dtype))
        dv_l.append(v_c.astype(x_prompt.dtype))
        hg_l.append(s_c.astype(x_prompt.dtype))
        ctx = dict(mla=cache_mla[:, l], dk=cache_diff_k[:, l], dv=cache_diff_v[:, l], hgrn=state_hgrn[:, l])
        x_s, _ = layer(x_s, c_lat, l, P, lb, angs, ctx)
    new_cache_mla = jnp.stack(mla_l, axis=1)
    new_cache_diff_k = jnp.stack(dk_l, axis=1)
    new_cache_diff_v = jnp.stack(dv_l, axis=1)
    new_state_hgrn = jnp.stack(hg_l, axis=1)
    return (x_p, x_s, new_cache_mla, new_cache_diff_k, new_cache_diff_v, new_state_hgrn)
```

```python
import functools
import math

import jax
import jax.numpy as jnp
import numpy as np
from jax import lax
from jax.experimental import pallas as pl
from jax.experimental.pallas import tpu as pltpu

F32 = jnp.float32
BF16 = jnp.bfloat16
HIGHEST = lax.Precision.HIGHEST

D_MODEL = 1024
DEPTH = 4
GRID_W = 64
N_MOD = 9
D_FF = 2816
EPS = 1e-6
ROPE_BASE = 10000.0
GROUP_W = 256
H_A, DK_A, DV_A = 4, 64, 64
MAX_INPUT_KEY = 1.0 - 1e-6
H_B, NOPE_B, ROPE_B, V_B = 4, 64, 32, 64
Q_LORA, KV_LORA = 256, 128
W_C, HY_ORDER, HY_BANDS, HY_FH, SHORT_K = 256, 2, 8, 64, 3
HY_EMB = 1 + 2 * HY_BANDS
H_D, DV_D, DH_D = 4, 64, 32

LANES = 128
MOD_ROWS = 16
TOKEN_TILE = 512
Q_TILE = 256
HGRN_BLOCK = 128
HGRN_CHUNK = 16
VMEM_LIMIT = 56 * 1024 * 1024

PA_W, PB_W, PC_W, PD_W = 1280, 512, 768, 768
IN_PAD_W = PA_W + PB_W + PC_W + PD_W
MLA_IN = Q_LORA + KV_LORA + ROPE_B
HEAD_PAD = 128


def _cparams(n_axes):
    return pltpu.CompilerParams(dimension_semantics=("arbitrary",) * n_axes, vmem_limit_bytes=VMEM_LIMIT)


def _nt_dot(a, b):
    return lax.dot_general(a, b, (((1,), (1,)), ((), ())), preferred_element_type=F32)


def _dot(a, b):
    return jnp.dot(a, b, preferred_element_type=F32)


def _dot_exact_rhs(a, b_bf16, passes=3):
    out = None
    rem = a
    for _ in range(passes):
        piece = rem.astype(BF16)
        term = _dot(piece, b_bf16)
        out = term if out is None else out + term
        rem = rem - piece.astype(F32)
    return out


def _dot_exact_lhs(a_bf16, b, passes=3):
    out = None
    rem = b
    for _ in range(passes):
        piece = rem.astype(BF16)
        term = _dot(a_bf16, piece)
        out = term if out is None else out + term
        rem = rem - piece.astype(F32)
    return out


def _silu(x):
    return x * jax.nn.sigmoid(x)


def _adaln(x, g, sc, sh):
    y = x * lax.rsqrt(jnp.mean(x * x, axis=-1, keepdims=True) + EPS)
    return (y * g) * (1.0 + sc) + sh


def _rope(x, cos, sin_hi, sin_lo):
    w = x.shape[-1]
    return x * cos + pltpu.roll(x, 8, 1) * sin_hi + pltpu.roll(x, w - 8, 1) * sin_lo


def _mod_kernel(c_ref, w_ref, b_ref, o_ref):
    a = _silu(c_ref[...])
    o_ref[0] = jnp.dot(a, w_ref[0], preferred_element_type=F32, precision=HIGHEST) + b_ref[0]


def _modulation(cond, w_mod, b_mod):
    tn = D_MODEL
    n_col = N_MOD * D_MODEL
    out = pl.pallas_call(
        _mod_kernel,
        grid=(DEPTH, n_col // tn),
        in_specs=[
            pl.BlockSpec((MOD_ROWS, D_MODEL), lambda l, j: (0, 0)),
            pl.BlockSpec((1, D_MODEL, tn), lambda l, j: (l, 0, j)),
            pl.BlockSpec((1, 1, tn), lambda l, j: (l, 0, j)),
        ],
        out_specs=pl.BlockSpec((1, MOD_ROWS, tn), lambda l, j: (l, 0, j)),
        out_shape=jax.ShapeDtypeStruct((DEPTH, MOD_ROWS, n_col), F32),
        compiler_params=_cparams(2),
        name="modulation",
    )(cond, w_mod, b_mod.reshape(DEPTH, 1, n_col))
    return out.reshape(DEPTH * MOD_ROWS, N_MOD, D_MODEL)


class _Tokens:
    def __init__(self, n_ctx_seq, ctx_len, n_lat_seq, lat_len):
        self.bc, self.sc, self.bl, self.sl = n_ctx_seq, ctx_len, n_lat_seq, lat_len
        self.tc, self.tl = n_ctx_seq * ctx_len, n_lat_seq * lat_len
        self.t = self.tc + self.tl
        self.tm = min(TOKEN_TILE, lat_len)
        assert self.tc % self.tm == 0 and lat_len % self.tm == 0
        self.ctx_tiles = self.tc // self.tm
        self.lat_tiles_per_seq = lat_len // self.tm
        self.n_tiles = self.t // self.tm

    def mod_row(self, layer):
        def f(i):
            lat = 1 + (i - self.ctx_tiles) // self.lat_tiles_per_seq
            return layer * MOD_ROWS + jnp.where(i < self.ctx_tiles, 0, lat)

        return f

    def rope_block(self, i):
        return jnp.where(i < self.ctx_tiles, 0, 1 + (i - self.ctx_tiles) % self.lat_tiles_per_seq)


def _mod_spec(tok, layer):
    row = tok.mod_row(layer)
    return pl.BlockSpec((1, N_MOD, D_MODEL), lambda i: (row(i), 0, 0))


def _const_spec(shape):
    zeros = (0,) * len(shape)
    return pl.BlockSpec(shape, lambda *_: zeros)


def _row_spec(tm, width):
    return pl.BlockSpec((tm, width), lambda i: (i, 0))


def _ffn_kernel(x_ref, mod_ref, g_ref, wgu_ref, wd_ref, o_ref, *, mod_base):
    x = x_ref[...]
    sh = mod_ref[0, mod_base : mod_base + 1, :]
    sc = mod_ref[0, mod_base + 1 : mod_base + 2, :]
    gate = mod_ref[0, mod_base + 2 : mod_base + 3, :]
    h = _adaln(x, g_ref[...], sc, sh).astype(BF16)
    au = _dot(h, wgu_ref[...])
    act = (_silu(au[:, :D_FF]) * au[:, D_FF:]).astype(BF16)
    o_ref[...] = x + (0.5 * gate) * _dot(act, wd_ref[...])


def _ffn(tok, x, mod, layer, which, norm_g, w_gu, w_down):
    tm = tok.tm
    return pl.pallas_call(
        functools.partial(_ffn_kernel, mod_base=6 * which),
        grid=(tok.n_tiles,),
        in_specs=[
            _row_spec(tm, D_MODEL),
            _mod_spec(tok, layer),
            _const_spec((1, D_MODEL)),
            pl.BlockSpec((D_MODEL, 2 * D_FF), lambda i: (0, 0), pipeline_mode=pl.Buffered(1)),
            pl.BlockSpec((D_FF, D_MODEL), lambda i: (0, 0), pipeline_mode=pl.Buffered(1)),
        ],
        out_specs=_row_spec(tm, D_MODEL),
        out_shape=jax.ShapeDtypeStruct((tok.t, D_MODEL), F32),
        compiler_params=_cparams(1),
        name="ffn",
    )(x, mod, norm_g.reshape(1, D_MODEL), w_gu, w_down)


def _inproj_kernel(x_ref, mod_ref, g_ref, w_ref, pa_ref, pb_ref, pc_ref, pd_ref):
    h = _adaln(x_ref[...], g_ref[...], mod_ref[0, 4:5, :], mod_ref[0, 3:4, :]).astype(BF16)
    p = _dot(h, w_ref[...])
    pa_ref[...] = p[:, :PA_W]
    pb_ref[...] = p[:, PA_W : PA_W + PB_W]
    pc_ref[...] = p[:, PA_W + PB_W : PA_W + PB_W + PC_W]
    pd_ref[...] = p[:, PA_W + PB_W + PC_W :]


def _inproj(tok, x, mod, layer, norm_g, w_in_pad):
    tm = tok.tm
    widths = (PA_W, PB_W, PC_W, PD_W)
    return pl.pallas_call(
        _inproj_kernel,
        grid=(tok.n_tiles,),
        in_specs=[
            _row_spec(tm, D_MODEL),
            _mod_spec(tok, layer),
            _const_spec((1, D_MODEL)),
            pl.BlockSpec((D_MODEL, IN_PAD_W), lambda i: (0, 0), pipeline_mode=pl.Buffered(1)),
        ],
        out_specs=[_row_spec(tm, w) for w in widths],
        out_shape=[jax.ShapeDtypeStruct((tok.t, w), F32) for w in widths],
        compiler_params=_cparams(1),
        name="inproj",
    )(x, mod, norm_g.reshape(1, D_MODEL), w_in_pad)


def _outproj_kernel(x_ref, mod_ref, oa_ref, ob_ref, oc_ref, od_ref, w_ref, o_ref):
    o = jnp.concatenate([oa_ref[...], ob_ref[...], oc_ref[...], od_ref[...]], axis=-1)
    o_ref[...] = x_ref[...] + mod_ref[0, 5:6, :] * _dot(o, w_ref[...])


def _outproj(tok, x, mod, layer, outs, w_out):
    tm = tok.tm
    return pl.pallas_call(
        _outproj_kernel,
        grid=(tok.n_tiles,),
        in_specs=[_row_spec(tm, D_MODEL), _mod_spec(tok, layer)]
        + [_row_spec(tm, GROUP_W)] * 4
        + [_const_spec((D_MODEL, D_MODEL))],
        out_specs=_row_spec(tm, D_MODEL),
        out_shape=jax.ShapeDtypeStruct((tok.t, D_MODEL), F32),
        compiler_params=_cparams(1),
        name="outproj",
    )(x, mod, *outs, w_out)


def _rope_group_tables(n_tok):
    t = np.arange(n_tok)
    pos = np.stack([t // GRID_W, t % GRID_W], axis=1).astype(np.float32)
    inv = (ROPE_BASE ** (-np.arange(8, dtype=np.float32) / 8)).astype(np.float32)
    lane = np.arange(32)
    ang = (pos[:, lane // 16] * inv[lane % 8][None, :]).astype(np.float32)
    second = (lane % 16) >= 8
    cos, sin = np.cos(ang), np.sin(ang)
    return cos, np.where(second[None], sin, 0.0), np.where(second[None], 0.0, -sin)


def _rope_tables(tok, lane_groups, width):
    cos = np.ones((tok.tm + tok.sl, width), np.float32)
    s_hi = np.zeros_like(cos)
    s_lo = np.zeros_like(cos)
    c, a, b = _rope_group_tables(tok.sl)
    for g in lane_groups:
        cos[tok.tm :, g : g + 32] = c
        s_hi[tok.tm :, g : g + 32] = a
        s_lo[tok.tm :, g : g + 32] = b
    return jnp.asarray(cos), jnp.asarray(s_hi), jnp.asarray(s_lo)


def _mla_keys_values(kv_in, wk_ref, wv_ref, kn_ref, rope):
    kraw = _dot(kv_in.astype(BF16), wk_ref[...])
    v = _dot(kv_in[:, :KV_LORA].astype(BF16), wv_ref[...])
    ks = []
    for h in range(H_B):
        kh = kraw[:, h * HEAD_PAD : (h + 1) * HEAD_PAD]
        ss = jnp.sum(kh * kh, axis=-1, keepdims=True) * (1.0 / (NOPE_B + ROPE_B))
        kh = kh * lax.rsqrt(ss + EPS) * kn_ref[...]
        if rope is not None:
            kh = _rope(kh, *rope)
        ks.append(kh.astype(BF16))
    return ks, v.astype(BF16)


def _mla_prep_kernel(pb_ref, gq_ref, gkv_ref, wuq_ref, wk_ref, wv_ref, qn_ref, kn_ref, cos_ref, shi_ref, slo_ref,
                     q_ref, k_ref, v_ref, cache_ref):
    pb = pb_ref[...]
    rope = (cos_ref[...], shi_ref[...], slo_ref[...])
    cq = pb[:, :Q_LORA]
    cq = cq * lax.rsqrt(jnp.mean(cq * cq, axis=-1, keepdims=True) + EPS) * gq_ref[...]
    qraw = _dot(cq.astype(BF16), wuq_ref[...])
    scale = (NOPE_B + ROPE_B) ** -0.5
    for h in range(H_B):
        qh = qraw[:, h * HEAD_PAD : (h + 1) * HEAD_PAD]
        ss = jnp.sum(qh * qh, axis=-1, keepdims=True) * (1.0 / (NOPE_B + ROPE_B))
        qh = _rope(qh * lax.rsqrt(ss + EPS) * qn_ref[...], *rope)
        q_ref[:, h * HEAD_PAD : (h + 1) * HEAD_PAD] = (qh * scale).astype(BF16)
    ckv = pb[:, Q_LORA : Q_LORA + KV_LORA]
    ckv = ckv * lax.rsqrt(jnp.mean(ckv * ckv, axis=-1, keepdims=True) + EPS) * gkv_ref[...]
    kv_in = jnp.concatenate([ckv, pb[:, Q_LORA + KV_LORA :]], axis=-1)
    cache_ref[...] = kv_in
    ks, v = _mla_keys_values(kv_in, wk_ref, wv_ref, kn_ref, rope)
    for h in range(H_B):
        k_ref[:, h * HEAD_PAD : (h + 1) * HEAD_PAD] = ks[h]
    v_ref[...] = v


def _mla_cache_kernel(c_ref, wk_ref, wv_ref, kn_ref, k_ref, v_ref):
    ks, v = _mla_keys_values(c_ref[...], wk_ref, wv_ref, kn_ref, None)
    for h in range(H_B):
        k_ref[:, h * HEAD_PAD : (h + 1) * HEAD_PAD] = ks[h]
    v_ref[...] = v


def _mla_weights(w_uq, w_ukv, qk_norm):
    wuq = jnp.pad(w_uq.reshape(Q_LORA, H_B, NOPE_B + ROPE_B), ((0, 0), (0, 0), (0, HEAD_PAD - NOPE_B - ROPE_B)))
    wuq = wuq.reshape(Q_LORA, H_B * HEAD_PAD).astype(BF16)
    ukv = w_ukv.reshape(KV_LORA, H_B, NOPE_B + V_B)
    wk_nope = jnp.pad(ukv[:, :, :NOPE_B], ((0, 0), (0, 0), (0, HEAD_PAD - NOPE_B)))
    place = np.zeros((2 * LANES - KV_LORA, H_B, HEAD_PAD), np.float32)
    for h in range(H_B):
        place[np.arange(ROPE_B), h, NOPE_B + np.arange(ROPE_B)] = 1.0
    wk = jnp.concatenate([wk_nope, jnp.asarray(place)], axis=0).reshape(2 * LANES, H_B * HEAD_PAD).astype(BF16)
    wv = ukv[:, :, NOPE_B:].reshape(KV_LORA, H_B * V_B).astype(BF16)
    pad = lambda g: jnp.pad(g, (0, HEAD_PAD - NOPE_B - ROPE_B)).reshape(1, HEAD_PAD)
    return wuq, wk, wv, pad(qk_norm[0]), pad(qk_norm[1])


def _mla_prep(tok, pb, gq, gkv, weights, tables):
    wuq, wk, wv, qn, kn = weights
    tm = tok.tm
    rope_spec = pl.BlockSpec((tm, HEAD_PAD), lambda i: (tok.rope_block(i), 0))
    return pl.pallas_call(
        _mla_prep_kernel,
        grid=(tok.n_tiles,),
        in_specs=[
            _row_spec(tm, PB_W),
            _const_spec((1, Q_LORA)),
            _const_spec((1, KV_LORA)),
            _const_spec(wuq.shape),
            _const_spec(wk.shape),
            _const_spec(wv.shape),
            _const_spec((1, HEAD_PAD)),
            _const_spec((1, HEAD_PAD)),
            rope_spec,
            rope_spec,
            rope_spec,
        ],
        out_specs=[_row_spec(tm, H_B * HEAD_PAD), _row_spec(tm, H_B * HEAD_PAD), _row_spec(tm, H_B * V_B),
                   _row_spec(tm, 2 * LANES)],
        out_shape=[
            jax.ShapeDtypeStruct((tok.t, H_B * HEAD_PAD), BF16),
            jax.ShapeDtypeStruct((tok.t, H_B * HEAD_PAD), BF16),
            jax.ShapeDtypeStruct((tok.t, H_B * V_B), BF16),
            jax.ShapeDtypeStruct((tok.t, 2 * LANES), F32),
        ],
        compiler_params=_cparams(1),
        name="mla_prep",
    )(pb, gq.reshape(1, -1), gkv.reshape(1, -1), wuq, wk, wv, qn, kn, *tables)


def _mla_cache_prep(cache_rows, weights):
    _, wk, wv, _, kn = weights
    rows = cache_rows.shape[0]
    tm = min(TOKEN_TILE, rows)
    return pl.pallas_call(
        _mla_cache_kernel,
        grid=(rows // tm,),
        in_specs=[_row_spec(tm, 2 * LANES), _const_spec(wk.shape), _const_spec(wv.shape), _const_spec((1, HEAD_PAD))],
        out_specs=[_row_spec(tm, H_B * HEAD_PAD), _row_spec(tm, H_B * V_B)],
        out_shape=[jax.ShapeDtypeStruct((rows, H_B * HEAD_PAD), BF16), jax.ShapeDtypeStruct((rows, H_B * V_B), BF16)],
        compiler_params=_cparams(1),
        name="mla_cache_prep",
    )(cache_rows, wk, wv, kn)


def _softmax_parts(scores):
    m = functools.reduce(jnp.maximum, [jnp.max(s, axis=-1, keepdims=True) for s in scores])
    ps = [jnp.exp(s - m) for s in scores]
    denom = functools.reduce(jnp.add, [jnp.sum(p, axis=-1, keepdims=True) for p in ps])
    return ps, denom


def _head_lane_mask(width, head, head_w):
    lane = lax.broadcasted_iota(jnp.int32, (1, width), 1)
    return (lane >= head * head_w) & (lane < (head + 1) * head_w)


def _mla_attn_kernel(*refs, n_kv):
    q_ref = refs[0]
    k_refs = refs[1 : 1 + n_kv]
    v_refs = refs[1 + n_kv : 1 + 2 * n_kv]
    o_ref = refs[1 + 2 * n_kv]
    out = None
    for h in range(H_B):
        sl = slice(h * HEAD_PAD, (h + 1) * HEAD_PAD)
        qh = q_ref[:, sl]
        ps, denom = _softmax_parts([_nt_dot(qh, k[:, sl]) for k in k_refs])
        mask = _head_lane_mask(H_B * V_B, h, V_B)
        acc = None
        for p, v in zip(ps, v_refs):
            term = _dot(p.astype(BF16), jnp.where(mask, v[...], jnp.zeros((), BF16)))
            acc = term if acc is None else acc + term
        acc = acc * (1.0 / denom)
        out = acc if out is None else out + acc
    o_ref[...] = out.astype(BF16)


def _seq_specs(n_seq, seq_len, row0, tq, q_width, kv_widths, kv_new, kv_cache):
    q_blocks = seq_len // tq
    q_spec = lambda w: pl.BlockSpec((tq, w), lambda b, i: (row0 // tq + b * q_blocks + i, 0))
    new_spec = lambda w: pl.BlockSpec((seq_len, w), lambda b, i: (row0 // seq_len + b, 0))
    specs = [q_spec(q_width)]
    for w in kv_widths:
        if kv_cache is not None:
            specs.append(pl.BlockSpec((kv_cache, w), lambda b, i: (b, 0)))
        specs.append(new_spec(w))
    return specs, q_spec


def _mla_attention(n_seq, seq_len, row0, total_rows, q, k, v, k_cache=None, v_cache=None):
    tq = min(Q_TILE, seq_len)
    has_cache = k_cache is not None
    cache_len = k_cache.shape[0] // n_seq if has_cache else None
    specs, q_spec = _seq_specs(n_seq, seq_len, row0, tq, H_B * HEAD_PAD, (H_B * HEAD_PAD, H_B * V_B), True, cache_len)
    args = [q] + ([k_cache, k] if has_cache else [k]) + ([v_cache, v] if has_cache else [v])
    return pl.pallas_call(
        functools.partial(_mla_attn_kernel, n_kv=2 if has_cache else 1),
        grid=(n_seq, seq_len // tq),
        in_specs=specs,
        out_specs=q_spec(H_B * V_B),
        out_shape=jax.ShapeDtypeStruct((total_rows, H_B * V_B), BF16),
        compiler_params=_cparams(2),
        name="mla_attention",
    )(*args)


DQK_W = H_D * 2 * HEAD_PAD


def _diff_place_matrix():
    m = np.zeros((H_D * 2 * DH_D, DQK_W), np.float32)
    g = np.arange(H_D * 2 * DH_D)
    m[g, (g // DH_D) * HEAD_PAD + g % DH_D] = 1.0
    return jnp.asarray(m, BF16)


def _group_mean_matrix(width, group):
    g = np.arange(width)
    return jnp.asarray((g[:, None] // group == g[None, :] // group).astype(np.float32), BF16)


def _diff_prep_kernel(pd_ref, gq_ref, gk_ref, gm_ref, place_ref, cos_ref, shi_ref, slo_ref,
                      q_ref, k_ref, v_ref, kcache_ref):
    pd = pd_ref[...]
    rope = (cos_ref[...], shi_ref[...], slo_ref[...])
    w = H_D * 2 * DH_D

    def norm(x, g):
        ms = _dot_exact_rhs(x * x, gm_ref[...], passes=2) * (1.0 / DH_D)
        return x * lax.rsqrt(ms + EPS) * g

    qn = _rope(norm(pd[:, :w], gq_ref[...]), *rope) * DH_D ** -0.5
    kn = norm(pd[:, w : 2 * w], gk_ref[...])
    kcache_ref[...] = kn
    q_ref[...] = _dot(qn.astype(BF16), place_ref[...]).astype(BF16)
    k_ref[...] = _dot(_rope(kn, *rope).astype(BF16), place_ref[...]).astype(BF16)
    v_ref[...] = pd[:, 2 * w :].astype(BF16)


def _diff_cache_kernel(k_in_ref, v_in_ref, place_ref, k_ref, v_ref):
    k_ref[...] = _dot(k_in_ref[...].astype(BF16), place_ref[...]).astype(BF16)
    v_ref[...] = v_in_ref[...].astype(BF16)


def _diff_prep(tok, pd, qk_norm, tables):
    tm = tok.tm
    w = H_D * 2 * DH_D
    rope_spec = pl.BlockSpec((tm, w), lambda i: (tok.rope_block(i), 0))
    tile_g = lambda g: jnp.tile(g, H_D * 2).reshape(1, w)
    return pl.pallas_call(
        _diff_prep_kernel,
        grid=(tok.n_tiles,),
        in_specs=[_row_spec(tm, PD_W), _const_spec((1, w)), _const_spec((1, w)), _const_spec((w, w)),
                  _const_spec((w, DQK_W)), rope_spec, rope_spec, rope_spec],
        out_specs=[_row_spec(tm, DQK_W), _row_spec(tm, DQK_W), _row_spec(tm, H_D * DV_D), _row_spec(tm, w)],
        out_shape=[
            jax.ShapeDtypeStruct((tok.t, DQK_W), BF16),
            jax.ShapeDtypeStruct((tok.t, DQK_W), BF16),
            jax.ShapeDtypeStruct((tok.t, H_D * DV_D), BF16),
            jax.ShapeDtypeStruct((tok.t, w), F32),
        ],
        compiler_params=_cparams(1),
        name="diff_prep",
    )(pd, tile_g(qk_norm[0]), tile_g(qk_norm[1]), _group_mean_matrix(w, DH_D), _diff_place_matrix(), *tables)


def _diff_cache_prep(k_rows, v_rows):
    rows = k_rows.shape[0]
    tm = min(TOKEN_TILE, rows)
    w = H_D * 2 * DH_D
    return pl.pallas_call(
        _diff_cache_kernel,
        grid=(rows // tm,),
        in_specs=[_row_spec(tm, w), _row_spec(tm, H_D * DV_D), _const_spec((w, DQK_W))],
        out_specs=[_row_spec(tm, DQK_W), _row_spec(tm, H_D * DV_D)],
        out_shape=[jax.ShapeDtypeStruct((rows, DQK_W), BF16), jax.ShapeDtypeStruct((rows, H_D * DV_D), BF16)],
        compiler_params=_cparams(1),
        name="diff_cache_prep",
    )(k_rows, v_rows, _diff_place_matrix())


def _diff_attn_kernel(*refs, n_kv, lam_init):
    q_ref, lam_ref, sub_ref = refs[0], refs[1], refs[2]
    k_refs = refs[3 : 3 + n_kv]
    v_refs = refs[3 + n_kv : 3 + 2 * n_kv]
    o_ref = refs[3 + 2 * n_kv]
    dl = lam_ref[...]
    lam = (jnp.exp(jnp.sum(dl[0:1] * dl[1:2], axis=-1, keepdims=True))
           - jnp.exp(jnp.sum(dl[2:3] * dl[3:4], axis=-1, keepdims=True)) + lam_init)
    out = None
    for h in range(H_D):
        probs = []
        for m in range(2):
            sl = slice((2 * h + m) * HEAD_PAD, (2 * h + m + 1) * HEAD_PAD)
            qh = q_ref[:, sl]
            ps, denom = _softmax_parts([_nt_dot(qh, k[:, sl]) for k in k_refs])
            inv = 1.0 / denom
            probs.append([p * inv for p in ps])
        mask = _head_lane_mask(H_D * DV_D, h, DV_D)
        acc = None
        for p0, p1, v in zip(probs[0], probs[1], v_refs):
            wgt = (p0 - lam * p1).astype(BF16)
            term = _dot(wgt, jnp.where(mask, v[...], jnp.zeros((), BF16)))
            acc = term if acc is None else acc + term
        ms = jnp.sum(acc * acc, axis=-1, keepdims=True) * (1.0 / DV_D)
        acc = acc * lax.rsqrt(ms + EPS)
        out = acc if out is None else out + acc
    o_ref[...] = (out * sub_ref[...] * (1.0 - lam_init)).astype(BF16)


def _diff_attention(layer, n_seq, seq_len, row0, total_rows, lam_p, sub_g, q, k, v, k_cache=None, v_cache=None):
    tq = min(Q_TILE, seq_len)
    has_cache = k_cache is not None
    cache_len = k_cache.shape[0] // n_seq if has_cache else None
    specs, q_spec = _seq_specs(n_seq, seq_len, row0, tq, DQK_W, (DQK_W, H_D * DV_D), True, cache_len)
    specs = [specs[0], pl.BlockSpec((4, DH_D), lambda b, i: (0, 0)), pl.BlockSpec((1, H_D * DV_D), lambda b, i: (0, 0))
             ] + specs[1:]
    args = [q, lam_p, jnp.tile(sub_g, H_D).reshape(1, H_D * DV_D)]
    args += ([k_cache, k] if has_cache else [k]) + ([v_cache, v] if has_cache else [v])
    lam_init = 0.8 - 0.6 * math.exp(-0.3 * layer)
    return pl.pallas_call(
        functools.partial(_diff_attn_kernel, n_kv=2 if has_cache else 1, lam_init=lam_init),
        grid=(n_seq, seq_len // tq),
        in_specs=specs,
        out_specs=q_spec(H_D * DV_D),
        out_shape=jax.ShapeDtypeStruct((total_rows, H_D * DV_D), BF16),
        compiler_params=_cparams(2),
        name="diff_attention",
    )(*args)


A_W = H_A * DK_A


def _hgrn_constants():
    t = np.arange(HGRN_BLOCK)
    same = (t[:, None] // HGRN_CHUNK) == (t[None, :] // HGRN_CHUNK)
    lower = same & (t[None, :] <= t[:, None])
    upper = same & (t[None, :] >= t[:, None])
    g = np.arange(A_W)
    heads = (g[:, None] // DK_A) == (g[None, :] // DK_A)
    as_bf16 = lambda m: jnp.asarray(m.astype(np.float32), BF16)
    return as_bf16(lower), as_bf16(upper), as_bf16(same), as_bf16(heads), jnp.asarray(heads.astype(np.float32))


def _hgrn_kernel(*refs, layer, seq_len, has_state):
    if has_state:
        pa_ref, lbl_ref, og_ref, s0_ref, lower_ref, upper_ref, same_ref, heads_ref, hmask_ref = refs[:9]
        rest = refs[9:]
    else:
        pa_ref, lbl_ref, og_ref, lower_ref, upper_ref, same_ref, heads_ref, hmask_ref = refs[:8]
        s0_ref = None
        rest = refs[8:]
    o_ref, sout_ref, st_ref, oacc_ref = rest
    n_blocks = seq_len // HGRN_BLOCK
    n_chunks = HGRN_BLOCK // HGRN_CHUNK
    row_in_chunk = lax.broadcasted_iota(jnp.int32, (HGRN_BLOCK, 1), 0) % HGRN_CHUNK

    for d in range(2):
        logits = lbl_ref[d]
        e = jnp.exp(logits - jnp.max(logits, axis=0, keepdims=True))
        p = e / jnp.sum(e, axis=0, keepdims=True)
        lb = jnp.zeros((1, A_W), F32)
        for j in range(1, layer + 1):
            lb = lb + p[j : j + 1, :]
        tri_ref = lower_ref if d == 0 else upper_ref
        if has_state:
            st_ref[...] = jnp.zeros((A_W, A_W), F32)
            for h in range(H_A):
                st_ref[h * DV_A : (h + 1) * DV_A, h * DK_A : (h + 1) * DK_A] = s0_ref[0, d, h].T
        else:
            st_ref[...] = jnp.zeros((A_W, A_W), F32)

        def block(i, carry, d=d, lb=lb, tri_ref=tri_ref):
            blk = i if d == 0 else n_blocks - 1 - i
            r0 = pl.multiple_of(blk * HGRN_BLOCK, HGRN_BLOCK)
            rows = pl.ds(r0, HGRN_BLOCK)
            q = _silu(pa_ref[rows, 0:A_W]) * DK_A ** -0.5
            v = pa_ref[rows, A_W : 2 * A_W]
            logit = pa_ref[rows, (2 + d) * A_W : (3 + d) * A_W]
            key = jnp.minimum((1.0 - lb) * jax.nn.sigmoid(-logit), MAX_INPUT_KEY)
            log_f = jnp.log1p(-key)
            cum = _dot_exact_lhs(tri_ref[...], log_f)
            tot = _dot_exact_lhs(same_ref[...], log_f)
            o = None
            for r in range(HGRN_CHUNK):
                shift = r if d == 0 else (HGRN_BLOCK - r) % HGRN_BLOCK
                if r == 0:
                    pair = q * key
                    vs = v
                else:
                    valid = (row_in_chunk >= r) if d == 0 else (row_in_chunk < HGRN_CHUNK - r)
                    ks = pltpu.roll(key, shift, 0)
                    cs = pltpu.roll(cum, shift, 0)
                    vs = pltpu.roll(v, shift, 0)
                    pair = jnp.where(valid, q * ks * jnp.exp(jnp.minimum(cum - cs, 0.0)), 0.0)
                w = _dot(pair.astype(BF16), heads_ref[...])
                o = w * vs if o is None else o + w * vs
            q_dec = (q * jnp.exp(cum)).astype(BF16)
            k_dec = (key * jnp.exp(tot - cum)).astype(BF16)
            v_b = v.astype(BF16)
            inter = [None] * n_chunks
            for cc in range(n_chunks):
                c = cc if d == 0 else n_chunks - 1 - cc
                sl = slice(c * HGRN_CHUNK, (c + 1) * HGRN_CHUNK)
                state = st_ref[...]
                inter[c] = _nt_dot(q_dec[sl], state.astype(BF16))
                upd = lax.dot_general(v_b[sl], k_dec[sl], (((0,), (0,)), ((), ())), preferred_element_type=F32)
                decay = jnp.exp(tot[c * HGRN_CHUNK : c * HGRN_CHUNK + 1, :])
                st_ref[...] = state * decay + upd * hmask_ref[...]
            o = o + jnp.concatenate(inter, axis=0)
            if d == 0:
                oacc_ref[rows, :] = o
            else:
                oacc_ref[rows, :] += o
            return carry

        lax.fori_loop(0, n_blocks, block, 0)
        for h in range(H_A):
            sout_ref[0, d, h] = st_ref[h * DV_A : (h + 1) * DV_A, h * DK_A : (h + 1) * DK_A].T

    o = oacc_ref[...]
    ms = _dot_exact_rhs(o * o, heads_ref[...], passes=2) * (1.0 / DV_A)
    o_ref[...] = (o * lax.rsqrt(ms + EPS) * og_ref[...] * _silu(pa_ref[:, 4 * A_W : 5 * A_W])).astype(BF16)


def _hgrn(layer, n_seq, seq_len, row0, total_rows, pa, lb_logits, onorm_g, s0):
    has_state = s0 is not None
    consts = _hgrn_constants()
    seq_spec = lambda w: pl.BlockSpec((seq_len, w), lambda b: (row0 // seq_len + b, 0))
    state_spec = pl.BlockSpec((1, 2, H_A, DK_A, DV_A), lambda b: (b, 0, 0, 0, 0))
    in_specs = [seq_spec(PA_W), _const_spec(lb_logits.shape), _const_spec((1, A_W))]
    args = [pa, lb_logits, jnp.tile(onorm_g, H_A).reshape(1, A_W)]
    if has_state:
        in_specs.append(state_spec)
        args.append(s0)
    in_specs += [_const_spec(c.shape) for c in consts]
    args += list(consts)
    return pl.pallas_call(
        functools.partial(_hgrn_kernel, layer=layer, seq_len=seq_len, has_state=has_state),
        grid=(n_seq,),
        in_specs=in_specs,
        out_specs=[seq_spec(A_W), state_spec],
        out_shape=[
            jax.ShapeDtypeStruct((total_rows, A_W), BF16),
            jax.ShapeDtypeStruct((n_seq, 2, H_A, DK_A, DV_A), F32),
        ],
        scratch_shapes=[pltpu.VMEM((A_W, A_W), F32), pltpu.VMEM((seq_len, A_W), F32)],
        compiler_params=_cparams(1),
        name="hgrn",
    )(*args)


HY_W = HY_ORDER * W_C
FFT_R = 64


def _hy_filter_kernel(w1_ref, b1_ref, w2_ref, b2_ref, w3_ref, ld_ref, hf_ref, hb_ref, *, n):
    f32dot = functools.partial(jnp.dot, preferred_element_type=F32, precision=HIGHEST)
    row = lax.broadcasted_iota(jnp.int32, (n, LANES), 0)
    lane = lax.broadcasted_iota(jnp.int32, (n, LANES), 1)
    tn = row.astype(F32) / n
    band = jnp.where(lane <= HY_BANDS, lane, lane - HY_BANDS).astype(F32)
    ang = (2.0 * math.pi) * tn * band
    feats = jnp.where(lane == 0, tn, jnp.where(lane <= HY_BANDS, jnp.cos(ang),
                                                jnp.where(lane <= 2 * HY_BANDS, jnp.sin(ang), 0.0)))
    h = jnp.sin(f32dot(feats, w1_ref[0]) + b1_ref[0])
    h = jnp.sin(f32dot(h, w2_ref[0]) + b2_ref[0])
    h = f32dot(h, w3_ref[0])
    h = h * jnp.exp(-jnp.exp(ld_ref[0]) * tn[:, 0:1])
    hf = h[:, :HY_W]
    hb = jnp.where(row[:, 0:1] == 0, 0.0, h[:, HY_W:])
    norm = jnp.sum(jnp.abs(hf), axis=0, keepdims=True) + jnp.sum(jnp.abs(hb), axis=0, keepdims=True) + EPS
    hf_ref[0] = hf / norm
    hb_ref[0] = hb / norm


def _hy_filters(n, w1, b1, w2, b2, w3, log_decay):
    w1p = jnp.pad(w1, ((0, 0), (0, LANES - HY_EMB), (0, 0)))
    lay = lambda shape: pl.BlockSpec((1,) + shape, lambda l: (l,) + (0,) * len(shape))
    return pl.pallas_call(
        functools.partial(_hy_filter_kernel, n=n),
        grid=(DEPTH,),
        in_specs=[lay((LANES, HY_FH)), lay((1, HY_FH)), lay((HY_FH, HY_FH)), lay((1, HY_FH)),
                  lay((HY_FH, 2 * HY_W)), lay((1, 2 * HY_W))],
        out_specs=[lay((n, HY_W)), lay((n, HY_W))],
        out_shape=[jax.ShapeDtypeStruct((DEPTH, n, HY_W), F32)] * 2,
        compiler_params=_cparams(1),
        name="hyena_filters",
    )(w1p, b1.reshape(DEPTH, 1, HY_FH), w2, b2.reshape(DEPTH, 1, HY_FH), w3, log_decay.reshape(DEPTH, 1, 2 * HY_W))


def _dft_tables_short(n):
    big = 2 * n
    k = np.arange(big)[:, None]
    t = np.arange(n)[None, :]
    ang = 2.0 * np.pi * ((k * t) % big) / big
    fwd = np.concatenate([np.cos(ang), -np.sin(ang)], axis=0)
    inv = np.concatenate([np.cos(ang).T, -np.sin(ang).T], axis=1) / big
    return fwd.astype(np.float32), inv.astype(np.float32)


def _hy_spec_short_kernel(hf_ref, hb_ref, fwd_ref, f_ref, *, n):
    f32dot = functools.partial(jnp.dot, preferred_element_type=F32, precision=HIGHEST)
    xf = f32dot(fwd_ref[...], hf_ref[0])
    xb = f32dot(fwd_ref[...], hb_ref[0])
    big = 2 * n
    f_ref[0, :big, :] = xf[:big] + xb[:big]
    f_ref[0, big:, :] = xf[big:] - xb[big:]


def _hy_spec_short(n, hf, hb):
    fwd, _ = _dft_tables_short(n)
    lay = lambda shape: pl.BlockSpec((1,) + shape, lambda l: (l,) + (0,) * len(shape))
    return pl.pallas_call(
        functools.partial(_hy_spec_short_kernel, n=n),
        grid=(DEPTH,),
        in_specs=[lay((n, HY_W)), lay((n, HY_W)), _const_spec(fwd.shape)],
        out_specs=lay((4 * n, HY_W)),
        out_shape=jax.ShapeDtypeStruct((DEPTH, 4 * n, HY_W), F32),
        compiler_params=_cparams(1),
        name="hyena_spectrum_short",
    )(hf, hb, jnp.asarray(fwd))


def _short_conv(x, w, n):
    row = lax.broadcasted_iota(jnp.int32, (n, 1), 0)
    prev = jnp.where(row == 0, 0.0, pltpu.roll(x, 1, 0))
    nxt = jnp.where(row == n - 1, 0.0, pltpu.roll(x, n - 1, 0))
    return prev * w[0:1, :] + x * w[1:2, :] + nxt * w[2:3, :]


def _hy_conv_short_kernel(pc_ref, ws_ref, bias_ref, f_ref, fwd_ref, inv_ref, o_ref, *, n):
    big = 2 * n
    u = _short_conv(pc_ref[...], ws_ref[...], n)
    v, x1, x2 = u[:, :W_C], u[:, W_C : 2 * W_C], u[:, 2 * W_C :]

    def conv(x, order):
        spec = _dot(fwd_ref[...], x.astype(BF16))
        fr = f_ref[:big, order * W_C : (order + 1) * W_C]
        fi = f_ref[big:, order * W_C : (order + 1) * W_C]
        zr = spec[:big] * fr - spec[big:] * fi
        zi = spec[:big] * fi + spec[big:] * fr
        return _dot(inv_ref[...], jnp.concatenate([zr, zi], axis=0).astype(BF16))

    z = x1 * (conv(v, 0) + v * bias_ref[0:1, :])
    z = x2 * (conv(z, 1) + z * bias_ref[1:2, :])
    o_ref[...] = z.astype(BF16)


def _hy_conv_short(n_seq, n, row0, total_rows, pc, w_short, bias, spec):
    fwd, inv = _dft_tables_short(n)
    seq_spec = lambda w: pl.BlockSpec((n, w), lambda b: (row0 // n + b, 0))
    return pl.pallas_call(
        functools.partial(_hy_conv_short_kernel, n=n),
        grid=(n_seq,),
        in_specs=[seq_spec(PC_W), _const_spec((SHORT_K, PC_W)), _const_spec((HY_ORDER, W_C)),
                  _const_spec((4 * n, HY_W)), _const_spec(fwd.shape), _const_spec(inv.shape)],
        out_specs=seq_spec(W_C),
        out_shape=jax.ShapeDtypeStruct((total_rows, W_C), BF16),
        compiler_params=_cparams(1),
        name="hyena_conv_short",
    )(pc, w_short, bias, spec, jnp.asarray(fwd, BF16), jnp.asarray(inv, BF16))


def _dft_tables_long():
    r = FFT_R
    big = r * r
    half = r // 2
    n2 = np.arange(r)[:, None, None]
    k1 = np.arange(r)[None, :, None]
    n1 = np.arange(half)[None, None, :]
    ang = 2.0 * np.pi * ((k1 * (r * n1 + n2)) % big) / big
    first = np.concatenate([np.cos(ang), -np.sin(ang)], axis=1)
    last = np.concatenate([np.cos(ang), -np.sin(ang)], axis=1).transpose(0, 2, 1) / big
    a = np.arange(r)
    ang_r = 2.0 * np.pi * ((a[:, None] * a[None, :]) % r) / r
    c, s = np.cos(ang_r), np.sin(ang_r)
    mid = np.block([[c, s], [-s, c]])
    mid_inv = np.block([[c, -s], [s, c]])
    f32 = lambda m: m.astype(np.float32)
    return f32(first), f32(mid), f32(mid_inv), f32(last)


def _ld_rows(ref, rows):
    return jnp.concatenate([ref[j, rows, :] for j in range(ref.shape[0])], axis=-1)


def _st_rows(ref, rows, val):
    for j in range(ref.shape[0]):
        ref[j, rows, :] = val[:, j * LANES : (j + 1) * LANES]


def _lane_split_scratch(rows, width):
    return pltpu.VMEM((width // LANES, rows, LANES), F32)


def _fft_long_forward(x_ref, first_ref, mid_ref, y_ref, out_fn, dot_fn, cast):
    r = FFT_R

    def stage_a(n2, carry):
        slab = cast(_ld_rows(x_ref, pl.ds(n2, r // 2, stride=r)))
        res = dot_fn(first_ref[n2], slab)
        _st_rows(y_ref, pl.ds(n2, r, stride=2 * r), res[:r])
        _st_rows(y_ref, pl.ds(r + n2, r, stride=2 * r), res[r:])
        return carry

    lax.fori_loop(0, r, stage_a, 0)

    def stage_c(k1, carry):
        base = pl.multiple_of(k1 * 2 * r, 2 * r)
        out_fn(k1, base, dot_fn(mid_ref[...], cast(_ld_rows(y_ref, pl.ds(base, 2 * r)))))
        return carry

    lax.fori_loop(0, r, stage_c, 0)


def _hy_spec_long_kernel(hf_ref, hb_ref, first_ref, mid_ref, f_ref, x_ref, y_ref, tmp_ref):
    r = FFT_R
    f32dot = functools.partial(jnp.dot, preferred_element_type=F32, precision=HIGHEST)
    ident = lambda x: x

    def write_fwd(k1, base, spec):
        tmp_ref[pl.ds(base, 2 * r), :] = spec

    _st_rows(x_ref, slice(None), hf_ref[0])
    _fft_long_forward(x_ref, first_ref, mid_ref, y_ref, write_fwd, f32dot, ident)

    def write_sum(k1, base, spec):
        prev = tmp_ref[pl.ds(base, 2 * r), :]
        f_ref[0, pl.ds(base, r), :] = prev[:r] + spec[:r]
        f_ref[0, pl.ds(base + r, r), :] = prev[r:] - spec[r:]

    _st_rows(x_ref, slice(None), hb_ref[0])
    _fft_long_forward(x_ref, first_ref, mid_ref, y_ref, write_sum, f32dot, ident)


def _hy_spec_long(n, hf, hb):
    first, mid, _, _ = _dft_tables_long()
    big = 2 * n
    lay = lambda rows: pl.BlockSpec((1, rows, W_C), lambda l, o: (l, 0, o))
    return pl.pallas_call(
        _hy_spec_long_kernel,
        grid=(DEPTH, HY_ORDER),
        in_specs=[lay(n), lay(n), _const_spec(first.shape), _const_spec(mid.shape)],
        out_specs=lay(2 * big),
        out_shape=jax.ShapeDtypeStruct((DEPTH, 2 * big, HY_W), F32),
        scratch_shapes=[_lane_split_scratch(n, W_C), _lane_split_scratch(2 * big, W_C),
                        pltpu.VMEM((2 * big, W_C), F32)],
        compiler_params=_cparams(2),
        name="hyena_spectrum_long",
    )(hf, hb, jnp.asarray(first), jnp.asarray(mid))


def _hy_conv_long_kernel(sig_ref, gate_ref, ws_ref, bias_ref, f_ref, first_ref, mid_ref, midinv_ref, last_ref, o_ref,
                         x_ref, y_ref, v_ref, out_ref, *, n, order):
    r = FFT_R
    to_bf16 = lambda x: x.astype(BF16)
    gate = _short_conv(gate_ref[...], ws_ref[:, (order + 1) * W_C : (order + 2) * W_C], n)
    sig = _short_conv(sig_ref[...], ws_ref[:, :W_C], n) if order == 0 else sig_ref[...]
    _st_rows(x_ref, slice(None), sig)

    def filter_and_invert(k1, base, spec):
        fr = f_ref[pl.ds(base, r), :]
        fi = f_ref[pl.ds(base + r, r), :]
        zr = spec[:r] * fr - spec[r:] * fi
        zi = spec[:r] * fi + spec[r:] * fr
        res = _dot(midinv_ref[...], jnp.concatenate([zr, zi], axis=0).astype(BF16))
        _st_rows(v_ref, pl.ds(k1, r, stride=2 * r), res[:r])
        _st_rows(v_ref, pl.ds(r + k1, r, stride=2 * r), res[r:])

    _fft_long_forward(x_ref, first_ref, mid_ref, y_ref, filter_and_invert, _dot, to_bf16)

    def stage_last(n2, carry):
        base = pl.multiple_of(n2 * 2 * r, 2 * r)
        res = _dot(last_ref[n2], _ld_rows(v_ref, pl.ds(base, 2 * r)).astype(BF16))
        _st_rows(out_ref, pl.ds(n2, r // 2, stride=r), res)
        return carry

    lax.fori_loop(0, r, stage_last, 0)
    z = gate * (_ld_rows(out_ref, slice(None)) + sig * bias_ref[order : order + 1, :])
    o_ref[...] = z.astype(o_ref.dtype)


def _hy_conv_long(n_seq, n, row0, total_rows, pc, w_short, bias, spec):
    assert 2 * n == FFT_R * FFT_R
    first, mid, mid_inv, last = _dft_tables_long()
    big = 2 * n
    bf = lambda m: jnp.asarray(m, BF16)
    lane_block = lambda j: pl.BlockSpec((n, W_C), lambda b: (row0 // n + b, j))
    z = None
    for order in range(HY_ORDER):
        sig = pc if order == 0 else z
        z = pl.pallas_call(
            functools.partial(_hy_conv_long_kernel, n=n, order=order),
            grid=(n_seq,),
            in_specs=[lane_block(0), lane_block(order + 1), _const_spec((SHORT_K, PC_W)),
                      _const_spec((HY_ORDER, W_C)), pl.BlockSpec((2 * big, W_C), lambda b, order=order: (0, order)),
                      _const_spec(first.shape), _const_spec(mid.shape), _const_spec(mid_inv.shape),
                      _const_spec(last.shape)],
            out_specs=lane_block(0),
            out_shape=jax.ShapeDtypeStruct((total_rows, W_C), F32 if order + 1 < HY_ORDER else BF16),
            scratch_shapes=[_lane_split_scratch(n, W_C), _lane_split_scratch(2 * big, W_C),
                            _lane_split_scratch(2 * big, W_C), _lane_split_scratch(n, W_C)],
            compiler_params=_cparams(1),
            name="hyena_conv_long",
        )(sig, pc, w_short, bias, spec, bf(first), bf(mid), bf(mid_inv), bf(last))
    return z


def _pad_in_weights(w_in):
    gap = jnp.zeros((D_MODEL, PB_W - MLA_IN), w_in.dtype)
    return jnp.concatenate([w_in[:, : PA_W + MLA_IN], gap, w_in[:, PA_W + MLA_IN :]], axis=1).astype(BF16)


def kernel(x_prompt, x_sample, c, cache_mla, cache_diff_k, cache_diff_v, state_hgrn, c_ctx, w_mod, b_mod, norm_g,
           ffn_w_gu, ffn_w_down, w_in, w_out, hgrn_lb_logits, hgrn_onorm, mla_q_norm, mla_kv_norm, mla_w_uq,
           mla_w_ukv, mla_qk_norm, hy_short, hy_w1, hy_b1, hy_w2, hy_b2, hy_w3, hy_log_decay, hy_bias,
           diff_qk_norm, diff_lambda, diff_subln):
    bc, sc, _ = x_prompt.shape
    bl, sl, _ = x_sample.shape
    past = cache_mla.shape[2]
    tok = _Tokens(bc, sc, bl, sl)
    assert bl + 1 <= MOD_ROWS and tok.tc % sl == 0

    x = jnp.concatenate([x_prompt.reshape(tok.tc, D_MODEL), x_sample.reshape(tok.tl, D_MODEL)], axis=0)
    cond = jnp.concatenate([c_ctx[None], c, jnp.zeros((MOD_ROWS - 1 - bl, D_MODEL), F32)], axis=0)
    mod = _modulation(cond, w_mod, b_mod)

    filters = {n: _hy_filters(n, hy_w1, hy_b1, hy_w2, hy_b2, hy_w3, hy_log_decay) for n in (sc, sl)}
    spec_ctx = _hy_spec_short(sc, *filters[sc])
    spec_lat = _hy_spec_long(sl, *filters[sl])

    mla_tables = _rope_tables(tok, [NOPE_B], HEAD_PAD)
    diff_tables = _rope_tables(tok, list(range(0, H_D * 2 * DH_D, DH_D)), H_D * 2 * DH_D)

    new_mla, new_dk, new_dv, new_state = [], [], [], []
    for l in range(DEPTH):
        x = _ffn(tok, x, mod, l, 0, norm_g[l, 0], ffn_w_gu[l, 0].astype(BF16), ffn_w_down[l, 0].astype(BF16))
        pa, pb, pc, pd = _inproj(tok, x, mod, l, norm_g[l, 1], _pad_in_weights(w_in[l]))

        oa_c, s_ctx = _hgrn(l, bc, sc, 0, tok.t, pa, hgrn_lb_logits, hgrn_onorm[l], None)
        oa_l, _ = _hgrn(l, bl, sl, tok.tc, tok.t, pa, hgrn_lb_logits, hgrn_onorm[l], state_hgrn[:, l])

        mla_w = _mla_weights(mla_w_uq[l], mla_w_ukv[l], mla_qk_norm[l])
        q_b, k_b, v_b, cache_b = _mla_prep(tok, pb, mla_q_norm[l], mla_kv_norm[l], mla_w, mla_tables)
        cache_rows = jnp.pad(cache_mla[:, l].reshape(bl * past, KV_LORA + ROPE_B),
                             ((0, 0), (0, 2 * LANES - KV_LORA - ROPE_B)))
        kc_b, vc_b = _mla_cache_prep(cache_rows, mla_w)
        ob_c = _mla_attention(bc, sc, 0, tok.t, q_b, k_b, v_b)
        ob_l = _mla_attention(bl, sl, tok.tc, tok.t, q_b, k_b, v_b, kc_b, vc_b)

        oc_c = _hy_conv_short(bc, sc, 0, tok.t, pc, hy_short[l], hy_bias[l], spec_ctx[l])
        oc_l = _hy_conv_long(bl, sl, tok.tc, tok.t, pc, hy_short[l], hy_bias[l], spec_lat[l])

        q_d, k_d, v_d, kcache_d = _diff_prep(tok, pd, diff_qk_norm[l], diff_tables)
        kc_d, vc_d = _diff_cache_prep(cache_diff_k[:, l].reshape(bl * past, H_D * 2 * DH_D),
                                      cache_diff_v[:, l].reshape(bl * past, H_D * DV_D))
        od_c = _diff_attention(l, bc, sc, 0, tok.t, diff_lambda[l], diff_subln[l], q_d, k_d, v_d)
        od_l = _diff_attention(l, bl, sl, tok.tc, tok.t, diff_lambda[l], diff_subln[l], q_d, k_d, v_d, kc_d, vc_d)

        stitch = lambda a, b: jnp.concatenate([a[: tok.tc], b[tok.tc :]], axis=0)
        outs = [stitch(oa_c, oa_l), stitch(ob_c, ob_l), stitch(oc_c, oc_l), stitch(od_c, od_l)]
        x = _outproj(tok, x, mod, l, outs, w_out[l].astype(BF16))
        x = _ffn(tok, x, mod, l, 1, norm_g[l, 2], ffn_w_gu[l, 1].astype(BF16), ffn_w_down[l, 1].astype(BF16))

        new_mla.append(cache_b[: tok.tc, : KV_LORA + ROPE_B].reshape(bc, sc, KV_LORA + ROPE_B))
        new_dk.append(kcache_d[: tok.tc].reshape(bc, sc, H_D, 2, DH_D))
        new_dv.append(pd[: tok.tc, 2 * H_D * 2 * DH_D :].reshape(bc, sc, H_D, DV_D))
        new_state.append(s_ctx)

    y_prompt = x[: tok.tc].reshape(bc, sc, D_MODEL)
    y_sample = x[tok.tc :].reshape(bl, sl, D_MODEL)
    return (y_prompt, y_sample, jnp.stack(new_mla, axis=1), jnp.stack(new_dk, axis=1), jnp.stack(new_dv, axis=1),
            jnp.stack(new_state, axis=1))
```

```python
import functools
import math

import jax
import jax.numpy as jnp
import numpy as np
from jax import lax
from jax.experimental import pallas as pl
from jax.experimental.pallas import tpu as pltpu

F32 = jnp.float32
BF16 = jnp.bfloat16
HIGHEST = lax.Precision.HIGHEST

D_MODEL = 1024
DEPTH = 4
GRID_W = 64
N_MOD = 9
D_FF = 2816
EPS = 1e-6
ROPE_BASE = 10000.0
GROUP_W = 256
H_A, DK_A, DV_A = 4, 64, 64
MAX_INPUT_KEY = 1.0 - 1e-6
H_B, NOPE_B, ROPE_B, V_B = 4, 64, 32, 64
Q_LORA, KV_LORA = 256, 128
W_C, HY_ORDER, HY_BANDS, HY_FH, SHORT_K = 256, 2, 8, 64, 3
HY_EMB = 1 + 2 * HY_BANDS
H_D, DV_D, DH_D = 4, 64, 32

LANES = 128
MOD_ROWS = 16
TOKEN_TILE = 512
Q_TILE = 256
HGRN_BLOCK = 128
HGRN_CHUNK = 16
VMEM_LIMIT = 56 * 1024 * 1024

PA_W, PB_W, PC_W, PD_W = 1280, 512, 768, 768
IN_PAD_W = PA_W + PB_W + PC_W + PD_W
MLA_IN = Q_LORA + KV_LORA + ROPE_B
HEAD_PAD = 128


def _cparams(n_axes):
    return pltpu.CompilerParams(dimension_semantics=("arbitrary",) * n_axes, vmem_limit_bytes=VMEM_LIMIT)


def _nt_dot(a, b):
    return lax.dot_general(a, b, (((1,), (1,)), ((), ())), preferred_element_type=F32)


def _dot(a, b):
    return jnp.dot(a, b, preferred_element_type=F32)


def _dot_exact_rhs(a, b_bf16, passes=3):
    out = None
    rem = a
    for _ in range(passes):
        piece = rem.astype(BF16)
        term = _dot(piece, b_bf16)
        out = term if out is None else out + term
        rem = rem - piece.astype(F32)
    return out


def _dot_exact_lhs(a_bf16, b, passes=3):
    out = None
    rem = b
    for _ in range(passes):
        piece = rem.astype(BF16)
        term = _dot(a_bf16, piece)
        out = term if out is None else out + term
        rem = rem - piece.astype(F32)
    return out


def _silu(x):
    return x * jax.nn.sigmoid(x)


def _adaln(x, g, sc, sh):
    y = x * lax.rsqrt(jnp.mean(x * x, axis=-1, keepdims=True) + EPS)
    return (y * g) * (1.0 + sc) + sh


def _rope(x, cos, sin_hi, sin_lo):
    w = x.shape[-1]
    return x * cos + pltpu.roll(x, 8, 1) * sin_hi + pltpu.roll(x, w - 8, 1) * sin_lo


def _mod_kernel(c_ref, w_ref, b_ref, o_ref):
    a = _silu(c_ref[...])
    o_ref[0] = jnp.dot(a, w_ref[0], preferred_element_type=F32, precision=HIGHEST) + b_ref[0]


def _modulation(cond, w_mod, b_mod):
    tn = D_MODEL
    n_col = N_MOD * D_MODEL
    out = pl.pallas_call(
        _mod_kernel,
        grid=(DEPTH, n_col // tn),
        in_specs=[
            pl.BlockSpec((MOD_ROWS, D_MODEL), lambda l, j: (0, 0)),
            pl.BlockSpec((1, D_MODEL, tn), lambda l, j: (l, 0, j)),
            pl.BlockSpec((1, 1, tn), lambda l, j: (l, 0, j)),
        ],
        out_specs=pl.BlockSpec((1, MOD_ROWS, tn), lambda l, j: (l, 0, j)),
        out_shape=jax.ShapeDtypeStruct((DEPTH, MOD_ROWS, n_col), F32),
        compiler_params=_cparams(2),
        name="modulation",
    )(cond, w_mod, b_mod.reshape(DEPTH, 1, n_col))
    return out.reshape(DEPTH * MOD_ROWS, N_MOD, D_MODEL)


class _Tokens:
    def __init__(self, n_ctx_seq, ctx_len, n_lat_seq, lat_len):
        self.bc, self.sc, self.bl, self.sl = n_ctx_seq, ctx_len, n_lat_seq, lat_len
        self.tc, self.tl = n_ctx_seq * ctx_len, n_lat_seq * lat_len
        self.t = self.tc + self.tl
        self.tm = min(TOKEN_TILE, lat_len)
        assert self.tc % self.tm == 0 and lat_len % self.tm == 0
        self.ctx_tiles = self.tc // self.tm
        self.lat_tiles_per_seq = lat_len // self.tm
        self.n_tiles = self.t // self.tm

    def mod_row(self, layer):
        def f(i):
            lat = 1 + (i - self.ctx_tiles) // self.lat_tiles_per_seq
            return layer * MOD_ROWS + jnp.where(i < self.ctx_tiles, 0, lat)

        return f

    def rope_block(self, i):
        return jnp.where(i < self.ctx_tiles, 0, 1 + (i - self.ctx_tiles) % self.lat_tiles_per_seq)


def _mod_spec(tok, layer):
    row = tok.mod_row(layer)
    return pl.BlockSpec((1, N_MOD, D_MODEL), lambda i: (row(i), 0, 0))


def _const_spec(shape):
    zeros = (0,) * len(shape)
    return pl.BlockSpec(shape, lambda *_: zeros)


def _row_spec(tm, width):
    return pl.BlockSpec((tm, width), lambda i: (i, 0))


def _keep_rows_of(prev, kernel_fn, in_specs, args):
    if prev is None:
        return kernel_fn, in_specs, args, {}
    idx = len(in_specs)

    def body(*refs):
        return kernel_fn(*refs[:idx], *refs[idx + 1 :])

    return body, in_specs + [pl.BlockSpec(memory_space=pl.ANY)], args + [prev], {idx: 0}


def _ffn_kernel(x_ref, mod_ref, g_ref, wgu_ref, wd_ref, o_ref, *, mod_base):
    x = x_ref[...]
    sh = mod_ref[0, mod_base : mod_base + 1, :]
    sc = mod_ref[0, mod_base + 1 : mod_base + 2, :]
    gate = mod_ref[0, mod_base + 2 : mod_base + 3, :]
    h = _adaln(x, g_ref[...], sc, sh).astype(BF16)
    au = _dot(h, wgu_ref[...])
    act = (_silu(au[:, :D_FF]) * au[:, D_FF:]).astype(BF16)
    o_ref[...] = x + (0.5 * gate) * _dot(act, wd_ref[...])


def _ffn(tok, x, mod, layer, which, norm_g, w_gu, w_down):
    tm = tok.tm
    return pl.pallas_call(
        functools.partial(_ffn_kernel, mod_base=6 * which),
        grid=(tok.n_tiles,),
        in_specs=[
            _row_spec(tm, D_MODEL),
            _mod_spec(tok, layer),
            _const_spec((1, D_MODEL)),
            pl.BlockSpec((D_MODEL, 2 * D_FF), lambda i: (0, 0), pipeline_mode=pl.Buffered(1)),
            pl.BlockSpec((D_FF, D_MODEL), lambda i: (0, 0), pipeline_mode=pl.Buffered(1)),
        ],
        out_specs=_row_spec(tm, D_MODEL),
        out_shape=jax.ShapeDtypeStruct((tok.t, D_MODEL), F32),
        compiler_params=_cparams(1),
        name="ffn",
    )(x, mod, norm_g.reshape(1, D_MODEL), w_gu, w_down)


def _inproj_kernel(x_ref, mod_ref, g_ref, w_ref, pa_ref, pb_ref, pc_ref, pd_ref):
    h = _adaln(x_ref[...], g_ref[...], mod_ref[0, 4:5, :], mod_ref[0, 3:4, :]).astype(BF16)
    p = _dot(h, w_ref[...])
    pa_ref[...] = p[:, :PA_W]
    pb_ref[...] = p[:, PA_W : PA_W + PB_W]
    pc_ref[...] = p[:, PA_W + PB_W : PA_W + PB_W + PC_W]
    pd_ref[...] = p[:, PA_W + PB_W + PC_W :]


def _inproj(tok, x, mod, layer, norm_g, w_in_pad):
    tm = tok.tm
    widths = (PA_W, PB_W, PC_W, PD_W)
    return pl.pallas_call(
        _inproj_kernel,
        grid=(tok.n_tiles,),
        in_specs=[
            _row_spec(tm, D_MODEL),
            _mod_spec(tok, layer),
            _const_spec((1, D_MODEL)),
            pl.BlockSpec((D_MODEL, IN_PAD_W), lambda i: (0, 0), pipeline_mode=pl.Buffered(1)),
        ],
        out_specs=[_row_spec(tm, w) for w in widths],
        out_shape=[jax.ShapeDtypeStruct((tok.t, w), F32) for w in widths],
        compiler_params=_cparams(1),
        name="inproj",
    )(x, mod, norm_g.reshape(1, D_MODEL), w_in_pad)


def _outproj_kernel(x_ref, mod_ref, oa_ref, ob_ref, oc_ref, od_ref, w_ref, o_ref):
    o = jnp.concatenate([oa_ref[...], ob_ref[...], oc_ref[...], od_ref[...]], axis=-1)
    o_ref[...] = x_ref[...] + mod_ref[0, 5:6, :] * _dot(o, w_ref[...])


def _outproj(tok, x, mod, layer, outs, w_out):
    tm = tok.tm
    return pl.pallas_call(
        _outproj_kernel,
        grid=(tok.n_tiles,),
        in_specs=[_row_spec(tm, D_MODEL), _mod_spec(tok, layer)]
        + [_row_spec(tm, GROUP_W)] * 4
        + [_const_spec((D_MODEL, D_MODEL))],
        out_specs=_row_spec(tm, D_MODEL),
        out_shape=jax.ShapeDtypeStruct((tok.t, D_MODEL), F32),
        compiler_params=_cparams(1),
        name="outproj",
    )(x, mod, *outs, w_out)


def _rope_group_tables(n_tok):
    t = np.arange(n_tok)
    pos = np.stack([t // GRID_W, t % GRID_W], axis=1).astype(np.float32)
    inv = (ROPE_BASE ** (-np.arange(8, dtype=np.float32) / 8)).astype(np.float32)
    lane = np.arange(32)
    ang = (pos[:, lane // 16] * inv[lane % 8][None, :]).astype(np.float32)
    second = (lane % 16) >= 8
    cos, sin = np.cos(ang), np.sin(ang)
    return cos, np.where(second[None], sin, 0.0), np.where(second[None], 0.0, -sin)


def _rope_tables(tok, lane_groups, width):
    cos = np.ones((tok.tm + tok.sl, width), np.float32)
    s_hi = np.zeros_like(cos)
    s_lo = np.zeros_like(cos)
    c, a, b = _rope_group_tables(tok.sl)
    for g in lane_groups:
        cos[tok.tm :, g : g + 32] = c
        s_hi[tok.tm :, g : g + 32] = a
        s_lo[tok.tm :, g : g + 32] = b
    return jnp.asarray(cos), jnp.asarray(s_hi), jnp.asarray(s_lo)


def _mla_keys_values(kv_in, wk_ref, wv_ref, kn_ref, rope):
    kraw = _dot(kv_in.astype(BF16), wk_ref[...])
    v = _dot(kv_in[:, :KV_LORA].astype(BF16), wv_ref[...])
    ks = []
    for h in range(H_B):
        kh = kraw[:, h * HEAD_PAD : (h + 1) * HEAD_PAD]
        ss = jnp.sum(kh * kh, axis=-1, keepdims=True) * (1.0 / (NOPE_B + ROPE_B))
        kh = kh * lax.rsqrt(ss + EPS) * kn_ref[...]
        if rope is not None:
            kh = _rope(kh, *rope)
        ks.append(kh.astype(BF16))
    return ks, v.astype(BF16)


def _mla_prep_kernel(pb_ref, gq_ref, gkv_ref, wuq_ref, wk_ref, wv_ref, qn_ref, kn_ref, cos_ref, shi_ref, slo_ref,
                     q_ref, k_ref, v_ref, cache_ref):
    pb = pb_ref[...]
    rope = (cos_ref[...], shi_ref[...], slo_ref[...])
    cq = pb[:, :Q_LORA]
    cq = cq * lax.rsqrt(jnp.mean(cq * cq, axis=-1, keepdims=True) + EPS) * gq_ref[...]
    qraw = _dot(cq.astype(BF16), wuq_ref[...])
    scale = (NOPE_B + ROPE_B) ** -0.5 * LOG2E
    for h in range(H_B):
        qh = qraw[:, h * HEAD_PAD : (h + 1) * HEAD_PAD]
        ss = jnp.sum(qh * qh, axis=-1, keepdims=True) * (1.0 / (NOPE_B + ROPE_B))
        qh = _rope(qh * lax.rsqrt(ss + EPS) * qn_ref[...], *rope)
        q_ref[:, h * HEAD_PAD : (h + 1) * HEAD_PAD] = (qh * scale).astype(BF16)
    ckv = pb[:, Q_LORA : Q_LORA + KV_LORA]
    ckv = ckv * lax.rsqrt(jnp.mean(ckv * ckv, axis=-1, keepdims=True) + EPS) * gkv_ref[...]
    kv_in = jnp.concatenate([ckv, pb[:, Q_LORA + KV_LORA :]], axis=-1)
    cache_ref[...] = kv_in
    ks, v = _mla_keys_values(kv_in, wk_ref, wv_ref, kn_ref, rope)
    for h in range(H_B):
        k_ref[:, h * HEAD_PAD : (h + 1) * HEAD_PAD] = ks[h]
    v_ref[...] = v


def _mla_cache_kernel(c_ref, wk_ref, wv_ref, kn_ref, k_ref, v_ref):
    ks, v = _mla_keys_values(c_ref[...], wk_ref, wv_ref, kn_ref, None)
    for h in range(H_B):
        k_ref[:, h * HEAD_PAD : (h + 1) * HEAD_PAD] = ks[h]
    v_ref[...] = v


def _mla_weights(w_uq, w_ukv, qk_norm):
    wuq = jnp.pad(w_uq.reshape(Q_LORA, H_B, NOPE_B + ROPE_B), ((0, 0), (0, 0), (0, HEAD_PAD - NOPE_B - ROPE_B)))
    wuq = wuq.reshape(Q_LORA, H_B * HEAD_PAD).astype(BF16)
    ukv = w_ukv.reshape(KV_LORA, H_B, NOPE_B + V_B)
    wk_nope = jnp.pad(ukv[:, :, :NOPE_B], ((0, 0), (0, 0), (0, HEAD_PAD - NOPE_B)))
    place = np.zeros((2 * LANES - KV_LORA, H_B, HEAD_PAD), np.float32)
    for h in range(H_B):
        place[np.arange(ROPE_B), h, NOPE_B + np.arange(ROPE_B)] = 1.0
    wk = jnp.concatenate([wk_nope, jnp.asarray(place)], axis=0).reshape(2 * LANES, H_B * HEAD_PAD).astype(BF16)
    wv = ukv[:, :, NOPE_B:].reshape(KV_LORA, H_B * V_B).astype(BF16)
    pad = lambda g: jnp.pad(g, (0, HEAD_PAD - NOPE_B - ROPE_B)).reshape(1, HEAD_PAD)
    return wuq, wk, wv, pad(qk_norm[0]), pad(qk_norm[1])


def _mla_prep(tok, pb, gq, gkv, weights, tables):
    wuq, wk, wv, qn, kn = weights
    tm = tok.tm
    rope_spec = pl.BlockSpec((tm, HEAD_PAD), lambda i: (tok.rope_block(i), 0))
    return pl.pallas_call(
        _mla_prep_kernel,
        grid=(tok.n_tiles,),
        in_specs=[
            _row_spec(tm, PB_W),
            _const_spec((1, Q_LORA)),
            _const_spec((1, KV_LORA)),
            _const_spec(wuq.shape),
            _const_spec(wk.shape),
            _const_spec(wv.shape),
            _const_spec((1, HEAD_PAD)),
            _const_spec((1, HEAD_PAD)),
            rope_spec,
            rope_spec,
            rope_spec,
        ],
        out_specs=[_row_spec(tm, H_B * HEAD_PAD), _row_spec(tm, H_B * HEAD_PAD), _row_spec(tm, H_B * V_B),
                   _row_spec(tm, 2 * LANES)],
        out_shape=[
            jax.ShapeDtypeStruct((tok.t, H_B * HEAD_PAD), BF16),
            jax.ShapeDtypeStruct((tok.t, H_B * HEAD_PAD), BF16),
            jax.ShapeDtypeStruct((tok.t, H_B * V_B), BF16),
            jax.ShapeDtypeStruct((tok.t, 2 * LANES), F32),
        ],
        compiler_params=_cparams(1),
        name="mla_prep",
    )(pb, gq.reshape(1, -1), gkv.reshape(1, -1), wuq, wk, wv, qn, kn, *tables)


def _mla_cache_prep(cache_rows, weights):
    _, wk, wv, _, kn = weights
    rows = cache_rows.shape[0]
    tm = min(TOKEN_TILE, rows)
    return pl.pallas_call(
        _mla_cache_kernel,
        grid=(rows // tm,),
        in_specs=[_row_spec(tm, 2 * LANES), _const_spec(wk.shape), _const_spec(wv.shape), _const_spec((1, HEAD_PAD))],
        out_specs=[_row_spec(tm, H_B * HEAD_PAD), _row_spec(tm, H_B * V_B)],
        out_shape=[jax.ShapeDtypeStruct((rows, H_B * HEAD_PAD), BF16), jax.ShapeDtypeStruct((rows, H_B * V_B), BF16)],
        compiler_params=_cparams(1),
        name="mla_cache_prep",
    )(cache_rows, wk, wv, kn)


LOG2E = math.log2(math.e)


def _softmax_parts(scores):
    m = functools.reduce(jnp.maximum, [jnp.max(s, axis=-1, keepdims=True) for s in scores])
    ps = [jnp.exp2(s - m) for s in scores]
    denom = functools.reduce(jnp.add, [jnp.sum(p, axis=-1, keepdims=True) for p in ps])
    return ps, denom


def _head_lane_mask(width, head, head_w):
    lane = lax.broadcasted_iota(jnp.int32, (1, width), 1)
    return (lane >= head * head_w) & (lane < (head + 1) * head_w)


def _mla_attn_kernel(*refs, n_kv):
    q_ref = refs[0]
    k_refs = refs[1 : 1 + n_kv]
    v_refs = refs[1 + n_kv : 1 + 2 * n_kv]
    o_ref = refs[1 + 2 * n_kv]
    out = None
    for h in range(H_B):
        sl = slice(h * HEAD_PAD, (h + 1) * HEAD_PAD)
        qh = q_ref[:, sl]
        ps, denom = _softmax_parts([_nt_dot(qh, k[:, sl]) for k in k_refs])
        mask = _head_lane_mask(H_B * V_B, h, V_B)
        acc = None
        for p, v in zip(ps, v_refs):
            term = _dot(p.astype(BF16), jnp.where(mask, v[...], jnp.zeros((), BF16)))
            acc = term if acc is None else acc + term
        acc = acc * (1.0 / denom)
        out = acc if out is None else out + acc
    o_ref[...] = out.astype(BF16)


def _seq_specs(n_seq, seq_len, row0, tq, q_width, kv_widths, kv_new, kv_cache):
    q_blocks = seq_len // tq
    q_spec = lambda w: pl.BlockSpec((tq, w), lambda b, i: (row0 // tq + b * q_blocks + i, 0))
    new_spec = lambda w: pl.BlockSpec((seq_len, w), lambda b, i: (row0 // seq_len + b, 0))
    specs = [q_spec(q_width)]
    for w in kv_widths:
        if kv_cache is not None:
            specs.append(pl.BlockSpec((kv_cache, w), lambda b, i: (b, 0)))
        specs.append(new_spec(w))
    return specs, q_spec


def _mla_attention(n_seq, seq_len, row0, total_rows, q, k, v, k_cache=None, v_cache=None, prev=None):
    tq = min(Q_TILE, seq_len)
    has_cache = k_cache is not None
    cache_len = k_cache.shape[0] // n_seq if has_cache else None
    specs, q_spec = _seq_specs(n_seq, seq_len, row0, tq, H_B * HEAD_PAD, (H_B * HEAD_PAD, H_B * V_B), True, cache_len)
    args = [q] + ([k_cache, k] if has_cache else [k]) + ([v_cache, v] if has_cache else [v])
    body = functools.partial(_mla_attn_kernel, n_kv=2 if has_cache else 1)
    body, specs, args, aliases = _keep_rows_of(prev, body, specs, args)
    return pl.pallas_call(
        body,
        grid=(n_seq, seq_len // tq),
        in_specs=specs,
        out_specs=q_spec(H_B * V_B),
        out_shape=jax.ShapeDtypeStruct((total_rows, H_B * V_B), BF16),
        input_output_aliases=aliases,
        compiler_params=_cparams(2),
        name="mla_attention",
    )(*args)


DQK_W = H_D * 2 * HEAD_PAD


def _diff_place_matrix():
    m = np.zeros((H_D * 2 * DH_D, DQK_W), np.float32)
    g = np.arange(H_D * 2 * DH_D)
    m[g, (g // DH_D) * HEAD_PAD + g % DH_D] = 1.0
    return jnp.asarray(m, BF16)


def _group_mean_matrix(width, group):
    g = np.arange(width)
    return jnp.asarray((g[:, None] // group == g[None, :] // group).astype(np.float32), BF16)


def _diff_prep_kernel(pd_ref, gq_ref, gk_ref, gm_ref, place_ref, cos_ref, shi_ref, slo_ref,
                      q_ref, k_ref, v_ref, kcache_ref):
    pd = pd_ref[...]
    rope = (cos_ref[...], shi_ref[...], slo_ref[...])
    w = H_D * 2 * DH_D

    def norm(x, g):
        ms = _dot_exact_rhs(x * x, gm_ref[...], passes=2) * (1.0 / DH_D)
        return x * lax.rsqrt(ms + EPS) * g

    qn = _rope(norm(pd[:, :w], gq_ref[...]), *rope) * (DH_D ** -0.5 * LOG2E)
    kn = norm(pd[:, w : 2 * w], gk_ref[...])
    kcache_ref[...] = kn
    q_ref[...] = _dot(qn.astype(BF16), place_ref[...]).astype(BF16)
    k_ref[...] = _dot(_rope(kn, *rope).astype(BF16), place_ref[...]).astype(BF16)
    v_ref[...] = pd[:, 2 * w :].astype(BF16)


def _diff_cache_kernel(k_in_ref, v_in_ref, place_ref, k_ref, v_ref):
    k_ref[...] = _dot(k_in_ref[...].astype(BF16), place_ref[...]).astype(BF16)
    v_ref[...] = v_in_ref[...].astype(BF16)


def _diff_prep(tok, pd, qk_norm, tables):
    tm = tok.tm
    w = H_D * 2 * DH_D
    rope_spec = pl.BlockSpec((tm, w), lambda i: (tok.rope_block(i), 0))
    tile_g = lambda g: jnp.tile(g, H_D * 2).reshape(1, w)
    return pl.pallas_call(
        _diff_prep_kernel,
        grid=(tok.n_tiles,),
        in_specs=[_row_spec(tm, PD_W), _const_spec((1, w)), _const_spec((1, w)), _const_spec((w, w)),
                  _const_spec((w, DQK_W)), rope_spec, rope_spec, rope_spec],
        out_specs=[_row_spec(tm, DQK_W), _row_spec(tm, DQK_W), _row_spec(tm, H_D * DV_D), _row_spec(tm, w)],
        out_shape=[
            jax.ShapeDtypeStruct((tok.t, DQK_W), BF16),
            jax.ShapeDtypeStruct((tok.t, DQK_W), BF16),
            jax.ShapeDtypeStruct((tok.t, H_D * DV_D), BF16),
            jax.ShapeDtypeStruct((tok.t, w), F32),
        ],
        compiler_params=_cparams(1),
        name="diff_prep",
    )(pd, tile_g(qk_norm[0]), tile_g(qk_norm[1]), _group_mean_matrix(w, DH_D), _diff_place_matrix(), *tables)


def _diff_cache_prep(k_rows, v_rows):
    rows = k_rows.shape[0]
    tm = min(TOKEN_TILE, rows)
    w = H_D * 2 * DH_D
    return pl.pallas_call(
        _diff_cache_kernel,
        grid=(rows // tm,),
        in_specs=[_row_spec(tm, w), _row_spec(tm, H_D * DV_D), _const_spec((w, DQK_W))],
        out_specs=[_row_spec(tm, DQK_W), _row_spec(tm, H_D * DV_D)],
        out_shape=[jax.ShapeDtypeStruct((rows, DQK_W), BF16), jax.ShapeDtypeStruct((rows, H_D * DV_D), BF16)],
        compiler_params=_cparams(1),
        name="diff_cache_prep",
    )(k_rows, v_rows, _diff_place_matrix())


def _diff_attn_kernel(*refs, n_kv, lam_init):
    q_ref, lam_ref, sub_ref = refs[0], refs[1], refs[2]
    k_refs = refs[3 : 3 + n_kv]
    v_refs = refs[3 + n_kv : 3 + 2 * n_kv]
    o_ref = refs[3 + 2 * n_kv]
    dl = lam_ref[...]
    lam = (jnp.exp(jnp.sum(dl[0:1] * dl[1:2], axis=-1, keepdims=True))
           - jnp.exp(jnp.sum(dl[2:3] * dl[3:4], axis=-1, keepdims=True)) + lam_init)
    out = None
    for h in range(H_D):
        probs = []
        for m in range(2):
            sl = slice((2 * h + m) * HEAD_PAD, (2 * h + m + 1) * HEAD_PAD)
            qh = q_ref[:, sl]
            ps, denom = _softmax_parts([_nt_dot(qh, k[:, sl]) for k in k_refs])
            probs.append((ps, (1.0 if m == 0 else lam) / denom))
        mask = _head_lane_mask(H_D * DV_D, h, DV_D)
        (ps0, a0), (ps1, a1) = probs
        acc = None
        for p0, p1, v in zip(ps0, ps1, v_refs):
            wgt = (p0 * a0 - p1 * a1).astype(BF16)
            term = _dot(wgt, jnp.where(mask, v[...], jnp.zeros((), BF16)))
            acc = term if acc is None else acc + term
        ms = jnp.sum(acc * acc, axis=-1, keepdims=True) * (1.0 / DV_D)
        acc = acc * lax.rsqrt(ms + EPS)
        out = acc if out is None else out + acc
    o_ref[...] = (out * sub_ref[...] * (1.0 - lam_init)).astype(BF16)


def _diff_attention(layer, n_seq, seq_len, row0, total_rows, lam_p, sub_g, q, k, v, k_cache=None, v_cache=None,
                    prev=None):
    tq = min(Q_TILE, seq_len)
    has_cache = k_cache is not None
    cache_len = k_cache.shape[0] // n_seq if has_cache else None
    specs, q_spec = _seq_specs(n_seq, seq_len, row0, tq, DQK_W, (DQK_W, H_D * DV_D), True, cache_len)
    specs = [specs[0], pl.BlockSpec((4, DH_D), lambda b, i: (0, 0)), pl.BlockSpec((1, H_D * DV_D), lambda b, i: (0, 0))
             ] + specs[1:]
    args = [q, lam_p, jnp.tile(sub_g, H_D).reshape(1, H_D * DV_D)]
    args += ([k_cache, k] if has_cache else [k]) + ([v_cache, v] if has_cache else [v])
    lam_init = 0.8 - 0.6 * math.exp(-0.3 * layer)
    body = functools.partial(_diff_attn_kernel, n_kv=2 if has_cache else 1, lam_init=lam_init)
    body, specs, args, aliases = _keep_rows_of(prev, body, specs, args)
    return pl.pallas_call(
        body,
        grid=(n_seq, seq_len // tq),
        in_specs=specs,
        out_specs=q_spec(H_D * DV_D),
        out_shape=jax.ShapeDtypeStruct((total_rows, H_D * DV_D), BF16),
        input_output_aliases=aliases,
        compiler_params=_cparams(2),
        name="diff_attention",
    )(*args)


A_W = H_A * DK_A


def _hgrn_constants():
    t = np.arange(HGRN_BLOCK)
    same = (t[:, None] // HGRN_CHUNK) == (t[None, :] // HGRN_CHUNK)
    lower = same & (t[None, :] <= t[:, None])
    upper = same & (t[None, :] >= t[:, None])
    g = np.arange(A_W)
    heads = (g[:, None] // DK_A) == (g[None, :] // DK_A)
    as_bf16 = lambda m: jnp.asarray(m.astype(np.float32), BF16)
    pair_mask = heads[: 2 * DK_A, : 2 * DK_A].astype(np.float32)
    return as_bf16(lower), as_bf16(upper), as_bf16(heads), jnp.asarray(pair_mask)


def _hgrn_kernel(*refs, layer, seq_len, has_state):
    if has_state:
        pa_ref, lbl_ref, og_ref, s0_ref, lower_ref, upper_ref, heads_ref, hmask_ref = refs[:8]
        rest = refs[8:]
    else:
        pa_ref, lbl_ref, og_ref, lower_ref, upper_ref, heads_ref, hmask_ref = refs[:7]
        s0_ref = None
        rest = refs[7:]
    o_ref, sout_ref, st_ref, oacc_ref = rest
    n_blocks = seq_len // HGRN_BLOCK
    n_chunks = HGRN_BLOCK // HGRN_CHUNK
    row_in_chunk = lax.broadcasted_iota(jnp.int32, (HGRN_BLOCK, 1), 0) % HGRN_CHUNK

    for d in range(2):
        logits = lbl_ref[d]
        e = jnp.exp(logits - jnp.max(logits, axis=0, keepdims=True))
        p = e / jnp.sum(e, axis=0, keepdims=True)
        lb = jnp.zeros((1, A_W), F32)
        for j in range(1, layer + 1):
            lb = lb + p[j : j + 1, :]
        tri_ref = lower_ref if d == 0 else upper_ref
        st_ref[...] = jnp.zeros(st_ref.shape, F32)
        if has_state:
            for h in range(H_A):
                off = (h % 2) * DK_A
                st_ref[h // 2, off : off + DV_A, off : off + DK_A] = s0_ref[0, d, h].T

        def block(i, carry, d=d, lb=lb, tri_ref=tri_ref):
            blk = i if d == 0 else n_blocks - 1 - i
            r0 = pl.multiple_of(blk * HGRN_BLOCK, HGRN_BLOCK)
            rows = pl.ds(r0, HGRN_BLOCK)
            q = _silu(pa_ref[rows, 0:A_W]) * DK_A ** -0.5
            v = pa_ref[rows, A_W : 2 * A_W]
            logit = pa_ref[rows, (2 + d) * A_W : (3 + d) * A_W]
            key = jnp.minimum((1.0 - lb) * jax.nn.sigmoid(-logit), MAX_INPUT_KEY)
            log_f = jnp.log1p(-key)
            cum = _dot_exact_lhs(tri_ref[...], log_f)
            cum3 = cum.reshape(n_chunks, HGRN_CHUNK, A_W)
            last = cum3[:, HGRN_CHUNK - 1 : HGRN_CHUNK, :] if d == 0 else cum3[:, 0:1, :]
            tot = jnp.broadcast_to(last, cum3.shape).reshape(HGRN_BLOCK, A_W)
            o = None
            for r in range(HGRN_CHUNK):
                shift = r if d == 0 else (HGRN_BLOCK - r) % HGRN_BLOCK
                if r == 0:
                    pair = q * key
                    vs = v
                else:
                    valid = (row_in_chunk >= r) if d == 0 else (row_in_chunk < HGRN_CHUNK - r)
                    ks = pltpu.roll(key, shift, 0)
                    cs = pltpu.roll(cum, shift, 0)
                    vs = pltpu.roll(v, shift, 0)
                    pair = jnp.where(valid, q * ks * jnp.exp(cum - cs), 0.0)
                w = _dot(pair.astype(BF16), heads_ref[...])
                o = w * vs if o is None else o + w * vs
            q_dec = (q * jnp.exp(cum)).astype(BF16)
            k_dec = (key * jnp.exp(tot - cum)).astype(BF16)
            v_b = v.astype(BF16)
            inter = [None] * n_chunks
            for cc in range(n_chunks):
                c = cc if d == 0 else n_chunks - 1 - cc
                sl = slice(c * HGRN_CHUNK, (c + 1) * HGRN_CHUNK)
                decay = jnp.exp(tot[c * HGRN_CHUNK : c * HGRN_CHUNK + 1, :])
                parts = []
                for pair_i in range(H_A // 2):
                    lanes = slice(pair_i * 2 * DK_A, (pair_i + 1) * 2 * DK_A)
                    state = st_ref[pair_i]
                    parts.append(_nt_dot(q_dec[sl, lanes], state.astype(BF16)))
                    upd = lax.dot_general(v_b[sl, lanes], k_dec[sl, lanes], (((0,), (0,)), ((), ())),
                                          preferred_element_type=F32)
                    st_ref[pair_i] = state * decay[:, lanes] + upd * hmask_ref[...]
                inter[c] = jnp.concatenate(parts, axis=1)
            o = o + jnp.concatenate(inter, axis=0)
            if d == 0:
                oacc_ref[rows, :] = o
            else:
                oacc_ref[rows, :] += o
            return carry

        lax.fori_loop(0, n_blocks, block, 0)
        for h in range(H_A):
            off = (h % 2) * DK_A
            sout_ref[0, d, h] = st_ref[h // 2, off : off + DV_A, off : off + DK_A].T

    o = oacc_ref[...]
    ms = _dot_exact_rhs(o * o, heads_ref[...], passes=2) * (1.0 / DV_A)
    o_ref[...] = (o * lax.rsqrt(ms + EPS) * og_ref[...] * _silu(pa_ref[:, 4 * A_W : 5 * A_W])).astype(BF16)


def _hgrn(layer, n_seq, seq_len, row0, total_rows, pa, lb_logits, onorm_g, s0, prev=None):
    has_state = s0 is not None
    consts = _hgrn_constants()
    seq_spec = lambda w: pl.BlockSpec((seq_len, w), lambda b: (row0 // seq_len + b, 0))
    state_spec = pl.BlockSpec((1, 2, H_A, DK_A, DV_A), lambda b: (b, 0, 0, 0, 0))
    in_specs = [seq_spec(PA_W), _const_spec(lb_logits.shape), _const_spec((1, A_W))]
    args = [pa, lb_logits, jnp.tile(onorm_g, H_A).reshape(1, A_W)]
    if has_state:
        in_specs.append(state_spec)
        args.append(s0)
    in_specs += [_const_spec(c.shape) for c in consts]
    args += list(consts)
    body = functools.partial(_hgrn_kernel, layer=layer, seq_len=seq_len, has_state=has_state)
    body, in_specs, args, aliases = _keep_rows_of(prev, body, in_specs, args)
    return pl.pallas_call(
        body,
        grid=(n_seq,),
        in_specs=in_specs,
        out_specs=[seq_spec(A_W), state_spec],
        out_shape=[
            jax.ShapeDtypeStruct((total_rows, A_W), BF16),
            jax.ShapeDtypeStruct((n_seq, 2, H_A, DK_A, DV_A), F32),
        ],
        input_output_aliases=aliases,
        scratch_shapes=[pltpu.VMEM((H_A // 2, 2 * DV_A, 2 * DK_A), F32), pltpu.VMEM((seq_len, A_W), F32)],
        compiler_params=_cparams(1),
        name="hgrn",
    )(*args)


HY_W = HY_ORDER * W_C
FFT_R = 64


def _hy_filter_kernel(w1_ref, b1_ref, w2_ref, b2_ref, w3_ref, ld_ref, hf_ref, hb_ref, *, n):
    f32dot = functools.partial(jnp.dot, preferred_element_type=F32, precision=HIGHEST)
    row = lax.broadcasted_iota(jnp.int32, (n, LANES), 0)
    lane = lax.broadcasted_iota(jnp.int32, (n, LANES), 1)
    tn = row.astype(F32) / n
    band = jnp.where(lane <= HY_BANDS, lane, lane - HY_BANDS).astype(F32)
    ang = (2.0 * math.pi) * tn * band
    feats = jnp.where(lane == 0, tn, jnp.where(lane <= HY_BANDS, jnp.cos(ang),
                                                jnp.where(lane <= 2 * HY_BANDS, jnp.sin(ang), 0.0)))
    h = jnp.sin(f32dot(feats, w1_ref[0]) + b1_ref[0])
    h = jnp.sin(f32dot(h, w2_ref[0]) + b2_ref[0])
    h = f32dot(h, w3_ref[0])
    h = h * jnp.exp(-jnp.exp(ld_ref[0]) * tn[:, 0:1])
    hf = h[:, :HY_W]
    hb = jnp.where(row[:, 0:1] == 0, 0.0, h[:, HY_W:])
    norm = jnp.sum(jnp.abs(hf), axis=0, keepdims=True) + jnp.sum(jnp.abs(hb), axis=0, keepdims=True) + EPS
    hf_ref[0] = hf / norm
    hb_ref[0] = hb / norm


def _hy_filters(n, w1, b1, w2, b2, w3, log_decay):
    w1p = jnp.pad(w1, ((0, 0), (0, LANES - HY_EMB), (0, 0)))
    lay = lambda shape: pl.BlockSpec((1,) + shape, lambda l: (l,) + (0,) * len(shape))
    return pl.pallas_call(
        functools.partial(_hy_filter_kernel, n=n),
        grid=(DEPTH,),
        in_specs=[lay((LANES, HY_FH)), lay((1, HY_FH)), lay((HY_FH, HY_FH)), lay((1, HY_FH)),
                  lay((HY_FH, 2 * HY_W)), lay((1, 2 * HY_W))],
        out_specs=[lay((n, HY_W)), lay((n, HY_W))],
        out_shape=[jax.ShapeDtypeStruct((DEPTH, n, HY_W), F32)] * 2,
        compiler_params=_cparams(1),
        name="hyena_filters",
    )(w1p, b1.reshape(DEPTH, 1, HY_FH), w2, b2.reshape(DEPTH, 1, HY_FH), w3, log_decay.reshape(DEPTH, 1, 2 * HY_W))


def _dft_tables_short(n):
    big = 2 * n
    k = np.arange(big)[:, None]
    t = np.arange(n)[None, :]
    ang = 2.0 * np.pi * ((k * t) % big) / big
    fwd = np.concatenate([np.cos(ang), -np.sin(ang)], axis=0)
    inv = np.concatenate([np.cos(ang).T, -np.sin(ang).T], axis=1) / big
    return fwd.astype(np.float32), inv.astype(np.float32)


def _hy_spec_short_kernel(hf_ref, hb_ref, fwd_ref, f_ref, *, n):
    f32dot = functools.partial(jnp.dot, preferred_element_type=F32, precision=HIGHEST)
    xf = f32dot(fwd_ref[...], hf_ref[0])
    xb = f32dot(fwd_ref[...], hb_ref[0])
    big = 2 * n
    f_ref[0, :big, :] = xf[:big] + xb[:big]
    f_ref[0, big:, :] = xf[big:] - xb[big:]


def _hy_spec_short(n, hf, hb):
    fwd, _ = _dft_tables_short(n)
    lay = lambda shape: pl.BlockSpec((1,) + shape, lambda l: (l,) + (0,) * len(shape))
    return pl.pallas_call(
        functools.partial(_hy_spec_short_kernel, n=n),
        grid=(DEPTH,),
        in_specs=[lay((n, HY_W)), lay((n, HY_W)), _const_spec(fwd.shape)],
        out_specs=lay((4 * n, HY_W)),
        out_shape=jax.ShapeDtypeStruct((DEPTH, 4 * n, HY_W), F32),
        compiler_params=_cparams(1),
        name="hyena_spectrum_short",
    )(hf, hb, jnp.asarray(fwd))


def _short_conv(x, w, n):
    row = lax.broadcasted_iota(jnp.int32, (n, 1), 0)
    prev = jnp.where(row == 0, 0.0, pltpu.roll(x, 1, 0))
    nxt = jnp.where(row == n - 1, 0.0, pltpu.roll(x, n - 1, 0))
    return prev * w[0:1, :] + x * w[1:2, :] + nxt * w[2:3, :]


def _hy_conv_short_kernel(pc_ref, ws_ref, bias_ref, f_ref, fwd_ref, inv_ref, o_ref, *, n):
    big = 2 * n
    u = _short_conv(pc_ref[...], ws_ref[...], n)
    v, x1, x2 = u[:, :W_C], u[:, W_C : 2 * W_C], u[:, 2 * W_C :]

    def conv(x, order):
        spec = _dot(fwd_ref[...], x.astype(BF16))
        fr = f_ref[:big, order * W_C : (order + 1) * W_C]
        fi = f_ref[big:, order * W_C : (order + 1) * W_C]
        zr = spec[:big] * fr - spec[big:] * fi
        zi = spec[:big] * fi + spec[big:] * fr
        return _dot(inv_ref[...], jnp.concatenate([zr, zi], axis=0).astype(BF16))

    z = x1 * (conv(v, 0) + v * bias_ref[0:1, :])
    z = x2 * (conv(z, 1) + z * bias_ref[1:2, :])
    o_ref[...] = z.astype(BF16)


def _hy_conv_short(n_seq, n, row0, total_rows, pc, w_short, bias, spec):
    fwd, inv = _dft_tables_short(n)
    seq_spec = lambda w: pl.BlockSpec((n, w), lambda b: (row0 // n + b, 0))
    return pl.pallas_call(
        functools.partial(_hy_conv_short_kernel, n=n),
        grid=(n_seq,),
        in_specs=[seq_spec(PC_W), _const_spec((SHORT_K, PC_W)), _const_spec((HY_ORDER, W_C)),
                  _const_spec((4 * n, HY_W)), _const_spec(fwd.shape), _const_spec(inv.shape)],
        out_specs=seq_spec(W_C),
        out_shape=jax.ShapeDtypeStruct((total_rows, W_C), BF16),
        compiler_params=_cparams(1),
        name="hyena_conv_short",
    )(pc, w_short, bias, spec, jnp.asarray(fwd, BF16), jnp.asarray(inv, BF16))


def _dft_tables_long():
    r = FFT_R
    big = r * r
    half = r // 2
    n2 = np.arange(r)[:, None, None]
    k1 = np.arange(r)[None, :, None]
    n1 = np.arange(half)[None, None, :]
    ang = 2.0 * np.pi * ((k1 * (r * n1 + n2)) % big) / big
    first = np.concatenate([np.cos(ang), -np.sin(ang)], axis=1)
    last = np.concatenate([np.cos(ang), -np.sin(ang)], axis=1).transpose(0, 2, 1) / big
    a = np.arange(r)
    ang_r = 2.0 * np.pi * ((a[:, None] * a[None, :]) % r) / r
    c, s = np.cos(ang_r), np.sin(ang_r)
    mid = np.block([[c, s], [-s, c]])
    mid_inv = np.block([[c, -s], [s, c]])
    f32 = lambda m: m.astype(np.float32)
    return f32(first), f32(mid), f32(mid_inv), f32(last)


def _ld_rows(ref, rows):
    return jnp.concatenate([ref[j, rows, :] for j in range(ref.shape[0])], axis=-1)


def _st_rows(ref, rows, val):
    for j in range(ref.shape[0]):
        ref[j, rows, :] = val[:, j * LANES : (j + 1) * LANES]


def _lane_split_scratch(rows, width):
    return pltpu.VMEM((width // LANES, rows, LANES), F32)


FFT_UNROLL = 4


def _fft_long_forward(x_ref, y_ref, dot_first, dot_mid, out_fn):
    r = FFT_R

    def stage_a(n2, carry):
        res = dot_first(n2, _ld_rows(x_ref, pl.ds(n2, r // 2, stride=r)))
        _st_rows(y_ref, pl.ds(n2, r, stride=2 * r), res[:r])
        _st_rows(y_ref, pl.ds(r + n2, r, stride=2 * r), res[r:])
        return carry

    lax.fori_loop(0, r, stage_a, 0, unroll=FFT_UNROLL)

    def stage_c(k1, carry):
        base = pl.multiple_of(k1 * 2 * r, 2 * r)
        out_fn(k1, base, dot_mid(_ld_rows(y_ref, pl.ds(base, 2 * r))))
        return carry

    lax.fori_loop(0, r, stage_c, 0, unroll=FFT_UNROLL)


def _dot_split(t_hi, t_lo, x):
    x_hi = x.astype(BF16)
    x_lo = (x - x_hi.astype(F32)).astype(BF16)
    return _dot(t_hi, x_hi) + _dot(t_hi, x_lo) + _dot(t_lo, x_hi)


def _split_table(m):
    hi = jnp.asarray(m, BF16)
    lo = (jnp.asarray(m) - hi.astype(F32)).astype(BF16)
    return hi, lo


def _hy_spec_long_kernel(hf_ref, hb_ref, first_hi_ref, first_lo_ref, mid_hi_ref, mid_lo_ref, f_ref,
                         x_ref, y_ref, tmp_ref):
    r = FFT_R
    dot_first = lambda n2, slab: _dot_split(first_hi_ref[n2], first_lo_ref[n2], slab)
    dot_mid = lambda block: _dot_split(mid_hi_ref[...], mid_lo_ref[...], block)

    def write_fwd(k1, base, spec):
        tmp_ref[pl.ds(base, 2 * r), :] = spec

    _st_rows(x_ref, slice(None), hf_ref[0])
    _fft_long_forward(x_ref, y_ref, dot_first, dot_mid, write_fwd)

    def write_sum(k1, base, spec):
        prev = tmp_ref[pl.ds(base, 2 * r), :]
        f_ref[0, pl.ds(base, r), :] = prev[:r] + spec[:r]
        f_ref[0, pl.ds(base + r, r), :] = prev[r:] - spec[r:]

    _st_rows(x_ref, slice(None), hb_ref[0])
    _fft_long_forward(x_ref, y_ref, dot_first, dot_mid, write_sum)


def _hy_spec_long(n, hf, hb):
    first, mid, _, _ = _dft_tables_long()
    big = 2 * n
    lay = lambda rows: pl.BlockSpec((1, rows, W_C), lambda l, o: (l, 0, o))
    return pl.pallas_call(
        _hy_spec_long_kernel,
        grid=(DEPTH, HY_ORDER),
        in_specs=[lay(n), lay(n)] + [_const_spec(first.shape)] * 2 + [_const_spec(mid.shape)] * 2,
        out_specs=lay(2 * big),
        out_shape=jax.ShapeDtypeStruct((DEPTH, 2 * big, HY_W), F32),
        scratch_shapes=[_lane_split_scratch(n, W_C), _lane_split_scratch(2 * big, W_C),
                        pltpu.VMEM((2 * big, W_C), F32)],
        compiler_params=_cparams(2),
        name="hyena_spectrum_long",
    )(hf, hb, *_split_table(first), *_split_table(mid))


def _hy_conv_long_kernel(sig_ref, gate_ref, ws_ref, bias_ref, f_ref, first_ref, mid_ref, midinv_ref, last_ref, o_ref,
                         x_ref, y_ref, v_ref, out_ref, *, n, order):
    r = FFT_R
    to_bf16 = lambda x: x.astype(BF16)
    gate = _short_conv(gate_ref[...], ws_ref[:, (order + 1) * W_C : (order + 2) * W_C], n)
    sig = _short_conv(sig_ref[...], ws_ref[:, :W_C], n) if order == 0 else sig_ref[...]
    _st_rows(x_ref, slice(None), sig)

    def filter_and_invert(k1, base, spec):
        fr = f_ref[pl.ds(base, r), :]
        fi = f_ref[pl.ds(base + r, r), :]
        zr = spec[:r] * fr - spec[r:] * fi
        zi = spec[:r] * fi + spec[r:] * fr
        res = _dot(midinv_ref[...], jnp.concatenate([zr, zi], axis=0).astype(BF16))
        _st_rows(v_ref, pl.ds(k1, r, stride=2 * r), res[:r])
        _st_rows(v_ref, pl.ds(r + k1, r, stride=2 * r), res[r:])

    dot_first = lambda n2, slab: _dot(first_ref[n2], to_bf16(slab))
    dot_mid = lambda block: _dot(mid_ref[...], to_bf16(block))
    _fft_long_forward(x_ref, y_ref, dot_first, dot_mid, filter_and_invert)

    def stage_last(n2, carry):
        base = pl.multiple_of(n2 * 2 * r, 2 * r)
        res = _dot(last_ref[n2], _ld_rows(v_ref, pl.ds(base, 2 * r)).astype(BF16))
        _st_rows(out_ref, pl.ds(n2, r // 2, stride=r), res)
        return carry

    lax.fori_loop(0, r, stage_last, 0, unroll=FFT_UNROLL)
    z = gate * (_ld_rows(out_ref, slice(None)) + sig * bias_ref[order : order + 1, :])
    o_ref[...] = z.astype(o_ref.dtype)


def _hy_conv_long(n_seq, n, row0, total_rows, pc, w_short, bias, spec, prev=None):
    assert 2 * n == FFT_R * FFT_R
    first, mid, mid_inv, last = _dft_tables_long()
    big = 2 * n
    bf = lambda m: jnp.asarray(m, BF16)
    lane_block = lambda j: pl.BlockSpec((n, W_C), lambda b: (row0 // n + b, j))
    z = None
    for order in range(HY_ORDER):
        final = order + 1 == HY_ORDER
        in_specs = [lane_block(0), lane_block(order + 1), _const_spec((SHORT_K, PC_W)),
                    _const_spec((HY_ORDER, W_C)), pl.BlockSpec((2 * big, W_C), lambda b, order=order: (0, order)),
                    _const_spec(first.shape), _const_spec(mid.shape), _const_spec(mid_inv.shape),
                    _const_spec(last.shape)]
        args = [pc if order == 0 else z, pc, w_short, bias, spec, bf(first), bf(mid), bf(mid_inv), bf(last)]
        body = functools.partial(_hy_conv_long_kernel, n=n, order=order)
        body, in_specs, args, aliases = _keep_rows_of(prev if final else None, body, in_specs, args)
        z = pl.pallas_call(
            body,
            grid=(n_seq,),
            in_specs=in_specs,
            out_specs=lane_block(0),
            out_shape=jax.ShapeDtypeStruct((total_rows, W_C), BF16 if final else F32),
            input_output_aliases=aliases,
            scratch_shapes=[_lane_split_scratch(n, W_C), _lane_split_scratch(2 * big, W_C),
                            _lane_split_scratch(2 * big, W_C), _lane_split_scratch(n, W_C)],
            compiler_params=_cparams(1),
            name="hyena_conv_long",
        )(*args)
    return z


def _pad_in_weights(w_in):
    gap = jnp.zeros((D_MODEL, PB_W - MLA_IN), w_in.dtype)
    return jnp.concatenate([w_in[:, : PA_W + MLA_IN], gap, w_in[:, PA_W + MLA_IN :]], axis=1).astype(BF16)


def kernel(x_prompt, x_sample, c, cache_mla, cache_diff_k, cache_diff_v, state_hgrn, c_ctx, w_mod, b_mod, norm_g,
           ffn_w_gu, ffn_w_down, w_in, w_out, hgrn_lb_logits, hgrn_onorm, mla_q_norm, mla_kv_norm, mla_w_uq,
           mla_w_ukv, mla_qk_norm, hy_short, hy_w1, hy_b1, hy_w2, hy_b2, hy_w3, hy_log_decay, hy_bias,
           diff_qk_norm, diff_lambda, diff_subln):
    bc, sc, _ = x_prompt.shape
    bl, sl, _ = x_sample.shape
    past = cache_mla.shape[2]
    tok = _Tokens(bc, sc, bl, sl)
    assert bl + 1 <= MOD_ROWS and tok.tc % sl == 0

    x = jnp.concatenate([x_prompt.reshape(tok.tc, D_MODEL), x_sample.reshape(tok.tl, D_MODEL)], axis=0)
    cond = jnp.concatenate([c_ctx[None], c, jnp.zeros((MOD_ROWS - 1 - bl, D_MODEL), F32)], axis=0)
    mod = _modulation(cond, w_mod, b_mod)

    filters = {n: _hy_filters(n, hy_w1, hy_b1, hy_w2, hy_b2, hy_w3, hy_log_decay) for n in (sc, sl)}
    spec_ctx = _hy_spec_short(sc, *filters[sc])
    spec_lat = _hy_spec_long(sl, *filters[sl])

    mla_tables = _rope_tables(tok, [NOPE_B], HEAD_PAD)
    diff_tables = _rope_tables(tok, list(range(0, H_D * 2 * DH_D, DH_D)), H_D * 2 * DH_D)

    new_mla, new_dk, new_dv, new_state = [], [], [], []
    for l in range(DEPTH):
        x = _ffn(tok, x, mod, l, 0, norm_g[l, 0], ffn_w_gu[l, 0].astype(BF16), ffn_w_down[l, 0].astype(BF16))
        pa, pb, pc, pd = _inproj(tok, x, mod, l, norm_g[l, 1], _pad_in_weights(w_in[l]))

        o_a, s_ctx = _hgrn(l, bc, sc, 0, tok.t, pa, hgrn_lb_logits, hgrn_onorm[l], None)
        o_a, _ = _hgrn(l, bl, sl, tok.tc, tok.t, pa, hgrn_lb_logits, hgrn_onorm[l], state_hgrn[:, l], prev=o_a)

        mla_w = _mla_weights(mla_w_uq[l], mla_w_ukv[l], mla_qk_norm[l])
        q_b, k_b, v_b, cache_b = _mla_prep(tok, pb, mla_q_norm[l], mla_kv_norm[l], mla_w, mla_tables)
        cache_rows = jnp.pad(cache_mla[:, l].reshape(bl * past, KV_LORA + ROPE_B),
                             ((0, 0), (0, 2 * LANES - KV_LORA - ROPE_B)))
        kc_b, vc_b = _mla_cache_prep(cache_rows, mla_w)
        o_b = _mla_attention(bc, sc, 0, tok.t, q_b, k_b, v_b)
        o_b = _mla_attention(bl, sl, tok.tc, tok.t, q_b, k_b, v_b, kc_b, vc_b, prev=o_b)

        o_c = _hy_conv_short(bc, sc, 0, tok.t, pc, hy_short[l], hy_bias[l], spec_ctx[l])
        o_c = _hy_conv_long(bl, sl, tok.tc, tok.t, pc, hy_short[l], hy_bias[l], spec_lat[l], prev=o_c)

        q_d, k_d, v_d, kcache_d = _diff_prep(tok, pd, diff_qk_norm[l], diff_tables)
        kc_d, vc_d = _diff_cache_prep(cache_diff_k[:, l].reshape(bl * past, H_D * 2 * DH_D),
                                      cache_diff_v[:, l].reshape(bl * past, H_D * DV_D))
        o_d = _diff_attention(l, bc, sc, 0, tok.t, diff_lambda[l], diff_subln[l], q_d, k_d, v_d)
        o_d = _diff_attention(l, bl, sl, tok.tc, tok.t, diff_lambda[l], diff_subln[l], q_d, k_d, v_d, kc_d, vc_d,
                              prev=o_d)

        x = _outproj(tok, x, mod, l, [o_a, o_b, o_c, o_d], w_out[l].astype(BF16))
        x = _ffn(tok, x, mod, l, 1, norm_g[l, 2], ffn_w_gu[l, 1].astype(BF16), ffn_w_down[l, 1].astype(BF16))

        new_mla.append(cache_b[: tok.tc, : KV_LORA + ROPE_B].reshape(bc, sc, KV_LORA + ROPE_B))
        new_dk.append(kcache_d[: tok.tc].reshape(bc, sc, H_D, 2, DH_D))
        new_dv.append(pd[: tok.tc, 2 * H_D * 2 * DH_D :].reshape(bc, sc, H_D, DV_D))
        new_state.append(s_ctx)

    y_prompt = x[: tok.tc].reshape(bc, sc, D_MODEL)
    y_sample = x[tok.tc :].reshape(bl, sl, D_MODEL)
    return (y_prompt, y_sample, jnp.stack(new_mla, axis=1), jnp.stack(new_dk, axis=1), jnp.stack(new_dv, axis=1),
            jnp.stack(new_state, axis=1))
```

```python
import functools
import math

import jax
import jax.numpy as jnp
import numpy as np
from jax import lax
from jax.experimental import pallas as pl
from jax.experimental.pallas import tpu as pltpu

F32 = jnp.float32
BF16 = jnp.bfloat16
HIGHEST = lax.Precision.HIGHEST

D_MODEL = 1024
DEPTH = 4
GRID_W = 64
N_MOD = 9
D_FF = 2816
EPS = 1e-6
ROPE_BASE = 10000.0
GROUP_W = 256
H_A, DK_A, DV_A = 4, 64, 64
MAX_INPUT_KEY = 1.0 - 1e-6
H_B, NOPE_B, ROPE_B, V_B = 4, 64, 32, 64
Q_LORA, KV_LORA = 256, 128
W_C, HY_ORDER, HY_BANDS, HY_FH, SHORT_K = 256, 2, 8, 64, 3
HY_EMB = 1 + 2 * HY_BANDS
H_D, DV_D, DH_D = 4, 64, 32

LANES = 128
MOD_ROWS = 16
TOKEN_TILE = 512
Q_TILE = 256
KEY_TILE = 512
HGRN_BLOCK = 128
VMEM_LIMIT = 56 * 1024 * 1024

PA_W, PB_W, PC_W, PD_W = 1280, 512, 768, 768
IN_PAD_W = PA_W + PB_W + PC_W + PD_W
MLA_IN = Q_LORA + KV_LORA + ROPE_B
HEAD_PAD = 128


def _cparams(n_axes):
    return pltpu.CompilerParams(dimension_semantics=("arbitrary",) * n_axes, vmem_limit_bytes=VMEM_LIMIT)


def _nt_dot(a, b):
    return lax.dot_general(a, b, (((1,), (1,)), ((), ())), preferred_element_type=F32)


def _dot(a, b):
    return jnp.dot(a, b, preferred_element_type=F32)


def _dot_exact_rhs(a, b_bf16, passes=3):
    out = None
    rem = a
    for _ in range(passes):
        piece = rem.astype(BF16)
        term = _dot(piece, b_bf16)
        out = term if out is None else out + term
        rem = rem - piece.astype(F32)
    return out


def _dot_exact_lhs(a_bf16, b, passes=3):
    out = None
    rem = b
    for _ in range(passes):
        piece = rem.astype(BF16)
        term = _dot(a_bf16, piece)
        out = term if out is None else out + term
        rem = rem - piece.astype(F32)
    return out


def _silu(x):
    return x * jax.nn.sigmoid(x)


def _adaln(x, g, sc, sh):
    y = x * lax.rsqrt(jnp.mean(x * x, axis=-1, keepdims=True) + EPS)
    return (y * g) * (1.0 + sc) + sh


def _rope(x, cos, sin_hi, sin_lo):
    w = x.shape[-1]
    return x * cos + pltpu.roll(x, 8, 1) * sin_hi + pltpu.roll(x, w - 8, 1) * sin_lo


def _mod_kernel(c_ref, w_ref, b_ref, o_ref):
    a = _silu(c_ref[...])
    o_ref[0] = jnp.dot(a, w_ref[0], preferred_element_type=F32, precision=HIGHEST) + b_ref[0]


def _modulation(cond, w_mod, b_mod):
    tn = D_MODEL
    n_col = N_MOD * D_MODEL
    out = pl.pallas_call(
        _mod_kernel,
        grid=(DEPTH, n_col // tn),
        in_specs=[
            pl.BlockSpec((MOD_ROWS, D_MODEL), lambda l, j: (0, 0)),
            pl.BlockSpec((1, D_MODEL, tn), lambda l, j: (l, 0, j)),
            pl.BlockSpec((1, 1, tn), lambda l, j: (l, 0, j)),
        ],
        out_specs=pl.BlockSpec((1, MOD_ROWS, tn), lambda l, j: (l, 0, j)),
        out_shape=jax.ShapeDtypeStruct((DEPTH, MOD_ROWS, n_col), F32),
        compiler_params=_cparams(2),
        name="modulation",
    )(cond, w_mod, b_mod.reshape(DEPTH, 1, n_col))
    return out.reshape(DEPTH * MOD_ROWS, N_MOD, D_MODEL)


class _Tokens:
    def __init__(self, n_ctx_seq, ctx_len, n_lat_seq, lat_len):
        self.bc, self.sc, self.bl, self.sl = n_ctx_seq, ctx_len, n_lat_seq, lat_len
        self.tc, self.tl = n_ctx_seq * ctx_len, n_lat_seq * lat_len
        self.t = self.tc + self.tl
        self.tm = min(TOKEN_TILE, lat_len)
        assert self.tc % self.tm == 0 and lat_len % self.tm == 0
        self.ctx_tiles = self.tc // self.tm
        self.lat_tiles_per_seq = lat_len // self.tm
        self.n_tiles = self.t // self.tm

    def mod_row(self, layer):
        def f(i):
            lat = 1 + (i - self.ctx_tiles) // self.lat_tiles_per_seq
            return layer * MOD_ROWS + jnp.where(i < self.ctx_tiles, 0, lat)

        return f

    def rope_block(self, i):
        return jnp.where(i < self.ctx_tiles, 0, 1 + (i - self.ctx_tiles) % self.lat_tiles_per_seq)


def _mod_spec(tok, layer):
    row = tok.mod_row(layer)
    return pl.BlockSpec((1, N_MOD, D_MODEL), lambda i: (row(i), 0, 0))


def _const_spec(shape):
    zeros = (0,) * len(shape)
    return pl.BlockSpec(shape, lambda *_: zeros)


def _row_spec(tm, width):
    return pl.BlockSpec((tm, width), lambda i: (i, 0))


def _keep_rows_of(prev, kernel_fn, in_specs, args):
    if prev is None:
        return kernel_fn, in_specs, args, {}
    idx = len(in_specs)

    def body(*refs):
        return kernel_fn(*refs[:idx], *refs[idx + 1 :])

    return body, in_specs + [pl.BlockSpec(memory_space=pl.ANY)], args + [prev], {idx: 0}


def _ffn_kernel(x_ref, mod_ref, g_ref, wgu_ref, wd_ref, o_ref, *, mod_base):
    x = x_ref[...]
    sh = mod_ref[0, mod_base : mod_base + 1, :]
    sc = mod_ref[0, mod_base + 1 : mod_base + 2, :]
    gate = mod_ref[0, mod_base + 2 : mod_base + 3, :]
    h = _adaln(x, g_ref[...], sc, sh).astype(BF16)
    au = _dot(h, wgu_ref[...])
    act = (_silu(au[:, :D_FF]) * au[:, D_FF:]).astype(BF16)
    o_ref[...] = x + (0.5 * gate) * _dot(act, wd_ref[...])


def _ffn(tok, x, mod, layer, which, norm_g, w_gu, w_down):
    tm = tok.tm
    return pl.pallas_call(
        functools.partial(_ffn_kernel, mod_base=6 * which),
        grid=(tok.n_tiles,),
        in_specs=[
            _row_spec(tm, D_MODEL),
            _mod_spec(tok, layer),
            _const_spec((1, D_MODEL)),
            pl.BlockSpec((D_MODEL, 2 * D_FF), lambda i: (0, 0), pipeline_mode=pl.Buffered(1)),
            pl.BlockSpec((D_FF, D_MODEL), lambda i: (0, 0), pipeline_mode=pl.Buffered(1)),
        ],
        out_specs=_row_spec(tm, D_MODEL),
        out_shape=jax.ShapeDtypeStruct((tok.t, D_MODEL), F32),
        compiler_params=_cparams(1),
        name="ffn",
    )(x, mod, norm_g.reshape(1, D_MODEL), w_gu, w_down)


def _inproj_kernel(x_ref, mod_ref, g_ref, w_ref, pa_ref, pb_ref, pc_ref, pd_ref):
    h = _adaln(x_ref[...], g_ref[...], mod_ref[0, 4:5, :], mod_ref[0, 3:4, :]).astype(BF16)
    p = _dot(h, w_ref[...])
    pa_ref[...] = p[:, :PA_W]
    pb_ref[...] = p[:, PA_W : PA_W + PB_W]
    pc_ref[...] = p[:, PA_W + PB_W : PA_W + PB_W + PC_W]
    pd_ref[...] = p[:, PA_W + PB_W + PC_W :]


def _inproj(tok, x, mod, layer, norm_g, w_in_pad):
    tm = tok.tm
    widths = (PA_W, PB_W, PC_W, PD_W)
    return pl.pallas_call(
        _inproj_kernel,
        grid=(tok.n_tiles,),
        in_specs=[
            _row_spec(tm, D_MODEL),
            _mod_spec(tok, layer),
            _const_spec((1, D_MODEL)),
            pl.BlockSpec((D_MODEL, IN_PAD_W), lambda i: (0, 0), pipeline_mode=pl.Buffered(1)),
        ],
        out_specs=[_row_spec(tm, w) for w in widths],
        out_shape=[jax.ShapeDtypeStruct((tok.t, w), F32) for w in widths],
        compiler_params=_cparams(1),
        name="inproj",
    )(x, mod, norm_g.reshape(1, D_MODEL), w_in_pad)


def _outproj_kernel(x_ref, mod_ref, oa_ref, ob_ref, oc_ref, od_ref, w_ref, o_ref):
    o = jnp.concatenate([oa_ref[...], ob_ref[...], oc_ref[...], od_ref[...]], axis=-1)
    o_ref[...] = x_ref[...] + mod_ref[0, 5:6, :] * _dot(o, w_ref[...])


def _outproj(tok, x, mod, layer, outs, w_out):
    tm = tok.tm
    return pl.pallas_call(
        _outproj_kernel,
        grid=(tok.n_tiles,),
        in_specs=[_row_spec(tm, D_MODEL), _mod_spec(tok, layer)]
        + [_row_spec(tm, GROUP_W)] * 4
        + [_const_spec((D_MODEL, D_MODEL))],
        out_specs=_row_spec(tm, D_MODEL),
        out_shape=jax.ShapeDtypeStruct((tok.t, D_MODEL), F32),
        compiler_params=_cparams(1),
        name="outproj",
    )(x, mod, *outs, w_out)


def _rope_group_tables(n_tok):
    t = np.arange(n_tok)
    pos = np.stack([t // GRID_W, t % GRID_W], axis=1).astype(np.float32)
    inv = (ROPE_BASE ** (-np.arange(8, dtype=np.float32) / 8)).astype(np.float32)
    lane = np.arange(32)
    ang = (pos[:, lane // 16] * inv[lane % 8][None, :]).astype(np.float32)
    second = (lane % 16) >= 8
    cos, sin = np.cos(ang), np.sin(ang)
    return cos, np.where(second[None], sin, 0.0), np.where(second[None], 0.0, -sin)


def _rope_tables(tok, lane_groups, width):
    cos = np.ones((tok.tm + tok.sl, width), np.float32)
    s_hi = np.zeros_like(cos)
    s_lo = np.zeros_like(cos)
    c, a, b = _rope_group_tables(tok.sl)
    for g in lane_groups:
        cos[tok.tm :, g : g + 32] = c
        s_hi[tok.tm :, g : g + 32] = a
        s_lo[tok.tm :, g : g + 32] = b
    return jnp.asarray(cos), jnp.asarray(s_hi), jnp.asarray(s_lo)


def _mla_keys_values(kv_in, wk_ref, wv_ref, kn_ref, rope):
    kraw = _dot(kv_in.astype(BF16), wk_ref[...])
    v = _dot(kv_in[:, :KV_LORA].astype(BF16), wv_ref[...])
    ks = []
    for h in range(H_B):
        kh = kraw[:, h * HEAD_PAD : (h + 1) * HEAD_PAD]
        ss = jnp.sum(kh * kh, axis=-1, keepdims=True) * (1.0 / (NOPE_B + ROPE_B))
        kh = kh * lax.rsqrt(ss + EPS) * kn_ref[...]
        if rope is not None:
            kh = _rope(kh, *rope)
        ks.append(kh.astype(BF16))
    return ks, v.astype(BF16)


def _mla_prep_kernel(pb_ref, gq_ref, gkv_ref, wuq_ref, wk_ref, wv_ref, qn_ref, kn_ref, cos_ref, shi_ref, slo_ref,
                     q_ref, k_ref, v_ref, cache_ref):
    pb = pb_ref[...]
    rope = (cos_ref[...], shi_ref[...], slo_ref[...])
    cq = pb[:, :Q_LORA]
    cq = cq * lax.rsqrt(jnp.mean(cq * cq, axis=-1, keepdims=True) + EPS) * gq_ref[...]
    qraw = _dot(cq.astype(BF16), wuq_ref[...])
    scale = (NOPE_B + ROPE_B) ** -0.5 * LOG2E
    for h in range(H_B):
        qh = qraw[:, h * HEAD_PAD : (h + 1) * HEAD_PAD]
        ss = jnp.sum(qh * qh, axis=-1, keepdims=True) * (1.0 / (NOPE_B + ROPE_B))
        qh = _rope(qh * lax.rsqrt(ss + EPS) * qn_ref[...], *rope)
        q_ref[:, h * HEAD_PAD : (h + 1) * HEAD_PAD] = (qh * scale).astype(BF16)
    ckv = pb[:, Q_LORA : Q_LORA + KV_LORA]
    ckv = ckv * lax.rsqrt(jnp.mean(ckv * ckv, axis=-1, keepdims=True) + EPS) * gkv_ref[...]
    kv_in = jnp.concatenate([ckv, pb[:, Q_LORA + KV_LORA :]], axis=-1)
    cache_ref[...] = kv_in
    ks, v = _mla_keys_values(kv_in, wk_ref, wv_ref, kn_ref, rope)
    for h in range(H_B):
        k_ref[:, h * HEAD_PAD : (h + 1) * HEAD_PAD] = ks[h]
    v_ref[...] = v


def _mla_cache_kernel(c_ref, wk_ref, wv_ref, kn_ref, k_ref, v_ref):
    ks, v = _mla_keys_values(c_ref[...], wk_ref, wv_ref, kn_ref, None)
    for h in range(H_B):
        k_ref[:, h * HEAD_PAD : (h + 1) * HEAD_PAD] = ks[h]
    v_ref[...] = v


def _mla_weights(w_uq, w_ukv, qk_norm):
    wuq = jnp.pad(w_uq.reshape(Q_LORA, H_B, NOPE_B + ROPE_B), ((0, 0), (0, 0), (0, HEAD_PAD - NOPE_B - ROPE_B)))
    wuq = wuq.reshape(Q_LORA, H_B * HEAD_PAD).astype(BF16)
    ukv = w_ukv.reshape(KV_LORA, H_B, NOPE_B + V_B)
    wk_nope = jnp.pad(ukv[:, :, :NOPE_B], ((0, 0), (0, 0), (0, HEAD_PAD - NOPE_B)))
    place = np.zeros((2 * LANES - KV_LORA, H_B, HEAD_PAD), np.float32)
    for h in range(H_B):
        place[np.arange(ROPE_B), h, NOPE_B + np.arange(ROPE_B)] = 1.0
    wk = jnp.concatenate([wk_nope, jnp.asarray(place)], axis=0).reshape(2 * LANES, H_B * HEAD_PAD).astype(BF16)
    wv = ukv[:, :, NOPE_B:].reshape(KV_LORA, H_B * V_B).astype(BF16)
    pad = lambda g: jnp.pad(g, (0, HEAD_PAD - NOPE_B - ROPE_B)).reshape(1, HEAD_PAD)
    return wuq, wk, wv, pad(qk_norm[0]), pad(qk_norm[1])


def _mla_prep(tok, pb, gq, gkv, weights, tables):
    wuq, wk, wv, qn, kn = weights
    tm = tok.tm
    rope_spec = pl.BlockSpec((tm, HEAD_PAD), lambda i: (tok.rope_block(i), 0))
    return pl.pallas_call(
        _mla_prep_kernel,
        grid=(tok.n_tiles,),
        in_specs=[
            _row_spec(tm, PB_W),
            _const_spec((1, Q_LORA)),
            _const_spec((1, KV_LORA)),
            _const_spec(wuq.shape),
            _const_spec(wk.shape),
            _const_spec(wv.shape),
            _const_spec((1, HEAD_PAD)),
            _const_spec((1, HEAD_PAD)),
            rope_spec,
            rope_spec,
            rope_spec,
        ],
        out_specs=[_row_spec(tm, H_B * HEAD_PAD), _row_spec(tm, H_B * HEAD_PAD), _row_spec(tm, H_B * V_B),
                   _row_spec(tm, 2 * LANES)],
        out_shape=[
            jax.ShapeDtypeStruct((tok.t, H_B * HEAD_PAD), BF16),
            jax.ShapeDtypeStruct((tok.t, H_B * HEAD_PAD), BF16),
            jax.ShapeDtypeStruct((tok.t, H_B * V_B), BF16),
            jax.ShapeDtypeStruct((tok.t, 2 * LANES), F32),
        ],
        compiler_params=_cparams(1),
        name="mla_prep",
    )(pb, gq.reshape(1, -1), gkv.reshape(1, -1), wuq, wk, wv, qn, kn, *tables)


def _mla_cache_prep(cache_rows, weights):
    _, wk, wv, _, kn = weights
    rows = cache_rows.shape[0]
    tm = min(TOKEN_TILE, rows)
    return pl.pallas_call(
        _mla_cache_kernel,
        grid=(rows // tm,),
        in_specs=[_row_spec(tm, 2 * LANES), _const_spec(wk.shape), _const_spec(wv.shape), _const_spec((1, HEAD_PAD))],
        out_specs=[_row_spec(tm, H_B * HEAD_PAD), _row_spec(tm, H_B * V_B)],
        out_shape=[jax.ShapeDtypeStruct((rows, H_B * HEAD_PAD), BF16), jax.ShapeDtypeStruct((rows, H_B * V_B), BF16)],
        compiler_params=_cparams(1),
        name="mla_cache_prep",
    )(cache_rows, wk, wv, kn)


LOG2E = math.log2(math.e)


def _softmax_parts(scores):
    m = functools.reduce(jnp.maximum, [jnp.max(s, axis=-1, keepdims=True) for s in scores])
    ps = [jnp.exp2(s - m) for s in scores]
    denom = functools.reduce(jnp.add, [jnp.sum(p, axis=-1, keepdims=True) for p in ps])
    return ps, denom


def _head_lane_mask(width, head, head_w):
    lane = lax.broadcasted_iota(jnp.int32, (1, width), 1)
    return (lane >= head * head_w) & (lane < (head + 1) * head_w)


def _mla_attn_kernel(*refs, n_kv):
    q_ref = refs[0]
    k_refs = refs[1 : 1 + n_kv]
    v_refs = refs[1 + n_kv : 1 + 2 * n_kv]
    o_ref = refs[1 + 2 * n_kv]
    tiles = [(k, v, j) for k, v in zip(k_refs, v_refs) for j in range(0, k.shape[0], KEY_TILE)]
    out = None
    for h in range(H_B):
        sl = slice(h * HEAD_PAD, (h + 1) * HEAD_PAD)
        qh = q_ref[:, sl]
        mask = _head_lane_mask(H_B * V_B, h, V_B)
        m = denom = acc = None
        for k, v, j in tiles:
            rows = slice(j, min(j + KEY_TILE, k.shape[0]))
            s = _nt_dot(qh, k[rows, sl])
            vm = jnp.where(mask, v[rows, :], jnp.zeros((), BF16))
            s_max = jnp.max(s, axis=-1, keepdims=True)
            if m is None:
                m = s_max
                p = jnp.exp2(s - m)
                denom = jnp.sum(p, axis=-1, keepdims=True)
                acc = _dot(p.astype(BF16), vm)
            else:
                m_new = jnp.maximum(m, s_max)
                alpha = jnp.exp2(m - m_new)
                p = jnp.exp2(s - m_new)
                denom = denom * alpha + jnp.sum(p, axis=-1, keepdims=True)
                acc = acc * alpha + _dot(p.astype(BF16), vm)
                m = m_new
        acc = acc * (1.0 / denom)
        out = acc if out is None else out + acc
    o_ref[...] = out.astype(BF16)


def _seq_specs(n_seq, seq_len, row0, tq, q_width, kv_widths, kv_new, kv_cache):
    q_blocks = seq_len // tq
    q_spec = lambda w: pl.BlockSpec((tq, w), lambda b, i: (row0 // tq + b * q_blocks + i, 0))
    new_spec = lambda w: pl.BlockSpec((seq_len, w), lambda b, i: (row0 // seq_len + b, 0))
    specs = [q_spec(q_width)]
    for w in kv_widths:
        if kv_cache is not None:
            specs.append(pl.BlockSpec((kv_cache, w), lambda b, i: (b, 0)))
        specs.append(new_spec(w))
    return specs, q_spec


def _mla_attention(n_seq, seq_len, row0, total_rows, q, k, v, k_cache=None, v_cache=None, prev=None):
    tq = min(Q_TILE, seq_len)
    has_cache = k_cache is not None
    cache_len = k_cache.shape[0] // n_seq if has_cache else None
    specs, q_spec = _seq_specs(n_seq, seq_len, row0, tq, H_B * HEAD_PAD, (H_B * HEAD_PAD, H_B * V_B), True, cache_len)
    args = [q] + ([k_cache, k] if has_cache else [k]) + ([v_cache, v] if has_cache else [v])
    body = functools.partial(_mla_attn_kernel, n_kv=2 if has_cache else 1)
    body, specs, args, aliases = _keep_rows_of(prev, body, specs, args)
    return pl.pallas_call(
        body,
        grid=(n_seq, seq_len // tq),
        in_specs=specs,
        out_specs=q_spec(H_B * V_B),
        out_shape=jax.ShapeDtypeStruct((total_rows, H_B * V_B), BF16),
        input_output_aliases=aliases,
        compiler_params=_cparams(2),
        name="mla_attention",
    )(*args)


DQK_W = H_D * 2 * HEAD_PAD


def _diff_place_matrix():
    m = np.zeros((H_D * 2 * DH_D, DQK_W), np.float32)
    g = np.arange(H_D * 2 * DH_D)
    m[g, (g // DH_D) * HEAD_PAD + g % DH_D] = 1.0
    return jnp.asarray(m, BF16)


def _group_mean_matrix(width, group):
    g = np.arange(width)
    return jnp.asarray((g[:, None] // group == g[None, :] // group).astype(np.float32), BF16)


def _diff_prep_kernel(pd_ref, gq_ref, gk_ref, gm_ref, place_ref, cos_ref, shi_ref, slo_ref,
                      q_ref, k_ref, v_ref, kcache_ref):
    pd = pd_ref[...]
    rope = (cos_ref[...], shi_ref[...], slo_ref[...])
    w = H_D * 2 * DH_D

    def norm(x, g):
        ms = _dot_exact_rhs(x * x, gm_ref[...], passes=2) * (1.0 / DH_D)
        return x * lax.rsqrt(ms + EPS) * g

    qn = _rope(norm(pd[:, :w], gq_ref[...]), *rope) * (DH_D ** -0.5 * LOG2E)
    kn = norm(pd[:, w : 2 * w], gk_ref[...])
    kcache_ref[...] = kn
    q_ref[...] = _dot(qn.astype(BF16), place_ref[...]).astype(BF16)
    k_ref[...] = _dot(_rope(kn, *rope).astype(BF16), place_ref[...]).astype(BF16)
    v_ref[...] = pd[:, 2 * w :].astype(BF16)


def _diff_cache_kernel(k_in_ref, v_in_ref, place_ref, k_ref, v_ref):
    k_ref[...] = _dot(k_in_ref[...].astype(BF16), place_ref[...]).astype(BF16)
    v_ref[...] = v_in_ref[...].astype(BF16)


def _diff_prep(tok, pd, qk_norm, tables):
    tm = tok.tm
    w = H_D * 2 * DH_D
    rope_spec = pl.BlockSpec((tm, w), lambda i: (tok.rope_block(i), 0))
    tile_g = lambda g: jnp.tile(g, H_D * 2).reshape(1, w)
    return pl.pallas_call(
        _diff_prep_kernel,
        grid=(tok.n_tiles,),
        in_specs=[_row_spec(tm, PD_W), _const_spec((1, w)), _const_spec((1, w)), _const_spec((w, w)),
                  _const_spec((w, DQK_W)), rope_spec, rope_spec, rope_spec],
        out_specs=[_row_spec(tm, DQK_W), _row_spec(tm, DQK_W), _row_spec(tm, H_D * DV_D), _row_spec(tm, w)],
        out_shape=[
            jax.ShapeDtypeStruct((tok.t, DQK_W), BF16),
            jax.ShapeDtypeStruct((tok.t, DQK_W), BF16),
            jax.ShapeDtypeStruct((tok.t, H_D * DV_D), BF16),
            jax.ShapeDtypeStruct((tok.t, w), F32),
        ],
        compiler_params=_cparams(1),
        name="diff_prep",
    )(pd, tile_g(qk_norm[0]), tile_g(qk_norm[1]), _group_mean_matrix(w, DH_D), _diff_place_matrix(), *tables)


def _diff_cache_prep(k_rows, v_rows):
    rows = k_rows.shape[0]
    tm = min(TOKEN_TILE, rows)
    w = H_D * 2 * DH_D
    return pl.pallas_call(
        _diff_cache_kernel,
        grid=(rows // tm,),
        in_specs=[_row_spec(tm, w), _row_spec(tm, H_D * DV_D), _const_spec((w, DQK_W))],
        out_specs=[_row_spec(tm, DQK_W), _row_spec(tm, H_D * DV_D)],
        out_shape=[jax.ShapeDtypeStruct((rows, DQK_W), BF16), jax.ShapeDtypeStruct((rows, H_D * DV_D), BF16)],
        compiler_params=_cparams(1),
        name="diff_cache_prep",
    )(k_rows, v_rows, _diff_place_matrix())


def _diff_attn_kernel(*refs, n_kv, lam_init):
    q_ref, lam_ref, sub_ref = refs[0], refs[1], refs[2]
    k_refs = refs[3 : 3 + n_kv]
    v_refs = refs[3 + n_kv : 3 + 2 * n_kv]
    o_ref = refs[3 + 2 * n_kv]
    dl = lam_ref[...]
    lam = (jnp.exp(jnp.sum(dl[0:1] * dl[1:2], axis=-1, keepdims=True))
           - jnp.exp(jnp.sum(dl[2:3] * dl[3:4], axis=-1, keepdims=True)) + lam_init)
    out = None
    for h in range(H_D):
        probs = []
        for m in range(2):
            sl = slice((2 * h + m) * HEAD_PAD, (2 * h + m + 1) * HEAD_PAD)
            qh = q_ref[:, sl]
            ps, denom = _softmax_parts([_nt_dot(qh, k[:, sl]) for k in k_refs])
            probs.append((ps, (1.0 if m == 0 else lam) / denom))
        mask = _head_lane_mask(H_D * DV_D, h, DV_D)
        (ps0, a0), (ps1, a1) = probs
        acc = None
        for p0, p1, v in zip(ps0, ps1, v_refs):
            wgt = (p0 * a0 - p1 * a1).astype(BF16)
            term = _dot(wgt, jnp.where(mask, v[...], jnp.zeros((), BF16)))
            acc = term if acc is None else acc + term
        ms = jnp.sum(acc * acc, axis=-1, keepdims=True) * (1.0 / DV_D)
        acc = acc * lax.rsqrt(ms + EPS)
        out = acc if out is None else out + acc
    o_ref[...] = (out * sub_ref[...] * (1.0 - lam_init)).astype(BF16)


def _diff_attention(layer, n_seq, seq_len, row0, total_rows, lam_p, sub_g, q, k, v, k_cache=None, v_cache=None,
                    prev=None):
    tq = min(Q_TILE, seq_len)
    has_cache = k_cache is not None
    cache_len = k_cache.shape[0] // n_seq if has_cache else None
    specs, q_spec = _seq_specs(n_seq, seq_len, row0, tq, DQK_W, (DQK_W, H_D * DV_D), True, cache_len)
    specs = [specs[0], pl.BlockSpec((4, DH_D), lambda b, i: (0, 0)), pl.BlockSpec((1, H_D * DV_D), lambda b, i: (0, 0))
             ] + specs[1:]
    args = [q, lam_p, jnp.tile(sub_g, H_D).reshape(1, H_D * DV_D)]
    args += ([k_cache, k] if has_cache else [k]) + ([v_cache, v] if has_cache else [v])
    lam_init = 0.8 - 0.6 * math.exp(-0.3 * layer)
    body = functools.partial(_diff_attn_kernel, n_kv=2 if has_cache else 1, lam_init=lam_init)
    body, specs, args, aliases = _keep_rows_of(prev, body, specs, args)
    return pl.pallas_call(
        body,
        grid=(n_seq, seq_len // tq),
        in_specs=specs,
        out_specs=q_spec(H_D * DV_D),
        out_shape=jax.ShapeDtypeStruct((total_rows, H_D * DV_D), BF16),
        input_output_aliases=aliases,
        compiler_params=_cparams(2),
        name="diff_attention",
    )(*args)


A_W = H_A * DK_A


HGRN_LEVELS = (2, 4, 8, 16, 32, 64, 128)
PAIR_W = 2 * DK_A


def _hgrn_constants():
    t = np.arange(HGRN_BLOCK)
    lower = t[None, :] <= t[:, None]
    upper = t[None, :] >= t[:, None]
    masks = [t[:, None] == t[None, :]] + [(t[:, None] // b) == (t[None, :] // b) for b in HGRN_LEVELS[:-1]]
    masks = np.stack([np.tile(m, (1, 2)) for m in masks]).astype(np.float32)
    g = np.arange(A_W)
    heads = (g[:, None] // DK_A) == (g[None, :] // DK_A)
    as_bf16 = lambda m: jnp.asarray(m.astype(np.float32), BF16)
    pair_mask = heads[:PAIR_W, :PAIR_W].astype(np.float32)
    return as_bf16(lower), as_bf16(upper), as_bf16(heads), jnp.asarray(pair_mask), jnp.asarray(masks)


def _level_reference(cum, b, forward):
    off = b // 2 - 1 if forward else b // 2
    if b >= 8:
        c3 = cum.reshape(HGRN_BLOCK // b, b, A_W)
        return jnp.broadcast_to(c3[:, off : off + 1, :], c3.shape).reshape(HGRN_BLOCK, A_W)
    c3 = cum.reshape(HGRN_BLOCK // 8, 8, A_W)
    sub = lax.broadcasted_iota(jnp.int32, (1, 8, 1), 1)
    out = None
    for g in range(8 // b):
        cand = jnp.broadcast_to(c3[:, g * b + off : g * b + off + 1, :], c3.shape)
        out = cand if out is None else jnp.where(sub >= g * b, cand, out)
    return out.reshape(HGRN_BLOCK, A_W)


def _hgrn_kernel(*refs, layer, seq_len, has_state):
    if has_state:
        pa_ref, lbl_ref, og_ref, s0_ref, lower_ref, upper_ref, heads_ref, hmask_ref, lvl_ref = refs[:9]
        rest = refs[9:]
    else:
        pa_ref, lbl_ref, og_ref, lower_ref, upper_ref, heads_ref, hmask_ref, lvl_ref = refs[:8]
        s0_ref = None
        rest = refs[8:]
    o_ref, sout_ref, st_ref, oacc_ref = rest
    n_blocks = seq_len // HGRN_BLOCK
    row = lax.broadcasted_iota(jnp.int32, (HGRN_BLOCK, 1), 0)
    first_head = lax.broadcasted_iota(jnp.int32, (1, PAIR_W), 1) < DK_A
    zero_bf16 = jnp.zeros((), BF16)

    def per_head_rows(x):
        return jnp.concatenate([jnp.where(first_head, x, zero_bf16), jnp.where(first_head, zero_bf16, x)], axis=0)

    for d in range(2):
        logits = lbl_ref[d]
        e = jnp.exp(logits - jnp.max(logits, axis=0, keepdims=True))
        p = e / jnp.sum(e, axis=0, keepdims=True)
        lb = jnp.zeros((1, A_W), F32)
        for j in range(1, layer + 1):
            lb = lb + p[j : j + 1, :]
        tri_ref = lower_ref if d == 0 else upper_ref
        st_ref[...] = jnp.zeros(st_ref.shape, F32)
        if has_state:
            for h in range(H_A):
                off = (h % 2) * DK_A
                st_ref[h // 2, off : off + DV_A, off : off + DK_A] = s0_ref[0, d, h].T

        def block(i, carry, d=d, lb=lb, tri_ref=tri_ref):
            blk = i if d == 0 else n_blocks - 1 - i
            r0 = pl.multiple_of(blk * HGRN_BLOCK, HGRN_BLOCK)
            rows = pl.ds(r0, HGRN_BLOCK)
            q = _silu(pa_ref[rows, 0:A_W]) * DK_A ** -0.5
            v = pa_ref[rows, A_W : 2 * A_W]
            logit = pa_ref[rows, (2 + d) * A_W : (3 + d) * A_W]
            key = jnp.minimum((1.0 - lb) * jax.nn.sigmoid(-logit), MAX_INPUT_KEY)
            log_f = jnp.log1p(-key)
            cum = _dot_exact_lhs(tri_ref[...], log_f)
            tot = cum[HGRN_BLOCK - 1 : HGRN_BLOCK, :] if d == 0 else cum[0:1, :]
            v_b = v.astype(BF16)
            scores = [None] * (H_A // 2)

            def add_pairs(qd, kd, mask_index):
                qd_b, kd_b = qd.astype(BF16), kd.astype(BF16)
                for pi in range(H_A // 2):
                    lanes = slice(pi * PAIR_W, (pi + 1) * PAIR_W)
                    s = _nt_dot(qd_b[:, lanes], per_head_rows(kd_b[:, lanes]))
                    if mask_index is not None:
                        s = s * lvl_ref[mask_index]
                    scores[pi] = s if scores[pi] is None else scores[pi] + s

            add_pairs(q, key, 0)
            for li, b in enumerate(HGRN_LEVELS):
                later = ((row % b) >= b // 2) if d == 0 else ((row % b) < b // 2)
                if b == 2:
                    qd = jnp.where(later, q * jnp.exp(log_f), 0.0)
                    kd = jnp.where(later, 0.0, key)
                else:
                    ref = _level_reference(cum, b, d == 0)
                    qd = jnp.where(later, q * jnp.exp(cum - ref), 0.0)
                    kd = jnp.where(later, 0.0, key * jnp.exp(ref - cum))
                add_pairs(qd, kd, li + 1 if b < HGRN_BLOCK else None)
            q_dec = (q * jnp.exp(cum)).astype(BF16)
            k_dec = (key * jnp.exp(tot - cum)).astype(BF16)
            decay = jnp.exp(tot)
            parts = []
            for pi in range(H_A // 2):
                lanes = slice(pi * PAIR_W, (pi + 1) * PAIR_W)
                state = st_ref[pi]
                o_pair = _dot(scores[pi].astype(BF16), per_head_rows(v_b[:, lanes]))
                parts.append(o_pair + _nt_dot(q_dec[:, lanes], state.astype(BF16)))
                upd = lax.dot_general(v_b[:, lanes], k_dec[:, lanes], (((0,), (0,)), ((), ())),
                                      preferred_element_type=F32)
                st_ref[pi] = state * decay[:, lanes] + upd * hmask_ref[...]
            o = jnp.concatenate(parts, axis=1)
            if d == 0:
                oacc_ref[rows, :] = o
            else:
                oacc_ref[rows, :] += o
            return carry

        lax.fori_loop(0, n_blocks, block, 0)
        for h in range(H_A):
            off = (h % 2) * DK_A
            sout_ref[0, d, h] = st_ref[h // 2, off : off + DV_A, off : off + DK_A].T

    o = oacc_ref[...]
    ms = _dot_exact_rhs(o * o, heads_ref[...], passes=2) * (1.0 / DV_A)
    o_ref[...] = (o * lax.rsqrt(ms + EPS) * og_ref[...] * _silu(pa_ref[:, 4 * A_W : 5 * A_W])).astype(BF16)


def _hgrn(layer, n_seq, seq_len, row0, total_rows, pa, lb_logits, onorm_g, s0, prev=None):
    has_state = s0 is not None
    consts = _hgrn_constants()
    seq_spec = lambda w: pl.BlockSpec((seq_len, w), lambda b: (row0 // seq_len + b, 0))
    state_spec = pl.BlockSpec((1, 2, H_A, DK_A, DV_A), lambda b: (b, 0, 0, 0, 0))
    in_specs = [seq_spec(PA_W), _const_spec(lb_logits.shape), _const_spec((1, A_W))]
    args = [pa, lb_logits, jnp.tile(onorm_g, H_A).reshape(1, A_W)]
    if has_state:
        in_specs.append(state_spec)
        args.append(s0)
    in_specs += [_const_spec(c.shape) for c in consts]
    args += list(consts)
    body = functools.partial(_hgrn_kernel, layer=layer, seq_len=seq_len, has_state=has_state)
    body, in_specs, args, aliases = _keep_rows_of(prev, body, in_specs, args)
    return pl.pallas_call(
        body,
        grid=(n_seq,),
        in_specs=in_specs,
        out_specs=[seq_spec(A_W), state_spec],
        out_shape=[
            jax.ShapeDtypeStruct((total_rows, A_W), BF16),
            jax.ShapeDtypeStruct((n_seq, 2, H_A, DK_A, DV_A), F32),
        ],
        input_output_aliases=aliases,
        scratch_shapes=[pltpu.VMEM((H_A // 2, 2 * DV_A, 2 * DK_A), F32), pltpu.VMEM((seq_len, A_W), F32)],
        compiler_params=_cparams(1),
        name="hgrn",
    )(*args)


HY_W = HY_ORDER * W_C
FFT_R = 64


def _hy_filter_kernel(w1_ref, b1_ref, w2_ref, b2_ref, w3_ref, ld_ref, hf_ref, hb_ref, *, n):
    f32dot = functools.partial(jnp.dot, preferred_element_type=F32, precision=HIGHEST)
    row = lax.broadcasted_iota(jnp.int32, (n, LANES), 0)
    lane = lax.broadcasted_iota(jnp.int32, (n, LANES), 1)
    tn = row.astype(F32) / n
    band = jnp.where(lane <= HY_BANDS, lane, lane - HY_BANDS).astype(F32)
    ang = (2.0 * math.pi) * tn * band
    feats = jnp.where(lane == 0, tn, jnp.where(lane <= HY_BANDS, jnp.cos(ang),
                                                jnp.where(lane <= 2 * HY_BANDS, jnp.sin(ang), 0.0)))
    h = jnp.sin(f32dot(feats, w1_ref[0]) + b1_ref[0])
    h = jnp.sin(f32dot(h, w2_ref[0]) + b2_ref[0])
    h = f32dot(h, w3_ref[0])
    h = h * jnp.exp(-jnp.exp(ld_ref[0]) * tn[:, 0:1])
    hf = h[:, :HY_W]
    hb = jnp.where(row[:, 0:1] == 0, 0.0, h[:, HY_W:])
    norm = jnp.sum(jnp.abs(hf), axis=0, keepdims=True) + jnp.sum(jnp.abs(hb), axis=0, keepdims=True) + EPS
    hf_ref[0] = hf / norm
    hb_ref[0] = hb / norm


def _hy_filters(n, w1, b1, w2, b2, w3, log_decay):
    w1p = jnp.pad(w1, ((0, 0), (0, LANES - HY_EMB), (0, 0)))
    lay = lambda shape: pl.BlockSpec((1,) + shape, lambda l: (l,) + (0,) * len(shape))
    return pl.pallas_call(
        functools.partial(_hy_filter_kernel, n=n),
        grid=(DEPTH,),
        in_specs=[lay((LANES, HY_FH)), lay((1, HY_FH)), lay((HY_FH, HY_FH)), lay((1, HY_FH)),
                  lay((HY_FH, 2 * HY_W)), lay((1, 2 * HY_W))],
        out_specs=[lay((n, HY_W)), lay((n, HY_W))],
        out_shape=[jax.ShapeDtypeStruct((DEPTH, n, HY_W), F32)] * 2,
        compiler_params=_cparams(1),
        name="hyena_filters",
    )(w1p, b1.reshape(DEPTH, 1, HY_FH), w2, b2.reshape(DEPTH, 1, HY_FH), w3, log_decay.reshape(DEPTH, 1, 2 * HY_W))


def _dft_tables_short(n):
    big = 2 * n
    k = np.arange(big)[:, None]
    t = np.arange(n)[None, :]
    ang = 2.0 * np.pi * ((k * t) % big) / big
    fwd = np.concatenate([np.cos(ang), -np.sin(ang)], axis=0)
    inv = np.concatenate([np.cos(ang).T, -np.sin(ang).T], axis=1) / big
    return fwd.astype(np.float32), inv.astype(np.float32)


def _hy_spec_short_kernel(hf_ref, hb_ref, fwd_ref, f_ref, *, n):
    f32dot = functools.partial(jnp.dot, preferred_element_type=F32, precision=HIGHEST)
    xf = f32dot(fwd_ref[...], hf_ref[0])
    xb = f32dot(fwd_ref[...], hb_ref[0])
    big = 2 * n
    f_ref[0, :big, :] = xf[:big] + xb[:big]
    f_ref[0, big:, :] = xf[big:] - xb[big:]


def _hy_spec_short(n, hf, hb):
    fwd, _ = _dft_tables_short(n)
    lay = lambda shape: pl.BlockSpec((1,) + shape, lambda l: (l,) + (0,) * len(shape))
    return pl.pallas_call(
        functools.partial(_hy_spec_short_kernel, n=n),
        grid=(DEPTH,),
        in_specs=[lay((n, HY_W)), lay((n, HY_W)), _const_spec(fwd.shape)],
        out_specs=lay((4 * n, HY_W)),
        out_shape=jax.ShapeDtypeStruct((DEPTH, 4 * n, HY_W), F32),
        compiler_params=_cparams(1),
        name="hyena_spectrum_short",
    )(hf, hb, jnp.asarray(fwd))


def _short_conv(x, w, n):
    row = lax.broadcasted_iota(jnp.int32, (n, 1), 0)
    prev = jnp.where(row == 0, 0.0, pltpu.roll(x, 1, 0))
    nxt = jnp.where(row == n - 1, 0.0, pltpu.roll(x, n - 1, 0))
    return prev * w[0:1, :] + x * w[1:2, :] + nxt * w[2:3, :]


def _hy_conv_short_kernel(pc_ref, ws_ref, bias_ref, f_ref, fwd_ref, inv_ref, o_ref, *, n):
    big = 2 * n
    u = _short_conv(pc_ref[...], ws_ref[...], n)
    v, x1, x2 = u[:, :W_C], u[:, W_C : 2 * W_C], u[:, 2 * W_C :]

    def conv(x, order):
        spec = _dot(fwd_ref[...], x.astype(BF16))
        fr = f_ref[:big, order * W_C : (order + 1) * W_C]
        fi = f_ref[big:, order * W_C : (order + 1) * W_C]
        zr = spec[:big] * fr - spec[big:] * fi
        zi = spec[:big] * fi + spec[big:] * fr
        return _dot(inv_ref[...], jnp.concatenate([zr, zi], axis=0).astype(BF16))

    z = x1 * (conv(v, 0) + v * bias_ref[0:1, :])
    z = x2 * (conv(z, 1) + z * bias_ref[1:2, :])
    o_ref[...] = z.astype(BF16)


def _hy_conv_short(n_seq, n, row0, total_rows, pc, w_short, bias, spec):
    fwd, inv = _dft_tables_short(n)
    seq_spec = lambda w: pl.BlockSpec((n, w), lambda b: (row0 // n + b, 0))
    return pl.pallas_call(
        functools.partial(_hy_conv_short_kernel, n=n),
        grid=(n_seq,),
        in_specs=[seq_spec(PC_W), _const_spec((SHORT_K, PC_W)), _const_spec((HY_ORDER, W_C)),
                  _const_spec((4 * n, HY_W)), _const_spec(fwd.shape), _const_spec(inv.shape)],
        out_specs=seq_spec(W_C),
        out_shape=jax.ShapeDtypeStruct((total_rows, W_C), BF16),
        compiler_params=_cparams(1),
        name="hyena_conv_short",
    )(pc, w_short, bias, spec, jnp.asarray(fwd, BF16), jnp.asarray(inv, BF16))


def _dft_tables_long():
    r = FFT_R
    big = r * r
    half = r // 2
    n2 = np.arange(r)[:, None, None]
    k1 = np.arange(r)[None, :, None]
    n1 = np.arange(half)[None, None, :]
    ang = 2.0 * np.pi * ((k1 * (r * n1 + n2)) % big) / big
    first = np.concatenate([np.cos(ang), -np.sin(ang)], axis=1)
    last = np.concatenate([np.cos(ang), -np.sin(ang)], axis=1).transpose(0, 2, 1) / big
    a = np.arange(r)
    ang_r = 2.0 * np.pi * ((a[:, None] * a[None, :]) % r) / r
    c, s = np.cos(ang_r), np.sin(ang_r)
    mid = np.block([[c, s], [-s, c]])
    mid_inv = np.block([[c, -s], [s, c]])
    f32 = lambda m: m.astype(np.float32)
    return f32(first), f32(mid), f32(mid_inv), f32(last)


def _ld_rows(ref, rows):
    return jnp.concatenate([ref[j, rows, :] for j in range(ref.shape[0])], axis=-1)


def _st_rows(ref, rows, val):
    for j in range(ref.shape[0]):
        ref[j, rows, :] = val[:, j * LANES : (j + 1) * LANES]


def _st_transposed(ref, j, val):
    r = FFT_R
    _st_rows(ref, pl.ds(j, r, stride=2 * r), val[:r])
    _st_rows(ref, pl.ds(r + j, r, stride=2 * r), val[r:])


def _lane_split_scratch(rows, width):
    return pltpu.VMEM((width // LANES, rows, LANES), F32)


FFT_UNROLL = 4


def _fft_long_forward(x_ref, y_ref, dot_first, dot_mid, out_fn):
    r = FFT_R

    def stage_a(n2, carry):
        res = dot_first(n2, _ld_rows(x_ref, pl.ds(n2, r // 2, stride=r)))
        _st_transposed(y_ref, n2, res)
        return carry

    lax.fori_loop(0, r, stage_a, 0, unroll=FFT_UNROLL)

    def stage_c(k1, carry):
        base = pl.multiple_of(k1 * 2 * r, 2 * r)
        out_fn(k1, base, dot_mid(_ld_rows(y_ref, pl.ds(base, 2 * r))))
        return carry

    lax.fori_loop(0, r, stage_c, 0, unroll=FFT_UNROLL)


def _dot_split(t_hi, t_lo, x):
    x_hi = x.astype(BF16)
    x_lo = (x - x_hi.astype(F32)).astype(BF16)
    return _dot(t_hi, x_hi) + _dot(t_hi, x_lo) + _dot(t_lo, x_hi)


def _split_table(m):
    hi = jnp.asarray(m, BF16)
    lo = (jnp.asarray(m) - hi.astype(F32)).astype(BF16)
    return hi, lo


def _hy_spec_long_kernel(hf_ref, hb_ref, first_hi_ref, first_lo_ref, mid_hi_ref, mid_lo_ref, f_ref,
                         x_ref, y_ref, tmp_ref):
    r = FFT_R
    dot_first = lambda n2, slab: _dot_split(first_hi_ref[n2], first_lo_ref[n2], slab)
    dot_mid = lambda block: _dot_split(mid_hi_ref[...], mid_lo_ref[...], block)

    def write_fwd(k1, base, spec):
        tmp_ref[pl.ds(base, 2 * r), :] = spec

    _st_rows(x_ref, slice(None), hf_ref[0])
    _fft_long_forward(x_ref, y_ref, dot_first, dot_mid, write_fwd)

    def write_sum(k1, base, spec):
        prev = tmp_ref[pl.ds(base, 2 * r), :]
        f_ref[0, pl.ds(base, r), :] = prev[:r] + spec[:r]
        f_ref[0, pl.ds(base + r, r), :] = prev[r:] - spec[r:]

    _st_rows(x_ref, slice(None), hb_ref[0])
    _fft_long_forward(x_ref, y_ref, dot_first, dot_mid, write_sum)


def _hy_spec_long(n, hf, hb):
    first, mid, _, _ = _dft_tables_long()
    big = 2 * n
    lay = lambda rows: pl.BlockSpec((1, rows, W_C), lambda l, o: (l, 0, o))
    return pl.pallas_call(
        _hy_spec_long_kernel,
        grid=(DEPTH, HY_ORDER),
        in_specs=[lay(n), lay(n)] + [_const_spec(first.shape)] * 2 + [_const_spec(mid.shape)] * 2,
        out_specs=lay(2 * big),
        out_shape=jax.ShapeDtypeStruct((DEPTH, 2 * big, HY_W), F32),
        scratch_shapes=[_lane_split_scratch(n, W_C), _lane_split_scratch(2 * big, W_C),
                        pltpu.VMEM((2 * big, W_C), F32)],
        compiler_params=_cparams(2),
        name="hyena_spectrum_long",
    )(hf, hb, *_split_table(first), *_split_table(mid))


def _hy_conv_long_kernel(sig_ref, gate_ref, ws_ref, bias_ref, f_ref, first_ref, mid_ref, midinv_ref, last_ref, o_ref,
                         x_ref, y_ref, v_ref, out_ref, *, n, order):
    r = FFT_R
    to_bf16 = lambda x: x.astype(BF16)
    gate = _short_conv(gate_ref[...], ws_ref[:, (order + 1) * W_C : (order + 2) * W_C], n)
    sig = _short_conv(sig_ref[...], ws_ref[:, :W_C], n) if order == 0 else sig_ref[...]
    _st_rows(x_ref, slice(None), sig)

    def filter_and_invert(k1, base, spec):
        fr = f_ref[pl.ds(base, r), :]
        fi = f_ref[pl.ds(base + r, r), :]
        zr = spec[:r] * fr - spec[r:] * fi
        zi = spec[:r] * fi + spec[r:] * fr
        res = _dot(midinv_ref[...], jnp.concatenate([zr, zi], axis=0).astype(BF16))
        _st_transposed(v_ref, k1, res)

    dot_first = lambda n2, slab: _dot(first_ref[n2], to_bf16(slab))
    dot_mid = lambda block: _dot(mid_ref[...], to_bf16(block))
    _fft_long_forward(x_ref, y_ref, dot_first, dot_mid, filter_and_invert)

    def stage_last(n2, carry):
        base = pl.multiple_of(n2 * 2 * r, 2 * r)
        res = _dot(last_ref[n2], _ld_rows(v_ref, pl.ds(base, 2 * r)).astype(BF16))
        _st_rows(out_ref, pl.ds(n2, r // 2, stride=r), res)
        return carry

    lax.fori_loop(0, r, stage_last, 0, unroll=FFT_UNROLL)
    z = gate * (_ld_rows(out_ref, slice(None)) + sig * bias_ref[order : order + 1, :])
    o_ref[...] = z.astype(o_ref.dtype)


def _hy_conv_long(n_seq, n, row0, total_rows, pc, w_short, bias, spec, prev=None):
    assert 2 * n == FFT_R * FFT_R
    first, mid, mid_inv, last = _dft_tables_long()
    big = 2 * n
    bf = lambda m: jnp.asarray(m, BF16)
    lane_block = lambda j: pl.BlockSpec((n, W_C), lambda b: (row0 // n + b, j))
    z = None
    for order in range(HY_ORDER):
        final = order + 1 == HY_ORDER
        in_specs = [lane_block(0), lane_block(order + 1), _const_spec((SHORT_K, PC_W)),
                    _const_spec((HY_ORDER, W_C)), pl.BlockSpec((2 * big, W_C), lambda b, order=order: (0, order)),
                    _const_spec(first.shape), _const_spec(mid.shape), _const_spec(mid_inv.shape),
                    _const_spec(last.shape)]
        args = [pc if order == 0 else z, pc, w_short, bias, spec, bf(first), bf(mid), bf(mid_inv), bf(last)]
        body = functools.partial(_hy_conv_long_kernel, n=n, order=order)
        body, in_specs, args, aliases = _keep_rows_of(prev if final else None, body, in_specs, args)
        z = pl.pallas_call(
            body,
            grid=(n_seq,),
            in_specs=in_specs,
            out_specs=lane_block(0),
            out_shape=jax.ShapeDtypeStruct((total_rows, W_C), BF16 if final else F32),
            input_output_aliases=aliases,
            scratch_shapes=[_lane_split_scratch(n, W_C), _lane_split_scratch(2 * big, W_C),
                            _lane_split_scratch(2 * big, W_C), _lane_split_scratch(n, W_C)],
            compiler_params=_cparams(1),
            name="hyena_conv_long",
        )(*args)
    return z


def _pad_in_weights(w_in):
    gap = jnp.zeros((D_MODEL, PB_W - MLA_IN), w_in.dtype)
    return jnp.concatenate([w_in[:, : PA_W + MLA_IN], gap, w_in[:, PA_W + MLA_IN :]], axis=1).astype(BF16)


def kernel(x_prompt, x_sample, c, cache_mla, cache_diff_k, cache_diff_v, state_hgrn, c_ctx, w_mod, b_mod, norm_g,
           ffn_w_gu, ffn_w_down, w_in, w_out, hgrn_lb_logits, hgrn_onorm, mla_q_norm, mla_kv_norm, mla_w_uq,
           mla_w_ukv, mla_qk_norm, hy_short, hy_w1, hy_b1, hy_w2, hy_b2, hy_w3, hy_log_decay, hy_bias,
           diff_qk_norm, diff_lambda, diff_subln):
    bc, sc, _ = x_prompt.shape
    bl, sl, _ = x_sample.shape
    past = cache_mla.shape[2]
    tok = _Tokens(bc, sc, bl, sl)
    assert bl + 1 <= MOD_ROWS and tok.tc % sl == 0

    x = jnp.concatenate([x_prompt.reshape(tok.tc, D_MODEL), x_sample.reshape(tok.tl, D_MODEL)], axis=0)
    cond = jnp.concatenate([c_ctx[None], c, jnp.zeros((MOD_ROWS - 1 - bl, D_MODEL), F32)], axis=0)
    mod = _modulation(cond, w_mod, b_mod)

    filters = {n: _hy_filters(n, hy_w1, hy_b1, hy_w2, hy_b2, hy_w3, hy_log_decay) for n in (sc, sl)}
    spec_ctx = _hy_spec_short(sc, *filters[sc])
    spec_lat = _hy_spec_long(sl, *filters[sl])

    mla_tables = _rope_tables(tok, [NOPE_B], HEAD_PAD)
    diff_tables = _rope_tables(tok, list(range(0, H_D * 2 * DH_D, DH_D)), H_D * 2 * DH_D)

    new_mla, new_dk, new_dv, new_state = [], [], [], []
    for l in range(DEPTH):
        x = _ffn(tok, x, mod, l, 0, norm_g[l, 0], ffn_w_gu[l, 0].astype(BF16), ffn_w_down[l, 0].astype(BF16))
        pa, pb, pc, pd = _inproj(tok, x, mod, l, norm_g[l, 1], _pad_in_weights(w_in[l]))

        o_a, s_ctx = _hgrn(l, bc, sc, 0, tok.t, pa, hgrn_lb_logits, hgrn_onorm[l], None)
        o_a, _ = _hgrn(l, bl, sl, tok.tc, tok.t, pa, hgrn_lb_logits, hgrn_onorm[l], state_hgrn[:, l], prev=o_a)

        mla_w = _mla_weights(mla_w_uq[l], mla_w_ukv[l], mla_qk_norm[l])
        q_b, k_b, v_b, cache_b = _mla_prep(tok, pb, mla_q_norm[l], mla_kv_norm[l], mla_w, mla_tables)
        cache_rows = jnp.pad(cache_mla[:, l].reshape(bl * past, KV_LORA + ROPE_B),
                             ((0, 0), (0, 2 * LANES - KV_LORA - ROPE_B)))
        kc_b, vc_b = _mla_cache_prep(cache_rows, mla_w)
        o_b = _mla_attention(bc, sc, 0, tok.t, q_b, k_b, v_b)
        o_b = _mla_attention(bl, sl, tok.tc, tok.t, q_b, k_b, v_b, kc_b, vc_b, prev=o_b)

        o_c = _hy_conv_short(bc, sc, 0, tok.t, pc, hy_short[l], hy_bias[l], spec_ctx[l])
        o_c = _hy_conv_long(bl, sl, tok.tc, tok.t, pc, hy_short[l], hy_bias[l], spec_lat[l], prev=o_c)

        q_d, k_d, v_d, kcache_d = _diff_prep(tok, pd, diff_qk_norm[l], diff_tables)
        kc_d, vc_d = _diff_cache_prep(cache_diff_k[:, l].reshape(bl * past, H_D * 2 * DH_D),
                                      cache_diff_v[:, l].reshape(bl * past, H_D * DV_D))
        o_d = _diff_attention(l, bc, sc, 0, tok.t, diff_lambda[l], diff_subln[l], q_d, k_d, v_d)
        o_d = _diff_attention(l, bl, sl, tok.tc, tok.t, diff_lambda[l], diff_subln[l], q_d, k_d, v_d, kc_d, vc_d,
                              prev=o_d)

        x = _outproj(tok, x, mod, l, [o_a, o_b, o_c, o_d], w_out[l].astype(BF16))
        x = _ffn(tok, x, mod, l, 1, norm_g[l, 2], ffn_w_gu[l, 1].astype(BF16), ffn_w_down[l, 1].astype(BF16))

        new_mla.append(cache_b[: tok.tc, : KV_LORA + ROPE_B].reshape(bc, sc, KV_LORA + ROPE_B))
        new_dk.append(kcache_d[: tok.tc].reshape(bc, sc, H_D, 2, DH_D))
        new_dv.append(pd[: tok.tc, 2 * H_D * 2 * DH_D :].reshape(bc, sc, H_D, DV_D))
        new_state.append(s_ctx)

    y_prompt = x[: tok.tc].reshape(bc, sc, D_MODEL)
    y_sample = x[tok.tc :].reshape(bl, sl, D_MODEL)
    return (y_prompt, y_sample, jnp.stack(new_mla, axis=1), jnp.stack(new_dk, axis=1), jnp.stack(new_dv, axis=1),
            jnp.stack(new_state, axis=1))
```

```python
import functools
import math

import jax
import jax.numpy as jnp
import numpy as np
from jax import lax
from jax.experimental import pallas as pl
from jax.experimental.pallas import tpu as pltpu

F32 = jnp.float32
BF16 = jnp.bfloat16
HIGHEST = lax.Precision.HIGHEST

D_MODEL = 1024
DEPTH = 4
GRID_W = 64
N_MOD = 9
D_FF = 2816
EPS = 1e-6
ROPE_BASE = 10000.0
GROUP_W = 256
H_A, DK_A, DV_A = 4, 64, 64
MAX_INPUT_KEY = 1.0 - 1e-6
H_B, NOPE_B, ROPE_B, V_B = 4, 64, 32, 64
Q_LORA, KV_LORA = 256, 128
W_C, HY_ORDER, HY_BANDS, HY_FH, SHORT_K = 256, 2, 8, 64, 3
HY_EMB = 1 + 2 * HY_BANDS
H_D, DV_D, DH_D = 4, 64, 32

LANES = 128
MOD_ROWS = 16
TOKEN_TILE = 512
Q_TILE = 256
KEY_TILE = 512
HGRN_BLOCK = 128
VMEM_LIMIT = 56 * 1024 * 1024

PA_W, PB_W, PC_W, PD_W = 1280, 512, 768, 768
IN_PAD_W = PA_W + PB_W + PC_W + PD_W
MLA_IN = Q_LORA + KV_LORA + ROPE_B
HEAD_PAD = 128


def _cparams(n_axes):
    return pltpu.CompilerParams(dimension_semantics=("arbitrary",) * n_axes, vmem_limit_bytes=VMEM_LIMIT)


def _nt_dot(a, b):
    return lax.dot_general(a, b, (((1,), (1,)), ((), ())), preferred_element_type=F32)


def _dot(a, b):
    return jnp.dot(a, b, preferred_element_type=F32)


def _dot_exact_rhs(a, b_bf16, passes=3):
    out = None
    rem = a
    for _ in range(passes):
        piece = rem.astype(BF16)
        term = _dot(piece, b_bf16)
        out = term if out is None else out + term
        rem = rem - piece.astype(F32)
    return out


def _dot_exact_lhs(a_bf16, b, passes=3):
    out = None
    rem = b
    for _ in range(passes):
        piece = rem.astype(BF16)
        term = _dot(a_bf16, piece)
        out = term if out is None else out + term
        rem = rem - piece.astype(F32)
    return out


def _silu(x):
    return x * jax.nn.sigmoid(x)


def _adaln(x, g, sc, sh):
    y = x * lax.rsqrt(jnp.mean(x * x, axis=-1, keepdims=True) + EPS)
    return (y * g) * (1.0 + sc) + sh


def _rope(x, cos, sin_hi, sin_lo):
    w = x.shape[-1]
    return x * cos + pltpu.roll(x, 8, 1) * sin_hi + pltpu.roll(x, w - 8, 1) * sin_lo


def _mod_kernel(c_ref, w_ref, b_ref, o_ref):
    a = _silu(c_ref[...])
    o_ref[0] = jnp.dot(a, w_ref[0], preferred_element_type=F32, precision=HIGHEST) + b_ref[0]


def _modulation(cond, w_mod, b_mod):
    tn = D_MODEL
    n_col = N_MOD * D_MODEL
    out = pl.pallas_call(
        _mod_kernel,
        grid=(DEPTH, n_col // tn),
        in_specs=[
            pl.BlockSpec((MOD_ROWS, D_MODEL), lambda l, j: (0, 0)),
            pl.BlockSpec((1, D_MODEL, tn), lambda l, j: (l, 0, j)),
            pl.BlockSpec((1, 1, tn), lambda l, j: (l, 0, j)),
        ],
        out_specs=pl.BlockSpec((1, MOD_ROWS, tn), lambda l, j: (l, 0, j)),
        out_shape=jax.ShapeDtypeStruct((DEPTH, MOD_ROWS, n_col), F32),
        compiler_params=_cparams(2),
        name="modulation",
    )(cond, w_mod, b_mod.reshape(DEPTH, 1, n_col))
    return out.reshape(DEPTH * MOD_ROWS, N_MOD, D_MODEL)


class _Tokens:
    def __init__(self, n_ctx_seq, ctx_len, n_lat_seq, lat_len):
        self.bc, self.sc, self.bl, self.sl = n_ctx_seq, ctx_len, n_lat_seq, lat_len
        self.tc, self.tl = n_ctx_seq * ctx_len, n_lat_seq * lat_len
        self.t = self.tc + self.tl
        self.tm = min(TOKEN_TILE, lat_len)
        assert self.tc % self.tm == 0 and lat_len % self.tm == 0
        self.ctx_tiles = self.tc // self.tm
        self.lat_tiles_per_seq = lat_len // self.tm
        self.n_tiles = self.t // self.tm

    def mod_row(self, layer):
        def f(i):
            lat = 1 + (i - self.ctx_tiles) // self.lat_tiles_per_seq
            return layer * MOD_ROWS + jnp.where(i < self.ctx_tiles, 0, lat)

        return f

    def rope_block(self, i):
        return jnp.where(i < self.ctx_tiles, 0, 1 + (i - self.ctx_tiles) % self.lat_tiles_per_seq)


def _mod_spec(tok, layer):
    row = tok.mod_row(layer)
    return pl.BlockSpec((1, N_MOD, D_MODEL), lambda i: (row(i), 0, 0))


def _const_spec(shape):
    zeros = (0,) * len(shape)
    return pl.BlockSpec(shape, lambda *_: zeros)


def _row_spec(tm, width):
    return pl.BlockSpec((tm, width), lambda i: (i, 0))


def _keep_rows_of(prev, kernel_fn, in_specs, args):
    if prev is None:
        return kernel_fn, in_specs, args, {}
    idx = len(in_specs)

    def body(*refs):
        return kernel_fn(*refs[:idx], *refs[idx + 1 :])

    return body, in_specs + [pl.BlockSpec(memory_space=pl.ANY)], args + [prev], {idx: 0}


def _ffn_kernel(x_ref, mod_ref, g_ref, wgu_ref, wd_ref, *rest, mod_base):
    x = x_ref[...]
    if len(rest) > 1:
        oa_ref, ob_ref, oc_ref, od_ref, wo_ref = rest[:5]
        mixed = jnp.concatenate([oa_ref[...], ob_ref[...], oc_ref[...], od_ref[...]], axis=-1)
        x = x + mod_ref[0, 5:6, :] * _dot(mixed, wo_ref[...])
    o_ref = rest[-1]
    sh = mod_ref[0, mod_base : mod_base + 1, :]
    sc = mod_ref[0, mod_base + 1 : mod_base + 2, :]
    gate = mod_ref[0, mod_base + 2 : mod_base + 3, :]
    h = _adaln(x, g_ref[...], sc, sh).astype(BF16)
    au = _dot(h, wgu_ref[...])
    act = (_silu(au[:, :D_FF]) * au[:, D_FF:]).astype(BF16)
    o_ref[...] = x + (0.5 * gate) * _dot(act, wd_ref[...])


def _ffn(tok, x, mod, layer, which, norm_g, w_gu, w_down, mixer_outs=None, w_out=None):
    tm = tok.tm
    in_specs = [
        _row_spec(tm, D_MODEL),
        _mod_spec(tok, layer),
        _const_spec((1, D_MODEL)),
        pl.BlockSpec((D_MODEL, 2 * D_FF), lambda i: (0, 0), pipeline_mode=pl.Buffered(1)),
        pl.BlockSpec((D_FF, D_MODEL), lambda i: (0, 0), pipeline_mode=pl.Buffered(1)),
    ]
    args = [x, mod, norm_g.reshape(1, D_MODEL), w_gu, w_down]
    if mixer_outs is not None:
        in_specs += [_row_spec(tm, GROUP_W)] * 4
        in_specs.append(pl.BlockSpec((D_MODEL, D_MODEL), lambda i: (0, 0), pipeline_mode=pl.Buffered(1)))
        args += list(mixer_outs) + [w_out]
    return pl.pallas_call(
        functools.partial(_ffn_kernel, mod_base=6 * which),
        grid=(tok.n_tiles,),
        in_specs=in_specs,
        out_specs=_row_spec(tm, D_MODEL),
        out_shape=jax.ShapeDtypeStruct((tok.t, D_MODEL), F32),
        compiler_params=_cparams(1),
        name="ffn",
    )(*args)


def _inproj_kernel(x_ref, mod_ref, g_ref, w_ref, pa_ref, pb_ref, pc_ref, pd_ref):
    h = _adaln(x_ref[...], g_ref[...], mod_ref[0, 4:5, :], mod_ref[0, 3:4, :]).astype(BF16)
    p = _dot(h, w_ref[...])
    pa_ref[...] = p[:, :PA_W]
    pb_ref[...] = p[:, PA_W : PA_W + PB_W]
    pc_ref[...] = p[:, PA_W + PB_W : PA_W + PB_W + PC_W]
    pd_ref[...] = p[:, PA_W + PB_W + PC_W :]


def _inproj(tok, x, mod, layer, norm_g, w_in_pad):
    tm = tok.tm
    widths = (PA_W, PB_W, PC_W, PD_W)
    return pl.pallas_call(
        _inproj_kernel,
        grid=(tok.n_tiles,),
        in_specs=[
            _row_spec(tm, D_MODEL),
            _mod_spec(tok, layer),
            _const_spec((1, D_MODEL)),
            pl.BlockSpec((D_MODEL, IN_PAD_W), lambda i: (0, 0), pipeline_mode=pl.Buffered(1)),
        ],
        out_specs=[_row_spec(tm, w) for w in widths],
        out_shape=[jax.ShapeDtypeStruct((tok.t, w), F32) for w in widths],
        compiler_params=_cparams(1),
        name="inproj",
    )(x, mod, norm_g.reshape(1, D_MODEL), w_in_pad)


def _rope_group_tables(n_tok):
    t = np.arange(n_tok)
    pos = np.stack([t // GRID_W, t % GRID_W], axis=1).astype(np.float32)
    inv = (ROPE_BASE ** (-np.arange(8, dtype=np.float32) / 8)).astype(np.float32)
    lane = np.arange(32)
    ang = (pos[:, lane // 16] * inv[lane % 8][None, :]).astype(np.float32)
    second = (lane % 16) >= 8
    cos, sin = np.cos(ang), np.sin(ang)
    return cos, np.where(second[None], sin, 0.0), np.where(second[None], 0.0, -sin)


def _rope_tables(tok, lane_groups, width):
    cos = np.ones((tok.tm + tok.sl, width), np.float32)
    s_hi = np.zeros_like(cos)
    s_lo = np.zeros_like(cos)
    c, a, b = _rope_group_tables(tok.sl)
    for g in lane_groups:
        cos[tok.tm :, g : g + 32] = c
        s_hi[tok.tm :, g : g + 32] = a
        s_lo[tok.tm :, g : g + 32] = b
    return jnp.asarray(cos), jnp.asarray(s_hi), jnp.asarray(s_lo)


def _mla_keys_values(kv_in, wk_ref, wv_ref, kn_ref, rope):
    kraw = _dot(kv_in.astype(BF16), wk_ref[...])
    v = _dot(kv_in[:, :KV_LORA].astype(BF16), wv_ref[...])
    ks = []
    for h in range(H_B):
        kh = kraw[:, h * HEAD_PAD : (h + 1) * HEAD_PAD]
        ss = jnp.sum(kh * kh, axis=-1, keepdims=True) * (1.0 / (NOPE_B + ROPE_B))
        kh = kh * lax.rsqrt(ss + EPS) * kn_ref[...]
        if rope is not None:
            kh = _rope(kh, *rope)
        ks.append(kh.astype(BF16))
    return ks, v.astype(BF16)


def _mla_prep_kernel(pb_ref, gq_ref, gkv_ref, wuq_ref, wk_ref, wv_ref, qn_ref, kn_ref, cos_ref, shi_ref, slo_ref,
                     q_ref, k_ref, v_ref, cache_ref):
    pb = pb_ref[...]
    rope = (cos_ref[...], shi_ref[...], slo_ref[...])
    cq = pb[:, :Q_LORA]
    cq = cq * lax.rsqrt(jnp.mean(cq * cq, axis=-1, keepdims=True) + EPS) * gq_ref[...]
    qraw = _dot(cq.astype(BF16), wuq_ref[...])
    scale = (NOPE_B + ROPE_B) ** -0.5 * LOG2E
    for h in range(H_B):
        qh = qraw[:, h * HEAD_PAD : (h + 1) * HEAD_PAD]
        ss = jnp.sum(qh * qh, axis=-1, keepdims=True) * (1.0 / (NOPE_B + ROPE_B))
        qh = _rope(qh * lax.rsqrt(ss + EPS) * qn_ref[...], *rope)
        q_ref[:, h * HEAD_PAD : (h + 1) * HEAD_PAD] = (qh * scale).astype(BF16)
    ckv = pb[:, Q_LORA : Q_LORA + KV_LORA]
    ckv = ckv * lax.rsqrt(jnp.mean(ckv * ckv, axis=-1, keepdims=True) + EPS) * gkv_ref[...]
    kv_in = jnp.concatenate([ckv, pb[:, Q_LORA + KV_LORA :]], axis=-1)
    cache_ref[...] = kv_in
    ks, v = _mla_keys_values(kv_in, wk_ref, wv_ref, kn_ref, rope)
    for h in range(H_B):
        k_ref[:, h * HEAD_PAD : (h + 1) * HEAD_PAD] = ks[h]
    v_ref[...] = v


def _mla_cache_kernel(c_ref, wk_ref, wv_ref, kn_ref, k_ref, v_ref):
    ks, v = _mla_keys_values(c_ref[...], wk_ref, wv_ref, kn_ref, None)
    for h in range(H_B):
        k_ref[:, h * HEAD_PAD : (h + 1) * HEAD_PAD] = ks[h]
    v_ref[...] = v


def _mla_weights(w_uq, w_ukv, qk_norm):
    wuq = jnp.pad(w_uq.reshape(Q_LORA, H_B, NOPE_B + ROPE_B), ((0, 0), (0, 0), (0, HEAD_PAD - NOPE_B - ROPE_B)))
    wuq = wuq.reshape(Q_LORA, H_B * HEAD_PAD).astype(BF16)
    ukv = w_ukv.reshape(KV_LORA, H_B, NOPE_B + V_B)
    wk_nope = jnp.pad(ukv[:, :, :NOPE_B], ((0, 0), (0, 0), (0, HEAD_PAD - NOPE_B)))
    place = np.zeros((2 * LANES - KV_LORA, H_B, HEAD_PAD), np.float32)
    for h in range(H_B):
        place[np.arange(ROPE_B), h, NOPE_B + np.arange(ROPE_B)] = 1.0
    wk = jnp.concatenate([wk_nope, jnp.asarray(place)], axis=0).reshape(2 * LANES, H_B * HEAD_PAD).astype(BF16)
    wv = ukv[:, :, NOPE_B:].reshape(KV_LORA, H_B * V_B).astype(BF16)
    pad = lambda g: jnp.pad(g, (0, HEAD_PAD - NOPE_B - ROPE_B)).reshape(1, HEAD_PAD)
    return wuq, wk, wv, pad(qk_norm[0]), pad(qk_norm[1])


def _mla_prep(tok, pb, gq, gkv, weights, tables):
    wuq, wk, wv, qn, kn = weights
    tm = tok.tm
    rope_spec = pl.BlockSpec((tm, HEAD_PAD), lambda i: (tok.rope_block(i), 0))
    return pl.pallas_call(
        _mla_prep_kernel,
        grid=(tok.n_tiles,),
        in_specs=[
            _row_spec(tm, PB_W),
            _const_spec((1, Q_LORA)),
            _const_spec((1, KV_LORA)),
            _const_spec(wuq.shape),
            _const_spec(wk.shape),
            _const_spec(wv.shape),
            _const_spec((1, HEAD_PAD)),
            _const_spec((1, HEAD_PAD)),
            rope_spec,
            rope_spec,
            rope_spec,
        ],
        out_specs=[_row_spec(tm, H_B * HEAD_PAD), _row_spec(tm, H_B * HEAD_PAD), _row_spec(tm, H_B * V_B),
                   _row_spec(tm, 2 * LANES)],
        out_shape=[
            jax.ShapeDtypeStruct((tok.t, H_B * HEAD_PAD), BF16),
            jax.ShapeDtypeStruct((tok.t, H_B * HEAD_PAD), BF16),
            jax.ShapeDtypeStruct((tok.t, H_B * V_B), BF16),
            jax.ShapeDtypeStruct((tok.t, 2 * LANES), F32),
        ],
        compiler_params=_cparams(1),
        name="mla_prep",
    )(pb, gq.reshape(1, -1), gkv.reshape(1, -1), wuq, wk, wv, qn, kn, *tables)


def _mla_cache_prep(cache_rows, weights):
    _, wk, wv, _, kn = weights
    rows = cache_rows.shape[0]
    tm = min(TOKEN_TILE, rows)
    return pl.pallas_call(
        _mla_cache_kernel,
        grid=(rows // tm,),
        in_specs=[_row_spec(tm, 2 * LANES), _const_spec(wk.shape), _const_spec(wv.shape), _const_spec((1, HEAD_PAD))],
        out_specs=[_row_spec(tm, H_B * HEAD_PAD), _row_spec(tm, H_B * V_B)],
        out_shape=[jax.ShapeDtypeStruct((rows, H_B * HEAD_PAD), BF16), jax.ShapeDtypeStruct((rows, H_B * V_B), BF16)],
        compiler_params=_cparams(1),
        name="mla_cache_prep",
    )(cache_rows, wk, wv, kn)


LOG2E = math.log2(math.e)


def _softmax_parts(scores):
    m = functools.reduce(jnp.maximum, [jnp.max(s, axis=-1, keepdims=True) for s in scores])
    ps = [jnp.exp2(s - m) for s in scores]
    denom = functools.reduce(jnp.add, [jnp.sum(p, axis=-1, keepdims=True) for p in ps])
    return ps, denom


def _head_lane_mask(width, head, head_w):
    lane = lax.broadcasted_iota(jnp.int32, (1, width), 1)
    return (lane >= head * head_w) & (lane < (head + 1) * head_w)


def _mla_attn_kernel(*refs, n_kv):
    q_ref = refs[0]
    k_refs = refs[1 : 1 + n_kv]
    v_refs = refs[1 + n_kv : 1 + 2 * n_kv]
    o_ref = refs[1 + 2 * n_kv]
    tiles = [(k, v, j) for k, v in zip(k_refs, v_refs) for j in range(0, k.shape[0], KEY_TILE)]
    out = None
    for h in range(H_B):
        sl = slice(h * HEAD_PAD, (h + 1) * HEAD_PAD)
        qh = q_ref[:, sl]
        mask = _head_lane_mask(H_B * V_B, h, V_B)
        m = denom = acc = None
        for k, v, j in tiles:
            rows = slice(j, min(j + KEY_TILE, k.shape[0]))
            s = _nt_dot(qh, k[rows, sl])
            vm = jnp.where(mask, v[rows, :], jnp.zeros((), BF16))
            s_max = jnp.max(s, axis=-1, keepdims=True)
            if m is None:
                m = s_max
                p = jnp.exp2(s - m)
                denom = jnp.sum(p, axis=-1, keepdims=True)
                acc = _dot(p.astype(BF16), vm)
            else:
                m_new = jnp.maximum(m, s_max)
                alpha = jnp.exp2(m - m_new)
                p = jnp.exp2(s - m_new)
                denom = denom * alpha + jnp.sum(p, axis=-1, keepdims=True)
                acc = acc * alpha + _dot(p.astype(BF16), vm)
                m = m_new
        acc = acc * (1.0 / denom)
        out = acc if out is None else out + acc
    o_ref[...] = out.astype(BF16)


def _seq_specs(n_seq, seq_len, row0, tq, q_width, kv_widths, kv_new, kv_cache):
    q_blocks = seq_len // tq
    q_spec = lambda w: pl.BlockSpec((tq, w), lambda b, i: (row0 // tq + b * q_blocks + i, 0))
    new_spec = lambda w: pl.BlockSpec((seq_len, w), lambda b, i: (row0 // seq_len + b, 0))
    specs = [q_spec(q_width)]
    for w in kv_widths:
        if kv_cache is not None:
            specs.append(pl.BlockSpec((kv_cache, w), lambda b, i: (b, 0)))
        specs.append(new_spec(w))
    return specs, q_spec


def _mla_attention(n_seq, seq_len, row0, total_rows, q, k, v, k_cache=None, v_cache=None, prev=None):
    tq = min(Q_TILE, seq_len)
    has_cache = k_cache is not None
    cache_len = k_cache.shape[0] // n_seq if has_cache else None
    specs, q_spec = _seq_specs(n_seq, seq_len, row0, tq, H_B * HEAD_PAD, (H_B * HEAD_PAD, H_B * V_B), True, cache_len)
    args = [q] + ([k_cache, k] if has_cache else [k]) + ([v_cache, v] if has_cache else [v])
    body = functools.partial(_mla_attn_kernel, n_kv=2 if has_cache else 1)
    body, specs, args, aliases = _keep_rows_of(prev, body, specs, args)
    return pl.pallas_call(
        body,
        grid=(n_seq, seq_len // tq),
        in_specs=specs,
        out_specs=q_spec(H_B * V_B),
        out_shape=jax.ShapeDtypeStruct((total_rows, H_B * V_B), BF16),
        input_output_aliases=aliases,
        compiler_params=_cparams(2),
        name="mla_attention",
    )(*args)


DQK_W = H_D * 2 * HEAD_PAD


def _diff_place_matrix():
    m = np.zeros((H_D * 2 * DH_D, DQK_W), np.float32)
    g = np.arange(H_D * 2 * DH_D)
    m[g, (g // DH_D) * HEAD_PAD + g % DH_D] = 1.0
    return jnp.asarray(m, BF16)


def _group_mean_matrix(width, group):
    g = np.arange(width)
    return jnp.asarray((g[:, None] // group == g[None, :] // group).astype(np.float32), BF16)


def _diff_prep_kernel(pd_ref, gq_ref, gk_ref, gm_ref, place_ref, cos_ref, shi_ref, slo_ref,
                      q_ref, k_ref, v_ref, kcache_ref):
    pd = pd_ref[...]
    rope = (cos_ref[...], shi_ref[...], slo_ref[...])
    w = H_D * 2 * DH_D

    def norm(x, g):
        ms = _dot_exact_rhs(x * x, gm_ref[...], passes=2) * (1.0 / DH_D)
        return x * lax.rsqrt(ms + EPS) * g

    qn = _rope(norm(pd[:, :w], gq_ref[...]), *rope) * (DH_D ** -0.5 * LOG2E)
    kn = norm(pd[:, w : 2 * w], gk_ref[...])
    kcache_ref[...] = kn
    q_ref[...] = _dot(qn.astype(BF16), place_ref[...]).astype(BF16)
    k_ref[...] = _dot(_rope(kn, *rope).astype(BF16), place_ref[...]).astype(BF16)
    v_ref[...] = pd[:, 2 * w :].astype(BF16)


def _diff_cache_kernel(k_in_ref, v_in_ref, place_ref, k_ref, v_ref):
    k_ref[...] = _dot(k_in_ref[...].astype(BF16), place_ref[...]).astype(BF16)
    v_ref[...] = v_in_ref[...].astype(BF16)


def _diff_prep(tok, pd, qk_norm, tables):
    tm = tok.tm
    w = H_D * 2 * DH_D
    rope_spec = pl.BlockSpec((tm, w), lambda i: (tok.rope_block(i), 0))
    tile_g = lambda g: jnp.tile(g, H_D * 2).reshape(1, w)
    return pl.pallas_call(
        _diff_prep_kernel,
        grid=(tok.n_tiles,),
        in_specs=[_row_spec(tm, PD_W), _const_spec((1, w)), _const_spec((1, w)), _const_spec((w, w)),
                  _const_spec((w, DQK_W)), rope_spec, rope_spec, rope_spec],
        out_specs=[_row_spec(tm, DQK_W), _row_spec(tm, DQK_W), _row_spec(tm, H_D * DV_D), _row_spec(tm, w)],
        out_shape=[
            jax.ShapeDtypeStruct((tok.t, DQK_W), BF16),
            jax.ShapeDtypeStruct((tok.t, DQK_W), BF16),
            jax.ShapeDtypeStruct((tok.t, H_D * DV_D), BF16),
            jax.ShapeDtypeStruct((tok.t, w), F32),
        ],
        compiler_params=_cparams(1),
        name="diff_prep",
    )(pd, tile_g(qk_norm[0]), tile_g(qk_norm[1]), _group_mean_matrix(w, DH_D), _diff_place_matrix(), *tables)


def _diff_cache_prep(k_rows, v_rows):
    rows = k_rows.shape[0]
    tm = min(TOKEN_TILE, rows)
    w = H_D * 2 * DH_D
    return pl.pallas_call(
        _diff_cache_kernel,
        grid=(rows // tm,),
        in_specs=[_row_spec(tm, w), _row_spec(tm, H_D * DV_D), _const_spec((w, DQK_W))],
        out_specs=[_row_spec(tm, DQK_W), _row_spec(tm, H_D * DV_D)],
        out_shape=[jax.ShapeDtypeStruct((rows, DQK_W), BF16), jax.ShapeDtypeStruct((rows, H_D * DV_D), BF16)],
        compiler_params=_cparams(1),
        name="diff_cache_prep",
    )(k_rows, v_rows, _diff_place_matrix())


def _diff_attn_kernel(*refs, n_kv, lam_init):
    q_ref, lam_ref, sub_ref = refs[0], refs[1], refs[2]
    k_refs = refs[3 : 3 + n_kv]
    v_refs = refs[3 + n_kv : 3 + 2 * n_kv]
    o_ref = refs[3 + 2 * n_kv]
    dl = lam_ref[...]
    lam = (jnp.exp(jnp.sum(dl[0:1] * dl[1:2], axis=-1, keepdims=True))
           - jnp.exp(jnp.sum(dl[2:3] * dl[3:4], axis=-1, keepdims=True)) + lam_init)
    out = None
    for h in range(H_D):
        probs = []
        for m in range(2):
            sl = slice((2 * h + m) * HEAD_PAD, (2 * h + m + 1) * HEAD_PAD)
            qh = q_ref[:, sl]
            ps, denom = _softmax_parts([_nt_dot(qh, k[:, sl]) for k in k_refs])
            probs.append((ps, (1.0 if m == 0 else lam) / denom))
        mask = _head_lane_mask(H_D * DV_D, h, DV_D)
        (ps0, a0), (ps1, a1) = probs
        acc = None
        for p0, p1, v in zip(ps0, ps1, v_refs):
            wgt = (p0 * a0 - p1 * a1).astype(BF16)
            term = _dot(wgt, jnp.where(mask, v[...], jnp.zeros((), BF16)))
            acc = term if acc is None else acc + term
        ms = jnp.sum(acc * acc, axis=-1, keepdims=True) * (1.0 / DV_D)
        acc = acc * lax.rsqrt(ms + EPS)
        out = acc if out is None else out + acc
    o_ref[...] = (out * sub_ref[...] * (1.0 - lam_init)).astype(BF16)


def _diff_attention(layer, n_seq, seq_len, row0, total_rows, lam_p, sub_g, q, k, v, k_cache=None, v_cache=None,
                    prev=None):
    tq = min(Q_TILE, seq_len)
    has_cache = k_cache is not None
    cache_len = k_cache.shape[0] // n_seq if has_cache else None
    specs, q_spec = _seq_specs(n_seq, seq_len, row0, tq, DQK_W, (DQK_W, H_D * DV_D), True, cache_len)
    specs = [specs[0], pl.BlockSpec((4, DH_D), lambda b, i: (0, 0)), pl.BlockSpec((1, H_D * DV_D), lambda b, i: (0, 0))
             ] + specs[1:]
    args = [q, lam_p, jnp.tile(sub_g, H_D).reshape(1, H_D * DV_D)]
    args += ([k_cache, k] if has_cache else [k]) + ([v_cache, v] if has_cache else [v])
    lam_init = 0.8 - 0.6 * math.exp(-0.3 * layer)
    body = functools.partial(_diff_attn_kernel, n_kv=2 if has_cache else 1, lam_init=lam_init)
    body, specs, args, aliases = _keep_rows_of(prev, body, specs, args)
    return pl.pallas_call(
        body,
        grid=(n_seq, seq_len // tq),
        in_specs=specs,
        out_specs=q_spec(H_D * DV_D),
        out_shape=jax.ShapeDtypeStruct((total_rows, H_D * DV_D), BF16),
        input_output_aliases=aliases,
        compiler_params=_cparams(2),
        name="diff_attention",
    )(*args)


A_W = H_A * DK_A


HGRN_LEVELS = (2, 4, 8, 16, 32, 64, 128)
PAIR_W = 2 * DK_A


def _hgrn_constants():
    t = np.arange(HGRN_BLOCK)
    lower = t[None, :] <= t[:, None]
    upper = t[None, :] >= t[:, None]
    masks = [t[:, None] == t[None, :]] + [(t[:, None] // b) == (t[None, :] // b) for b in HGRN_LEVELS[:-1]]
    masks = np.stack([np.tile(m, (1, 2)) for m in masks]).astype(np.float32)
    g = np.arange(A_W)
    heads = (g[:, None] // DK_A) == (g[None, :] // DK_A)
    as_bf16 = lambda m: jnp.asarray(m.astype(np.float32), BF16)
    pair_mask = heads[:PAIR_W, :PAIR_W].astype(np.float32)
    return as_bf16(lower), as_bf16(upper), as_bf16(heads), jnp.asarray(pair_mask), jnp.asarray(masks)


def _level_reference(cum, b, forward):
    off = b // 2 - 1 if forward else b // 2
    if b >= 8:
        c3 = cum.reshape(HGRN_BLOCK // b, b, A_W)
        return jnp.broadcast_to(c3[:, off : off + 1, :], c3.shape).reshape(HGRN_BLOCK, A_W)
    c3 = cum.reshape(HGRN_BLOCK // 8, 8, A_W)
    sub = lax.broadcasted_iota(jnp.int32, (1, 8, 1), 1)
    out = None
    for g in range(8 // b):
        cand = jnp.broadcast_to(c3[:, g * b + off : g * b + off + 1, :], c3.shape)
        out = cand if out is None else jnp.where(sub >= g * b, cand, out)
    return out.reshape(HGRN_BLOCK, A_W)


def _hgrn_kernel(*refs, layer, seq_len, has_state):
    if has_state:
        pa_ref, lbl_ref, og_ref, s0_ref, lower_ref, upper_ref, heads_ref, hmask_ref, lvl_ref = refs[:9]
        rest = refs[9:]
    else:
        pa_ref, lbl_ref, og_ref, lower_ref, upper_ref, heads_ref, hmask_ref, lvl_ref = refs[:8]
        s0_ref = None
        rest = refs[8:]
    o_ref, sout_ref, st_ref, oacc_ref = rest
    n_blocks = seq_len // HGRN_BLOCK
    row = lax.broadcasted_iota(jnp.int32, (HGRN_BLOCK, 1), 0)
    first_head = lax.broadcasted_iota(jnp.int32, (1, PAIR_W), 1) < DK_A
    zero_bf16 = jnp.zeros((), BF16)

    def per_head_rows(x):
        return jnp.concatenate([jnp.where(first_head, x, zero_bf16), jnp.where(first_head, zero_bf16, x)], axis=0)

    st_ref[...] = jnp.zeros(st_ref.shape, F32)
    lower_bounds = []
    for d in range(2):
        logits = lbl_ref[d]
        e = jnp.exp(logits - jnp.max(logits, axis=0, keepdims=True))
        p = e / jnp.sum(e, axis=0, keepdims=True)
        lb = jnp.zeros((1, A_W), F32)
        for j in range(1, layer + 1):
            lb = lb + p[j : j + 1, :]
        lower_bounds.append(lb)
        if has_state:
            for h in range(H_A):
                off = (h % 2) * DK_A
                st_ref[d, h // 2, off : off + DV_A, off : off + DK_A] = s0_ref[0, d, h].T

    def both_directions(i, carry):
        for d in range(2):
            lb = lower_bounds[d]
            tri_ref = lower_ref if d == 0 else upper_ref
            blk = i if d == 0 else n_blocks - 1 - i
            r0 = pl.multiple_of(blk * HGRN_BLOCK, HGRN_BLOCK)
            rows = pl.ds(r0, HGRN_BLOCK)
            q = _silu(pa_ref[rows, 0:A_W]) * DK_A ** -0.5
            v = pa_ref[rows, A_W : 2 * A_W]
            logit = pa_ref[rows, (2 + d) * A_W : (3 + d) * A_W]
            key = jnp.minimum((1.0 - lb) * jax.nn.sigmoid(-logit), MAX_INPUT_KEY)
            log_f = jnp.log1p(-key)
            cum = _dot_exact_lhs(tri_ref[...], log_f)
            tot = cum[HGRN_BLOCK - 1 : HGRN_BLOCK, :] if d == 0 else cum[0:1, :]
            v_b = v.astype(BF16)
            scores = [None] * (H_A // 2)

            def add_pairs(qd, kd, mask_index, scores=scores):
                qd_b, kd_b = qd.astype(BF16), kd.astype(BF16)
                for pi in range(H_A // 2):
                    lanes = slice(pi * PAIR_W, (pi + 1) * PAIR_W)
                    s = _nt_dot(qd_b[:, lanes], per_head_rows(kd_b[:, lanes]))
                    if mask_index is not None:
                        s = s * lvl_ref[mask_index]
                    scores[pi] = s if scores[pi] is None else scores[pi] + s

            add_pairs(q, key, 0)
            for li, b in enumerate(HGRN_LEVELS):
                later = ((row % b) >= b // 2) if d == 0 else ((row % b) < b // 2)
                if b == 2:
                    qd = jnp.where(later, q * jnp.exp(log_f), 0.0)
                    kd = jnp.where(later, 0.0, key)
                else:
                    ref = _level_reference(cum, b, d == 0)
                    qd = jnp.where(later, q * jnp.exp(cum - ref), 0.0)
                    kd = jnp.where(later, 0.0, key * jnp.exp(ref - cum))
                add_pairs(qd, kd, li + 1 if b < HGRN_BLOCK else None)
            q_dec = (q * jnp.exp(cum)).astype(BF16)
            k_dec = (key * jnp.exp(tot - cum)).astype(BF16)
            decay = jnp.exp(tot)
            parts = []
            for pi in range(H_A // 2):
                lanes = slice(pi * PAIR_W, (pi + 1) * PAIR_W)
                state = st_ref[d, pi]
                o_pair = _dot(scores[pi].astype(BF16), per_head_rows(v_b[:, lanes]))
                parts.append(o_pair + _nt_dot(q_dec[:, lanes], state.astype(BF16)))
                upd = lax.dot_general(v_b[:, lanes], k_dec[:, lanes], (((0,), (0,)), ((), ())),
                                      preferred_element_type=F32)
                st_ref[d, pi] = state * decay[:, lanes] + upd * hmask_ref[...]
            oacc_ref[d, rows, :] = jnp.concatenate(parts, axis=1)
        return carry

    lax.fori_loop(0, n_blocks, both_directions, 0)
    for d in range(2):
        for h in range(H_A):
            off = (h % 2) * DK_A
            sout_ref[0, d, h] = st_ref[d, h // 2, off : off + DV_A, off : off + DK_A].T

    o = oacc_ref[0] + oacc_ref[1]
    ms = _dot_exact_rhs(o * o, heads_ref[...], passes=2) * (1.0 / DV_A)
    o_ref[...] = (o * lax.rsqrt(ms + EPS) * og_ref[...] * _silu(pa_ref[:, 4 * A_W : 5 * A_W])).astype(BF16)


def _hgrn(layer, n_seq, seq_len, row0, total_rows, pa, lb_logits, onorm_g, s0, prev=None):
    has_state = s0 is not None
    consts = _hgrn_constants()
    seq_spec = lambda w: pl.BlockSpec((seq_len, w), lambda b: (row0 // seq_len + b, 0))
    state_spec = pl.BlockSpec((1, 2, H_A, DK_A, DV_A), lambda b: (b, 0, 0, 0, 0))
    in_specs = [seq_spec(PA_W), _const_spec(lb_logits.shape), _const_spec((1, A_W))]
    args = [pa, lb_logits, jnp.tile(onorm_g, H_A).reshape(1, A_W)]
    if has_state:
        in_specs.append(state_spec)
        args.append(s0)
    in_specs += [_const_spec(c.shape) for c in consts]
    args += list(consts)
    body = functools.partial(_hgrn_kernel, layer=layer, seq_len=seq_len, has_state=has_state)
    body, in_specs, args, aliases = _keep_rows_of(prev, body, in_specs, args)
    return pl.pallas_call(
        body,
        grid=(n_seq,),
        in_specs=in_specs,
        out_specs=[seq_spec(A_W), state_spec],
        out_shape=[
            jax.ShapeDtypeStruct((total_rows, A_W), BF16),
            jax.ShapeDtypeStruct((n_seq, 2, H_A, DK_A, DV_A), F32),
        ],
        input_output_aliases=aliases,
        scratch_shapes=[pltpu.VMEM((2, H_A // 2, 2 * DV_A, 2 * DK_A), F32), pltpu.VMEM((2, seq_len, A_W), F32)],
        compiler_params=_cparams(1),
        name="hgrn",
    )(*args)


HY_W = HY_ORDER * W_C
FFT_R = 64


def _hy_filter_kernel(w1_ref, b1_ref, w2_ref, b2_ref, w3_ref, ld_ref, hf_ref, hb_ref, *, n):
    f32dot = functools.partial(jnp.dot, preferred_element_type=F32, precision=HIGHEST)
    row = lax.broadcasted_iota(jnp.int32, (n, LANES), 0)
    lane = lax.broadcasted_iota(jnp.int32, (n, LANES), 1)
    tn = row.astype(F32) / n
    band = jnp.where(lane <= HY_BANDS, lane, lane - HY_BANDS).astype(F32)
    ang = (2.0 * math.pi) * tn * band
    feats = jnp.where(lane == 0, tn, jnp.where(lane <= HY_BANDS, jnp.cos(ang),
                                                jnp.where(lane <= 2 * HY_BANDS, jnp.sin(ang), 0.0)))
    h = jnp.sin(f32dot(feats, w1_ref[0]) + b1_ref[0])
    h = jnp.sin(f32dot(h, w2_ref[0]) + b2_ref[0])
    h = f32dot(h, w3_ref[0])
    h = h * jnp.exp(-jnp.exp(ld_ref[0]) * tn[:, 0:1])
    hf = h[:, :HY_W]
    hb = jnp.where(row[:, 0:1] == 0, 0.0, h[:, HY_W:])
    norm = jnp.sum(jnp.abs(hf), axis=0, keepdims=True) + jnp.sum(jnp.abs(hb), axis=0, keepdims=True) + EPS
    hf_ref[0] = hf / norm
    hb_ref[0] = hb / norm


def _hy_filters(n, w1, b1, w2, b2, w3, log_decay):
    w1p = jnp.pad(w1, ((0, 0), (0, LANES - HY_EMB), (0, 0)))
    lay = lambda shape: pl.BlockSpec((1,) + shape, lambda l: (l,) + (0,) * len(shape))
    return pl.pallas_call(
        functools.partial(_hy_filter_kernel, n=n),
        grid=(DEPTH,),
        in_specs=[lay((LANES, HY_FH)), lay((1, HY_FH)), lay((HY_FH, HY_FH)), lay((1, HY_FH)),
                  lay((HY_FH, 2 * HY_W)), lay((1, 2 * HY_W))],
        out_specs=[lay((n, HY_W)), lay((n, HY_W))],
        out_shape=[jax.ShapeDtypeStruct((DEPTH, n, HY_W), F32)] * 2,
        compiler_params=_cparams(1),
        name="hyena_filters",
    )(w1p, b1.reshape(DEPTH, 1, HY_FH), w2, b2.reshape(DEPTH, 1, HY_FH), w3, log_decay.reshape(DEPTH, 1, 2 * HY_W))


def _dft_tables_short(n):
    big = 2 * n
    k = np.arange(big)[:, None]
    t = np.arange(n)[None, :]
    ang = 2.0 * np.pi * ((k * t) % big) / big
    fwd = np.concatenate([np.cos(ang), -np.sin(ang)], axis=0)
    inv = np.concatenate([np.cos(ang).T, -np.sin(ang).T], axis=1) / big
    return fwd.astype(np.float32), inv.astype(np.float32)


def _hy_spec_short_kernel(hf_ref, hb_ref, fwd_ref, f_ref, *, n):
    f32dot = functools.partial(jnp.dot, preferred_element_type=F32, precision=HIGHEST)
    xf = f32dot(fwd_ref[...], hf_ref[0])
    xb = f32dot(fwd_ref[...], hb_ref[0])
    big = 2 * n
    f_ref[0, :big, :] = xf[:big] + xb[:big]
    f_ref[0, big:, :] = xf[big:] - xb[big:]


def _hy_spec_short(n, hf, hb):
    fwd, _ = _dft_tables_short(n)
    lay = lambda shape: pl.BlockSpec((1,) + shape, lambda l: (l,) + (0,) * len(shape))
    return pl.pallas_call(
        functools.partial(_hy_spec_short_kernel, n=n),
        grid=(DEPTH,),
        in_specs=[lay((n, HY_W)), lay((n, HY_W)), _const_spec(fwd.shape)],
        out_specs=lay((4 * n, HY_W)),
        out_shape=jax.ShapeDtypeStruct((DEPTH, 4 * n, HY_W), F32),
        compiler_params=_cparams(1),
        name="hyena_spectrum_short",
    )(hf, hb, jnp.asarray(fwd))


def _short_conv(x, w, n):
    row = lax.broadcasted_iota(jnp.int32, (n, 1), 0)
    prev = jnp.where(row == 0, 0.0, pltpu.roll(x, 1, 0))
    nxt = jnp.where(row == n - 1, 0.0, pltpu.roll(x, n - 1, 0))
    return prev * w[0:1, :] + x * w[1:2, :] + nxt * w[2:3, :]


def _hy_conv_short_kernel(pc_ref, ws_ref, bias_ref, f_ref, fwd_ref, inv_ref, o_ref, *, n):
    big = 2 * n
    u = _short_conv(pc_ref[...], ws_ref[...], n)
    v, x1, x2 = u[:, :W_C], u[:, W_C : 2 * W_C], u[:, 2 * W_C :]

    def conv(x, order):
        spec = _dot(fwd_ref[...], x.astype(BF16))
        fr = f_ref[:big, order * W_C : (order + 1) * W_C]
        fi = f_ref[big:, order * W_C : (order + 1) * W_C]
        zr = spec[:big] * fr - spec[big:] * fi
        zi = spec[:big] * fi + spec[big:] * fr
        return _dot(inv_ref[...], jnp.concatenate([zr, zi], axis=0).astype(BF16))

    z = x1 * (conv(v, 0) + v * bias_ref[0:1, :])
    z = x2 * (conv(z, 1) + z * bias_ref[1:2, :])
    o_ref[...] = z.astype(BF16)


def _hy_conv_short(n_seq, n, row0, total_rows, pc, w_short, bias, spec):
    fwd, inv = _dft_tables_short(n)
    seq_spec = lambda w: pl.BlockSpec((n, w), lambda b: (row0 // n + b, 0))
    return pl.pallas_call(
        functools.partial(_hy_conv_short_kernel, n=n),
        grid=(n_seq,),
        in_specs=[seq_spec(PC_W), _const_spec((SHORT_K, PC_W)), _const_spec((HY_ORDER, W_C)),
                  _const_spec((4 * n, HY_W)), _const_spec(fwd.shape), _const_spec(inv.shape)],
        out_specs=seq_spec(W_C),
        out_shape=jax.ShapeDtypeStruct((total_rows, W_C), BF16),
        compiler_params=_cparams(1),
        name="hyena_conv_short",
    )(pc, w_short, bias, spec, jnp.asarray(fwd, BF16), jnp.asarray(inv, BF16))


def _dft_tables_long():
    r = FFT_R
    big = r * r
    half = r // 2
    n2 = np.arange(r)[:, None, None]
    k1 = np.arange(r)[None, :, None]
    n1 = np.arange(half)[None, None, :]
    ang = 2.0 * np.pi * ((k1 * (r * n1 + n2)) % big) / big
    first = np.concatenate([np.cos(ang), -np.sin(ang)], axis=1)
    last = np.concatenate([np.cos(ang), -np.sin(ang)], axis=1).transpose(0, 2, 1) / big
    a = np.arange(r)
    ang_r = 2.0 * np.pi * ((a[:, None] * a[None, :]) % r) / r
    c, s = np.cos(ang_r), np.sin(ang_r)
    mid = np.block([[c, s], [-s, c]])
    mid_inv = np.block([[c, -s], [s, c]])
    f32 = lambda m: m.astype(np.float32)
    return f32(first), f32(mid), f32(mid_inv), f32(last)


FFT_K1 = FFT_R // 2 + 1
FFT_K1_PAD = 40


def _dft_tables_long_half():
    first, mid, mid_inv, last = _dft_tables_long()
    r = FFT_R
    keep = np.zeros((FFT_K1_PAD,), np.float32)
    keep[:FFT_K1] = 1.0
    weight = np.zeros((FFT_K1_PAD,), np.float32)
    weight[:FFT_K1] = 2.0
    weight[0] = weight[r // 2] = 1.0
    first_h = np.concatenate([first[:, :FFT_K1_PAD] * keep[None, :, None],
                              first[:, r : r + FFT_K1_PAD] * keep[None, :, None]], axis=1)
    last_h = np.concatenate([last[:, :, :FFT_K1_PAD] * weight, last[:, :, r : r + FFT_K1_PAD] * weight], axis=2)
    return first_h, mid, mid_inv, last_h


def _ld_rows(ref, rows):
    return jnp.concatenate([ref[j, rows, :] for j in range(ref.shape[0])], axis=-1)


def _st_rows(ref, rows, val):
    for j in range(ref.shape[0]):
        ref[j, rows, :] = val[:, j * LANES : (j + 1) * LANES]


def _st_transposed(ref, j, val, half):
    n = val.shape[0] // 2
    _st_rows(ref, pl.ds(j, n, stride=2 * half), val[:n])
    _st_rows(ref, pl.ds(half + j, n, stride=2 * half), val[n:])


def _lane_split_scratch(rows, width):
    return pltpu.VMEM((width // LANES, rows, LANES), F32)


FFT_UNROLL = 4


def _fft_long_forward(x_ref, y_ref, dot_first, dot_mid, out_fn, n_k1=FFT_R, unroll_k1=FFT_UNROLL):
    r = FFT_R

    def stage_a(n2, carry):
        res = dot_first(n2, _ld_rows(x_ref, pl.ds(n2, r // 2, stride=r)))
        _st_transposed(y_ref, n2, res, r)
        return carry

    lax.fori_loop(0, r, stage_a, 0, unroll=FFT_UNROLL)

    def stage_c(k1, carry):
        base = pl.multiple_of(k1 * 2 * r, 2 * r)
        out_fn(k1, base, dot_mid(_ld_rows(y_ref, pl.ds(base, 2 * r))))
        return carry

    lax.fori_loop(0, n_k1, stage_c, 0, unroll=unroll_k1)


def _dot_split(t_hi, t_lo, x):
    x_hi = x.astype(BF16)
    x_lo = (x - x_hi.astype(F32)).astype(BF16)
    return _dot(t_hi, x_hi) + _dot(t_hi, x_lo) + _dot(t_lo, x_hi)


def _split_table(m):
    hi = jnp.asarray(m, BF16)
    lo = (jnp.asarray(m) - hi.astype(F32)).astype(BF16)
    return hi, lo


def _hy_spec_long_kernel(hf_ref, hb_ref, first_hi_ref, first_lo_ref, mid_hi_ref, mid_lo_ref, f_ref,
                         x_ref, y_ref, tmp_ref):
    r = FFT_R
    dot_first = lambda n2, slab: _dot_split(first_hi_ref[n2], first_lo_ref[n2], slab)
    dot_mid = lambda block: _dot_split(mid_hi_ref[...], mid_lo_ref[...], block)

    def write_fwd(k1, base, spec):
        tmp_ref[pl.ds(base, 2 * r), :] = spec

    _st_rows(x_ref, slice(None), hf_ref[0])
    _fft_long_forward(x_ref, y_ref, dot_first, dot_mid, write_fwd, n_k1=FFT_K1, unroll_k1=3)

    def write_sum(k1, base, spec):
        prev = tmp_ref[pl.ds(base, 2 * r), :]
        f_ref[0, pl.ds(base, r), :] = prev[:r] + spec[:r]
        f_ref[0, pl.ds(base + r, r), :] = prev[r:] - spec[r:]

    _st_rows(x_ref, slice(None), hb_ref[0])
    _fft_long_forward(x_ref, y_ref, dot_first, dot_mid, write_sum, n_k1=FFT_K1, unroll_k1=3)


SPEC_ROWS = 2 * FFT_R * FFT_K1


def _hy_spec_long(n, hf, hb):
    first, mid, _, _ = _dft_tables_long_half()
    lay = lambda rows: pl.BlockSpec((1, rows, W_C), lambda l, o: (l, 0, o))
    return pl.pallas_call(
        _hy_spec_long_kernel,
        grid=(DEPTH, HY_ORDER),
        in_specs=[lay(n), lay(n)] + [_const_spec(first.shape)] * 2 + [_const_spec(mid.shape)] * 2,
        out_specs=lay(SPEC_ROWS),
        out_shape=jax.ShapeDtypeStruct((DEPTH, SPEC_ROWS, HY_W), F32),
        scratch_shapes=[_lane_split_scratch(n, W_C), _lane_split_scratch(2 * FFT_R * FFT_K1_PAD, W_C),
                        pltpu.VMEM((SPEC_ROWS, W_C), F32)],
        compiler_params=_cparams(2),
        name="hyena_spectrum_long",
    )(hf, hb, *_split_table(first), *_split_table(mid))


def _hy_conv_long_kernel(sig_ref, gate_ref, ws_ref, bias_ref, f_ref, first_ref, mid_ref, midinv_ref, last_ref, o_ref,
                         x_ref, y_ref, v_ref, out_ref, *, n, order):
    r = FFT_R
    to_bf16 = lambda x: x.astype(BF16)
    gate = _short_conv(gate_ref[...], ws_ref[:, (order + 1) * W_C : (order + 2) * W_C], n)
    sig = _short_conv(sig_ref[...], ws_ref[:, :W_C], n) if order == 0 else sig_ref[...]
    _st_rows(x_ref, slice(None), sig)
    v_ref[...] = jnp.zeros(v_ref.shape, F32)

    def filter_and_invert(k1, base, spec):
        fr = f_ref[pl.ds(base, r), :]
        fi = f_ref[pl.ds(base + r, r), :]
        zr = spec[:r] * fr - spec[r:] * fi
        zi = spec[:r] * fi + spec[r:] * fr
        res = _dot(midinv_ref[...], jnp.concatenate([zr, zi], axis=0).astype(BF16))
        _st_transposed(v_ref, k1, res, FFT_K1_PAD)

    dot_first = lambda n2, slab: _dot(first_ref[n2], to_bf16(slab))
    dot_mid = lambda block: _dot(mid_ref[...], to_bf16(block))
    _fft_long_forward(x_ref, y_ref, dot_first, dot_mid, filter_and_invert, n_k1=FFT_K1, unroll_k1=3)

    def stage_last(n2, carry):
        base = pl.multiple_of(n2 * 2 * FFT_K1_PAD, 2 * FFT_K1_PAD)
        res = _dot(last_ref[n2], _ld_rows(v_ref, pl.ds(base, 2 * FFT_K1_PAD)).astype(BF16))
        _st_rows(out_ref, pl.ds(n2, r // 2, stride=r), res)
        return carry

    lax.fori_loop(0, r, stage_last, 0, unroll=FFT_UNROLL)
    z = gate * (_ld_rows(out_ref, slice(None)) + sig * bias_ref[order : order + 1, :])
    o_ref[...] = z.astype(o_ref.dtype)


def _hy_conv_long(n_seq, n, row0, total_rows, pc, w_short, bias, spec, prev=None):
    assert 2 * n == FFT_R * FFT_R and FFT_K1 % 3 == 0
    first, mid, mid_inv, last = _dft_tables_long_half()
    big = 2 * n
    bf = lambda m: jnp.asarray(m, BF16)
    lane_block = lambda j: pl.BlockSpec((n, W_C), lambda b: (row0 // n + b, j))
    z = None
    for order in range(HY_ORDER):
        final = order + 1 == HY_ORDER
        in_specs = [lane_block(0), lane_block(order + 1), _const_spec((SHORT_K, PC_W)),
                    _const_spec((HY_ORDER, W_C)), pl.BlockSpec((SPEC_ROWS, W_C), lambda b, order=order: (0, order)),
                    _const_spec(first.shape), _const_spec(mid.shape), _const_spec(mid_inv.shape),
                    _const_spec(last.shape)]
        args = [pc if order == 0 else z, pc, w_short, bias, spec, bf(first), bf(mid), bf(mid_inv), bf(last)]
        body = functools.partial(_hy_conv_long_kernel, n=n, order=order)
        body, in_specs, args, aliases = _keep_rows_of(prev if final else None, body, in_specs, args)
        z = pl.pallas_call(
            body,
            grid=(n_seq,),
            in_specs=in_specs,
            out_specs=lane_block(0),
            out_shape=jax.ShapeDtypeStruct((total_rows, W_C), BF16 if final else F32),
            input_output_aliases=aliases,
            scratch_shapes=[_lane_split_scratch(n, W_C), _lane_split_scratch(2 * FFT_R * FFT_K1_PAD, W_C),
                            _lane_split_scratch(2 * FFT_K1_PAD * FFT_R, W_C), _lane_split_scratch(n, W_C)],
            compiler_params=_cparams(1),
            name="hyena_conv_long",
        )(*args)
    return z


def _pad_in_weights(w_in):
    gap = jnp.zeros((D_MODEL, PB_W - MLA_IN), w_in.dtype)
    return jnp.concatenate([w_in[:, : PA_W + MLA_IN], gap, w_in[:, PA_W + MLA_IN :]], axis=1).astype(BF16)


def kernel(x_prompt, x_sample, c, cache_mla, cache_diff_k, cache_diff_v, state_hgrn, c_ctx, w_mod, b_mod, norm_g,
           ffn_w_gu, ffn_w_down, w_in, w_out, hgrn_lb_logits, hgrn_onorm, mla_q_norm, mla_kv_norm, mla_w_uq,
           mla_w_ukv, mla_qk_norm, hy_short, hy_w1, hy_b1, hy_w2, hy_b2, hy_w3, hy_log_decay, hy_bias,
           diff_qk_norm, diff_lambda, diff_subln):
    bc, sc, _ = x_prompt.shape
    bl, sl, _ = x_sample.shape
    past = cache_mla.shape[2]
    tok = _Tokens(bc, sc, bl, sl)
    assert bl + 1 <= MOD_ROWS and tok.tc % sl == 0

    x = jnp.concatenate([x_prompt.reshape(tok.tc, D_MODEL), x_sample.reshape(tok.tl, D_MODEL)], axis=0)
    cond = jnp.concatenate([c_ctx[None], c, jnp.zeros((MOD_ROWS - 1 - bl, D_MODEL), F32)], axis=0)
    mod = _modulation(cond, w_mod, b_mod)

    filters = {n: _hy_filters(n, hy_w1, hy_b1, hy_w2, hy_b2, hy_w3, hy_log_decay) for n in (sc, sl)}
    spec_ctx = _hy_spec_short(sc, *filters[sc])
    spec_lat = _hy_spec_long(sl, *filters[sl])

    mla_tables = _rope_tables(tok, [NOPE_B], HEAD_PAD)
    diff_tables = _rope_tables(tok, list(range(0, H_D * 2 * DH_D, DH_D)), H_D * 2 * DH_D)

    new_mla, new_dk, new_dv, new_state = [], [], [], []
    for l in range(DEPTH):
        x = _ffn(tok, x, mod, l, 0, norm_g[l, 0], ffn_w_gu[l, 0].astype(BF16), ffn_w_down[l, 0].astype(BF16))
        pa, pb, pc, pd = _inproj(tok, x, mod, l, norm_g[l, 1], _pad_in_weights(w_in[l]))

        o_a, s_ctx = _hgrn(l, bc, sc, 0, tok.t, pa, hgrn_lb_logits, hgrn_onorm[l], None)
        o_a, _ = _hgrn(l, bl, sl, tok.tc, tok.t, pa, hgrn_lb_logits, hgrn_onorm[l], state_hgrn[:, l], prev=o_a)

        mla_w = _mla_weights(mla_w_uq[l], mla_w_ukv[l], mla_qk_norm[l])
        q_b, k_b, v_b, cache_b = _mla_prep(tok, pb, mla_q_norm[l], mla_kv_norm[l], mla_w, mla_tables)
        cache_rows = jnp.pad(cache_mla[:, l].reshape(bl * past, KV_LORA + ROPE_B),
                             ((0, 0), (0, 2 * LANES - KV_LORA - ROPE_B)))
        kc_b, vc_b = _mla_cache_prep(cache_rows, mla_w)
        o_b = _mla_attention(bc, sc, 0, tok.t, q_b, k_b, v_b)
        o_b = _mla_attention(bl, sl, tok.tc, tok.t, q_b, k_b, v_b, kc_b, vc_b, prev=o_b)

        o_c = _hy_conv_short(bc, sc, 0, tok.t, pc, hy_short[l], hy_bias[l], spec_ctx[l])
        o_c = _hy_conv_long(bl, sl, tok.tc, tok.t, pc, hy_short[l], hy_bias[l], spec_lat[l], prev=o_c)

        q_d, k_d, v_d, kcache_d = _diff_prep(tok, pd, diff_qk_norm[l], diff_tables)
        kc_d, vc_d = _diff_cache_prep(cache_diff_k[:, l].reshape(bl * past, H_D * 2 * DH_D),
                                      cache_diff_v[:, l].reshape(bl * past, H_D * DV_D))
        o_d = _diff_attention(l, bc, sc, 0, tok.t, diff_lambda[l], diff_subln[l], q_d, k_d, v_d)
        o_d = _diff_attention(l, bl, sl, tok.tc, tok.t, diff_lambda[l], diff_subln[l], q_d, k_d, v_d, kc_d, vc_d,
                              prev=o_d)

        x = _ffn(tok, x, mod, l, 1, norm_g[l, 2], ffn_w_gu[l, 1].astype(BF16), ffn_w_down[l, 1].astype(BF16),
                 mixer_outs=[o_a, o_b, o_c, o_d], w_out=w_out[l].astype(BF16))

        new_mla.append(cache_b[: tok.tc, : KV_LORA + ROPE_B].reshape(bc, sc, KV_LORA + ROPE_B))
        new_dk.append(kcache_d[: tok.tc].reshape(bc, sc, H_D, 2, DH_D))
        new_dv.append(pd[: tok.tc, 2 * H_D * 2 * DH_D :].reshape(bc, sc, H_D, DV_D))
        new_state.append(s_ctx)

    y_prompt = x[: tok.tc].reshape(bc, sc, D_MODEL)
    y_sample = x[tok.tc :].reshape(bl, sl, D_MODEL)
    return (y_prompt, y_sample, jnp.stack(new_mla, axis=1), jnp.stack(new_dk, axis=1), jnp.stack(new_dv, axis=1),
            jnp.stack(new_state, axis=1))
```

```python
import functools
import math

import jax
import jax.numpy as jnp
import numpy as np
from jax import lax
from jax.experimental import pallas as pl
from jax.experimental.pallas import tpu as pltpu

F32 = jnp.float32
BF16 = jnp.bfloat16
HIGHEST = lax.Precision.HIGHEST

D_MODEL = 1024
DEPTH = 4
GRID_W = 64
N_MOD = 9
D_FF = 2816
EPS = 1e-6
ROPE_BASE = 10000.0
GROUP_W = 256
H_A, DK_A, DV_A = 4, 64, 64
MAX_INPUT_KEY = 1.0 - 1e-6
H_B, NOPE_B, ROPE_B, V_B = 4, 64, 32, 64
Q_LORA, KV_LORA = 256, 128
W_C, HY_ORDER, HY_BANDS, HY_FH, SHORT_K = 256, 2, 8, 64, 3
HY_EMB = 1 + 2 * HY_BANDS
H_D, DV_D, DH_D = 4, 64, 32

LANES = 128
MOD_ROWS = 16
TOKEN_TILE = 512
Q_TILE = 256
KEY_TILE = 512
HGRN_BLOCK = 128
VMEM_LIMIT = 56 * 1024 * 1024

PA_W, PB_W, PC_W, PD_W = 1280, 512, 768, 768
IN_PAD_W = PA_W + PB_W + PC_W + PD_W
MLA_IN = Q_LORA + KV_LORA + ROPE_B
HEAD_PAD = 128


def _cparams(n_axes):
    return pltpu.CompilerParams(dimension_semantics=("arbitrary",) * n_axes, vmem_limit_bytes=VMEM_LIMIT)


def _nt_dot(a, b):
    return lax.dot_general(a, b, (((1,), (1,)), ((), ())), preferred_element_type=F32)


def _dot(a, b):
    return jnp.dot(a, b, preferred_element_type=F32)


def _dot_exact_rhs(a, b_bf16, passes=3):
    out = None
    rem = a
    for _ in range(passes):
        piece = rem.astype(BF16)
        term = _dot(piece, b_bf16)
        out = term if out is None else out + term
        rem = rem - piece.astype(F32)
    return out


def _dot_exact_lhs(a_bf16, b, passes=3):
    out = None
    rem = b
    for _ in range(passes):
        piece = rem.astype(BF16)
        term = _dot(a_bf16, piece)
        out = term if out is None else out + term
        rem = rem - piece.astype(F32)
    return out


def _silu(x):
    return x * jax.nn.sigmoid(x)


def _adaln(x, g, sc, sh):
    y = x * lax.rsqrt(jnp.mean(x * x, axis=-1, keepdims=True) + EPS)
    return (y * g) * (1.0 + sc) + sh


def _rope(x, cos, sin_hi, sin_lo):
    w = x.shape[-1]
    return x * cos + pltpu.roll(x, 8, 1) * sin_hi + pltpu.roll(x, w - 8, 1) * sin_lo


def _mod_kernel(c_ref, w_ref, b_ref, o_ref):
    a = _silu(c_ref[...])
    o_ref[0] = jnp.dot(a, w_ref[0], preferred_element_type=F32, precision=HIGHEST) + b_ref[0]


def _modulation(cond, w_mod, b_mod):
    tn = D_MODEL
    n_col = N_MOD * D_MODEL
    out = pl.pallas_call(
        _mod_kernel,
        grid=(DEPTH, n_col // tn),
        in_specs=[
            pl.BlockSpec((MOD_ROWS, D_MODEL), lambda l, j: (0, 0)),
            pl.BlockSpec((1, D_MODEL, tn), lambda l, j: (l, 0, j)),
            pl.BlockSpec((1, 1, tn), lambda l, j: (l, 0, j)),
        ],
        out_specs=pl.BlockSpec((1, MOD_ROWS, tn), lambda l, j: (l, 0, j)),
        out_shape=jax.ShapeDtypeStruct((DEPTH, MOD_ROWS, n_col), F32),
        compiler_params=_cparams(2),
        name="modulation",
    )(cond, w_mod, b_mod.reshape(DEPTH, 1, n_col))
    return out.reshape(DEPTH * MOD_ROWS, N_MOD, D_MODEL)


class _Tokens:
    def __init__(self, n_ctx_seq, ctx_len, n_lat_seq, lat_len):
        self.bc, self.sc, self.bl, self.sl = n_ctx_seq, ctx_len, n_lat_seq, lat_len
        self.tc, self.tl = n_ctx_seq * ctx_len, n_lat_seq * lat_len
        self.t = self.tc + self.tl
        self.tm = min(TOKEN_TILE, lat_len)
        assert self.tc % self.tm == 0 and lat_len % self.tm == 0
        self.ctx_tiles = self.tc // self.tm
        self.lat_tiles_per_seq = lat_len // self.tm
        self.n_tiles = self.t // self.tm

    def mod_row(self, layer):
        def f(i):
            lat = 1 + (i - self.ctx_tiles) // self.lat_tiles_per_seq
            return layer * MOD_ROWS + jnp.where(i < self.ctx_tiles, 0, lat)

        return f

    def rope_block(self, i):
        return jnp.where(i < self.ctx_tiles, 0, 1 + (i - self.ctx_tiles) % self.lat_tiles_per_seq)


def _mod_spec(tok, layer):
    row = tok.mod_row(layer)
    return pl.BlockSpec((1, N_MOD, D_MODEL), lambda i: (row(i), 0, 0))


def _const_spec(shape):
    zeros = (0,) * len(shape)
    return pl.BlockSpec(shape, lambda *_: zeros)


def _row_spec(tm, width):
    return pl.BlockSpec((tm, width), lambda i: (i, 0))


def _keep_rows_of(prev, kernel_fn, in_specs, args):
    if prev is None:
        return kernel_fn, in_specs, args, {}
    idx = len(in_specs)

    def body(*refs):
        return kernel_fn(*refs[:idx], *refs[idx + 1 :])

    return body, in_specs + [pl.BlockSpec(memory_space=pl.ANY)], args + [prev], {idx: 0}


def _ffn_kernel(x_ref, mod_ref, g_ref, wgu_ref, wd_ref, *rest, mod_base):
    x = x_ref[...]
    if len(rest) > 1:
        oa_ref, ob_ref, oc_ref, od_ref, wo_ref = rest[:5]
        mixed = jnp.concatenate([oa_ref[...], ob_ref[...], oc_ref[...], od_ref[...]], axis=-1)
        x = x + mod_ref[0, 5:6, :] * _dot(mixed, wo_ref[...])
    o_ref = rest[-1]
    sh = mod_ref[0, mod_base : mod_base + 1, :]
    sc = mod_ref[0, mod_base + 1 : mod_base + 2, :]
    gate = mod_ref[0, mod_base + 2 : mod_base + 3, :]
    h = _adaln(x, g_ref[...], sc, sh).astype(BF16)
    au = _dot(h, wgu_ref[...])
    act = (_silu(au[:, :D_FF]) * au[:, D_FF:]).astype(BF16)
    o_ref[...] = x + (0.5 * gate) * _dot(act, wd_ref[...])


def _resident(lead_index, shape):
    zeros = (0,) * len(shape)
    return pl.BlockSpec((None,) * len(lead_index) + tuple(shape), lambda i: tuple(lead_index) + zeros,
                        pipeline_mode=pl.Buffered(1))


def _ffn(tok, x, mod, layer, which, norm_g, w_gu, w_down, mixer_outs=None, w_out=None):
    tm = tok.tm
    in_specs = [
        _row_spec(tm, D_MODEL),
        _mod_spec(tok, layer),
        _const_spec((1, D_MODEL)),
        _resident((layer, which), (D_MODEL, 2 * D_FF)),
        _resident((layer, which), (D_FF, D_MODEL)),
    ]
    args = [x, mod, norm_g.reshape(1, D_MODEL), w_gu, w_down]
    if mixer_outs is not None:
        in_specs += [_row_spec(tm, GROUP_W)] * 4
        in_specs.append(_resident((layer,), (D_MODEL, D_MODEL)))
        args += list(mixer_outs) + [w_out]
    return pl.pallas_call(
        functools.partial(_ffn_kernel, mod_base=6 * which),
        grid=(tok.n_tiles,),
        in_specs=in_specs,
        out_specs=_row_spec(tm, D_MODEL),
        out_shape=jax.ShapeDtypeStruct((tok.t, D_MODEL), F32),
        compiler_params=_cparams(1),
        name="ffn",
    )(*args)


def _inproj_kernel(x_ref, mod_ref, g_ref, w_ref, pa_ref, pb_ref, pc_ref, pd_ref):
    h = _adaln(x_ref[...], g_ref[...], mod_ref[0, 4:5, :], mod_ref[0, 3:4, :]).astype(BF16)
    p = _dot(h, w_ref[...])
    pa_ref[...] = p[:, :PA_W]
    pb_ref[...] = p[:, PA_W : PA_W + PB_W]
    pc_ref[...] = p[:, PA_W + PB_W : PA_W + PB_W + PC_W]
    pd_ref[...] = p[:, PA_W + PB_W + PC_W :]


def _inproj(tok, x, mod, layer, norm_g, w_in_pad):
    tm = tok.tm
    widths = (PA_W, PB_W, PC_W, PD_W)
    return pl.pallas_call(
        _inproj_kernel,
        grid=(tok.n_tiles,),
        in_specs=[
            _row_spec(tm, D_MODEL),
            _mod_spec(tok, layer),
            _const_spec((1, D_MODEL)),
            _resident((layer,), (D_MODEL, IN_PAD_W)),
        ],
        out_specs=[_row_spec(tm, w) for w in widths],
        out_shape=[jax.ShapeDtypeStruct((tok.t, w), F32) for w in widths],
        compiler_params=_cparams(1),
        name="inproj",
    )(x, mod, norm_g.reshape(1, D_MODEL), w_in_pad)


def _rope_group_tables(n_tok):
    t = np.arange(n_tok)
    pos = np.stack([t // GRID_W, t % GRID_W], axis=1).astype(np.float32)
    inv = (ROPE_BASE ** (-np.arange(8, dtype=np.float32) / 8)).astype(np.float32)
    lane = np.arange(32)
    ang = (pos[:, lane // 16] * inv[lane % 8][None, :]).astype(np.float32)
    second = (lane % 16) >= 8
    cos, sin = np.cos(ang), np.sin(ang)
    return cos, np.where(second[None], sin, 0.0), np.where(second[None], 0.0, -sin)


def _rope_tables(tok, lane_groups, width):
    cos = np.ones((tok.tm + tok.sl, width), np.float32)
    s_hi = np.zeros_like(cos)
    s_lo = np.zeros_like(cos)
    c, a, b = _rope_group_tables(tok.sl)
    for g in lane_groups:
        cos[tok.tm :, g : g + 32] = c
        s_hi[tok.tm :, g : g + 32] = a
        s_lo[tok.tm :, g : g + 32] = b
    return jnp.asarray(cos), jnp.asarray(s_hi), jnp.asarray(s_lo)


def _mla_keys_values(kv_in, wk_ref, wv_ref, kn_ref, rope):
    kraw = _dot(kv_in.astype(BF16), wk_ref[...])
    v = _dot(kv_in[:, :KV_LORA].astype(BF16), wv_ref[...])
    ks = []
    for h in range(H_B):
        kh = kraw[:, h * HEAD_PAD : (h + 1) * HEAD_PAD]
        ss = jnp.sum(kh * kh, axis=-1, keepdims=True) * (1.0 / (NOPE_B + ROPE_B))
        kh = kh * lax.rsqrt(ss + EPS) * kn_ref[...]
        if rope is not None:
            kh = _rope(kh, *rope)
        ks.append(kh.astype(BF16))
    return ks, v.astype(BF16)


def _mla_prep_body(pb, gq_ref, gkv_ref, wuq_ref, wk_ref, wv_ref, qn_ref, kn_ref, cos_ref, shi_ref, slo_ref,
                   q_ref, k_ref, v_ref, cache_ref):
    rope = (cos_ref[...], shi_ref[...], slo_ref[...])
    cq = pb[:, :Q_LORA]
    cq = cq * lax.rsqrt(jnp.mean(cq * cq, axis=-1, keepdims=True) + EPS) * gq_ref[...]
    qraw = _dot(cq.astype(BF16), wuq_ref[...])
    scale = (NOPE_B + ROPE_B) ** -0.5 * LOG2E
    for h in range(H_B):
        qh = qraw[:, h * HEAD_PAD : (h + 1) * HEAD_PAD]
        ss = jnp.sum(qh * qh, axis=-1, keepdims=True) * (1.0 / (NOPE_B + ROPE_B))
        qh = _rope(qh * lax.rsqrt(ss + EPS) * qn_ref[...], *rope)
        q_ref[:, h * HEAD_PAD : (h + 1) * HEAD_PAD] = (qh * scale).astype(BF16)
    ckv = pb[:, Q_LORA : Q_LORA + KV_LORA]
    ckv = ckv * lax.rsqrt(jnp.mean(ckv * ckv, axis=-1, keepdims=True) + EPS) * gkv_ref[...]
    kv_in = jnp.concatenate([ckv, pb[:, Q_LORA + KV_LORA :]], axis=-1)
    cache_ref[...] = kv_in
    ks, v = _mla_keys_values(kv_in, wk_ref, wv_ref, kn_ref, rope)
    for h in range(H_B):
        k_ref[:, h * HEAD_PAD : (h + 1) * HEAD_PAD] = ks[h]
    v_ref[...] = v


def _mla_cache_kernel(c_ref, wk_ref, wv_ref, kn_ref, k_ref, v_ref):
    ks, v = _mla_keys_values(c_ref[...], wk_ref, wv_ref, kn_ref, None)
    for h in range(H_B):
        k_ref[:, h * HEAD_PAD : (h + 1) * HEAD_PAD] = ks[h]
    v_ref[...] = v


def _mla_weights(w_uq, w_ukv, qk_norm):
    wuq = jnp.pad(w_uq.reshape(Q_LORA, H_B, NOPE_B + ROPE_B), ((0, 0), (0, 0), (0, HEAD_PAD - NOPE_B - ROPE_B)))
    wuq = wuq.reshape(Q_LORA, H_B * HEAD_PAD).astype(BF16)
    ukv = w_ukv.reshape(KV_LORA, H_B, NOPE_B + V_B)
    wk_nope = jnp.pad(ukv[:, :, :NOPE_B], ((0, 0), (0, 0), (0, HEAD_PAD - NOPE_B)))
    place = np.zeros((2 * LANES - KV_LORA, H_B, HEAD_PAD), np.float32)
    for h in range(H_B):
        place[np.arange(ROPE_B), h, NOPE_B + np.arange(ROPE_B)] = 1.0
    wk = jnp.concatenate([wk_nope, jnp.asarray(place)], axis=0).reshape(2 * LANES, H_B * HEAD_PAD).astype(BF16)
    wv = ukv[:, :, NOPE_B:].reshape(KV_LORA, H_B * V_B).astype(BF16)
    pad = lambda g: jnp.pad(g, (0, HEAD_PAD - NOPE_B - ROPE_B)).reshape(1, HEAD_PAD)
    return wuq, wk, wv, pad(qk_norm[0]), pad(qk_norm[1])


def _mla_prep_operands(tok, gq, gkv, weights, tables):
    wuq, wk, wv, qn, kn = weights
    tm = tok.tm
    rope_spec = pl.BlockSpec((tm, HEAD_PAD), lambda i: (tok.rope_block(i), 0))
    args = [gq.reshape(1, -1), gkv.reshape(1, -1), wuq, wk, wv, qn, kn, *tables]
    in_specs = [_const_spec((1, Q_LORA)), _const_spec((1, KV_LORA)), _const_spec(wuq.shape), _const_spec(wk.shape),
                _const_spec(wv.shape), _const_spec((1, HEAD_PAD)), _const_spec((1, HEAD_PAD))] + [rope_spec] * 3
    widths = [(H_B * HEAD_PAD, BF16), (H_B * HEAD_PAD, BF16), (H_B * V_B, BF16), (2 * LANES, F32)]
    return args, in_specs, widths


def _prep_call(tok, body, name, proj, operands):
    args, in_specs, outs = operands
    tm = tok.tm

    def kernel_fn(proj_ref, *refs):
        body(proj_ref[...], *refs)

    return pl.pallas_call(
        kernel_fn,
        grid=(tok.n_tiles,),
        in_specs=[_row_spec(tm, proj.shape[1])] + in_specs,
        out_specs=[_row_spec(tm, w) for w, _ in outs],
        out_shape=[jax.ShapeDtypeStruct((tok.t, w), dt) for w, dt in outs],
        compiler_params=_cparams(1),
        name=name,
    )(proj, *args)


def _mla_cache_prep(cache_rows, weights):
    _, wk, wv, _, kn = weights
    rows = cache_rows.shape[0]
    tm = min(TOKEN_TILE, rows)
    return pl.pallas_call(
        _mla_cache_kernel,
        grid=(rows // tm,),
        in_specs=[_row_spec(tm, 2 * LANES), _const_spec(wk.shape), _const_spec(wv.shape), _const_spec((1, HEAD_PAD))],
        out_specs=[_row_spec(tm, H_B * HEAD_PAD), _row_spec(tm, H_B * V_B)],
        out_shape=[jax.ShapeDtypeStruct((rows, H_B * HEAD_PAD), BF16), jax.ShapeDtypeStruct((rows, H_B * V_B), BF16)],
        compiler_params=_cparams(1),
        name="mla_cache_prep",
    )(cache_rows, wk, wv, kn)


LOG2E = math.log2(math.e)


def _softmax_parts(scores):
    m = functools.reduce(jnp.maximum, [jnp.max(s, axis=-1, keepdims=True) for s in scores])
    ps = [jnp.exp2(s - m) for s in scores]
    denom = functools.reduce(jnp.add, [jnp.sum(p, axis=-1, keepdims=True) for p in ps])
    return ps, denom


def _head_lane_mask(width, head, head_w):
    lane = lax.broadcasted_iota(jnp.int32, (1, width), 1)
    return (lane >= head * head_w) & (lane < (head + 1) * head_w)


def _mla_attn_kernel(*refs, n_kv):
    q_ref = refs[0]
    k_refs = refs[1 : 1 + n_kv]
    v_refs = refs[1 + n_kv : 1 + 2 * n_kv]
    o_ref = refs[1 + 2 * n_kv]
    tiles = [(k, v, j) for k, v in zip(k_refs, v_refs) for j in range(0, k.shape[0], KEY_TILE)]
    out = None
    for h in range(H_B):
        sl = slice(h * HEAD_PAD, (h + 1) * HEAD_PAD)
        qh = q_ref[:, sl]
        mask = _head_lane_mask(H_B * V_B, h, V_B)
        m = denom = acc = None
        for k, v, j in tiles:
            rows = slice(j, min(j + KEY_TILE, k.shape[0]))
            s = _nt_dot(qh, k[rows, sl])
            vm = jnp.where(mask, v[rows, :], jnp.zeros((), BF16))
            s_max = jnp.max(s, axis=-1, keepdims=True)
            if m is None:
                m = s_max
                p = jnp.exp2(s - m)
                denom = jnp.sum(p, axis=-1, keepdims=True)
                acc = _dot(p.astype(BF16), vm)
            else:
                m_new = jnp.maximum(m, s_max)
                alpha = jnp.exp2(m - m_new)
                p = jnp.exp2(s - m_new)
                denom = denom * alpha + jnp.sum(p, axis=-1, keepdims=True)
                acc = acc * alpha + _dot(p.astype(BF16), vm)
                m = m_new
        acc = acc * (1.0 / denom)
        out = acc if out is None else out + acc
    o_ref[...] = out.astype(BF16)


def _seq_specs(n_seq, seq_len, row0, tq, q_width, kv_widths, kv_new, kv_cache):
    q_blocks = seq_len // tq
    q_spec = lambda w: pl.BlockSpec((tq, w), lambda b, i: (row0 // tq + b * q_blocks + i, 0))
    new_spec = lambda w: pl.BlockSpec((seq_len, w), lambda b, i: (row0 // seq_len + b, 0))
    specs = [q_spec(q_width)]
    for w in kv_widths:
        if kv_cache is not None:
            specs.append(pl.BlockSpec((kv_cache, w), lambda b, i: (b, 0)))
        specs.append(new_spec(w))
    return specs, q_spec


def _mla_attention(n_seq, seq_len, row0, total_rows, q, k, v, k_cache=None, v_cache=None, prev=None):
    tq = min(Q_TILE, seq_len)
    has_cache = k_cache is not None
    cache_len = k_cache.shape[0] // n_seq if has_cache else None
    specs, q_spec = _seq_specs(n_seq, seq_len, row0, tq, H_B * HEAD_PAD, (H_B * HEAD_PAD, H_B * V_B), True, cache_len)
    args = [q] + ([k_cache, k] if has_cache else [k]) + ([v_cache, v] if has_cache else [v])
    body = functools.partial(_mla_attn_kernel, n_kv=2 if has_cache else 1)
    body, specs, args, aliases = _keep_rows_of(prev, body, specs, args)
    return pl.pallas_call(
        body,
        grid=(n_seq, seq_len // tq),
        in_specs=specs,
        out_specs=q_spec(H_B * V_B),
        out_shape=jax.ShapeDtypeStruct((total_rows, H_B * V_B), BF16),
        input_output_aliases=aliases,
        compiler_params=_cparams(2),
        name="mla_attention",
    )(*args)


DQK_W = H_D * 2 * HEAD_PAD


def _diff_place_matrix():
    m = np.zeros((H_D * 2 * DH_D, DQK_W), np.float32)
    g = np.arange(H_D * 2 * DH_D)
    m[g, (g // DH_D) * HEAD_PAD + g % DH_D] = 1.0
    return jnp.asarray(m, BF16)


def _group_mean_matrix(width, group):
    g = np.arange(width)
    return jnp.asarray((g[:, None] // group == g[None, :] // group).astype(np.float32), BF16)


def _diff_prep_body(pd, gq_ref, gk_ref, gm_ref, place_ref, cos_ref, shi_ref, slo_ref,
                    q_ref, k_ref, v_ref, kcache_ref, vcache_ref):
    rope = (cos_ref[...], shi_ref[...], slo_ref[...])
    w = H_D * 2 * DH_D

    def norm(x, g):
        ms = _dot_exact_rhs(x * x, gm_ref[...], passes=2) * (1.0 / DH_D)
        return x * lax.rsqrt(ms + EPS) * g

    qn = _rope(norm(pd[:, :w], gq_ref[...]), *rope) * (DH_D ** -0.5 * LOG2E)
    kn = norm(pd[:, w : 2 * w], gk_ref[...])
    kcache_ref[...] = kn
    q_ref[...] = _dot(qn.astype(BF16), place_ref[...]).astype(BF16)
    k_ref[...] = _dot(_rope(kn, *rope).astype(BF16), place_ref[...]).astype(BF16)
    vcache_ref[...] = pd[:, 2 * w :]
    v_ref[...] = pd[:, 2 * w :].astype(BF16)


def _diff_cache_kernel(k_in_ref, v_in_ref, place_ref, k_ref, v_ref):
    k_ref[...] = _dot(k_in_ref[...].astype(BF16), place_ref[...]).astype(BF16)
    v_ref[...] = v_in_ref[...].astype(BF16)


def _diff_prep_operands(tok, qk_norm, tables):
    tm = tok.tm
    w = H_D * 2 * DH_D
    rope_spec = pl.BlockSpec((tm, w), lambda i: (tok.rope_block(i), 0))
    tile_g = lambda g: jnp.tile(g, H_D * 2).reshape(1, w)
    args = [tile_g(qk_norm[0]), tile_g(qk_norm[1]), _group_mean_matrix(w, DH_D), _diff_place_matrix(), *tables]
    in_specs = [_const_spec((1, w)), _const_spec((1, w)), _const_spec((w, w)), _const_spec((w, DQK_W))] + [rope_spec] * 3
    widths = [(DQK_W, BF16), (DQK_W, BF16), (H_D * DV_D, BF16), (w, F32), (H_D * DV_D, F32)]
    return args, in_specs, widths


def _diff_cache_prep(k_rows, v_rows):
    rows = k_rows.shape[0]
    tm = min(TOKEN_TILE, rows)
    w = H_D * 2 * DH_D
    return pl.pallas_call(
        _diff_cache_kernel,
        grid=(rows // tm,),
        in_specs=[_row_spec(tm, w), _row_spec(tm, H_D * DV_D), _const_spec((w, DQK_W))],
        out_specs=[_row_spec(tm, DQK_W), _row_spec(tm, H_D * DV_D)],
        out_shape=[jax.ShapeDtypeStruct((rows, DQK_W), BF16), jax.ShapeDtypeStruct((rows, H_D * DV_D), BF16)],
        compiler_params=_cparams(1),
        name="diff_cache_prep",
    )(k_rows, v_rows, _diff_place_matrix())


def _diff_attn_kernel(*refs, n_kv, lam_init):
    q_ref, lam_ref, sub_ref = refs[0], refs[1], refs[2]
    k_refs = refs[3 : 3 + n_kv]
    v_refs = refs[3 + n_kv : 3 + 2 * n_kv]
    o_ref = refs[3 + 2 * n_kv]
    dl = lam_ref[...]
    lam = (jnp.exp(jnp.sum(dl[0:1] * dl[1:2], axis=-1, keepdims=True))
           - jnp.exp(jnp.sum(dl[2:3] * dl[3:4], axis=-1, keepdims=True)) + lam_init)
    out = None
    for h in range(H_D):
        probs = []
        for m in range(2):
            sl = slice((2 * h + m) * HEAD_PAD, (2 * h + m + 1) * HEAD_PAD)
            qh = q_ref[:, sl]
            ps, denom = _softmax_parts([_nt_dot(qh, k[:, sl]) for k in k_refs])
            probs.append((ps, (1.0 if m == 0 else lam) / denom))
        mask = _head_lane_mask(H_D * DV_D, h, DV_D)
        (ps0, a0), (ps1, a1) = probs
        acc = None
        for p0, p1, v in zip(ps0, ps1, v_refs):
            wgt = (p0 * a0 - p1 * a1).astype(BF16)
            term = _dot(wgt, jnp.where(mask, v[...], jnp.zeros((), BF16)))
            acc = term if acc is None else acc + term
        ms = jnp.sum(acc * acc, axis=-1, keepdims=True) * (1.0 / DV_D)
        acc = acc * lax.rsqrt(ms + EPS)
        out = acc if out is None else out + acc
    o_ref[...] = (out * sub_ref[...] * (1.0 - lam_init)).astype(BF16)


def _diff_attention(layer, n_seq, seq_len, row0, total_rows, lam_p, sub_g, q, k, v, k_cache=None, v_cache=None,
                    prev=None):
    tq = min(Q_TILE, seq_len)
    has_cache = k_cache is not None
    cache_len = k_cache.shape[0] // n_seq if has_cache else None
    specs, q_spec = _seq_specs(n_seq, seq_len, row0, tq, DQK_W, (DQK_W, H_D * DV_D), True, cache_len)
    specs = [specs[0], pl.BlockSpec((4, DH_D), lambda b, i: (0, 0)), pl.BlockSpec((1, H_D * DV_D), lambda b, i: (0, 0))
             ] + specs[1:]
    args = [q, lam_p, jnp.tile(sub_g, H_D).reshape(1, H_D * DV_D)]
    args += ([k_cache, k] if has_cache else [k]) + ([v_cache, v] if has_cache else [v])
    lam_init = 0.8 - 0.6 * math.exp(-0.3 * layer)
    body = functools.partial(_diff_attn_kernel, n_kv=2 if has_cache else 1, lam_init=lam_init)
    body, specs, args, aliases = _keep_rows_of(prev, body, specs, args)
    return pl.pallas_call(
        body,
        grid=(n_seq, seq_len // tq),
        in_specs=specs,
        out_specs=q_spec(H_D * DV_D),
        out_shape=jax.ShapeDtypeStruct((total_rows, H_D * DV_D), BF16),
        input_output_aliases=aliases,
        compiler_params=_cparams(2),
        name="diff_attention",
    )(*args)


A_W = H_A * DK_A


HGRN_LEVELS = (2, 4, 8, 16, 32, 64, 128)
PAIR_W = 2 * DK_A


def _hgrn_constants():
    t = np.arange(HGRN_BLOCK)
    lower = t[None, :] <= t[:, None]
    upper = t[None, :] >= t[:, None]
    masks = [t[:, None] == t[None, :]] + [(t[:, None] // b) == (t[None, :] // b) for b in HGRN_LEVELS[:-1]]
    masks = np.stack([np.tile(m, (1, 2)) for m in masks]).astype(np.float32)
    g = np.arange(A_W)
    heads = (g[:, None] // DK_A) == (g[None, :] // DK_A)
    as_bf16 = lambda m: jnp.asarray(m.astype(np.float32), BF16)
    pair_mask = heads[:PAIR_W, :PAIR_W].astype(np.float32)
    return as_bf16(lower), as_bf16(upper), as_bf16(heads), jnp.asarray(pair_mask), jnp.asarray(masks)


def _level_reference(cum, b, forward):
    off = b // 2 - 1 if forward else b // 2
    if b >= 8:
        c3 = cum.reshape(HGRN_BLOCK // b, b, A_W)
        return jnp.broadcast_to(c3[:, off : off + 1, :], c3.shape).reshape(HGRN_BLOCK, A_W)
    c3 = cum.reshape(HGRN_BLOCK // 8, 8, A_W)
    sub = lax.broadcasted_iota(jnp.int32, (1, 8, 1), 1)
    out = None
    for g in range(8 // b):
        cand = jnp.broadcast_to(c3[:, g * b + off : g * b + off + 1, :], c3.shape)
        out = cand if out is None else jnp.where(sub >= g * b, cand, out)
    return out.reshape(HGRN_BLOCK, A_W)


def _hgrn_kernel(*refs, layer, seq_len, has_state):
    if has_state:
        pa_ref, lbl_ref, og_ref, s0_ref, lower_ref, upper_ref, heads_ref, hmask_ref, lvl_ref = refs[:9]
        rest = refs[9:]
    else:
        pa_ref, lbl_ref, og_ref, lower_ref, upper_ref, heads_ref, hmask_ref, lvl_ref = refs[:8]
        s0_ref = None
        rest = refs[8:]
    o_ref, sout_ref, st_ref, oacc_ref = rest
    n_blocks = seq_len // HGRN_BLOCK
    row = lax.broadcasted_iota(jnp.int32, (HGRN_BLOCK, 1), 0)
    first_head = lax.broadcasted_iota(jnp.int32, (1, PAIR_W), 1) < DK_A
    zero_bf16 = jnp.zeros((), BF16)

    def per_head_rows(x):
        return jnp.concatenate([jnp.where(first_head, x, zero_bf16), jnp.where(first_head, zero_bf16, x)], axis=0)

    st_ref[...] = jnp.zeros(st_ref.shape, F32)
    lower_bounds = []
    for d in range(2):
        logits = lbl_ref[d]
        e = jnp.exp(logits - jnp.max(logits, axis=0, keepdims=True))
        p = e / jnp.sum(e, axis=0, keepdims=True)
        lb = jnp.zeros((1, A_W), F32)
        for j in range(1, layer + 1):
            lb = lb + p[j : j + 1, :]
        lower_bounds.append(lb)
        if has_state:
            for h in range(H_A):
                off = (h % 2) * DK_A
                st_ref[d, h // 2, off : off + DV_A, off : off + DK_A] = s0_ref[0, d, h].T

    def both_directions(i, carry):
        for d in range(2):
            lb = lower_bounds[d]
            tri_ref = lower_ref if d == 0 else upper_ref
            blk = i if d == 0 else n_blocks - 1 - i
            r0 = pl.multiple_of(blk * HGRN_BLOCK, HGRN_BLOCK)
            rows = pl.ds(r0, HGRN_BLOCK)
            q = _silu(pa_ref[rows, 0:A_W]) * DK_A ** -0.5
            v = pa_ref[rows, A_W : 2 * A_W]
            logit = pa_ref[rows, (2 + d) * A_W : (3 + d) * A_W]
            key = jnp.minimum((1.0 - lb) * jax.nn.sigmoid(-logit), MAX_INPUT_KEY)
            log_f = jnp.log1p(-key)
            cum = _dot_exact_lhs(tri_ref[...], log_f)
            tot = cum[HGRN_BLOCK - 1 : HGRN_BLOCK, :] if d == 0 else cum[0:1, :]
            v_b = v.astype(BF16)
            scores = [None] * (H_A // 2)

            def add_pairs(qd, kd, mask_index, scores=scores):
                qd_b, kd_b = qd.astype(BF16), kd.astype(BF16)
                for pi in range(H_A // 2):
                    lanes = slice(pi * PAIR_W, (pi + 1) * PAIR_W)
                    s = _nt_dot(qd_b[:, lanes], per_head_rows(kd_b[:, lanes]))
                    if mask_index is not None:
                        s = s * lvl_ref[mask_index]
                    scores[pi] = s if scores[pi] is None else scores[pi] + s

            add_pairs(q, key, 0)
            for li, b in enumerate(HGRN_LEVELS):
                later = ((row % b) >= b // 2) if d == 0 else ((row % b) < b // 2)
                if b == 2:
                    qd = jnp.where(later, q * jnp.exp(log_f), 0.0)
                    kd = jnp.where(later, 0.0, key)
                else:
                    ref = _level_reference(cum, b, d == 0)
                    qd = jnp.where(later, q * jnp.exp(cum - ref), 0.0)
                    kd = jnp.where(later, 0.0, key * jnp.exp(ref - cum))
                add_pairs(qd, kd, li + 1 if b < HGRN_BLOCK else None)
            q_dec = (q * jnp.exp(cum)).astype(BF16)
            k_dec = (key * jnp.exp(tot - cum)).astype(BF16)
            decay = jnp.exp(tot)
            parts = []
            for pi in range(H_A // 2):
                lanes = slice(pi * PAIR_W, (pi + 1) * PAIR_W)
                state = st_ref[d, pi]
                o_pair = _dot(scores[pi].astype(BF16), per_head_rows(v_b[:, lanes]))
                parts.append(o_pair + _nt_dot(q_dec[:, lanes], state.astype(BF16)))
                upd = lax.dot_general(v_b[:, lanes], k_dec[:, lanes], (((0,), (0,)), ((), ())),
                                      preferred_element_type=F32)
                st_ref[d, pi] = state * decay[:, lanes] + upd * hmask_ref[...]
            oacc_ref[d, rows, :] = jnp.concatenate(parts, axis=1)
        return carry

    lax.fori_loop(0, n_blocks, both_directions, 0)
    for d in range(2):
        for h in range(H_A):
            off = (h % 2) * DK_A
            sout_ref[0, d, h] = st_ref[d, h // 2, off : off + DV_A, off : off + DK_A].T

    o = oacc_ref[0] + oacc_ref[1]
    ms = _dot_exact_rhs(o * o, heads_ref[...], passes=2) * (1.0 / DV_A)
    o_ref[...] = (o * lax.rsqrt(ms + EPS) * og_ref[...] * _silu(pa_ref[:, 4 * A_W : 5 * A_W])).astype(BF16)


def _hgrn(layer, n_seq, seq_len, row0, total_rows, pa, lb_logits, onorm_g, s0, prev=None):
    has_state = s0 is not None
    consts = _hgrn_constants()
    seq_spec = lambda w: pl.BlockSpec((seq_len, w), lambda b: (row0 // seq_len + b, 0))
    state_spec = pl.BlockSpec((1, 2, H_A, DK_A, DV_A), lambda b: (b, 0, 0, 0, 0))
    in_specs = [seq_spec(PA_W), _const_spec(lb_logits.shape), _const_spec((1, A_W))]
    args = [pa, lb_logits, jnp.tile(onorm_g, H_A).reshape(1, A_W)]
    if has_state:
        in_specs.append(state_spec)
        args.append(s0)
    in_specs += [_const_spec(c.shape) for c in consts]
    args += list(consts)
    body = functools.partial(_hgrn_kernel, layer=layer, seq_len=seq_len, has_state=has_state)
    body, in_specs, args, aliases = _keep_rows_of(prev, body, in_specs, args)
    return pl.pallas_call(
        body,
        grid=(n_seq,),
        in_specs=in_specs,
        out_specs=[seq_spec(A_W), state_spec],
        out_shape=[
            jax.ShapeDtypeStruct((total_rows, A_W), BF16),
            jax.ShapeDtypeStruct((n_seq, 2, H_A, DK_A, DV_A), F32),
        ],
        input_output_aliases=aliases,
        scratch_shapes=[pltpu.VMEM((2, H_A // 2, 2 * DV_A, 2 * DK_A), F32), pltpu.VMEM((2, seq_len, A_W), F32)],
        compiler_params=_cparams(1),
        name="hgrn",
    )(*args)


HY_W = HY_ORDER * W_C
FFT_R = 64


def _hy_filter_kernel(w1_ref, b1_ref, w2_ref, b2_ref, w3_ref, ld_ref, hf_ref, hb_ref, *, n):
    f32dot = functools.partial(jnp.dot, preferred_element_type=F32, precision=HIGHEST)
    row = lax.broadcasted_iota(jnp.int32, (n, LANES), 0)
    lane = lax.broadcasted_iota(jnp.int32, (n, LANES), 1)
    tn = row.astype(F32) / n
    band = jnp.where(lane <= HY_BANDS, lane, lane - HY_BANDS).astype(F32)
    ang = (2.0 * math.pi) * tn * band
    feats = jnp.where(lane == 0, tn, jnp.where(lane <= HY_BANDS, jnp.cos(ang),
                                                jnp.where(lane <= 2 * HY_BANDS, jnp.sin(ang), 0.0)))
    h = jnp.sin(f32dot(feats, w1_ref[0]) + b1_ref[0])
    h = jnp.sin(f32dot(h, w2_ref[0]) + b2_ref[0])
    h = f32dot(h, w3_ref[0])
    h = h * jnp.exp(-jnp.exp(ld_ref[0]) * tn[:, 0:1])
    hf = h[:, :HY_W]
    hb = jnp.where(row[:, 0:1] == 0, 0.0, h[:, HY_W:])
    norm = jnp.sum(jnp.abs(hf), axis=0, keepdims=True) + jnp.sum(jnp.abs(hb), axis=0, keepdims=True) + EPS
    hf_ref[0] = hf / norm
    hb_ref[0] = hb / norm


def _hy_filters(n, w1, b1, w2, b2, w3, log_decay):
    w1p = jnp.pad(w1, ((0, 0), (0, LANES - HY_EMB), (0, 0)))
    lay = lambda shape: pl.BlockSpec((1,) + shape, lambda l: (l,) + (0,) * len(shape))
    return pl.pallas_call(
        functools.partial(_hy_filter_kernel, n=n),
        grid=(DEPTH,),
        in_specs=[lay((LANES, HY_FH)), lay((1, HY_FH)), lay((HY_FH, HY_FH)), lay((1, HY_FH)),
                  lay((HY_FH, 2 * HY_W)), lay((1, 2 * HY_W))],
        out_specs=[lay((n, HY_W)), lay((n, HY_W))],
        out_shape=[jax.ShapeDtypeStruct((DEPTH, n, HY_W), F32)] * 2,
        compiler_params=_cparams(1),
        name="hyena_filters",
    )(w1p, b1.reshape(DEPTH, 1, HY_FH), w2, b2.reshape(DEPTH, 1, HY_FH), w3, log_decay.reshape(DEPTH, 1, 2 * HY_W))


def _dft_tables_short(n):
    big = 2 * n
    k = np.arange(big)[:, None]
    t = np.arange(n)[None, :]
    ang = 2.0 * np.pi * ((k * t) % big) / big
    fwd = np.concatenate([np.cos(ang), -np.sin(ang)], axis=0)
    inv = np.concatenate([np.cos(ang).T, -np.sin(ang).T], axis=1) / big
    return fwd.astype(np.float32), inv.astype(np.float32)


def _hy_spec_short_kernel(hf_ref, hb_ref, fwd_ref, f_ref, *, n):
    f32dot = functools.partial(jnp.dot, preferred_element_type=F32, precision=HIGHEST)
    xf = f32dot(fwd_ref[...], hf_ref[0])
    xb = f32dot(fwd_ref[...], hb_ref[0])
    big = 2 * n
    f_ref[0, :big, :] = xf[:big] + xb[:big]
    f_ref[0, big:, :] = xf[big:] - xb[big:]


def _hy_spec_short(n, hf, hb):
    fwd, _ = _dft_tables_short(n)
    lay = lambda shape: pl.BlockSpec((1,) + shape, lambda l: (l,) + (0,) * len(shape))
    return pl.pallas_call(
        functools.partial(_hy_spec_short_kernel, n=n),
        grid=(DEPTH,),
        in_specs=[lay((n, HY_W)), lay((n, HY_W)), _const_spec(fwd.shape)],
        out_specs=lay((4 * n, HY_W)),
        out_shape=jax.ShapeDtypeStruct((DEPTH, 4 * n, HY_W), F32),
        compiler_params=_cparams(1),
        name="hyena_spectrum_short",
    )(hf, hb, jnp.asarray(fwd))


def _short_conv(x, w, n):
    row = lax.broadcasted_iota(jnp.int32, (n, 1), 0)
    prev = jnp.where(row == 0, 0.0, pltpu.roll(x, 1, 0))
    nxt = jnp.where(row == n - 1, 0.0, pltpu.roll(x, n - 1, 0))
    return prev * w[0:1, :] + x * w[1:2, :] + nxt * w[2:3, :]


def _hy_conv_short_kernel(pc_ref, ws_ref, bias_ref, f_ref, fwd_ref, inv_ref, o_ref, *, n):
    big = 2 * n
    u = _short_conv(pc_ref[...], ws_ref[...], n)
    v, x1, x2 = u[:, :W_C], u[:, W_C : 2 * W_C], u[:, 2 * W_C :]

    def conv(x, order):
        spec = _dot(fwd_ref[...], x.astype(BF16))
        fr = f_ref[:big, order * W_C : (order + 1) * W_C]
        fi = f_ref[big:, order * W_C : (order + 1) * W_C]
        zr = spec[:big] * fr - spec[big:] * fi
        zi = spec[:big] * fi + spec[big:] * fr
        return _dot(inv_ref[...], jnp.concatenate([zr, zi], axis=0).astype(BF16))

    z = x1 * (conv(v, 0) + v * bias_ref[0:1, :])
    z = x2 * (conv(z, 1) + z * bias_ref[1:2, :])
    o_ref[...] = z.astype(BF16)


def _hy_conv_short(n_seq, n, row0, total_rows, pc, w_short, bias, spec):
    fwd, inv = _dft_tables_short(n)
    seq_spec = lambda w: pl.BlockSpec((n, w), lambda b: (row0 // n + b, 0))
    return pl.pallas_call(
        functools.partial(_hy_conv_short_kernel, n=n),
        grid=(n_seq,),
        in_specs=[seq_spec(PC_W), _const_spec((SHORT_K, PC_W)), _const_spec((HY_ORDER, W_C)),
                  _const_spec((4 * n, HY_W)), _const_spec(fwd.shape), _const_spec(inv.shape)],
        out_specs=seq_spec(W_C),
        out_shape=jax.ShapeDtypeStruct((total_rows, W_C), BF16),
        compiler_params=_cparams(1),
        name="hyena_conv_short",
    )(pc, w_short, bias, spec, jnp.asarray(fwd, BF16), jnp.asarray(inv, BF16))


def _dft_tables_long():
    r = FFT_R
    big = r * r
    half = r // 2
    n2 = np.arange(r)[:, None, None]
    k1 = np.arange(r)[None, :, None]
    n1 = np.arange(half)[None, None, :]
    ang = 2.0 * np.pi * ((k1 * (r * n1 + n2)) % big) / big
    first = np.concatenate([np.cos(ang), -np.sin(ang)], axis=1)
    last = np.concatenate([np.cos(ang), -np.sin(ang)], axis=1).transpose(0, 2, 1) / big
    a = np.arange(r)
    ang_r = 2.0 * np.pi * ((a[:, None] * a[None, :]) % r) / r
    c, s = np.cos(ang_r), np.sin(ang_r)
    mid = np.block([[c, s], [-s, c]])
    mid_inv = np.block([[c, -s], [s, c]])
    f32 = lambda m: m.astype(np.float32)
    return f32(first), f32(mid), f32(mid_inv), f32(last)


FFT_K1 = FFT_R // 2 + 1
FFT_K1_PAD = 40
FFT_K1_UNROLL = 11


def _dft_tables_long_half():
    first, mid, mid_inv, last = _dft_tables_long()
    r = FFT_R
    keep = np.zeros((FFT_K1_PAD,), np.float32)
    keep[:FFT_K1] = 1.0
    weight = np.zeros((FFT_K1_PAD,), np.float32)
    weight[:FFT_K1] = 2.0
    weight[0] = weight[r // 2] = 1.0
    first_h = np.concatenate([first[:, :FFT_K1_PAD] * keep[None, :, None],
                              first[:, r : r + FFT_K1_PAD] * keep[None, :, None]], axis=1)
    last_h = np.concatenate([last[:, :, :FFT_K1_PAD] * weight, last[:, :, r : r + FFT_K1_PAD] * weight], axis=2)
    return first_h, mid, mid_inv, last_h


def _ld_rows(ref, rows):
    return jnp.concatenate([ref[j, rows, :] for j in range(ref.shape[0])], axis=-1)


def _st_rows(ref, rows, val):
    for j in range(ref.shape[0]):
        ref[j, rows, :] = val[:, j * LANES : (j + 1) * LANES]


def _st_transposed(ref, j, val, half):
    n = val.shape[0] // 2
    _st_rows(ref, pl.ds(j, n, stride=2 * half), val[:n])
    _st_rows(ref, pl.ds(half + j, n, stride=2 * half), val[n:])


def _lane_split_scratch(rows, width):
    return pltpu.VMEM((width // LANES, rows, LANES), F32)


FFT_UNROLL = 8


def _fft_long_forward(x_ref, y_ref, dot_first, dot_mid, out_fn, n_k1=FFT_R, unroll_k1=FFT_UNROLL):
    r = FFT_R

    def stage_a(n2, carry):
        res = dot_first(n2, _ld_rows(x_ref, pl.ds(n2, r // 2, stride=r)))
        _st_transposed(y_ref, n2, res, r)
        return carry

    lax.fori_loop(0, r, stage_a, 0, unroll=FFT_UNROLL)

    def stage_c(k1, carry):
        base = pl.multiple_of(k1 * 2 * r, 2 * r)
        out_fn(k1, base, dot_mid(_ld_rows(y_ref, pl.ds(base, 2 * r))))
        return carry

    lax.fori_loop(0, n_k1, stage_c, 0, unroll=unroll_k1)


def _dot_split(t_hi, t_lo, x):
    x_hi = x.astype(BF16)
    x_lo = (x - x_hi.astype(F32)).astype(BF16)
    return _dot(t_hi, x_hi) + _dot(t_hi, x_lo) + _dot(t_lo, x_hi)


def _split_table(m):
    hi = jnp.asarray(m, BF16)
    lo = (jnp.asarray(m) - hi.astype(F32)).astype(BF16)
    return hi, lo


def _hy_spec_long_kernel(hf_ref, hb_ref, first_hi_ref, first_lo_ref, mid_hi_ref, mid_lo_ref, f_ref,
                         x_ref, y_ref, tmp_ref):
    r = FFT_R
    dot_first = lambda n2, slab: _dot_split(first_hi_ref[n2], first_lo_ref[n2], slab)
    dot_mid = lambda block: _dot_split(mid_hi_ref[...], mid_lo_ref[...], block)

    def write_fwd(k1, base, spec):
        tmp_ref[pl.ds(base, 2 * r), :] = spec

    _st_rows(x_ref, slice(None), hf_ref[0])
    _fft_long_forward(x_ref, y_ref, dot_first, dot_mid, write_fwd, n_k1=FFT_K1, unroll_k1=FFT_K1_UNROLL)

    def write_sum(k1, base, spec):
        prev = tmp_ref[pl.ds(base, 2 * r), :]
        f_ref[0, pl.ds(base, r), :] = prev[:r] + spec[:r]
        f_ref[0, pl.ds(base + r, r), :] = prev[r:] - spec[r:]

    _st_rows(x_ref, slice(None), hb_ref[0])
    _fft_long_forward(x_ref, y_ref, dot_first, dot_mid, write_sum, n_k1=FFT_K1, unroll_k1=FFT_K1_UNROLL)


SPEC_ROWS = 2 * FFT_R * FFT_K1


def _hy_spec_long(n, hf, hb):
    first, mid, _, _ = _dft_tables_long_half()
    lay = lambda rows: pl.BlockSpec((1, rows, W_C), lambda l, o: (l, 0, o))
    return pl.pallas_call(
        _hy_spec_long_kernel,
        grid=(DEPTH, HY_ORDER),
        in_specs=[lay(n), lay(n)] + [_const_spec(first.shape)] * 2 + [_const_spec(mid.shape)] * 2,
        out_specs=lay(SPEC_ROWS),
        out_shape=jax.ShapeDtypeStruct((DEPTH, SPEC_ROWS, HY_W), F32),
        scratch_shapes=[_lane_split_scratch(n, W_C), _lane_split_scratch(2 * FFT_R * FFT_K1_PAD, W_C),
                        pltpu.VMEM((SPEC_ROWS, W_C), F32)],
        compiler_params=_cparams(2),
        name="hyena_spectrum_long",
    )(hf, hb, *_split_table(first), *_split_table(mid))


def _hy_conv_long_kernel(sig_ref, gate_ref, ws_ref, bias_ref, f_ref, first_ref, mid_ref, midinv_ref, last_ref, o_ref,
                         x_ref, y_ref, v_ref, out_ref, *, n, order):
    r = FFT_R
    to_bf16 = lambda x: x.astype(BF16)
    gate = _short_conv(gate_ref[...], ws_ref[:, (order + 1) * W_C : (order + 2) * W_C], n)
    sig = _short_conv(sig_ref[...], ws_ref[:, :W_C], n) if order == 0 else sig_ref[...]
    _st_rows(x_ref, slice(None), sig)
    v_ref[...] = jnp.zeros(v_ref.shape, F32)

    def filter_and_invert(k1, base, spec):
        fr = f_ref[pl.ds(base, r), :]
        fi = f_ref[pl.ds(base + r, r), :]
        zr = spec[:r] * fr - spec[r:] * fi
        zi = spec[:r] * fi + spec[r:] * fr
        res = _dot(midinv_ref[...], jnp.concatenate([zr, zi], axis=0).astype(BF16))
        _st_transposed(v_ref, k1, res, FFT_K1_PAD)

    dot_first = lambda n2, slab: _dot(first_ref[n2], to_bf16(slab))
    dot_mid = lambda block: _dot(mid_ref[...], to_bf16(block))
    _fft_long_forward(x_ref, y_ref, dot_first, dot_mid, filter_and_invert, n_k1=FFT_K1, unroll_k1=FFT_K1_UNROLL)

    def stage_last(n2, carry):
        base = pl.multiple_of(n2 * 2 * FFT_K1_PAD, 2 * FFT_K1_PAD)
        res = _dot(last_ref[n2], _ld_rows(v_ref, pl.ds(base, 2 * FFT_K1_PAD)).astype(BF16))
        _st_rows(out_ref, pl.ds(n2, r // 2, stride=r), res)
        return carry

    lax.fori_loop(0, r, stage_last, 0, unroll=FFT_UNROLL)
    z = gate * (_ld_rows(out_ref, slice(None)) + sig * bias_ref[order : order + 1, :])
    o_ref[...] = z.astype(o_ref.dtype)


def _hy_conv_long(n_seq, n, row0, total_rows, pc, w_short, bias, spec, prev=None):
    assert 2 * n == FFT_R * FFT_R and FFT_K1 % FFT_K1_UNROLL == 0
    first, mid, mid_inv, last = _dft_tables_long_half()
    big = 2 * n
    bf = lambda m: jnp.asarray(m, BF16)
    lane_block = lambda j: pl.BlockSpec((n, W_C), lambda b: (row0 // n + b, j))
    z = None
    for order in range(HY_ORDER):
        final = order + 1 == HY_ORDER
        in_specs = [lane_block(0), lane_block(order + 1), _const_spec((SHORT_K, PC_W)),
                    _const_spec((HY_ORDER, W_C)), pl.BlockSpec((SPEC_ROWS, W_C), lambda b, order=order: (0, order)),
                    _const_spec(first.shape), _const_spec(mid.shape), _const_spec(mid_inv.shape),
                    _const_spec(last.shape)]
        args = [pc if order == 0 else z, pc, w_short, bias, spec, bf(first), bf(mid), bf(mid_inv), bf(last)]
        body = functools.partial(_hy_conv_long_kernel, n=n, order=order)
        body, in_specs, args, aliases = _keep_rows_of(prev if final else None, body, in_specs, args)
        z = pl.pallas_call(
            body,
            grid=(n_seq,),
            in_specs=in_specs,
            out_specs=lane_block(0),
            out_shape=jax.ShapeDtypeStruct((total_rows, W_C), BF16 if final else F32),
            input_output_aliases=aliases,
            scratch_shapes=[_lane_split_scratch(n, W_C), _lane_split_scratch(2 * FFT_R * FFT_K1_PAD, W_C),
                            _lane_split_scratch(2 * FFT_K1_PAD * FFT_R, W_C), _lane_split_scratch(n, W_C)],
            compiler_params=_cparams(1),
            name="hyena_conv_long",
        )(*args)
    return z


def _pad_in_weights(w_in):
    gap = jnp.zeros(w_in.shape[:2] + (PB_W - MLA_IN,), BF16)
    w = w_in.astype(BF16)
    return jnp.concatenate([w[..., : PA_W + MLA_IN], gap, w[..., PA_W + MLA_IN :]], axis=-1)


def kernel(x_prompt, x_sample, c, cache_mla, cache_diff_k, cache_diff_v, state_hgrn, c_ctx, w_mod, b_mod, norm_g,
           ffn_w_gu, ffn_w_down, w_in, w_out, hgrn_lb_logits, hgrn_onorm, mla_q_norm, mla_kv_norm, mla_w_uq,
           mla_w_ukv, mla_qk_norm, hy_short, hy_w1, hy_b1, hy_w2, hy_b2, hy_w3, hy_log_decay, hy_bias,
           diff_qk_norm, diff_lambda, diff_subln):
    bc, sc, _ = x_prompt.shape
    bl, sl, _ = x_sample.shape
    past = cache_mla.shape[2]
    tok = _Tokens(bc, sc, bl, sl)
    assert bl + 1 <= MOD_ROWS and tok.tc % sl == 0

    x = jnp.concatenate([x_prompt.reshape(tok.tc, D_MODEL), x_sample.reshape(tok.tl, D_MODEL)], axis=0)
    cond = jnp.concatenate([c_ctx[None], c, jnp.zeros((MOD_ROWS - 1 - bl, D_MODEL), F32)], axis=0)
    mod = _modulation(cond, w_mod, b_mod)

    filters = {n: _hy_filters(n, hy_w1, hy_b1, hy_w2, hy_b2, hy_w3, hy_log_decay) for n in (sc, sl)}
    spec_ctx = _hy_spec_short(sc, *filters[sc])
    spec_lat = _hy_spec_long(sl, *filters[sl])

    mla_tables = _rope_tables(tok, [NOPE_B], HEAD_PAD)
    diff_tables = _rope_tables(tok, list(range(0, H_D * 2 * DH_D, DH_D)), H_D * 2 * DH_D)

    w_gu_b, w_down_b, w_out_b, w_in_b = (ffn_w_gu.astype(BF16), ffn_w_down.astype(BF16), w_out.astype(BF16),
                                         _pad_in_weights(w_in))

    new_mla, new_dk, new_dv, new_state = [], [], [], []
    for l in range(DEPTH):
        x = _ffn(tok, x, mod, l, 0, norm_g[l, 0], w_gu_b, w_down_b)
        pa, pb, pc, pd = _inproj(tok, x, mod, l, norm_g[l, 1], w_in_b)
        mla_w = _mla_weights(mla_w_uq[l], mla_w_ukv[l], mla_qk_norm[l])
        q_b, k_b, v_b, cache_b = _prep_call(
            tok, _mla_prep_body, "mla_prep", pb,
            _mla_prep_operands(tok, mla_q_norm[l], mla_kv_norm[l], mla_w, mla_tables))
        q_d, k_d, v_d, kcache_d, vcache_d = _prep_call(
            tok, _diff_prep_body, "diff_prep", pd, _diff_prep_operands(tok, diff_qk_norm[l], diff_tables))

        o_a, s_ctx = _hgrn(l, bc, sc, 0, tok.t, pa, hgrn_lb_logits, hgrn_onorm[l], None)
        o_a, _ = _hgrn(l, bl, sl, tok.tc, tok.t, pa, hgrn_lb_logits, hgrn_onorm[l], state_hgrn[:, l], prev=o_a)

        cache_rows = jnp.pad(cache_mla[:, l].reshape(bl * past, KV_LORA + ROPE_B),
                             ((0, 0), (0, 2 * LANES - KV_LORA - ROPE_B)))
        kc_b, vc_b = _mla_cache_prep(cache_rows, mla_w)
        o_b = _mla_attention(bc, sc, 0, tok.t, q_b, k_b, v_b)
        o_b = _mla_attention(bl, sl, tok.tc, tok.t, q_b, k_b, v_b, kc_b, vc_b, prev=o_b)

        o_c = _hy_conv_short(bc, sc, 0, tok.t, pc, hy_short[l], hy_bias[l], spec_ctx[l])
        o_c = _hy_conv_long(bl, sl, tok.tc, tok.t, pc, hy_short[l], hy_bias[l], spec_lat[l], prev=o_c)

        kc_d, vc_d = _diff_cache_prep(cache_diff_k[:, l].reshape(bl * past, H_D * 2 * DH_D),
                                      cache_diff_v[:, l].reshape(bl * past, H_D * DV_D))
        o_d = _diff_attention(l, bc, sc, 0, tok.t, diff_lambda[l], diff_subln[l], q_d, k_d, v_d)
        o_d = _diff_attention(l, bl, sl, tok.tc, tok.t, diff_lambda[l], diff_subln[l], q_d, k_d, v_d, kc_d, vc_d,
                              prev=o_d)

        x = _ffn(tok, x, mod, l, 1, norm_g[l, 2], w_gu_b, w_down_b, mixer_outs=[o_a, o_b, o_c, o_d], w_out=w_out_b)

        new_mla.append(cache_b[: tok.tc, : KV_LORA + ROPE_B].reshape(bc, sc, KV_LORA + ROPE_B))
        new_dk.append(kcache_d[: tok.tc].reshape(bc, sc, H_D, 2, DH_D))
        new_dv.append(vcache_d[: tok.tc].reshape(bc, sc, H_D, DV_D))
        new_state.append(s_ctx)

    y_prompt = x[: tok.tc].reshape(bc, sc, D_MODEL)
    y_sample = x[tok.tc :].reshape(bl, sl, D_MODEL)
    return (y_prompt, y_sample, jnp.stack(new_mla, axis=1), jnp.stack(new_dk, axis=1), jnp.stack(new_dv, axis=1),
            jnp.stack(new_state, axis=1))
```

```python
import functools
import math

import jax
import jax.numpy as jnp
import numpy as np
from jax import lax
from jax.experimental import pallas as pl
from jax.experimental.pallas import tpu as pltpu

F32 = jnp.float32
BF16 = jnp.bfloat16
HIGHEST = lax.Precision.HIGHEST

D_MODEL = 1024
DEPTH = 4
GRID_W = 64
N_MOD = 9
D_FF = 2816
EPS = 1e-6
ROPE_BASE = 10000.0
GROUP_W = 256
H_A, DK_A, DV_A = 4, 64, 64
MAX_INPUT_KEY = 1.0 - 1e-6
H_B, NOPE_B, ROPE_B, V_B = 4, 64, 32, 64
Q_LORA, KV_LORA = 256, 128
W_C, HY_ORDER, HY_BANDS, HY_FH, SHORT_K = 256, 2, 8, 64, 3
HY_EMB = 1 + 2 * HY_BANDS
H_D, DV_D, DH_D = 4, 64, 32

LANES = 128
MOD_ROWS = 16
TOKEN_TILE = 512
Q_TILE = 256
KEY_TILE = 512
ATTN_LOOKAHEAD = 3
HGRN_BLOCK = 128
VMEM_LIMIT = 56 * 1024 * 1024

PA_W, PB_W, PC_W, PD_W = 1280, 512, 768, 768
IN_PAD_W = PA_W + PB_W + PC_W + PD_W
MLA_IN = Q_LORA + KV_LORA + ROPE_B
HEAD_PAD = 128


def _cparams(n_axes):
    return pltpu.CompilerParams(dimension_semantics=("arbitrary",) * n_axes, vmem_limit_bytes=VMEM_LIMIT)


def _nt_dot(a, b):
    return lax.dot_general(a, b, (((1,), (1,)), ((), ())), preferred_element_type=F32)


def _dot(a, b):
    return jnp.dot(a, b, preferred_element_type=F32)


def _dot_exact_rhs(a, b_bf16, passes=3):
    out = None
    rem = a
    for _ in range(passes):
        piece = rem.astype(BF16)
        term = _dot(piece, b_bf16)
        out = term if out is None else out + term
        rem = rem - piece.astype(F32)
    return out


def _dot_exact_lhs(a_bf16, b, passes=3):
    out = None
    rem = b
    for _ in range(passes):
        piece = rem.astype(BF16)
        term = _dot(a_bf16, piece)
        out = term if out is None else out + term
        rem = rem - piece.astype(F32)
    return out


def _silu(x):
    return x * jax.nn.sigmoid(x)


def _adaln(x, g, sc, sh):
    y = x * lax.rsqrt(jnp.mean(x * x, axis=-1, keepdims=True) + EPS)
    return (y * g) * (1.0 + sc) + sh


def _rope(x, cos, sin_hi, sin_lo):
    w = x.shape[-1]
    return x * cos + pltpu.roll(x, 8, 1) * sin_hi + pltpu.roll(x, w - 8, 1) * sin_lo


def _mod_kernel(c_ref, w_ref, b_ref, o_ref):
    a = _silu(c_ref[...])
    o_ref[0] = jnp.dot(a, w_ref[0], preferred_element_type=F32, precision=HIGHEST) + b_ref[0]


def _modulation(cond, w_mod, b_mod):
    tn = D_MODEL
    n_col = N_MOD * D_MODEL
    out = pl.pallas_call(
        _mod_kernel,
        grid=(DEPTH, n_col // tn),
        in_specs=[
            pl.BlockSpec((MOD_ROWS, D_MODEL), lambda l, j: (0, 0)),
            pl.BlockSpec((1, D_MODEL, tn), lambda l, j: (l, 0, j)),
            pl.BlockSpec((1, 1, tn), lambda l, j: (l, 0, j)),
        ],
        out_specs=pl.BlockSpec((1, MOD_ROWS, tn), lambda l, j: (l, 0, j)),
        out_shape=jax.ShapeDtypeStruct((DEPTH, MOD_ROWS, n_col), F32),
        compiler_params=_cparams(2),
        name="modulation",
    )(cond, w_mod, b_mod.reshape(DEPTH, 1, n_col))
    return out.reshape(DEPTH * MOD_ROWS, N_MOD, D_MODEL)


class _Tokens:
    def __init__(self, n_ctx_seq, ctx_len, n_lat_seq, lat_len):
        self.bc, self.sc, self.bl, self.sl = n_ctx_seq, ctx_len, n_lat_seq, lat_len
        self.tc, self.tl = n_ctx_seq * ctx_len, n_lat_seq * lat_len
        self.t = self.tc + self.tl
        self.tm = min(TOKEN_TILE, lat_len)
        assert self.tc % self.tm == 0 and lat_len % self.tm == 0
        self.ctx_tiles = self.tc // self.tm
        self.lat_tiles_per_seq = lat_len // self.tm
        self.n_tiles = self.t // self.tm

    def mod_row(self, layer):
        def f(i):
            lat = 1 + (i - self.ctx_tiles) // self.lat_tiles_per_seq
            return layer * MOD_ROWS + jnp.where(i < self.ctx_tiles, 0, lat)

        return f

    def rope_block(self, i):
        return jnp.where(i < self.ctx_tiles, 0, 1 + (i - self.ctx_tiles) % self.lat_tiles_per_seq)


def _mod_spec(tok, layer):
    row = tok.mod_row(layer)
    return pl.BlockSpec((1, N_MOD, D_MODEL), lambda i: (row(i), 0, 0))


def _const_spec(shape):
    zeros = (0,) * len(shape)
    return pl.BlockSpec(shape, lambda *_: zeros)


def _row_spec(tm, width):
    return pl.BlockSpec((tm, width), lambda i: (i, 0))


def _keep_rows_of(prev, kernel_fn, in_specs, args):
    if prev is None:
        return kernel_fn, in_specs, args, {}
    idx = len(in_specs)

    def body(*refs):
        return kernel_fn(*refs[:idx], *refs[idx + 1 :])

    return body, in_specs + [pl.BlockSpec(memory_space=pl.ANY)], args + [prev], {idx: 0}


def _ffn_kernel(*refs, mod_base, split_in, fused, ctx_tiles):
    n_x = 2 if split_in else 1
    x_refs, (mod_ref, g_ref, wgu_ref, wd_ref), rest = refs[:n_x], refs[n_x : n_x + 4], refs[n_x + 4 :]
    is_ctx = pl.program_id(0) < ctx_tiles
    x = jnp.where(is_ctx, x_refs[0][...], x_refs[1][...]) if split_in else x_refs[0][...]
    if fused:
        oa_ref, ob_ref, oc_ref, od_ref, wo_ref = rest[:5]
        mixed = jnp.concatenate([oa_ref[...], ob_ref[...], oc_ref[...], od_ref[...]], axis=-1)
        x = x + mod_ref[0, 5:6, :] * _dot(mixed, wo_ref[...])
    sh = mod_ref[0, mod_base : mod_base + 1, :]
    sc = mod_ref[0, mod_base + 1 : mod_base + 2, :]
    gate = mod_ref[0, mod_base + 2 : mod_base + 3, :]
    h = _adaln(x, g_ref[...], sc, sh).astype(BF16)
    au = _dot(h, wgu_ref[...])
    act = (_silu(au[:, :D_FF]) * au[:, D_FF:]).astype(BF16)
    rest[-1][...] = x + (0.5 * gate) * _dot(act, wd_ref[...])


def _resident(lead_index, shape):
    zeros = (0,) * len(shape)
    return pl.BlockSpec((None,) * len(lead_index) + tuple(shape), lambda i: tuple(lead_index) + zeros,
                        pipeline_mode=pl.Buffered(1))


def _ffn(tok, x, mod, layer, which, norm_g, w_gu, w_down, mixer_outs=None, w_out=None):
    tm = tok.tm
    split_in = isinstance(x, tuple)
    ctx_spec = pl.BlockSpec((tm, D_MODEL), lambda i: (jnp.minimum(i, tok.ctx_tiles - 1), 0))
    lat_spec = pl.BlockSpec((tm, D_MODEL), lambda i: (jnp.maximum(i - tok.ctx_tiles, 0), 0))
    in_specs = ([ctx_spec, lat_spec] if split_in else [_row_spec(tm, D_MODEL)]) + [
        _mod_spec(tok, layer),
        _const_spec((1, D_MODEL)),
        _resident((layer, which), (D_MODEL, 2 * D_FF)),
        _resident((layer, which), (D_FF, D_MODEL)),
    ]
    args = (list(x) if split_in else [x]) + [mod, norm_g.reshape(1, D_MODEL), w_gu, w_down]
    if mixer_outs is not None:
        in_specs += [_row_spec(tm, GROUP_W)] * 4
        in_specs.append(_resident((layer,), (D_MODEL, D_MODEL)))
        args += list(mixer_outs) + [w_out]
    return pl.pallas_call(
        functools.partial(_ffn_kernel, mod_base=6 * which, split_in=split_in, fused=mixer_outs is not None,
                          ctx_tiles=tok.ctx_tiles),
        grid=(tok.n_tiles,),
        in_specs=in_specs,
        out_specs=_row_spec(tm, D_MODEL),
        out_shape=jax.ShapeDtypeStruct((tok.t, D_MODEL), F32),
        compiler_params=_cparams(1),
        name="ffn",
    )(*args)


def _inproj_kernel(x_ref, mod_ref, g_ref, w_ref, pa_ref, pb_ref, pc_ref, pd_ref):
    h = _adaln(x_ref[...], g_ref[...], mod_ref[0, 4:5, :], mod_ref[0, 3:4, :]).astype(BF16)
    p = _dot(h, w_ref[...])
    pa_ref[...] = p[:, :PA_W]
    pb_ref[...] = p[:, PA_W : PA_W + PB_W]
    pc_ref[...] = p[:, PA_W + PB_W : PA_W + PB_W + PC_W]
    pd_ref[...] = p[:, PA_W + PB_W + PC_W :]


def _inproj(tok, x, mod, layer, norm_g, w_in_pad):
    tm = tok.tm
    widths = (PA_W, PB_W, PC_W, PD_W)
    return pl.pallas_call(
        _inproj_kernel,
        grid=(tok.n_tiles,),
        in_specs=[
            _row_spec(tm, D_MODEL),
            _mod_spec(tok, layer),
            _const_spec((1, D_MODEL)),
            _resident((layer,), (D_MODEL, IN_PAD_W)),
        ],
        out_specs=[_row_spec(tm, w) for w in widths],
        out_shape=[jax.ShapeDtypeStruct((tok.t, w), F32) for w in widths],
        compiler_params=_cparams(1),
        name="inproj",
    )(x, mod, norm_g.reshape(1, D_MODEL), w_in_pad)


def _rope_group_tables(n_tok):
    t = np.arange(n_tok)
    pos = np.stack([t // GRID_W, t % GRID_W], axis=1).astype(np.float32)
    inv = (ROPE_BASE ** (-np.arange(8, dtype=np.float32) / 8)).astype(np.float32)
    lane = np.arange(32)
    ang = (pos[:, lane // 16] * inv[lane % 8][None, :]).astype(np.float32)
    second = (lane % 16) >= 8
    cos, sin = np.cos(ang), np.sin(ang)
    return cos, np.where(second[None], sin, 0.0), np.where(second[None], 0.0, -sin)


def _rope_tables(tok, lane_groups, width):
    cos = np.ones((tok.tm + tok.sl, width), np.float32)
    s_hi = np.zeros_like(cos)
    s_lo = np.zeros_like(cos)
    c, a, b = _rope_group_tables(tok.sl)
    for g in lane_groups:
        cos[tok.tm :, g : g + 32] = c
        s_hi[tok.tm :, g : g + 32] = a
        s_lo[tok.tm :, g : g + 32] = b
    return jnp.asarray(cos), jnp.asarray(s_hi), jnp.asarray(s_lo)


def _mla_keys_values(kv_in, wk_ref, wv_ref, kn_ref, rope):
    kraw = _dot(kv_in.astype(BF16), wk_ref[...])
    v = _dot(kv_in[:, :KV_LORA].astype(BF16), wv_ref[...])
    ks = []
    for h in range(H_B):
        kh = kraw[:, h * HEAD_PAD : (h + 1) * HEAD_PAD]
        ss = jnp.sum(kh * kh, axis=-1, keepdims=True) * (1.0 / (NOPE_B + ROPE_B))
        kh = kh * lax.rsqrt(ss + EPS) * kn_ref[...]
        if rope is not None:
            kh = _rope_by_matmul(kh, *rope)
        ks.append(kh.astype(BF16))
    return ks, v.astype(BF16)


def _rope_by_matmul(x, cos, sin, perm):
    return x * cos + _dot(x.astype(BF16), perm) * sin


def _mla_prep_body(pb, gq_ref, gkv_ref, wuq_ref, wk_ref, wv_ref, qn_ref, kn_ref, cos_ref, shi_ref, slo_ref, perm_ref,
                   q_ref, k_ref, v_ref, cache_ref):
    rope = (cos_ref[...], shi_ref[...] + slo_ref[...], perm_ref[...])
    cq = pb[:, :Q_LORA]
    cq = cq * lax.rsqrt(jnp.mean(cq * cq, axis=-1, keepdims=True) + EPS) * gq_ref[...]
    qraw = _dot(cq.astype(BF16), wuq_ref[...])
    scale = (NOPE_B + ROPE_B) ** -0.5 * LOG2E
    for h in range(H_B):
        qh = qraw[:, h * HEAD_PAD : (h + 1) * HEAD_PAD]
        ss = jnp.sum(qh * qh, axis=-1, keepdims=True) * (1.0 / (NOPE_B + ROPE_B))
        qh = _rope_by_matmul(qh * lax.rsqrt(ss + EPS) * qn_ref[...], *rope)
        q_ref[:, h * HEAD_PAD : (h + 1) * HEAD_PAD] = (qh * scale).astype(BF16)
    ckv = pb[:, Q_LORA : Q_LORA + KV_LORA]
    ckv = ckv * lax.rsqrt(jnp.mean(ckv * ckv, axis=-1, keepdims=True) + EPS) * gkv_ref[...]
    kv_in = jnp.concatenate([ckv, pb[:, Q_LORA + KV_LORA :]], axis=-1)
    cache_ref[...] = kv_in
    ks, v = _mla_keys_values(kv_in, wk_ref, wv_ref, kn_ref, rope)
    for h in range(H_B):
        k_ref[:, h * HEAD_PAD : (h + 1) * HEAD_PAD] = ks[h]
    v_ref[...] = v


def _mla_cache_kernel(c_ref, wk_ref, wv_ref, kn_ref, k_ref, v_ref):
    ks, v = _mla_keys_values(c_ref[...], wk_ref, wv_ref, kn_ref, None)
    for h in range(H_B):
        k_ref[:, h * HEAD_PAD : (h + 1) * HEAD_PAD] = ks[h]
    v_ref[...] = v


def _mla_weights(w_uq, w_ukv, qk_norm):
    wuq = jnp.pad(w_uq.reshape(Q_LORA, H_B, NOPE_B + ROPE_B), ((0, 0), (0, 0), (0, HEAD_PAD - NOPE_B - ROPE_B)))
    wuq = wuq.reshape(Q_LORA, H_B * HEAD_PAD).astype(BF16)
    ukv = w_ukv.reshape(KV_LORA, H_B, NOPE_B + V_B)
    wk_nope = jnp.pad(ukv[:, :, :NOPE_B], ((0, 0), (0, 0), (0, HEAD_PAD - NOPE_B)))
    place = np.zeros((2 * LANES - KV_LORA, H_B, HEAD_PAD), np.float32)
    for h in range(H_B):
        place[np.arange(ROPE_B), h, NOPE_B + np.arange(ROPE_B)] = 1.0
    wk = jnp.concatenate([wk_nope, jnp.asarray(place)], axis=0).reshape(2 * LANES, H_B * HEAD_PAD).astype(BF16)
    wv = ukv[:, :, NOPE_B:].reshape(KV_LORA, H_B * V_B).astype(BF16)
    pad = lambda g: jnp.pad(g, (0, HEAD_PAD - NOPE_B - ROPE_B)).reshape(1, HEAD_PAD)
    return wuq, wk, wv, pad(qk_norm[0]), pad(qk_norm[1])


def _mla_prep_operands(tok, gq, gkv, weights, tables):
    wuq, wk, wv, qn, kn = weights
    tm = tok.tm
    rope_spec = pl.BlockSpec((tm, HEAD_PAD), lambda i: (tok.rope_block(i), 0))
    perm = np.zeros((HEAD_PAD, HEAD_PAD), np.float32)
    j = NOPE_B + np.arange(ROPE_B)
    perm[np.where((j - NOPE_B) % 16 < 8, j + 8, j - 8), j] = 1.0
    args = [gq.reshape(1, -1), gkv.reshape(1, -1), wuq, wk, wv, qn, kn, *tables, jnp.asarray(perm, BF16)]
    in_specs = [_const_spec((1, Q_LORA)), _const_spec((1, KV_LORA)), _const_spec(wuq.shape), _const_spec(wk.shape),
                _const_spec(wv.shape), _const_spec((1, HEAD_PAD)), _const_spec((1, HEAD_PAD))] + [rope_spec] * 3
    in_specs.append(_const_spec((HEAD_PAD, HEAD_PAD)))
    widths = [(H_B * HEAD_PAD, BF16), (H_B * HEAD_PAD, BF16), (H_B * V_B, BF16), (2 * LANES, F32)]
    return args, in_specs, widths


def _prep_call(tok, body, name, proj, operands):
    args, in_specs, outs = operands
    tm = tok.tm

    def kernel_fn(proj_ref, *refs):
        body(proj_ref[...], *refs)

    return pl.pallas_call(
        kernel_fn,
        grid=(tok.n_tiles,),
        in_specs=[_row_spec(tm, proj.shape[1])] + in_specs,
        out_specs=[_row_spec(tm, w) for w, _ in outs],
        out_shape=[jax.ShapeDtypeStruct((tok.t, w), dt) for w, dt in outs],
        compiler_params=_cparams(1),
        name=name,
    )(proj, *args)


def _mla_cache_prep(cache_rows, weights):
    _, wk, wv, _, kn = weights
    rows = cache_rows.shape[0]
    tm = min(TOKEN_TILE, rows)
    return pl.pallas_call(
        _mla_cache_kernel,
        grid=(rows // tm,),
        in_specs=[_row_spec(tm, 2 * LANES), _const_spec(wk.shape), _const_spec(wv.shape), _const_spec((1, HEAD_PAD))],
        out_specs=[_row_spec(tm, H_B * HEAD_PAD), _row_spec(tm, H_B * V_B)],
        out_shape=[jax.ShapeDtypeStruct((rows, H_B * HEAD_PAD), BF16), jax.ShapeDtypeStruct((rows, H_B * V_B), BF16)],
        compiler_params=_cparams(1),
        name="mla_cache_prep",
    )(cache_rows, wk, wv, kn)


LOG2E = math.log2(math.e)


def _softmax_parts(scores):
    m = functools.reduce(jnp.maximum, [jnp.max(s, axis=-1, keepdims=True) for s in scores])
    ps = [jnp.exp2(s - m) for s in scores]
    denom = functools.reduce(jnp.add, [jnp.sum(p, axis=-1, keepdims=True) for p in ps])
    return ps, denom


def _head_lane_mask(width, head, head_w):
    lane = lax.broadcasted_iota(jnp.int32, (1, width), 1)
    return (lane >= head * head_w) & (lane < (head + 1) * head_w)


def _mla_attn_kernel(*refs, n_kv):
    q_ref = refs[0]
    k_refs = refs[1 : 1 + n_kv]
    v_refs = refs[1 + n_kv : 1 + 2 * n_kv]
    o_ref = refs[1 + 2 * n_kv]
    tiles = [(k, v, j) for k, v in zip(k_refs, v_refs) for j in range(0, k.shape[0], KEY_TILE)]
    def tile_scores(unit):
        h, (k, _, j) = divmod(unit, len(tiles))[0], tiles[unit % len(tiles)]
        sl = slice(h * HEAD_PAD, (h + 1) * HEAD_PAD)
        return _nt_dot(q_ref[:, sl], k[j : min(j + KEY_TILE, k.shape[0]), sl])

    lookahead = 1
    n_units = H_B * len(tiles)
    pending = [tile_scores(u) for u in range(min(lookahead, n_units))]
    out = None
    for h in range(H_B):
        mask = _head_lane_mask(H_B * V_B, h, V_B)
        m = denom = acc = None
        for t, (k, v, j) in enumerate(tiles):
            rows = slice(j, min(j + KEY_TILE, k.shape[0]))
            s = pending.pop(0)
            nxt = h * len(tiles) + t + lookahead
            if nxt < n_units:
                pending.append(tile_scores(nxt))
            vm = jnp.where(mask, v[rows, :], jnp.zeros((), BF16))
            s_max = jnp.max(s, axis=-1, keepdims=True)
            if m is None:
                m = s_max
                p = jnp.exp2(s - m)
                denom = jnp.sum(p, axis=-1, keepdims=True)
                acc = _dot(p.astype(BF16), vm)
            else:
                m_new = jnp.maximum(m, s_max)
                alpha = jnp.exp2(m - m_new)
                p = jnp.exp2(s - m_new)
                denom = denom * alpha + jnp.sum(p, axis=-1, keepdims=True)
                acc = acc * alpha + _dot(p.astype(BF16), vm)
                m = m_new
        acc = acc * (1.0 / denom)
        out = acc if out is None else out + acc
    o_ref[...] = out.astype(BF16)


def _seq_specs(n_seq, seq_len, row0, tq, q_width, kv_widths, kv_new, kv_cache):
    q_blocks = seq_len // tq
    q_spec = lambda w: pl.BlockSpec((tq, w), lambda b, i: (row0 // tq + b * q_blocks + i, 0))
    new_spec = lambda w: pl.BlockSpec((seq_len, w), lambda b, i: (row0 // seq_len + b, 0))
    specs = [q_spec(q_width)]
    for w in kv_widths:
        if kv_cache is not None:
            specs.append(pl.BlockSpec((kv_cache, w), lambda b, i: (b, 0)))
        specs.append(new_spec(w))
    return specs, q_spec


def _mla_attention(n_seq, seq_len, row0, total_rows, q, k, v, k_cache=None, v_cache=None, prev=None):
    tq = min(Q_TILE, seq_len)
    has_cache = k_cache is not None
    cache_len = k_cache.shape[0] // n_seq if has_cache else None
    specs, q_spec = _seq_specs(n_seq, seq_len, row0, tq, H_B * HEAD_PAD, (H_B * HEAD_PAD, H_B * V_B), True, cache_len)
    args = [q] + ([k_cache, k] if has_cache else [k]) + ([v_cache, v] if has_cache else [v])
    body = functools.partial(_mla_attn_kernel, n_kv=2 if has_cache else 1)
    body, specs, args, aliases = _keep_rows_of(prev, body, specs, args)
    return pl.pallas_call(
        body,
        grid=(n_seq, seq_len // tq),
        in_specs=specs,
        out_specs=q_spec(H_B * V_B),
        out_shape=jax.ShapeDtypeStruct((total_rows, H_B * V_B), BF16),
        input_output_aliases=aliases,
        compiler_params=_cparams(2),
        name="mla_attention",
    )(*args)


DQK_W = H_D * 2 * HEAD_PAD


def _diff_place_matrix():
    m = np.zeros((H_D * 2 * DH_D, DQK_W), np.float32)
    g = np.arange(H_D * 2 * DH_D)
    m[g, (g // DH_D) * HEAD_PAD + g % DH_D] = 1.0
    return jnp.asarray(m, BF16)


def _group_mean_matrix(width, group):
    g = np.arange(width)
    return jnp.asarray((g[:, None] // group == g[None, :] // group).astype(np.float32), BF16)


def _diff_prep_body(pd, gq_ref, gk_ref, gm_ref, place_ref, cos_ref, shi_ref, slo_ref,
                    q_ref, k_ref, v_ref, kcache_ref, vcache_ref):
    rope = (cos_ref[...], shi_ref[...], slo_ref[...])
    w = H_D * 2 * DH_D

    def norm(x, g):
        ms = _dot_exact_rhs(x * x, gm_ref[...], passes=2) * (1.0 / DH_D)
        return x * lax.rsqrt(ms + EPS) * g

    qn = _rope(norm(pd[:, :w], gq_ref[...]), *rope) * (DH_D ** -0.5 * LOG2E)
    kn = norm(pd[:, w : 2 * w], gk_ref[...])
    kcache_ref[...] = kn
    q_ref[...] = _dot(qn.astype(BF16), place_ref[...]).astype(BF16)
    k_ref[...] = _dot(_rope(kn, *rope).astype(BF16), place_ref[...]).astype(BF16)
    vcache_ref[...] = pd[:, 2 * w :]
    v_ref[...] = pd[:, 2 * w :].astype(BF16)


def _diff_cache_kernel(k_in_ref, v_in_ref, place_ref, k_ref, v_ref):
    k_ref[...] = _dot(k_in_ref[...].astype(BF16), place_ref[...]).astype(BF16)
    v_ref[...] = v_in_ref[...].astype(BF16)


def _diff_prep_operands(tok, qk_norm, tables):
    tm = tok.tm
    w = H_D * 2 * DH_D
    rope_spec = pl.BlockSpec((tm, w), lambda i: (tok.rope_block(i), 0))
    tile_g = lambda g: jnp.tile(g, H_D * 2).reshape(1, w)
    args = [tile_g(qk_norm[0]), tile_g(qk_norm[1]), _group_mean_matrix(w, DH_D), _diff_place_matrix(), *tables]
    in_specs = [_const_spec((1, w)), _const_spec((1, w)), _const_spec((w, w)), _const_spec((w, DQK_W))] + [rope_spec] * 3
    widths = [(DQK_W, BF16), (DQK_W, BF16), (H_D * DV_D, BF16), (w, F32), (H_D * DV_D, F32)]
    return args, in_specs, widths


def _diff_cache_prep(k_rows, v_rows):
    rows = k_rows.shape[0]
    tm = min(TOKEN_TILE, rows)
    w = H_D * 2 * DH_D
    return pl.pallas_call(
        _diff_cache_kernel,
        grid=(rows // tm,),
        in_specs=[_row_spec(tm, w), _row_spec(tm, H_D * DV_D), _const_spec((w, DQK_W))],
        out_specs=[_row_spec(tm, DQK_W), _row_spec(tm, H_D * DV_D)],
        out_shape=[jax.ShapeDtypeStruct((rows, DQK_W), BF16), jax.ShapeDtypeStruct((rows, H_D * DV_D), BF16)],
        compiler_params=_cparams(1),
        name="diff_cache_prep",
    )(k_rows, v_rows, _diff_place_matrix())


def _diff_attn_kernel(*refs, n_kv, lam_init):
    q_ref, lam_ref, sub_ref = refs[0], refs[1], refs[2]
    k_refs = refs[3 : 3 + n_kv]
    v_refs = refs[3 + n_kv : 3 + 2 * n_kv]
    o_ref = refs[3 + 2 * n_kv]
    dl = lam_ref[...]
    lam = (jnp.exp(jnp.sum(dl[0:1] * dl[1:2], axis=-1, keepdims=True))
           - jnp.exp(jnp.sum(dl[2:3] * dl[3:4], axis=-1, keepdims=True)) + lam_init)
    def map_scores(g):
        sl = slice(g * HEAD_PAD, (g + 1) * HEAD_PAD)
        return [_nt_dot(q_ref[:, sl], k[:, sl]) for k in k_refs]

    n_maps = 2 * H_D
    pending = [map_scores(g) for g in range(min(ATTN_LOOKAHEAD, n_maps))]
    out = None
    for h in range(H_D):
        probs = []
        for m in range(2):
            scores = pending.pop(0)
            nxt = 2 * h + m + ATTN_LOOKAHEAD
            if nxt < n_maps:
                pending.append(map_scores(nxt))
            ps, denom = _softmax_parts(scores)
            probs.append((ps, (1.0 if m == 0 else lam) / denom))
        mask = _head_lane_mask(H_D * DV_D, h, DV_D)
        (ps0, a0), (ps1, a1) = probs
        acc = None
        for p0, p1, v in zip(ps0, ps1, v_refs):
            wgt = (p0 * a0 - p1 * a1).astype(BF16)
            term = _dot(wgt, jnp.where(mask, v[...], jnp.zeros((), BF16)))
            acc = term if acc is None else acc + term
        ms = jnp.sum(acc * acc, axis=-1, keepdims=True) * (1.0 / DV_D)
        acc = acc * lax.rsqrt(ms + EPS)
        out = acc if out is None else out + acc
    o_ref[...] = (out * sub_ref[...] * (1.0 - lam_init)).astype(BF16)


def _diff_attention(layer, n_seq, seq_len, row0, total_rows, lam_p, sub_g, q, k, v, k_cache=None, v_cache=None,
                    prev=None):
    tq = min(Q_TILE, seq_len)
    has_cache = k_cache is not None
    cache_len = k_cache.shape[0] // n_seq if has_cache else None
    specs, q_spec = _seq_specs(n_seq, seq_len, row0, tq, DQK_W, (DQK_W, H_D * DV_D), True, cache_len)
    specs = [specs[0], pl.BlockSpec((4, DH_D), lambda b, i: (0, 0)), pl.BlockSpec((1, H_D * DV_D), lambda b, i: (0, 0))
             ] + specs[1:]
    args = [q, lam_p, jnp.tile(sub_g, H_D).reshape(1, H_D * DV_D)]
    args += ([k_cache, k] if has_cache else [k]) + ([v_cache, v] if has_cache else [v])
    lam_init = 0.8 - 0.6 * math.exp(-0.3 * layer)
    body = functools.partial(_diff_attn_kernel, n_kv=2 if has_cache else 1, lam_init=lam_init)
    body, specs, args, aliases = _keep_rows_of(prev, body, specs, args)
    return pl.pallas_call(
        body,
        grid=(n_seq, seq_len // tq),
        in_specs=specs,
        out_specs=q_spec(H_D * DV_D),
        out_shape=jax.ShapeDtypeStruct((total_rows, H_D * DV_D), BF16),
        input_output_aliases=aliases,
        compiler_params=_cparams(2),
        name="diff_attention",
    )(*args)


A_W = H_A * DK_A


HGRN_LEVELS = (2, 4, 8, 16, 32, 64, 128)
PAIR_W = 2 * DK_A


def _hgrn_constants():
    t = np.arange(HGRN_BLOCK)
    lower = t[None, :] <= t[:, None]
    upper = t[None, :] >= t[:, None]
    masks = [t[:, None] == t[None, :]] + [(t[:, None] // b) == (t[None, :] // b) for b in HGRN_LEVELS[:-1]]
    masks = np.stack([np.tile(m, (1, 2)) for m in masks]).astype(np.float32)
    g = np.arange(A_W)
    heads = (g[:, None] // DK_A) == (g[None, :] // DK_A)
    as_bf16 = lambda m: jnp.asarray(m.astype(np.float32), BF16)
    pair_mask = heads[:PAIR_W, :PAIR_W].astype(np.float32)
    return as_bf16(lower), as_bf16(upper), as_bf16(heads), jnp.asarray(pair_mask), jnp.asarray(masks)


def _level_reference(cum, b, forward):
    off = b // 2 - 1 if forward else b // 2
    if b >= 8:
        c3 = cum.reshape(HGRN_BLOCK // b, b, A_W)
        return jnp.broadcast_to(c3[:, off : off + 1, :], c3.shape).reshape(HGRN_BLOCK, A_W)
    c3 = cum.reshape(HGRN_BLOCK // 8, 8, A_W)
    sub = lax.broadcasted_iota(jnp.int32, (1, 8, 1), 1)
    out = None
    for g in range(8 // b):
        cand = jnp.broadcast_to(c3[:, g * b + off : g * b + off + 1, :], c3.shape)
        out = cand if out is None else jnp.where(sub >= g * b, cand, out)
    return out.reshape(HGRN_BLOCK, A_W)


def _hgrn_kernel(*refs, layer, seq_len, has_state):
    if has_state:
        pa_ref, lbl_ref, og_ref, s0_ref, lower_ref, upper_ref, heads_ref, hmask_ref, lvl_ref = refs[:9]
        rest = refs[9:]
    else:
        pa_ref, lbl_ref, og_ref, lower_ref, upper_ref, heads_ref, hmask_ref, lvl_ref = refs[:8]
        s0_ref = None
        rest = refs[8:]
    o_ref, sout_ref, st_ref, oacc_ref = rest
    n_blocks = seq_len // HGRN_BLOCK
    row = lax.broadcasted_iota(jnp.int32, (HGRN_BLOCK, 1), 0)
    first_head = lax.broadcasted_iota(jnp.int32, (1, PAIR_W), 1) < DK_A
    zero_bf16 = jnp.zeros((), BF16)

    def per_head_rows(x):
        return jnp.concatenate([jnp.where(first_head, x, zero_bf16), jnp.where(first_head, zero_bf16, x)], axis=0)

    st_ref[...] = jnp.zeros(st_ref.shape, F32)
    lower_bounds = []
    for d in range(2):
        logits = lbl_ref[d]
        e = jnp.exp(logits - jnp.max(logits, axis=0, keepdims=True))
        p = e / jnp.sum(e, axis=0, keepdims=True)
        lb = jnp.zeros((1, A_W), F32)
        for j in range(1, layer + 1):
            lb = lb + p[j : j + 1, :]
        lower_bounds.append(lb)
        if has_state:
            for h in range(H_A):
                off = (h % 2) * DK_A
                st_ref[d, h // 2, off : off + DV_A, off : off + DK_A] = s0_ref[0, d, h].T

    def both_directions(i, carry):
        for d in range(2):
            lb = lower_bounds[d]
            tri_ref = lower_ref if d == 0 else upper_ref
            blk = i if d == 0 else n_blocks - 1 - i
            r0 = pl.multiple_of(blk * HGRN_BLOCK, HGRN_BLOCK)
            rows = pl.ds(r0, HGRN_BLOCK)
            q = _silu(pa_ref[rows, 0:A_W]) * DK_A ** -0.5
            v = pa_ref[rows, A_W : 2 * A_W]
            logit = pa_ref[rows, (2 + d) * A_W : (3 + d) * A_W]
            key = jnp.minimum((1.0 - lb) * jax.nn.sigmoid(-logit), MAX_INPUT_KEY)
            log_f = jnp.log1p(-key)
            cum = _dot_exact_lhs(tri_ref[...], log_f)
            tot = cum[HGRN_BLOCK - 1 : HGRN_BLOCK, :] if d == 0 else cum[0:1, :]
            v_b = v.astype(BF16)
            scores = [None] * (H_A // 2)

            def add_pairs(qd, kd, mask_index, scores=scores):
                qd_b, kd_b = qd.astype(BF16), kd.astype(BF16)
                for pi in range(H_A // 2):
                    lanes = slice(pi * PAIR_W, (pi + 1) * PAIR_W)
                    s = _nt_dot(qd_b[:, lanes], per_head_rows(kd_b[:, lanes]))
                    if mask_index is not None:
                        s = s * lvl_ref[mask_index]
                    scores[pi] = s if scores[pi] is None else scores[pi] + s

            add_pairs(q, key, 0)
            for li, b in enumerate(HGRN_LEVELS):
                later = ((row % b) >= b // 2) if d == 0 else ((row % b) < b // 2)
                if b == 2:
                    qd = jnp.where(later, q * jnp.exp(log_f), 0.0)
                    kd = jnp.where(later, 0.0, key)
                else:
                    ref = _level_reference(cum, b, d == 0)
                    qd = jnp.where(later, q * jnp.exp(cum - ref), 0.0)
                    kd = jnp.where(later, 0.0, key * jnp.exp(ref - cum))
                add_pairs(qd, kd, li + 1 if b < HGRN_BLOCK else None)
            q_dec = (q * jnp.exp(cum)).astype(BF16)
            k_dec = (key * jnp.exp(tot - cum)).astype(BF16)
            decay = jnp.exp(tot)
            parts = []
            for pi in range(H_A // 2):
                lanes = slice(pi * PAIR_W, (pi + 1) * PAIR_W)
                state = st_ref[d, pi]
                o_pair = _dot(scores[pi].astype(BF16), per_head_rows(v_b[:, lanes]))
                parts.append(o_pair + _nt_dot(q_dec[:, lanes], state.astype(BF16)))
                upd = lax.dot_general(v_b[:, lanes], k_dec[:, lanes], (((0,), (0,)), ((), ())),
                                      preferred_element_type=F32)
                st_ref[d, pi] = state * decay[:, lanes] + upd * hmask_ref[...]
            oacc_ref[d, rows, :] = jnp.concatenate(parts, axis=1)
        return carry

    lax.fori_loop(0, n_blocks, both_directions, 0)
    for d in range(2):
        for h in range(H_A):
            off = (h % 2) * DK_A
            sout_ref[0, d, h] = st_ref[d, h // 2, off : off + DV_A, off : off + DK_A].T

    o = oacc_ref[0] + oacc_ref[1]
    ms = _dot_exact_rhs(o * o, heads_ref[...], passes=2) * (1.0 / DV_A)
    o_ref[...] = (o * lax.rsqrt(ms + EPS) * og_ref[...] * _silu(pa_ref[:, 4 * A_W : 5 * A_W])).astype(BF16)


def _hgrn(layer, n_seq, seq_len, row0, total_rows, pa, lb_logits, onorm_g, s0, prev=None):
    has_state = s0 is not None
    consts = _hgrn_constants()
    seq_spec = lambda w: pl.BlockSpec((seq_len, w), lambda b: (row0 // seq_len + b, 0))
    state_spec = pl.BlockSpec((1, 2, H_A, DK_A, DV_A), lambda b: (b, 0, 0, 0, 0))
    in_specs = [seq_spec(PA_W), _const_spec(lb_logits.shape), _const_spec((1, A_W))]
    args = [pa, lb_logits, jnp.tile(onorm_g, H_A).reshape(1, A_W)]
    if has_state:
        in_specs.append(state_spec)
        args.append(s0)
    in_specs += [_const_spec(c.shape) for c in consts]
    args += list(consts)
    body = functools.partial(_hgrn_kernel, layer=layer, seq_len=seq_len, has_state=has_state)
    body, in_specs, args, aliases = _keep_rows_of(prev, body, in_specs, args)
    return pl.pallas_call(
        body,
        grid=(n_seq,),
        in_specs=in_specs,
        out_specs=[seq_spec(A_W), state_spec],
        out_shape=[
            jax.ShapeDtypeStruct((total_rows, A_W), BF16),
            jax.ShapeDtypeStruct((n_seq, 2, H_A, DK_A, DV_A), F32),
        ],
        input_output_aliases=aliases,
        scratch_shapes=[pltpu.VMEM((2, H_A // 2, 2 * DV_A, 2 * DK_A), F32), pltpu.VMEM((2, seq_len, A_W), F32)],
        compiler_params=_cparams(1),
        name="hgrn",
    )(*args)


HY_W = HY_ORDER * W_C
FFT_R = 64


def _hy_filter_kernel(w1_ref, b1_ref, w2_ref, b2_ref, w3_ref, ld_ref, hf_ref, hb_ref, *, n):
    f32dot = functools.partial(jnp.dot, preferred_element_type=F32, precision=HIGHEST)
    row = lax.broadcasted_iota(jnp.int32, (n, LANES), 0)
    lane = lax.broadcasted_iota(jnp.int32, (n, LANES), 1)
    tn = row.astype(F32) / n
    band = jnp.where(lane <= HY_BANDS, lane, lane - HY_BANDS).astype(F32)
    ang = (2.0 * math.pi) * tn * band
    feats = jnp.where(lane == 0, tn, jnp.where(lane <= HY_BANDS, jnp.cos(ang),
                                                jnp.where(lane <= 2 * HY_BANDS, jnp.sin(ang), 0.0)))
    h = jnp.sin(f32dot(feats, w1_ref[0]) + b1_ref[0])
    h = jnp.sin(f32dot(h, w2_ref[0]) + b2_ref[0])
    h = f32dot(h, w3_ref[0])
    h = h * jnp.exp(-jnp.exp(ld_ref[0]) * tn[:, 0:1])
    hf = h[:, :HY_W]
    hb = jnp.where(row[:, 0:1] == 0, 0.0, h[:, HY_W:])
    norm = jnp.sum(jnp.abs(hf), axis=0, keepdims=True) + jnp.sum(jnp.abs(hb), axis=0, keepdims=True) + EPS
    hf_ref[0] = hf / norm
    hb_ref[0] = hb / norm


def _hy_filters(n, w1, b1, w2, b2, w3, log_decay):
    w1p = jnp.pad(w1, ((0, 0), (0, LANES - HY_EMB), (0, 0)))
    lay = lambda shape: pl.BlockSpec((1,) + shape, lambda l: (l,) + (0,) * len(shape))
    return pl.pallas_call(
        functools.partial(_hy_filter_kernel, n=n),
        grid=(DEPTH,),
        in_specs=[lay((LANES, HY_FH)), lay((1, HY_FH)), lay((HY_FH, HY_FH)), lay((1, HY_FH)),
                  lay((HY_FH, 2 * HY_W)), lay((1, 2 * HY_W))],
        out_specs=[lay((n, HY_W)), lay((n, HY_W))],
        out_shape=[jax.ShapeDtypeStruct((DEPTH, n, HY_W), F32)] * 2,
        compiler_params=_cparams(1),
        name="hyena_filters",
    )(w1p, b1.reshape(DEPTH, 1, HY_FH), w2, b2.reshape(DEPTH, 1, HY_FH), w3, log_decay.reshape(DEPTH, 1, 2 * HY_W))


def _dft_tables_short(n):
    big = 2 * n
    k = np.arange(big)[:, None]
    t = np.arange(n)[None, :]
    ang = 2.0 * np.pi * ((k * t) % big) / big
    fwd = np.concatenate([np.cos(ang), -np.sin(ang)], axis=0)
    inv = np.concatenate([np.cos(ang).T, -np.sin(ang).T], axis=1) / big
    return fwd.astype(np.float32), inv.astype(np.float32)


def _hy_spec_short_kernel(hf_ref, hb_ref, fwd_ref, f_ref, *, n):
    f32dot = functools.partial(jnp.dot, preferred_element_type=F32, precision=HIGHEST)
    xf = f32dot(fwd_ref[...], hf_ref[0])
    xb = f32dot(fwd_ref[...], hb_ref[0])
    big = 2 * n
    f_ref[0, :big, :] = xf[:big] + xb[:big]
    f_ref[0, big:, :] = xf[big:] - xb[big:]


def _hy_spec_short(n, hf, hb):
    fwd, _ = _dft_tables_short(n)
    lay = lambda shape: pl.BlockSpec((1,) + shape, lambda l: (l,) + (0,) * len(shape))
    return pl.pallas_call(
        functools.partial(_hy_spec_short_kernel, n=n),
        grid=(DEPTH,),
        in_specs=[lay((n, HY_W)), lay((n, HY_W)), _const_spec(fwd.shape)],
        out_specs=lay((4 * n, HY_W)),
        out_shape=jax.ShapeDtypeStruct((DEPTH, 4 * n, HY_W), F32),
        compiler_params=_cparams(1),
        name="hyena_spectrum_short",
    )(hf, hb, jnp.asarray(fwd))


def _short_conv(x, w, n):
    row = lax.broadcasted_iota(jnp.int32, (n, 1), 0)
    prev = jnp.where(row == 0, 0.0, pltpu.roll(x, 1, 0))
    nxt = jnp.where(row == n - 1, 0.0, pltpu.roll(x, n - 1, 0))
    return prev * w[0:1, :] + x * w[1:2, :] + nxt * w[2:3, :]


def _hy_conv_short_kernel(pc_ref, ws_ref, bias_ref, f_ref, fwd_ref, inv_ref, o_ref, *, n):
    big = 2 * n
    u = _short_conv(pc_ref[...], ws_ref[...], n)
    v, x1, x2 = u[:, :W_C], u[:, W_C : 2 * W_C], u[:, 2 * W_C :]

    def conv(x, order):
        spec = _dot(fwd_ref[...], x.astype(BF16))
        fr = f_ref[:big, order * W_C : (order + 1) * W_C]
        fi = f_ref[big:, order * W_C : (order + 1) * W_C]
        zr = spec[:big] * fr - spec[big:] * fi
        zi = spec[:big] * fi + spec[big:] * fr
        return _dot(inv_ref[...], jnp.concatenate([zr, zi], axis=0).astype(BF16))

    z = x1 * (conv(v, 0) + v * bias_ref[0:1, :])
    z = x2 * (conv(z, 1) + z * bias_ref[1:2, :])
    o_ref[...] = z.astype(BF16)


def _hy_conv_short(n_seq, n, row0, total_rows, pc, w_short, bias, spec):
    fwd, inv = _dft_tables_short(n)
    seq_spec = lambda w: pl.BlockSpec((n, w), lambda b: (row0 // n + b, 0))
    return pl.pallas_call(
        functools.partial(_hy_conv_short_kernel, n=n),
        grid=(n_seq,),
        in_specs=[seq_spec(PC_W), _const_spec((SHORT_K, PC_W)), _const_spec((HY_ORDER, W_C)),
                  _const_spec((4 * n, HY_W)), _const_spec(fwd.shape), _const_spec(inv.shape)],
        out_specs=seq_spec(W_C),
        out_shape=jax.ShapeDtypeStruct((total_rows, W_C), BF16),
        compiler_params=_cparams(1),
        name="hyena_conv_short",
    )(pc, w_short, bias, spec, jnp.asarray(fwd, BF16), jnp.asarray(inv, BF16))


def _dft_tables_long():
    r = FFT_R
    big = r * r
    half = r // 2
    n2 = np.arange(r)[:, None, None]
    k1 = np.arange(r)[None, :, None]
    n1 = np.arange(half)[None, None, :]
    ang = 2.0 * np.pi * ((k1 * (r * n1 + n2)) % big) / big
    first = np.concatenate([np.cos(ang), -np.sin(ang)], axis=1)
    last = np.concatenate([np.cos(ang), -np.sin(ang)], axis=1).transpose(0, 2, 1) / big
    a = np.arange(r)
    ang_r = 2.0 * np.pi * ((a[:, None] * a[None, :]) % r) / r
    c, s = np.cos(ang_r), np.sin(ang_r)
    mid = np.block([[c, s], [-s, c]])
    mid_inv = np.block([[c, -s], [s, c]])
    f32 = lambda m: m.astype(np.float32)
    return f32(first), f32(mid), f32(mid_inv), f32(last)


FFT_K1 = FFT_R // 2 + 1
FFT_K1_PAD = 40
FFT_K1_UNROLL = 11


def _dft_tables_long_half():
    first, mid, mid_inv, last = _dft_tables_long()
    r = FFT_R
    keep = np.zeros((FFT_K1_PAD,), np.float32)
    keep[:FFT_K1] = 1.0
    weight = np.zeros((FFT_K1_PAD,), np.float32)
    weight[:FFT_K1] = 2.0
    weight[0] = weight[r // 2] = 1.0
    first_h = np.concatenate([first[:, :FFT_K1_PAD] * keep[None, :, None],
                              first[:, r : r + FFT_K1_PAD] * keep[None, :, None]], axis=1)
    last_h = np.concatenate([last[:, :, :FFT_K1_PAD] * weight, last[:, :, r : r + FFT_K1_PAD] * weight], axis=2)
    return first_h, mid, mid_inv, last_h


def _ld_rows(ref, rows):
    return jnp.concatenate([ref[j, rows, :] for j in range(ref.shape[0])], axis=-1)


def _st_rows(ref, rows, val):
    for j in range(ref.shape[0]):
        ref[j, rows, :] = val[:, j * LANES : (j + 1) * LANES]


def _st_transposed(ref, j, val, half):
    n = val.shape[0] // 2
    _st_rows(ref, pl.ds(j, n, stride=2 * half), val[:n])
    _st_rows(ref, pl.ds(half + j, n, stride=2 * half), val[n:])


def _lane_split_scratch(rows, width):
    return pltpu.VMEM((width // LANES, rows, LANES), F32)


FFT_UNROLL = 8


def _fft_long_forward(x_ref, y_ref, dot_first, dot_mid, out_fn, n_k1=FFT_R, unroll_k1=FFT_UNROLL):
    r = FFT_R

    def stage_a(n2, carry):
        res = dot_first(n2, _ld_rows(x_ref, pl.ds(n2, r // 2, stride=r)))
        _st_transposed(y_ref, n2, res, r)
        return carry

    lax.fori_loop(0, r, stage_a, 0, unroll=FFT_UNROLL)

    def stage_c(k1, carry):
        base = pl.multiple_of(k1 * 2 * r, 2 * r)
        out_fn(k1, base, dot_mid(_ld_rows(y_ref, pl.ds(base, 2 * r))))
        return carry

    lax.fori_loop(0, n_k1, stage_c, 0, unroll=unroll_k1)


def _dot_split(t_hi, t_lo, x):
    x_hi = x.astype(BF16)
    x_lo = (x - x_hi.astype(F32)).astype(BF16)
    return _dot(t_hi, x_hi) + _dot(t_hi, x_lo) + _dot(t_lo, x_hi)


def _split_table(m):
    hi = jnp.asarray(m, BF16)
    lo = (jnp.asarray(m) - hi.astype(F32)).astype(BF16)
    return hi, lo


def _hy_spec_long_kernel(hf_ref, hb_ref, first_hi_ref, first_lo_ref, mid_hi_ref, mid_lo_ref, f_ref,
                         x_ref, y_ref, tmp_ref):
    r = FFT_R
    dot_first = lambda n2, slab: _dot_split(first_hi_ref[n2], first_lo_ref[n2], slab)
    dot_mid = lambda block: _dot_split(mid_hi_ref[...], mid_lo_ref[...], block)

    def write_fwd(k1, base, spec):
        tmp_ref[pl.ds(base, 2 * r), :] = spec

    _st_rows(x_ref, slice(None), hf_ref[0])
    _fft_long_forward(x_ref, y_ref, dot_first, dot_mid, write_fwd, n_k1=FFT_K1, unroll_k1=FFT_K1_UNROLL)

    def write_sum(k1, base, spec):
        prev = tmp_ref[pl.ds(base, 2 * r), :]
        f_ref[0, pl.ds(base, r), :] = prev[:r] + spec[:r]
        f_ref[0, pl.ds(base + r, r), :] = prev[r:] - spec[r:]

    _st_rows(x_ref, slice(None), hb_ref[0])
    _fft_long_forward(x_ref, y_ref, dot_first, dot_mid, write_sum, n_k1=FFT_K1, unroll_k1=FFT_K1_UNROLL)


SPEC_ROWS = 2 * FFT_R * FFT_K1


def _hy_spec_long(n, hf, hb):
    first, mid, _, _ = _dft_tables_long_half()
    lay = lambda rows: pl.BlockSpec((1, rows, W_C), lambda l, o: (l, 0, o))
    return pl.pallas_call(
        _hy_spec_long_kernel,
        grid=(DEPTH, HY_ORDER),
        in_specs=[lay(n), lay(n)] + [_const_spec(first.shape)] * 2 + [_const_spec(mid.shape)] * 2,
        out_specs=lay(SPEC_ROWS),
        out_shape=jax.ShapeDtypeStruct((DEPTH, SPEC_ROWS, HY_W), F32),
        scratch_shapes=[_lane_split_scratch(n, W_C), _lane_split_scratch(2 * FFT_R * FFT_K1_PAD, W_C),
                        pltpu.VMEM((SPEC_ROWS, W_C), F32)],
        compiler_params=_cparams(2),
        name="hyena_spectrum_long",
    )(hf, hb, *_split_table(first), *_split_table(mid))


def _hy_conv_long_kernel(sig_ref, gate_ref, ws_ref, bias_ref, f_ref, first_ref, mid_ref, midinv_ref, last_ref, o_ref,
                         x_ref, y_ref, v_ref, out_ref, *, n, order):
    r = FFT_R
    to_bf16 = lambda x: x.astype(BF16)
    gate = _short_conv(gate_ref[...], ws_ref[:, (order + 1) * W_C : (order + 2) * W_C], n)
    sig = _short_conv(sig_ref[...], ws_ref[:, :W_C], n) if order == 0 else sig_ref[...]
    _st_rows(x_ref, slice(None), sig)
    v_ref[...] = jnp.zeros(v_ref.shape, F32)

    def filter_and_invert(k1, base, spec):
        fr = f_ref[pl.ds(base, r), :]
        fi = f_ref[pl.ds(base + r, r), :]
        zr = spec[:r] * fr - spec[r:] * fi
        zi = spec[:r] * fi + spec[r:] * fr
        res = _dot(midinv_ref[...], jnp.concatenate([zr, zi], axis=0).astype(BF16))
        _st_transposed(v_ref, k1, res, FFT_K1_PAD)

    dot_first = lambda n2, slab: _dot(first_ref[n2], to_bf16(slab))
    dot_mid = lambda block: _dot(mid_ref[...], to_bf16(block))
    _fft_long_forward(x_ref, y_ref, dot_first, dot_mid, filter_and_invert, n_k1=FFT_K1, unroll_k1=FFT_K1_UNROLL)

    def stage_last(n2, carry):
        base = pl.multiple_of(n2 * 2 * FFT_K1_PAD, 2 * FFT_K1_PAD)
        res = _dot(last_ref[n2], _ld_rows(v_ref, pl.ds(base, 2 * FFT_K1_PAD)).astype(BF16))
        _st_rows(out_ref, pl.ds(n2, r // 2, stride=r), res)
        return carry

    lax.fori_loop(0, r, stage_last, 0, unroll=FFT_UNROLL)
    z = gate * (_ld_rows(out_ref, slice(None)) + sig * bias_ref[order : order + 1, :])
    o_ref[...] = z.astype(o_ref.dtype)


def _hy_conv_long(n_seq, n, row0, total_rows, pc, w_short, bias, spec, prev=None):
    assert 2 * n == FFT_R * FFT_R and FFT_K1 % FFT_K1_UNROLL == 0
    first, mid, mid_inv, last = _dft_tables_long_half()
    big = 2 * n
    bf = lambda m: jnp.asarray(m, BF16)
    lane_block = lambda j: pl.BlockSpec((n, W_C), lambda b: (row0 // n + b, j))
    z = None
    for order in range(HY_ORDER):
        final = order + 1 == HY_ORDER
        in_specs = [lane_block(0), lane_block(order + 1), _const_spec((SHORT_K, PC_W)),
                    _const_spec((HY_ORDER, W_C)), pl.BlockSpec((SPEC_ROWS, W_C), lambda b, order=order: (0, order)),
                    _const_spec(first.shape), _const_spec(mid.shape), _const_spec(mid_inv.shape),
                    _const_spec(last.shape)]
        args = [pc if order == 0 else z, pc, w_short, bias, spec, bf(first), bf(mid), bf(mid_inv), bf(last)]
        body = functools.partial(_hy_conv_long_kernel, n=n, order=order)
        body, in_specs, args, aliases = _keep_rows_of(prev if final else None, body, in_specs, args)
        z = pl.pallas_call(
            body,
            grid=(n_seq,),
            in_specs=in_specs,
            out_specs=lane_block(0),
            out_shape=jax.ShapeDtypeStruct((total_rows, W_C), BF16 if final else F32),
            input_output_aliases=aliases,
            scratch_shapes=[_lane_split_scratch(n, W_C), _lane_split_scratch(2 * FFT_R * FFT_K1_PAD, W_C),
                            _lane_split_scratch(2 * FFT_K1_PAD * FFT_R, W_C), _lane_split_scratch(n, W_C)],
            compiler_params=_cparams(1),
            name="hyena_conv_long",
        )(*args)
    return z


def _pad_in_weights(w_in):
    gap = jnp.zeros(w_in.shape[:2] + (PB_W - MLA_IN,), BF16)
    w = w_in.astype(BF16)
    return jnp.concatenate([w[..., : PA_W + MLA_IN], gap, w[..., PA_W + MLA_IN :]], axis=-1)


def kernel(x_prompt, x_sample, c, cache_mla, cache_diff_k, cache_diff_v, state_hgrn, c_ctx, w_mod, b_mod, norm_g,
           ffn_w_gu, ffn_w_down, w_in, w_out, hgrn_lb_logits, hgrn_onorm, mla_q_norm, mla_kv_norm, mla_w_uq,
           mla_w_ukv, mla_qk_norm, hy_short, hy_w1, hy_b1, hy_w2, hy_b2, hy_w3, hy_log_decay, hy_bias,
           diff_qk_norm, diff_lambda, diff_subln):
    bc, sc, _ = x_prompt.shape
    bl, sl, _ = x_sample.shape
    past = cache_mla.shape[2]
    tok = _Tokens(bc, sc, bl, sl)
    assert bl + 1 <= MOD_ROWS and tok.tc % sl == 0

    x = (x_prompt.reshape(tok.tc, D_MODEL), x_sample.reshape(tok.tl, D_MODEL))
    cond = jnp.concatenate([c_ctx[None], c, jnp.zeros((MOD_ROWS - 1 - bl, D_MODEL), F32)], axis=0)
    mod = _modulation(cond, w_mod, b_mod)

    filters = {n: _hy_filters(n, hy_w1, hy_b1, hy_w2, hy_b2, hy_w3, hy_log_decay) for n in (sc, sl)}
    spec_ctx = _hy_spec_short(sc, *filters[sc])
    spec_lat = _hy_spec_long(sl, *filters[sl])

    mla_tables = _rope_tables(tok, [NOPE_B], HEAD_PAD)
    diff_tables = _rope_tables(tok, list(range(0, H_D * 2 * DH_D, DH_D)), H_D * 2 * DH_D)

    w_gu_b, w_down_b, w_out_b, w_in_b = (ffn_w_gu.astype(BF16), ffn_w_down.astype(BF16), w_out.astype(BF16),
                                         _pad_in_weights(w_in))

    new_mla, new_dk, new_dv, new_state = [], [], [], []
    for l in range(DEPTH):
        x = _ffn(tok, x, mod, l, 0, norm_g[l, 0], w_gu_b, w_down_b)
        pa, pb, pc, pd = _inproj(tok, x, mod, l, norm_g[l, 1], w_in_b)
        mla_w = _mla_weights(mla_w_uq[l], mla_w_ukv[l], mla_qk_norm[l])
        q_b, k_b, v_b, cache_b = _prep_call(
            tok, _mla_prep_body, "mla_prep", pb,
            _mla_prep_operands(tok, mla_q_norm[l], mla_kv_norm[l], mla_w, mla_tables))
        q_d, k_d, v_d, kcache_d, vcache_d = _prep_call(
            tok, _diff_prep_body, "diff_prep", pd, _diff_prep_operands(tok, diff_qk_norm[l], diff_tables))

        o_a, s_ctx = _hgrn(l, bc, sc, 0, tok.t, pa, hgrn_lb_logits, hgrn_onorm[l], None)
        o_a, _ = _hgrn(l, bl, sl, tok.tc, tok.t, pa, hgrn_lb_logits, hgrn_onorm[l], state_hgrn[:, l], prev=o_a)

        cache_rows = jnp.pad(cache_mla[:, l].reshape(bl * past, KV_LORA + ROPE_B),
                             ((0, 0), (0, 2 * LANES - KV_LORA - ROPE_B)))
        kc_b, vc_b = _mla_cache_prep(cache_rows, mla_w)
        o_b = _mla_attention(bc, sc, 0, tok.t, q_b, k_b, v_b)
        o_b = _mla_attention(bl, sl, tok.tc, tok.t, q_b, k_b, v_b, kc_b, vc_b, prev=o_b)

        o_c = _hy_conv_short(bc, sc, 0, tok.t, pc, hy_short[l], hy_bias[l], spec_ctx[l])
        o_c = _hy_conv_long(bl, sl, tok.tc, tok.t, pc, hy_short[l], hy_bias[l], spec_lat[l], prev=o_c)

        kc_d, vc_d = _diff_cache_prep(cache_diff_k[:, l].reshape(bl * past, H_D * 2 * DH_D),
                                      cache_diff_v[:, l].reshape(bl * past, H_D * DV_D))
        o_d = _diff_attention(l, bc, sc, 0, tok.t, diff_lambda[l], diff_subln[l], q_d, k_d, v_d)
        o_d = _diff_attention(l, bl, sl, tok.tc, tok.t, diff_lambda[l], diff_subln[l], q_d, k_d, v_d, kc_d, vc_d,
                              prev=o_d)

        x = _ffn(tok, x, mod, l, 1, norm_g[l, 2], w_gu_b, w_down_b, mixer_outs=[o_a, o_b, o_c, o_d], w_out=w_out_b)

        new_mla.append(cache_b[: tok.tc, : KV_LORA + ROPE_B].reshape(bc, sc, KV_LORA + ROPE_B))
        new_dk.append(kcache_d[: tok.tc].reshape(bc, sc, H_D, 2, DH_D))
        new_dv.append(vcache_d[: tok.tc].reshape(bc, sc, H_D, DV_D))
        new_state.append(s_ctx)

    y_prompt = x[: tok.tc].reshape(bc, sc, D_MODEL)
    y_sample = x[tok.tc :].reshape(bl, sl, D_MODEL)
    return (y_prompt, y_sample, jnp.stack(new_mla, axis=1), jnp.stack(new_dk, axis=1), jnp.stack(new_dv, axis=1),
            jnp.stack(new_state, axis=1))
```

```python
import functools
import math

import jax
import jax.numpy as jnp
import numpy as np
from jax import lax
from jax.experimental import pallas as pl
from jax.experimental.pallas import tpu as pltpu

F32 = jnp.float32
BF16 = jnp.bfloat16
HIGHEST = lax.Precision.HIGHEST

D_MODEL = 1024
DEPTH = 4
GRID_W = 64
N_MOD = 9
D_FF = 2816
EPS = 1e-6
ROPE_BASE = 10000.0
GROUP_W = 256
H_A, DK_A, DV_A = 4, 64, 64
MAX_INPUT_KEY = 1.0 - 1e-6
H_B, NOPE_B, ROPE_B, V_B = 4, 64, 32, 64
Q_LORA, KV_LORA = 256, 128
W_C, HY_ORDER, HY_BANDS, HY_FH, SHORT_K = 256, 2, 8, 64, 3
HY_EMB = 1 + 2 * HY_BANDS
H_D, DV_D, DH_D = 4, 64, 32

LANES = 128
MOD_ROWS = 16
TOKEN_TILE = 512
Q_TILE = 256
KEY_TILE = 512
ATTN_LOOKAHEAD = 3
HGRN_BLOCK = 128
VMEM_LIMIT = 56 * 1024 * 1024

PA_W, PB_W, PC_W, PD_W = 1280, 512, 768, 768
IN_PAD_W = PA_W + PB_W + PC_W + PD_W
MLA_IN = Q_LORA + KV_LORA + ROPE_B
HEAD_PAD = 128


def _cparams(n_axes):
    return pltpu.CompilerParams(dimension_semantics=("arbitrary",) * n_axes, vmem_limit_bytes=VMEM_LIMIT)


def _nt_dot(a, b):
    return lax.dot_general(a, b, (((1,), (1,)), ((), ())), preferred_element_type=F32)


def _dot(a, b):
    return jnp.dot(a, b, preferred_element_type=F32)


def _dot_exact_rhs(a, b_bf16, passes=3):
    out = None
    rem = a
    for _ in range(passes):
        piece = rem.astype(BF16)
        term = _dot(piece, b_bf16)
        out = term if out is None else out + term
        rem = rem - piece.astype(F32)
    return out


def _dot_exact_lhs(a_bf16, b, passes=3):
    out = None
    rem = b
    for _ in range(passes):
        piece = rem.astype(BF16)
        term = _dot(a_bf16, piece)
        out = term if out is None else out + term
        rem = rem - piece.astype(F32)
    return out


def _silu(x):
    return x * jax.nn.sigmoid(x)


def _adaln(x, g, sc, sh):
    y = x * lax.rsqrt(jnp.mean(x * x, axis=-1, keepdims=True) + EPS)
    return (y * g) * (1.0 + sc) + sh


def _rope(x, cos, sin_hi, sin_lo):
    w = x.shape[-1]
    return x * cos + pltpu.roll(x, 8, 1) * sin_hi + pltpu.roll(x, w - 8, 1) * sin_lo


def _mod_kernel(c_ref, w_ref, b_ref, o_ref):
    a = _silu(c_ref[...])
    o_ref[0] = jnp.dot(a, w_ref[0], preferred_element_type=F32, precision=HIGHEST) + b_ref[0]


def _modulation(cond, w_mod, b_mod):
    tn = D_MODEL
    n_col = N_MOD * D_MODEL
    out = pl.pallas_call(
        _mod_kernel,
        grid=(DEPTH, n_col // tn),
        in_specs=[
            pl.BlockSpec((MOD_ROWS, D_MODEL), lambda l, j: (0, 0)),
            pl.BlockSpec((1, D_MODEL, tn), lambda l, j: (l, 0, j)),
            pl.BlockSpec((1, 1, tn), lambda l, j: (l, 0, j)),
        ],
        out_specs=pl.BlockSpec((1, MOD_ROWS, tn), lambda l, j: (l, 0, j)),
        out_shape=jax.ShapeDtypeStruct((DEPTH, MOD_ROWS, n_col), F32),
        compiler_params=_cparams(2),
        name="modulation",
    )(cond, w_mod, b_mod.reshape(DEPTH, 1, n_col))
    return out.reshape(DEPTH * MOD_ROWS, N_MOD, D_MODEL)


class _Tokens:
    def __init__(self, n_ctx_seq, ctx_len, n_lat_seq, lat_len):
        self.bc, self.sc, self.bl, self.sl = n_ctx_seq, ctx_len, n_lat_seq, lat_len
        self.tc, self.tl = n_ctx_seq * ctx_len, n_lat_seq * lat_len
        self.t = self.tc + self.tl
        self.tm = min(TOKEN_TILE, lat_len)
        assert self.tc % self.tm == 0 and lat_len % self.tm == 0
        self.ctx_tiles = self.tc // self.tm
        self.lat_tiles_per_seq = lat_len // self.tm
        self.n_tiles = self.t // self.tm

    def mod_row(self, layer):
        def f(i):
            lat = 1 + (i - self.ctx_tiles) // self.lat_tiles_per_seq
            return layer * MOD_ROWS + jnp.where(i < self.ctx_tiles, 0, lat)

        return f

    def rope_block(self, i):
        return jnp.where(i < self.ctx_tiles, 0, 1 + (i - self.ctx_tiles) % self.lat_tiles_per_seq)


def _mod_spec(tok, layer):
    row = tok.mod_row(layer)
    return pl.BlockSpec((1, N_MOD, D_MODEL), lambda i: (row(i), 0, 0))


def _const_spec(shape):
    zeros = (0,) * len(shape)
    return pl.BlockSpec(shape, lambda *_: zeros)


def _row_spec(tm, width):
    return pl.BlockSpec((tm, width), lambda i: (i, 0))


def _keep_rows_of(prev, kernel_fn, in_specs, args):
    if prev is None:
        return kernel_fn, in_specs, args, {}
    idx = len(in_specs)

    def body(*refs):
        return kernel_fn(*refs[:idx], *refs[idx + 1 :])

    return body, in_specs + [pl.BlockSpec(memory_space=pl.ANY)], args + [prev], {idx: 0}


def _ffn_kernel(*refs, mod_base, split_in, fused, ctx_tiles):
    n_x = 2 if split_in else 1
    x_refs, (mod_ref, g_ref, wgu_ref, wd_ref), rest = refs[:n_x], refs[n_x : n_x + 4], refs[n_x + 4 :]
    is_ctx = pl.program_id(0) < ctx_tiles
    x = jnp.where(is_ctx, x_refs[0][...], x_refs[1][...]) if split_in else x_refs[0][...]
    if fused:
        oa_ref, ob_ref, oc_ref, od_ref, wo_ref = rest[:5]
        mixed = jnp.concatenate([oa_ref[...], ob_ref[...], oc_ref[...], od_ref[...]], axis=-1)
        x = x + mod_ref[0, 5:6, :] * _dot(mixed, wo_ref[...])
    sh = mod_ref[0, mod_base : mod_base + 1, :]
    sc = mod_ref[0, mod_base + 1 : mod_base + 2, :]
    gate = mod_ref[0, mod_base + 2 : mod_base + 3, :]
    h = _adaln(x, g_ref[...], sc, sh).astype(BF16)
    au = _dot(h, wgu_ref[...])
    act = (_silu(au[:, :D_FF]) * au[:, D_FF:]).astype(BF16)
    rest[-1][...] = x + (0.5 * gate) * _dot(act, wd_ref[...])


def _resident(lead_index, shape):
    zeros = (0,) * len(shape)
    return pl.BlockSpec((None,) * len(lead_index) + tuple(shape), lambda i: tuple(lead_index) + zeros,
                        pipeline_mode=pl.Buffered(1))


def _ffn(tok, x, mod, layer, which, norm_g, w_gu, w_down, mixer_outs=None, w_out=None):
    tm = tok.tm
    split_in = isinstance(x, tuple)
    ctx_spec = pl.BlockSpec((tm, D_MODEL), lambda i: (jnp.minimum(i, tok.ctx_tiles - 1), 0))
    lat_spec = pl.BlockSpec((tm, D_MODEL), lambda i: (jnp.maximum(i - tok.ctx_tiles, 0), 0))
    in_specs = ([ctx_spec, lat_spec] if split_in else [_row_spec(tm, D_MODEL)]) + [
        _mod_spec(tok, layer),
        _const_spec((1, D_MODEL)),
        _resident((layer, which), (D_MODEL, 2 * D_FF)),
        _resident((layer, which), (D_FF, D_MODEL)),
    ]
    args = (list(x) if split_in else [x]) + [mod, norm_g.reshape(1, D_MODEL), w_gu, w_down]
    if mixer_outs is not None:
        in_specs += [_row_spec(tm, GROUP_W)] * 4
        in_specs.append(_resident((layer,), (D_MODEL, D_MODEL)))
        args += list(mixer_outs) + [w_out]
    return pl.pallas_call(
        functools.partial(_ffn_kernel, mod_base=6 * which, split_in=split_in, fused=mixer_outs is not None,
                          ctx_tiles=tok.ctx_tiles),
        grid=(tok.n_tiles,),
        in_specs=in_specs,
        out_specs=_row_spec(tm, D_MODEL),
        out_shape=jax.ShapeDtypeStruct((tok.t, D_MODEL), F32),
        compiler_params=_cparams(1),
        name="ffn",
    )(*args)


def _inproj_kernel(x_ref, mod_ref, g_ref, w_ref, pa_ref, pb_ref, pc_ref, pd_ref):
    h = _adaln(x_ref[...], g_ref[...], mod_ref[0, 4:5, :], mod_ref[0, 3:4, :]).astype(BF16)
    p = _dot(h, w_ref[...])
    pa_ref[...] = p[:, :PA_W]
    pb_ref[...] = p[:, PA_W : PA_W + PB_W]
    pc_ref[...] = p[:, PA_W + PB_W : PA_W + PB_W + PC_W]
    pd_ref[...] = p[:, PA_W + PB_W + PC_W :]


def _inproj(tok, x, mod, layer, norm_g, w_in_pad):
    tm = tok.tm
    widths = (PA_W, PB_W, PC_W, PD_W)
    return pl.pallas_call(
        _inproj_kernel,
        grid=(tok.n_tiles,),
        in_specs=[
            _row_spec(tm, D_MODEL),
            _mod_spec(tok, layer),
            _const_spec((1, D_MODEL)),
            _resident((layer,), (D_MODEL, IN_PAD_W)),
        ],
        out_specs=[_row_spec(tm, w) for w in widths],
        out_shape=[jax.ShapeDtypeStruct((tok.t, w), F32) for w in widths],
        compiler_params=_cparams(1),
        name="inproj",
    )(x, mod, norm_g.reshape(1, D_MODEL), w_in_pad)


def _rope_group_tables(n_tok):
    t = np.arange(n_tok)
    pos = np.stack([t // GRID_W, t % GRID_W], axis=1).astype(np.float32)
    inv = (ROPE_BASE ** (-np.arange(8, dtype=np.float32) / 8)).astype(np.float32)
    lane = np.arange(32)
    ang = (pos[:, lane // 16] * inv[lane % 8][None, :]).astype(np.float32)
    second = (lane % 16) >= 8
    cos, sin = np.cos(ang), np.sin(ang)
    return cos, np.where(second[None], sin, 0.0), np.where(second[None], 0.0, -sin)


def _rope_tables(tok, lane_groups, width):
    cos = np.ones((tok.tm + tok.sl, width), np.float32)
    s_hi = np.zeros_like(cos)
    s_lo = np.zeros_like(cos)
    c, a, b = _rope_group_tables(tok.sl)
    for g in lane_groups:
        cos[tok.tm :, g : g + 32] = c
        s_hi[tok.tm :, g : g + 32] = a
        s_lo[tok.tm :, g : g + 32] = b
    return jnp.asarray(cos), jnp.asarray(s_hi), jnp.asarray(s_lo)


def _mla_keys_values(kv_in, wk_ref, wv_ref, kn_ref, rope):
    kraw = _dot(kv_in.astype(BF16), wk_ref[...])
    v = _dot(kv_in[:, :KV_LORA].astype(BF16), wv_ref[...])
    ks = []
    for h in range(H_B):
        kh = kraw[:, h * HEAD_PAD : (h + 1) * HEAD_PAD]
        ss = jnp.sum(kh * kh, axis=-1, keepdims=True) * (1.0 / (NOPE_B + ROPE_B))
        kh = kh * lax.rsqrt(ss + EPS) * kn_ref[...]
        if rope is not None:
            kh = _rope_by_matmul(kh, *rope)
        ks.append(kh.astype(BF16))
    return ks, v.astype(BF16)


def _rope_by_matmul(x, cos, sin, perm):
    return x * cos + _dot(x.astype(BF16), perm) * sin


def _mla_prep_body(pb, gq_ref, gkv_ref, wuq_ref, wk_ref, wv_ref, qn_ref, kn_ref, cos_ref, shi_ref, slo_ref, perm_ref,
                   q_ref, k_ref, v_ref, cache_ref):
    rope = (cos_ref[...], shi_ref[...] + slo_ref[...], perm_ref[...])
    cq = pb[:, :Q_LORA]
    cq = cq * lax.rsqrt(jnp.mean(cq * cq, axis=-1, keepdims=True) + EPS) * gq_ref[...]
    qraw = _dot(cq.astype(BF16), wuq_ref[...])
    scale = (NOPE_B + ROPE_B) ** -0.5 * LOG2E
    for h in range(H_B):
        qh = qraw[:, h * HEAD_PAD : (h + 1) * HEAD_PAD]
        ss = jnp.sum(qh * qh, axis=-1, keepdims=True) * (1.0 / (NOPE_B + ROPE_B))
        qh = _rope_by_matmul(qh * lax.rsqrt(ss + EPS) * qn_ref[...], *rope)
        q_ref[:, h * HEAD_PAD : (h + 1) * HEAD_PAD] = (qh * scale).astype(BF16)
    ckv = pb[:, Q_LORA : Q_LORA + KV_LORA]
    ckv = ckv * lax.rsqrt(jnp.mean(ckv * ckv, axis=-1, keepdims=True) + EPS) * gkv_ref[...]
    kv_in = jnp.concatenate([ckv, pb[:, Q_LORA + KV_LORA :]], axis=-1)
    cache_ref[...] = kv_in
    ks, v = _mla_keys_values(kv_in, wk_ref, wv_ref, kn_ref, rope)
    for h in range(H_B):
        k_ref[:, h * HEAD_PAD : (h + 1) * HEAD_PAD] = ks[h]
    v_ref[...] = v


def _mla_cache_kernel(c_ref, wk_ref, wv_ref, kn_ref, k_ref, v_ref):
    ks, v = _mla_keys_values(c_ref[...], wk_ref, wv_ref, kn_ref, None)
    for h in range(H_B):
        k_ref[:, h * HEAD_PAD : (h + 1) * HEAD_PAD] = ks[h]
    v_ref[...] = v


def _mla_weights(w_uq, w_ukv, qk_norm):
    wuq = jnp.pad(w_uq.reshape(Q_LORA, H_B, NOPE_B + ROPE_B), ((0, 0), (0, 0), (0, HEAD_PAD - NOPE_B - ROPE_B)))
    wuq = wuq.reshape(Q_LORA, H_B * HEAD_PAD).astype(BF16)
    ukv = w_ukv.reshape(KV_LORA, H_B, NOPE_B + V_B)
    wk_nope = jnp.pad(ukv[:, :, :NOPE_B], ((0, 0), (0, 0), (0, HEAD_PAD - NOPE_B)))
    place = np.zeros((2 * LANES - KV_LORA, H_B, HEAD_PAD), np.float32)
    for h in range(H_B):
        place[np.arange(ROPE_B), h, NOPE_B + np.arange(ROPE_B)] = 1.0
    wk = jnp.concatenate([wk_nope, jnp.asarray(place)], axis=0).reshape(2 * LANES, H_B * HEAD_PAD).astype(BF16)
    wv = ukv[:, :, NOPE_B:].reshape(KV_LORA, H_B * V_B).astype(BF16)
    pad = lambda g: jnp.pad(g, (0, HEAD_PAD - NOPE_B - ROPE_B)).reshape(1, HEAD_PAD)
    return wuq, wk, wv, pad(qk_norm[0]), pad(qk_norm[1])


def _mla_prep_operands(tok, gq, gkv, weights, tables):
    wuq, wk, wv, qn, kn = weights
    tm = tok.tm
    rope_spec = pl.BlockSpec((tm, HEAD_PAD), lambda i: (tok.rope_block(i), 0))
    perm = np.zeros((HEAD_PAD, HEAD_PAD), np.float32)
    j = NOPE_B + np.arange(ROPE_B)
    perm[np.where((j - NOPE_B) % 16 < 8, j + 8, j - 8), j] = 1.0
    args = [gq.reshape(1, -1), gkv.reshape(1, -1), wuq, wk, wv, qn, kn, *tables, jnp.asarray(perm, BF16)]
    in_specs = [_const_spec((1, Q_LORA)), _const_spec((1, KV_LORA)), _const_spec(wuq.shape), _const_spec(wk.shape),
                _const_spec(wv.shape), _const_spec((1, HEAD_PAD)), _const_spec((1, HEAD_PAD))] + [rope_spec] * 3
    in_specs.append(_const_spec((HEAD_PAD, HEAD_PAD)))
    widths = [(H_B * HEAD_PAD, BF16), (H_B * HEAD_PAD, BF16), (H_B * V_B, BF16), (2 * LANES, F32)]
    return args, in_specs, widths


def _prep_call(tok, body, name, proj, operands):
    args, in_specs, outs = operands
    tm = tok.tm

    def kernel_fn(proj_ref, *refs):
        body(proj_ref[...], *refs)

    return pl.pallas_call(
        kernel_fn,
        grid=(tok.n_tiles,),
        in_specs=[_row_spec(tm, proj.shape[1])] + in_specs,
        out_specs=[_row_spec(tm, w) for w, _ in outs],
        out_shape=[jax.ShapeDtypeStruct((tok.t, w), dt) for w, dt in outs],
        compiler_params=_cparams(1),
        name=name,
    )(proj, *args)


def _mla_cache_prep(cache_rows, weights):
    _, wk, wv, _, kn = weights
    rows = cache_rows.shape[0]
    tm = min(TOKEN_TILE, rows)
    return pl.pallas_call(
        _mla_cache_kernel,
        grid=(rows // tm,),
        in_specs=[_row_spec(tm, 2 * LANES), _const_spec(wk.shape), _const_spec(wv.shape), _const_spec((1, HEAD_PAD))],
        out_specs=[_row_spec(tm, H_B * HEAD_PAD), _row_spec(tm, H_B * V_B)],
        out_shape=[jax.ShapeDtypeStruct((rows, H_B * HEAD_PAD), BF16), jax.ShapeDtypeStruct((rows, H_B * V_B), BF16)],
        compiler_params=_cparams(1),
        name="mla_cache_prep",
    )(cache_rows, wk, wv, kn)


LOG2E = math.log2(math.e)


def _softmax_parts(scores):
    m = functools.reduce(jnp.maximum, [jnp.max(s, axis=-1, keepdims=True) for s in scores])
    ps = [jnp.exp2(s - m) for s in scores]
    denom = functools.reduce(jnp.add, [jnp.sum(p, axis=-1, keepdims=True) for p in ps])
    return ps, denom


def _head_lane_mask(width, head, head_w):
    lane = lax.broadcasted_iota(jnp.int32, (1, width), 1)
    return (lane >= head * head_w) & (lane < (head + 1) * head_w)


def _mla_attn_kernel(*refs, n_kv):
    q_ref = refs[0]
    k_refs = refs[1 : 1 + n_kv]
    v_refs = refs[1 + n_kv : 1 + 2 * n_kv]
    o_ref = refs[1 + 2 * n_kv]
    tiles = [(k, v, j) for k, v in zip(k_refs, v_refs) for j in range(0, k.shape[0], KEY_TILE)]
    def tile_scores(unit):
        h, (k, _, j) = divmod(unit, len(tiles))[0], tiles[unit % len(tiles)]
        sl = slice(h * HEAD_PAD, (h + 1) * HEAD_PAD)
        return _nt_dot(q_ref[:, sl], k[j : min(j + KEY_TILE, k.shape[0]), sl])

    lookahead = 1
    n_units = H_B * len(tiles)
    pending = [tile_scores(u) for u in range(min(lookahead, n_units))]
    out = None
    for h in range(H_B):
        mask = _head_lane_mask(H_B * V_B, h, V_B)
        m = denom = acc = None
        for t, (k, v, j) in enumerate(tiles):
            rows = slice(j, min(j + KEY_TILE, k.shape[0]))
            s = pending.pop(0)
            nxt = h * len(tiles) + t + lookahead
            if nxt < n_units:
                pending.append(tile_scores(nxt))
            vm = jnp.where(mask, v[rows, :], jnp.zeros((), BF16))
            s_max = jnp.max(s, axis=-1, keepdims=True)
            if m is None:
                m = s_max
                p = jnp.exp2(s - m)
                denom = jnp.sum(p, axis=-1, keepdims=True)
                acc = _dot(p.astype(BF16), vm)
            else:
                m_new = jnp.maximum(m, s_max)
                alpha = jnp.exp2(m - m_new)
                p = jnp.exp2(s - m_new)
                denom = denom * alpha + jnp.sum(p, axis=-1, keepdims=True)
                acc = acc * alpha + _dot(p.astype(BF16), vm)
                m = m_new
        acc = acc * (1.0 / denom)
        out = acc if out is None else out + acc
    o_ref[...] = out.astype(BF16)


def _seq_specs(n_seq, seq_len, row0, tq, q_width, kv_widths, kv_new, kv_cache):
    q_blocks = seq_len // tq
    q_spec = lambda w: pl.BlockSpec((tq, w), lambda b, i: (row0 // tq + b * q_blocks + i, 0))
    new_spec = lambda w: pl.BlockSpec((seq_len, w), lambda b, i: (row0 // seq_len + b, 0))
    specs = [q_spec(q_width)]
    for w in kv_widths:
        if kv_cache is not None:
            specs.append(pl.BlockSpec((kv_cache, w), lambda b, i: (b, 0)))
        specs.append(new_spec(w))
    return specs, q_spec


def _mla_attention(n_seq, seq_len, row0, total_rows, q, k, v, k_cache=None, v_cache=None, prev=None):
    tq = min(Q_TILE, seq_len)
    has_cache = k_cache is not None
    cache_len = k_cache.shape[0] // n_seq if has_cache else None
    specs, q_spec = _seq_specs(n_seq, seq_len, row0, tq, H_B * HEAD_PAD, (H_B * HEAD_PAD, H_B * V_B), True, cache_len)
    args = [q] + ([k_cache, k] if has_cache else [k]) + ([v_cache, v] if has_cache else [v])
    body = functools.partial(_mla_attn_kernel, n_kv=2 if has_cache else 1)
    body, specs, args, aliases = _keep_rows_of(prev, body, specs, args)
    return pl.pallas_call(
        body,
        grid=(n_seq, seq_len // tq),
        in_specs=specs,
        out_specs=q_spec(H_B * V_B),
        out_shape=jax.ShapeDtypeStruct((total_rows, H_B * V_B), BF16),
        input_output_aliases=aliases,
        compiler_params=_cparams(2),
        name="mla_attention",
    )(*args)


DQK_W = H_D * 2 * DH_D


def _group_mean_matrix(width, group):
    g = np.arange(width)
    return jnp.asarray((g[:, None] // group == g[None, :] // group).astype(np.float32), BF16)


def _diff_prep_body(pd, gq_ref, gk_ref, gm_ref, cos_ref, shi_ref, slo_ref,
                    q_ref, k_ref, v_ref, kcache_ref, vcache_ref):
    rope = (cos_ref[...], shi_ref[...], slo_ref[...])
    w = H_D * 2 * DH_D

    def norm(x, g):
        ms = _dot_exact_rhs(x * x, gm_ref[...], passes=2) * (1.0 / DH_D)
        return x * lax.rsqrt(ms + EPS) * g

    qn = _rope(norm(pd[:, :w], gq_ref[...]), *rope) * (DH_D ** -0.5 * LOG2E)
    kn = norm(pd[:, w : 2 * w], gk_ref[...])
    kcache_ref[...] = kn
    q_ref[...] = qn.astype(BF16)
    k_ref[...] = _rope(kn, *rope).astype(BF16)
    vcache_ref[...] = pd[:, 2 * w :]
    v_ref[...] = pd[:, 2 * w :].astype(BF16)


def _diff_cache_kernel(k_in_ref, v_in_ref, k_ref, v_ref):
    k_ref[...] = k_in_ref[...].astype(BF16)
    v_ref[...] = v_in_ref[...].astype(BF16)


def _diff_prep_operands(tok, qk_norm, tables):
    tm = tok.tm
    w = H_D * 2 * DH_D
    rope_spec = pl.BlockSpec((tm, w), lambda i: (tok.rope_block(i), 0))
    tile_g = lambda g: jnp.tile(g, H_D * 2).reshape(1, w)
    args = [tile_g(qk_norm[0]), tile_g(qk_norm[1]), _group_mean_matrix(w, DH_D), *tables]
    in_specs = [_const_spec((1, w)), _const_spec((1, w)), _const_spec((w, w))] + [rope_spec] * 3
    widths = [(DQK_W, BF16), (DQK_W, BF16), (H_D * DV_D, BF16), (w, F32), (H_D * DV_D, F32)]
    return args, in_specs, widths


def _diff_cache_prep(k_rows, v_rows):
    rows = k_rows.shape[0]
    tm = min(TOKEN_TILE, rows)
    w = H_D * 2 * DH_D
    return pl.pallas_call(
        _diff_cache_kernel,
        grid=(rows // tm,),
        in_specs=[_row_spec(tm, w), _row_spec(tm, H_D * DV_D)],
        out_specs=[_row_spec(tm, DQK_W), _row_spec(tm, H_D * DV_D)],
        out_shape=[jax.ShapeDtypeStruct((rows, DQK_W), BF16), jax.ShapeDtypeStruct((rows, H_D * DV_D), BF16)],
        compiler_params=_cparams(1),
        name="diff_cache_prep",
    )(k_rows, v_rows)


def _diff_attn_kernel(*refs, n_kv, lam_init):
    q_ref, lam_ref, sub_ref = refs[0], refs[1], refs[2]
    k_refs = refs[3 : 3 + n_kv]
    v_refs = refs[3 + n_kv : 3 + 2 * n_kv]
    o_ref = refs[3 + 2 * n_kv]
    dl = lam_ref[...]
    lam = (jnp.exp(jnp.sum(dl[0:1] * dl[1:2], axis=-1, keepdims=True))
           - jnp.exp(jnp.sum(dl[2:3] * dl[3:4], axis=-1, keepdims=True)) + lam_init)
    groups_per_tile = LANES // DH_D
    lane = lax.broadcasted_iota(jnp.int32, (1, LANES), 1)

    def map_scores(g):
        sl = slice((g // groups_per_tile) * LANES, (g // groups_per_tile + 1) * LANES)
        first = (g % groups_per_tile) * DH_D
        qg = jnp.where((lane >= first) & (lane < first + DH_D), q_ref[:, sl], jnp.zeros((), BF16))
        return [_nt_dot(qg, k[:, sl]) for k in k_refs]

    n_maps = 2 * H_D
    pending = [map_scores(g) for g in range(min(ATTN_LOOKAHEAD, n_maps))]
    out = None
    for h in range(H_D):
        probs = []
        for m in range(2):
            scores = pending.pop(0)
            nxt = 2 * h + m + ATTN_LOOKAHEAD
            if nxt < n_maps:
                pending.append(map_scores(nxt))
            ps, denom = _softmax_parts(scores)
            probs.append((ps, (1.0 if m == 0 else lam) / denom))
        mask = _head_lane_mask(H_D * DV_D, h, DV_D)
        (ps0, a0), (ps1, a1) = probs
        acc = None
        for p0, p1, v in zip(ps0, ps1, v_refs):
            wgt = (p0 * a0 - p1 * a1).astype(BF16)
            term = _dot(wgt, jnp.where(mask, v[...], jnp.zeros((), BF16)))
            acc = term if acc is None else acc + term
        ms = jnp.sum(acc * acc, axis=-1, keepdims=True) * (1.0 / DV_D)
        acc = acc * lax.rsqrt(ms + EPS)
        out = acc if out is None else out + acc
    o_ref[...] = (out * sub_ref[...] * (1.0 - lam_init)).astype(BF16)


def _diff_attention(layer, n_seq, seq_len, row0, total_rows, lam_p, sub_g, q, k, v, k_cache=None, v_cache=None,
                    prev=None):
    tq = min(Q_TILE, seq_len)
    has_cache = k_cache is not None
    cache_len = k_cache.shape[0] // n_seq if has_cache else None
    specs, q_spec = _seq_specs(n_seq, seq_len, row0, tq, DQK_W, (DQK_W, H_D * DV_D), True, cache_len)
    specs = [specs[0], pl.BlockSpec((4, DH_D), lambda b, i: (0, 0)), pl.BlockSpec((1, H_D * DV_D), lambda b, i: (0, 0))
             ] + specs[1:]
    args = [q, lam_p, jnp.tile(sub_g, H_D).reshape(1, H_D * DV_D)]
    args += ([k_cache, k] if has_cache else [k]) + ([v_cache, v] if has_cache else [v])
    lam_init = 0.8 - 0.6 * math.exp(-0.3 * layer)
    body = functools.partial(_diff_attn_kernel, n_kv=2 if has_cache else 1, lam_init=lam_init)
    body, specs, args, aliases = _keep_rows_of(prev, body, specs, args)
    return pl.pallas_call(
        body,
        grid=(n_seq, seq_len // tq),
        in_specs=specs,
        out_specs=q_spec(H_D * DV_D),
        out_shape=jax.ShapeDtypeStruct((total_rows, H_D * DV_D), BF16),
        input_output_aliases=aliases,
        compiler_params=_cparams(2),
        name="diff_attention",
    )(*args)


A_W = H_A * DK_A


HGRN_LEVELS = (2, 4, 8, 16, 32, 64, 128)
PAIR_W = 2 * DK_A


def _hgrn_constants():
    t = np.arange(HGRN_BLOCK)
    lower = t[None, :] <= t[:, None]
    upper = t[None, :] >= t[:, None]
    masks = [t[:, None] == t[None, :]] + [(t[:, None] // b) == (t[None, :] // b) for b in HGRN_LEVELS[:-1]]
    masks = np.stack([np.tile(m, (1, 2)) for m in masks]).astype(np.float32)
    g = np.arange(A_W)
    heads = (g[:, None] // DK_A) == (g[None, :] // DK_A)
    as_bf16 = lambda m: jnp.asarray(m.astype(np.float32), BF16)
    pair_mask = heads[:PAIR_W, :PAIR_W].astype(np.float32)
    return as_bf16(lower), as_bf16(upper), as_bf16(heads), jnp.asarray(pair_mask), jnp.asarray(masks)


def _level_reference(cum, b, forward):
    off = b // 2 - 1 if forward else b // 2
    if b >= 8:
        c3 = cum.reshape(HGRN_BLOCK // b, b, A_W)
        return jnp.broadcast_to(c3[:, off : off + 1, :], c3.shape).reshape(HGRN_BLOCK, A_W)
    c3 = cum.reshape(HGRN_BLOCK // 8, 8, A_W)
    sub = lax.broadcasted_iota(jnp.int32, (1, 8, 1), 1)
    out = None
    for g in range(8 // b):
        cand = jnp.broadcast_to(c3[:, g * b + off : g * b + off + 1, :], c3.shape)
        out = cand if out is None else jnp.where(sub >= g * b, cand, out)
    return out.reshape(HGRN_BLOCK, A_W)


def _hgrn_kernel(*refs, layer, seq_len, has_state):
    if has_state:
        pa_ref, lbl_ref, og_ref, s0_ref, lower_ref, upper_ref, heads_ref, hmask_ref, lvl_ref = refs[:9]
        rest = refs[9:]
    else:
        pa_ref, lbl_ref, og_ref, lower_ref, upper_ref, heads_ref, hmask_ref, lvl_ref = refs[:8]
        s0_ref = None
        rest = refs[8:]
    o_ref, sout_ref, st_ref, oacc_ref = rest
    n_blocks = seq_len // HGRN_BLOCK
    row = lax.broadcasted_iota(jnp.int32, (HGRN_BLOCK, 1), 0)
    first_head = lax.broadcasted_iota(jnp.int32, (1, PAIR_W), 1) < DK_A
    zero_bf16 = jnp.zeros((), BF16)

    def per_head_rows(x):
        return jnp.concatenate([jnp.where(first_head, x, zero_bf16), jnp.where(first_head, zero_bf16, x)], axis=0)

    st_ref[...] = jnp.zeros(st_ref.shape, F32)
    lower_bounds = []
    for d in range(2):
        logits = lbl_ref[d]
        e = jnp.exp(logits - jnp.max(logits, axis=0, keepdims=True))
        p = e / jnp.sum(e, axis=0, keepdims=True)
        lb = jnp.zeros((1, A_W), F32)
        for j in range(1, layer + 1):
            lb = lb + p[j : j + 1, :]
        lower_bounds.append(lb)
        if has_state:
            for h in range(H_A):
                off = (h % 2) * DK_A
                st_ref[d, h // 2, off : off + DV_A, off : off + DK_A] = s0_ref[0, d, h].T

    def both_directions(i, carry):
        for d in range(2):
            lb = lower_bounds[d]
            tri_ref = lower_ref if d == 0 else upper_ref
            blk = i if d == 0 else n_blocks - 1 - i
            r0 = pl.multiple_of(blk * HGRN_BLOCK, HGRN_BLOCK)
            rows = pl.ds(r0, HGRN_BLOCK)
            q = _silu(pa_ref[rows, 0:A_W]) * DK_A ** -0.5
            v = pa_ref[rows, A_W : 2 * A_W]
            logit = pa_ref[rows, (2 + d) * A_W : (3 + d) * A_W]
            key = jnp.minimum((1.0 - lb) * jax.nn.sigmoid(-logit), MAX_INPUT_KEY)
            log_f = jnp.log1p(-key)
            cum = _dot_exact_lhs(tri_ref[...], log_f)
            tot = cum[HGRN_BLOCK - 1 : HGRN_BLOCK, :] if d == 0 else cum[0:1, :]
            v_b = v.astype(BF16)
            scores = [None] * (H_A // 2)

            def add_pairs(qd, kd, mask_index, scores=scores):
                qd_b, kd_b = qd.astype(BF16), kd.astype(BF16)
                for pi in range(H_A // 2):
                    lanes = slice(pi * PAIR_W, (pi + 1) * PAIR_W)
                    s = _nt_dot(qd_b[:, lanes], per_head_rows(kd_b[:, lanes]))
                    if mask_index is not None:
                        s = s * lvl_ref[mask_index]
                    scores[pi] = s if scores[pi] is None else scores[pi] + s

            add_pairs(q, key, 0)
            for li, b in enumerate(HGRN_LEVELS):
                later = ((row % b) >= b // 2) if d == 0 else ((row % b) < b // 2)
                if b == 2:
                    qd = jnp.where(later, q * jnp.exp(log_f), 0.0)
                    kd = jnp.where(later, 0.0, key)
                else:
                    ref = _level_reference(cum, b, d == 0)
                    qd = jnp.where(later, q * jnp.exp(cum - ref), 0.0)
                    kd = jnp.where(later, 0.0, key * jnp.exp(ref - cum))
                add_pairs(qd, kd, li + 1 if b < HGRN_BLOCK else None)
            q_dec = (q * jnp.exp(cum)).astype(BF16)
            k_dec = (key * jnp.exp(tot - cum)).astype(BF16)
            decay = jnp.exp(tot)
            parts = []
            for pi in range(H_A // 2):
                lanes = slice(pi * PAIR_W, (pi + 1) * PAIR_W)
                state = st_ref[d, pi]
                o_pair = _dot(scores[pi].astype(BF16), per_head_rows(v_b[:, lanes]))
                parts.append(o_pair + _nt_dot(q_dec[:, lanes], state.astype(BF16)))
                upd = lax.dot_general(v_b[:, lanes], k_dec[:, lanes], (((0,), (0,)), ((), ())),
                                      preferred_element_type=F32)
                st_ref[d, pi] = state * decay[:, lanes] + upd * hmask_ref[...]
            oacc_ref[d, rows, :] = jnp.concatenate(parts, axis=1)
        return carry

    lax.fori_loop(0, n_blocks, both_directions, 0)
    for d in range(2):
        for h in range(H_A):
            off = (h % 2) * DK_A
            sout_ref[0, d, h] = st_ref[d, h // 2, off : off + DV_A, off : off + DK_A].T

    o = oacc_ref[0] + oacc_ref[1]
    ms = _dot_exact_rhs(o * o, heads_ref[...], passes=2) * (1.0 / DV_A)
    o_ref[...] = (o * lax.rsqrt(ms + EPS) * og_ref[...] * _silu(pa_ref[:, 4 * A_W : 5 * A_W])).astype(BF16)


def _hgrn(layer, n_seq, seq_len, row0, total_rows, pa, lb_logits, onorm_g, s0, prev=None):
    has_state = s0 is not None
    consts = _hgrn_constants()
    seq_spec = lambda w: pl.BlockSpec((seq_len, w), lambda b: (row0 // seq_len + b, 0))
    state_spec = pl.BlockSpec((1, 2, H_A, DK_A, DV_A), lambda b: (b, 0, 0, 0, 0))
    in_specs = [seq_spec(PA_W), _const_spec(lb_logits.shape), _const_spec((1, A_W))]
    args = [pa, lb_logits, jnp.tile(onorm_g, H_A).reshape(1, A_W)]
    if has_state:
        in_specs.append(state_spec)
        args.append(s0)
    in_specs += [_const_spec(c.shape) for c in consts]
    args += list(consts)
    body = functools.partial(_hgrn_kernel, layer=layer, seq_len=seq_len, has_state=has_state)
    body, in_specs, args, aliases = _keep_rows_of(prev, body, in_specs, args)
    return pl.pallas_call(
        body,
        grid=(n_seq,),
        in_specs=in_specs,
        out_specs=[seq_spec(A_W), state_spec],
        out_shape=[
            jax.ShapeDtypeStruct((total_rows, A_W), BF16),
            jax.ShapeDtypeStruct((n_seq, 2, H_A, DK_A, DV_A), F32),
        ],
        input_output_aliases=aliases,
        scratch_shapes=[pltpu.VMEM((2, H_A // 2, 2 * DV_A, 2 * DK_A), F32), pltpu.VMEM((2, seq_len, A_W), F32)],
        compiler_params=_cparams(1),
        name="hgrn",
    )(*args)


HY_W = HY_ORDER * W_C
FFT_R = 64


def _hy_filter_kernel(w1_ref, b1_ref, w2_ref, b2_ref, w3_ref, ld_ref, hf_ref, hb_ref, *, n):
    f32dot = functools.partial(jnp.dot, preferred_element_type=F32, precision=HIGHEST)
    row = lax.broadcasted_iota(jnp.int32, (n, LANES), 0)
    lane = lax.broadcasted_iota(jnp.int32, (n, LANES), 1)
    tn = row.astype(F32) / n
    band = jnp.where(lane <= HY_BANDS, lane, lane - HY_BANDS).astype(F32)
    ang = (2.0 * math.pi) * tn * band
    feats = jnp.where(lane == 0, tn, jnp.where(lane <= HY_BANDS, jnp.cos(ang),
                                                jnp.where(lane <= 2 * HY_BANDS, jnp.sin(ang), 0.0)))
    h = jnp.sin(f32dot(feats, w1_ref[0]) + b1_ref[0])
    h = jnp.sin(f32dot(h, w2_ref[0]) + b2_ref[0])
    h = f32dot(h, w3_ref[0])
    h = h * jnp.exp(-jnp.exp(ld_ref[0]) * tn[:, 0:1])
    hf = h[:, :HY_W]
    hb = jnp.where(row[:, 0:1] == 0, 0.0, h[:, HY_W:])
    norm = jnp.sum(jnp.abs(hf), axis=0, keepdims=True) + jnp.sum(jnp.abs(hb), axis=0, keepdims=True) + EPS
    hf_ref[0] = hf / norm
    hb_ref[0] = hb / norm


def _hy_filters(n, w1, b1, w2, b2, w3, log_decay):
    w1p = jnp.pad(w1, ((0, 0), (0, LANES - HY_EMB), (0, 0)))
    lay = lambda shape: pl.BlockSpec((1,) + shape, lambda l: (l,) + (0,) * len(shape))
    return pl.pallas_call(
        functools.partial(_hy_filter_kernel, n=n),
        grid=(DEPTH,),
        in_specs=[lay((LANES, HY_FH)), lay((1, HY_FH)), lay((HY_FH, HY_FH)), lay((1, HY_FH)),
                  lay((HY_FH, 2 * HY_W)), lay((1, 2 * HY_W))],
        out_specs=[lay((n, HY_W)), lay((n, HY_W))],
        out_shape=[jax.ShapeDtypeStruct((DEPTH, n, HY_W), F32)] * 2,
        compiler_params=_cparams(1),
        name="hyena_filters",
    )(w1p, b1.reshape(DEPTH, 1, HY_FH), w2, b2.reshape(DEPTH, 1, HY_FH), w3, log_decay.reshape(DEPTH, 1, 2 * HY_W))


def _dft_tables_short(n):
    big = 2 * n
    k = np.arange(big)[:, None]
    t = np.arange(n)[None, :]
    ang = 2.0 * np.pi * ((k * t) % big) / big
    fwd = np.concatenate([np.cos(ang), -np.sin(ang)], axis=0)
    inv = np.concatenate([np.cos(ang).T, -np.sin(ang).T], axis=1) / big
    return fwd.astype(np.float32), inv.astype(np.float32)


def _hy_spec_short_kernel(hf_ref, hb_ref, fwd_ref, f_ref, *, n):
    f32dot = functools.partial(jnp.dot, preferred_element_type=F32, precision=HIGHEST)
    xf = f32dot(fwd_ref[...], hf_ref[0])
    xb = f32dot(fwd_ref[...], hb_ref[0])
    big = 2 * n
    f_ref[0, :big, :] = xf[:big] + xb[:big]
    f_ref[0, big:, :] = xf[big:] - xb[big:]


def _hy_spec_short(n, hf, hb):
    fwd, _ = _dft_tables_short(n)
    lay = lambda shape: pl.BlockSpec((1,) + shape, lambda l: (l,) + (0,) * len(shape))
    return pl.pallas_call(
        functools.partial(_hy_spec_short_kernel, n=n),
        grid=(DEPTH,),
        in_specs=[lay((n, HY_W)), lay((n, HY_W)), _const_spec(fwd.shape)],
        out_specs=lay((4 * n, HY_W)),
        out_shape=jax.ShapeDtypeStruct((DEPTH, 4 * n, HY_W), F32),
        compiler_params=_cparams(1),
        name="hyena_spectrum_short",
    )(hf, hb, jnp.asarray(fwd))


def _short_conv(x, w, n):
    row = lax.broadcasted_iota(jnp.int32, (n, 1), 0)
    prev = jnp.where(row == 0, 0.0, pltpu.roll(x, 1, 0))
    nxt = jnp.where(row == n - 1, 0.0, pltpu.roll(x, n - 1, 0))
    return prev * w[0:1, :] + x * w[1:2, :] + nxt * w[2:3, :]


def _hy_conv_short_kernel(pc_ref, ws_ref, bias_ref, f_ref, fwd_ref, inv_ref, o_ref, *, n):
    big = 2 * n
    u = _short_conv(pc_ref[...], ws_ref[...], n)
    v, x1, x2 = u[:, :W_C], u[:, W_C : 2 * W_C], u[:, 2 * W_C :]

    def conv(x, order):
        spec = _dot(fwd_ref[...], x.astype(BF16))
        fr = f_ref[:big, order * W_C : (order + 1) * W_C]
        fi = f_ref[big:, order * W_C : (order + 1) * W_C]
        zr = spec[:big] * fr - spec[big:] * fi
        zi = spec[:big] * fi + spec[big:] * fr
        return _dot(inv_ref[...], jnp.concatenate([zr, zi], axis=0).astype(BF16))

    z = x1 * (conv(v, 0) + v * bias_ref[0:1, :])
    z = x2 * (conv(z, 1) + z * bias_ref[1:2, :])
    o_ref[...] = z.astype(BF16)


def _hy_conv_short(n_seq, n, row0, total_rows, pc, w_short, bias, spec):
    fwd, inv = _dft_tables_short(n)
    seq_spec = lambda w: pl.BlockSpec((n, w), lambda b: (row0 // n + b, 0))
    return pl.pallas_call(
        functools.partial(_hy_conv_short_kernel, n=n),
        grid=(n_seq,),
        in_specs=[seq_spec(PC_W), _const_spec((SHORT_K, PC_W)), _const_spec((HY_ORDER, W_C)),
                  _const_spec((4 * n, HY_W)), _const_spec(fwd.shape), _const_spec(inv.shape)],
        out_specs=seq_spec(W_C),
        out_shape=jax.ShapeDtypeStruct((total_rows, W_C), BF16),
        compiler_params=_cparams(1),
        name="hyena_conv_short",
    )(pc, w_short, bias, spec, jnp.asarray(fwd, BF16), jnp.asarray(inv, BF16))


def _dft_tables_long():
    r = FFT_R
    big = r * r
    half = r // 2
    n2 = np.arange(r)[:, None, None]
    k1 = np.arange(r)[None, :, None]
    n1 = np.arange(half)[None, None, :]
    ang = 2.0 * np.pi * ((k1 * (r * n1 + n2)) % big) / big
    first = np.concatenate([np.cos(ang), -np.sin(ang)], axis=1)
    last = np.concatenate([np.cos(ang), -np.sin(ang)], axis=1).transpose(0, 2, 1) / big
    a = np.arange(r)
    ang_r = 2.0 * np.pi * ((a[:, None] * a[None, :]) % r) / r
    c, s = np.cos(ang_r), np.sin(ang_r)
    mid = np.block([[c, s], [-s, c]])
    mid_inv = np.block([[c, -s], [s, c]])
    f32 = lambda m: m.astype(np.float32)
    return f32(first), f32(mid), f32(mid_inv), f32(last)


FFT_K1 = FFT_R // 2 + 1
FFT_K1_PAD = 40
FFT_K1_UNROLL = 11


def _dft_tables_long_half():
    first, mid, mid_inv, last = _dft_tables_long()
    r = FFT_R
    keep = np.zeros((FFT_K1_PAD,), np.float32)
    keep[:FFT_K1] = 1.0
    weight = np.zeros((FFT_K1_PAD,), np.float32)
    weight[:FFT_K1] = 2.0
    weight[0] = weight[r // 2] = 1.0
    first_h = np.concatenate([first[:, :FFT_K1_PAD] * keep[None, :, None],
                              first[:, r : r + FFT_K1_PAD] * keep[None, :, None]], axis=1)
    last_h = np.concatenate([last[:, :, :FFT_K1_PAD] * weight, last[:, :, r : r + FFT_K1_PAD] * weight], axis=2)
    return first_h, mid, mid_inv, last_h


def _ld_rows(ref, rows):
    return jnp.concatenate([ref[j, rows, :] for j in range(ref.shape[0])], axis=-1)


def _st_rows(ref, rows, val):
    for j in range(ref.shape[0]):
        ref[j, rows, :] = val[:, j * LANES : (j + 1) * LANES]


def _st_transposed(ref, j, val, half):
    n = val.shape[0] // 2
    _st_rows(ref, pl.ds(j, n, stride=2 * half), val[:n])
    _st_rows(ref, pl.ds(half + j, n, stride=2 * half), val[n:])


def _lane_split_scratch(rows, width):
    return pltpu.VMEM((width // LANES, rows, LANES), F32)


FFT_UNROLL = 8


def _fft_long_forward(x_ref, y_ref, dot_first, dot_mid, out_fn, n_k1=FFT_R, unroll_k1=FFT_UNROLL):
    r = FFT_R

    def stage_a(n2, carry):
        res = dot_first(n2, _ld_rows(x_ref, pl.ds(n2, r // 2, stride=r)))
        _st_transposed(y_ref, n2, res, r)
        return carry

    lax.fori_loop(0, r, stage_a, 0, unroll=FFT_UNROLL)

    def stage_c(k1, carry):
        base = pl.multiple_of(k1 * 2 * r, 2 * r)
        out_fn(k1, base, dot_mid(_ld_rows(y_ref, pl.ds(base, 2 * r))))
        return carry

    lax.fori_loop(0, n_k1, stage_c, 0, unroll=unroll_k1)


def _dot_split(t_hi, t_lo, x):
    x_hi = x.astype(BF16)
    x_lo = (x - x_hi.astype(F32)).astype(BF16)
    return _dot(t_hi, x_hi) + _dot(t_hi, x_lo) + _dot(t_lo, x_hi)


def _split_table(m):
    hi = jnp.asarray(m, BF16)
    lo = (jnp.asarray(m) - hi.astype(F32)).astype(BF16)
    return hi, lo


def _hy_spec_long_kernel(hf_ref, hb_ref, first_hi_ref, first_lo_ref, mid_hi_ref, mid_lo_ref, f_ref,
                         x_ref, y_ref, tmp_ref):
    r = FFT_R
    dot_first = lambda n2, slab: _dot_split(first_hi_ref[n2], first_lo_ref[n2], slab)
    dot_mid = lambda block: _dot_split(mid_hi_ref[...], mid_lo_ref[...], block)

    def write_fwd(k1, base, spec):
        tmp_ref[pl.ds(base, 2 * r), :] = spec

    _st_rows(x_ref, slice(None), hf_ref[0])
    _fft_long_forward(x_ref, y_ref, dot_first, dot_mid, write_fwd, n_k1=FFT_K1, unroll_k1=FFT_K1_UNROLL)

    def write_sum(k1, base, spec):
        prev = tmp_ref[pl.ds(base, 2 * r), :]
        f_ref[0, pl.ds(base, r), :] = prev[:r] + spec[:r]
        f_ref[0, pl.ds(base + r, r), :] = prev[r:] - spec[r:]

    _st_rows(x_ref, slice(None), hb_ref[0])
    _fft_long_forward(x_ref, y_ref, dot_first, dot_mid, write_sum, n_k1=FFT_K1, unroll_k1=FFT_K1_UNROLL)


SPEC_ROWS = 2 * FFT_R * FFT_K1


def _hy_spec_long(n, hf, hb):
    first, mid, _, _ = _dft_tables_long_half()
    lay = lambda rows: pl.BlockSpec((1, rows, W_C), lambda l, o: (l, 0, o))
    return pl.pallas_call(
        _hy_spec_long_kernel,
        grid=(DEPTH, HY_ORDER),
        in_specs=[lay(n), lay(n)] + [_const_spec(first.shape)] * 2 + [_const_spec(mid.shape)] * 2,
        out_specs=lay(SPEC_ROWS),
        out_shape=jax.ShapeDtypeStruct((DEPTH, SPEC_ROWS, HY_W), F32),
        scratch_shapes=[_lane_split_scratch(n, W_C), _lane_split_scratch(2 * FFT_R * FFT_K1_PAD, W_C),
                        pltpu.VMEM((SPEC_ROWS, W_C), F32)],
        compiler_params=_cparams(2),
        name="hyena_spectrum_long",
    )(hf, hb, *_split_table(first), *_split_table(mid))


def _hy_conv_long_kernel(sig_ref, gate_ref, ws_ref, bias_ref, f_ref, first_ref, mid_ref, midinv_ref, last_ref, o_ref,
                         x_ref, y_ref, v_ref, out_ref, *, n, order):
    r = FFT_R
    to_bf16 = lambda x: x.astype(BF16)
    gate = _short_conv(gate_ref[...], ws_ref[:, (order + 1) * W_C : (order + 2) * W_C], n)
    sig = _short_conv(sig_ref[...], ws_ref[:, :W_C], n) if order == 0 else sig_ref[...]
    _st_rows(x_ref, slice(None), sig)
    @pl.when(pl.program_id(0) == 0)
    def _():
        v_ref[...] = jnp.zeros(v_ref.shape, F32)

    def filter_and_invert(k1, base, spec):
        fr = f_ref[pl.ds(base, r), :]
        fi = f_ref[pl.ds(base + r, r), :]
        zr = spec[:r] * fr - spec[r:] * fi
        zi = spec[:r] * fi + spec[r:] * fr
        res = _dot(midinv_ref[...], jnp.concatenate([zr, zi], axis=0).astype(BF16))
        _st_transposed(v_ref, k1, res, FFT_K1_PAD)

    dot_first = lambda n2, slab: _dot(first_ref[n2], to_bf16(slab))
    dot_mid = lambda block: _dot(mid_ref[...], to_bf16(block))
    _fft_long_forward(x_ref, y_ref, dot_first, dot_mid, filter_and_invert, n_k1=FFT_K1, unroll_k1=FFT_K1_UNROLL)

    def stage_last(n2, carry):
        base = pl.multiple_of(n2 * 2 * FFT_K1_PAD, 2 * FFT_K1_PAD)
        res = _dot(last_ref[n2], _ld_rows(v_ref, pl.ds(base, 2 * FFT_K1_PAD)).astype(BF16))
        _st_rows(out_ref, pl.ds(n2, r // 2, stride=r), res)
        return carry

    lax.fori_loop(0, r, stage_last, 0, unroll=FFT_UNROLL)
    z = gate * (_ld_rows(out_ref, slice(None)) + sig * bias_ref[order : order + 1, :])
    o_ref[...] = z.astype(o_ref.dtype)


def _hy_conv_long(n_seq, n, row0, total_rows, pc, w_short, bias, spec, prev=None):
    assert 2 * n == FFT_R * FFT_R and FFT_K1 % FFT_K1_UNROLL == 0
    first, mid, mid_inv, last = _dft_tables_long_half()
    big = 2 * n
    bf = lambda m: jnp.asarray(m, BF16)
    lane_block = lambda j: pl.BlockSpec((n, W_C), lambda b: (row0 // n + b, j))
    z = None
    for order in range(HY_ORDER):
        final = order + 1 == HY_ORDER
        in_specs = [lane_block(0), lane_block(order + 1), _const_spec((SHORT_K, PC_W)),
                    _const_spec((HY_ORDER, W_C)), pl.BlockSpec((SPEC_ROWS, W_C), lambda b, order=order: (0, order)),
                    _const_spec(first.shape), _const_spec(mid.shape), _const_spec(mid_inv.shape),
                    _const_spec(last.shape)]
        args = [pc if order == 0 else z, pc, w_short, bias, spec, bf(first), bf(mid), bf(mid_inv), bf(last)]
        body = functools.partial(_hy_conv_long_kernel, n=n, order=order)
        body, in_specs, args, aliases = _keep_rows_of(prev if final else None, body, in_specs, args)
        z = pl.pallas_call(
            body,
            grid=(n_seq,),
            in_specs=in_specs,
            out_specs=lane_block(0),
            out_shape=jax.ShapeDtypeStruct((total_rows, W_C), BF16 if final else F32),
            input_output_aliases=aliases,
            scratch_shapes=[_lane_split_scratch(n, W_C), _lane_split_scratch(2 * FFT_R * FFT_K1_PAD, W_C),
                            _lane_split_scratch(2 * FFT_K1_PAD * FFT_R, W_C), _lane_split_scratch(n, W_C)],
            compiler_params=_cparams(1),
            name="hyena_conv_long",
        )(*args)
    return z


def _pad_in_weights(w_in):
    gap = jnp.zeros(w_in.shape[:2] + (PB_W - MLA_IN,), BF16)
    w = w_in.astype(BF16)
    return jnp.concatenate([w[..., : PA_W + MLA_IN], gap, w[..., PA_W + MLA_IN :]], axis=-1)


def kernel(x_prompt, x_sample, c, cache_mla, cache_diff_k, cache_diff_v, state_hgrn, c_ctx, w_mod, b_mod, norm_g,
           ffn_w_gu, ffn_w_down, w_in, w_out, hgrn_lb_logits, hgrn_onorm, mla_q_norm, mla_kv_norm, mla_w_uq,
           mla_w_ukv, mla_qk_norm, hy_short, hy_w1, hy_b1, hy_w2, hy_b2, hy_w3, hy_log_decay, hy_bias,
           diff_qk_norm, diff_lambda, diff_subln):
    bc, sc, _ = x_prompt.shape
    bl, sl, _ = x_sample.shape
    past = cache_mla.shape[2]
    tok = _Tokens(bc, sc, bl, sl)
    assert bl + 1 <= MOD_ROWS and tok.tc % sl == 0

    x = (x_prompt.reshape(tok.tc, D_MODEL), x_sample.reshape(tok.tl, D_MODEL))
    cond = jnp.concatenate([c_ctx[None], c, jnp.zeros((MOD_ROWS - 1 - bl, D_MODEL), F32)], axis=0)
    mod = _modulation(cond, w_mod, b_mod)

    filters = {n: _hy_filters(n, hy_w1, hy_b1, hy_w2, hy_b2, hy_w3, hy_log_decay) for n in (sc, sl)}
    spec_ctx = _hy_spec_short(sc, *filters[sc])
    spec_lat = _hy_spec_long(sl, *filters[sl])

    mla_tables = _rope_tables(tok, [NOPE_B], HEAD_PAD)
    diff_tables = _rope_tables(tok, list(range(0, H_D * 2 * DH_D, DH_D)), H_D * 2 * DH_D)

    w_gu_b, w_down_b, w_out_b, w_in_b = (ffn_w_gu.astype(BF16), ffn_w_down.astype(BF16), w_out.astype(BF16),
                                         _pad_in_weights(w_in))

    new_mla, new_dk, new_dv, new_state = [], [], [], []
    for l in range(DEPTH):
        x = _ffn(tok, x, mod, l, 0, norm_g[l, 0], w_gu_b, w_down_b)
        pa, pb, pc, pd = _inproj(tok, x, mod, l, norm_g[l, 1], w_in_b)
        mla_w = _mla_weights(mla_w_uq[l], mla_w_ukv[l], mla_qk_norm[l])
        q_b, k_b, v_b, cache_b = _prep_call(
            tok, _mla_prep_body, "mla_prep", pb,
            _mla_prep_operands(tok, mla_q_norm[l], mla_kv_norm[l], mla_w, mla_tables))
        q_d, k_d, v_d, kcache_d, vcache_d = _prep_call(
            tok, _diff_prep_body, "diff_prep", pd, _diff_prep_operands(tok, diff_qk_norm[l], diff_tables))

        o_a, s_ctx = _hgrn(l, bc, sc, 0, tok.t, pa, hgrn_lb_logits, hgrn_onorm[l], None)
        o_a, _ = _hgrn(l, bl, sl, tok.tc, tok.t, pa, hgrn_lb_logits, hgrn_onorm[l], state_hgrn[:, l], prev=o_a)

        cache_rows = jnp.pad(cache_mla[:, l].reshape(bl * past, KV_LORA + ROPE_B),
                             ((0, 0), (0, 2 * LANES - KV_LORA - ROPE_B)))
        kc_b, vc_b = _mla_cache_prep(cache_rows, mla_w)
        o_b = _mla_attention(bc, sc, 0, tok.t, q_b, k_b, v_b)
        o_b = _mla_attention(bl, sl, tok.tc, tok.t, q_b, k_b, v_b, kc_b, vc_b, prev=o_b)

        o_c = _hy_conv_short(bc, sc, 0, tok.t, pc, hy_short[l], hy_bias[l], spec_ctx[l])
        o_c = _hy_conv_long(bl, sl, tok.tc, tok.t, pc, hy_short[l], hy_bias[l], spec_lat[l], prev=o_c)

        kc_d, vc_d = _diff_cache_prep(cache_diff_k[:, l].reshape(bl * past, H_D * 2 * DH_D),
                                      cache_diff_v[:, l].reshape(bl * past, H_D * DV_D))
        o_d = _diff_attention(l, bc, sc, 0, tok.t, diff_lambda[l], diff_subln[l], q_d, k_d, v_d)
        o_d = _diff_attention(l, bl, sl, tok.tc, tok.t, diff_lambda[l], diff_subln[l], q_d, k_d, v_d, kc_d, vc_d,
                              prev=o_d)

        x = _ffn(tok, x, mod, l, 1, norm_g[l, 2], w_gu_b, w_down_b, mixer_outs=[o_a, o_b, o_c, o_d], w_out=w_out_b)

        new_mla.append(cache_b[: tok.tc, : KV_LORA + ROPE_B].reshape(bc, sc, KV_LORA + ROPE_B))
        new_dk.append(kcache_d[: tok.tc].reshape(bc, sc, H_D, 2, DH_D))
        new_dv.append(vcache_d[: tok.tc].reshape(bc, sc, H_D, DV_D))
        new_state.append(s_ctx)

    y_prompt = x[: tok.tc].reshape(bc, sc, D_MODEL)
    y_sample = x[tok.tc :].reshape(bl, sl, D_MODEL)
    return (y_prompt, y_sample, jnp.stack(new_mla, axis=1), jnp.stack(new_dk, axis=1), jnp.stack(new_dv, axis=1),
            jnp.stack(new_state, axis=1))
```

```python
import functools
import math

import jax
import jax.numpy as jnp
import numpy as np
from jax import lax
from jax.experimental import pallas as pl
from jax.experimental.pallas import tpu as pltpu

F32 = jnp.float32
BF16 = jnp.bfloat16
HIGHEST = lax.Precision.HIGHEST

D_MODEL = 1024
DEPTH = 4
GRID_W = 64
N_MOD = 9
D_FF = 2816
EPS = 1e-6
ROPE_BASE = 10000.0
GROUP_W = 256
H_A, DK_A, DV_A = 4, 64, 64
MAX_INPUT_KEY = 1.0 - 1e-6
H_B, NOPE_B, ROPE_B, V_B = 4, 64, 32, 64
Q_LORA, KV_LORA = 256, 128
W_C, HY_ORDER, HY_BANDS, HY_FH, SHORT_K = 256, 2, 8, 64, 3
HY_EMB = 1 + 2 * HY_BANDS
H_D, DV_D, DH_D = 4, 64, 32

LANES = 128
MOD_ROWS = 16
TOKEN_TILE = 512
Q_TILE = 256
KEY_TILE = 512
ATTN_LOOKAHEAD = 3
HGRN_BLOCK = 128
VMEM_LIMIT = 56 * 1024 * 1024

PA_W, PB_W, PC_W, PD_W = 1280, 512, 768, 768
IN_PAD_W = PA_W + PB_W + PC_W + PD_W
MLA_IN = Q_LORA + KV_LORA + ROPE_B
HEAD_PAD = 128


def _cparams(n_axes):
    return pltpu.CompilerParams(dimension_semantics=("arbitrary",) * n_axes, vmem_limit_bytes=VMEM_LIMIT)


def _nt_dot(a, b):
    return lax.dot_general(a, b, (((1,), (1,)), ((), ())), preferred_element_type=F32)


def _dot(a, b):
    return jnp.dot(a, b, preferred_element_type=F32)


def _dot_exact_rhs(a, b_bf16, passes=3):
    out = None
    rem = a
    for _ in range(passes):
        piece = rem.astype(BF16)
        term = _dot(piece, b_bf16)
        out = term if out is None else out + term
        rem = rem - piece.astype(F32)
    return out


def _dot_exact_lhs(a_bf16, b, passes=3):
    out = None
    rem = b
    for _ in range(passes):
        piece = rem.astype(BF16)
        term = _dot(a_bf16, piece)
        out = term if out is None else out + term
        rem = rem - piece.astype(F32)
    return out


def _silu(x):
    return x * jax.nn.sigmoid(x)


def _adaln(x, g, sc, sh):
    y = x * lax.rsqrt(jnp.mean(x * x, axis=-1, keepdims=True) + EPS)
    return (y * g) * (1.0 + sc) + sh


def _rope(x, cos, sin_hi, sin_lo):
    w = x.shape[-1]
    return x * cos + pltpu.roll(x, 8, 1) * sin_hi + pltpu.roll(x, w - 8, 1) * sin_lo


def _mod_kernel(c_ref, w_ref, b_ref, o_ref):
    a = _silu(c_ref[...])
    o_ref[0] = jnp.dot(a, w_ref[0], preferred_element_type=F32, precision=HIGHEST) + b_ref[0]


def _modulation(cond, w_mod, b_mod):
    tn = D_MODEL
    n_col = N_MOD * D_MODEL
    out = pl.pallas_call(
        _mod_kernel,
        grid=(DEPTH, n_col // tn),
        in_specs=[
            pl.BlockSpec((MOD_ROWS, D_MODEL), lambda l, j: (0, 0)),
            pl.BlockSpec((1, D_MODEL, tn), lambda l, j: (l, 0, j)),
            pl.BlockSpec((1, 1, tn), lambda l, j: (l, 0, j)),
        ],
        out_specs=pl.BlockSpec((1, MOD_ROWS, tn), lambda l, j: (l, 0, j)),
        out_shape=jax.ShapeDtypeStruct((DEPTH, MOD_ROWS, n_col), F32),
        compiler_params=_cparams(2),
        name="modulation",
    )(cond, w_mod, b_mod.reshape(DEPTH, 1, n_col))
    return out.reshape(DEPTH * MOD_ROWS, N_MOD, D_MODEL)


class _Tokens:
    def __init__(self, n_ctx_seq, ctx_len, n_lat_seq, lat_len):
        self.bc, self.sc, self.bl, self.sl = n_ctx_seq, ctx_len, n_lat_seq, lat_len
        self.tc, self.tl = n_ctx_seq * ctx_len, n_lat_seq * lat_len
        self.t = self.tc + self.tl
        self.tm = min(TOKEN_TILE, lat_len)
        assert self.tc % self.tm == 0 and lat_len % self.tm == 0
        self.ctx_tiles = self.tc // self.tm
        self.lat_tiles_per_seq = lat_len // self.tm
        self.n_tiles = self.t // self.tm

    def mod_row(self, layer):
        def f(i):
            lat = 1 + (i - self.ctx_tiles) // self.lat_tiles_per_seq
            return layer * MOD_ROWS + jnp.where(i < self.ctx_tiles, 0, lat)

        return f

    def rope_block(self, i):
        return jnp.where(i < self.ctx_tiles, 0, 1 + (i - self.ctx_tiles) % self.lat_tiles_per_seq)


def _mod_spec(tok, layer):
    row = tok.mod_row(layer)
    return pl.BlockSpec((1, N_MOD, D_MODEL), lambda i: (row(i), 0, 0))


def _const_spec(shape):
    zeros = (0,) * len(shape)
    return pl.BlockSpec(shape, lambda *_: zeros)


def _row_spec(tm, width):
    return pl.BlockSpec((tm, width), lambda i: (i, 0))


def _keep_rows_of(prev, kernel_fn, in_specs, args):
    if prev is None:
        return kernel_fn, in_specs, args, {}
    idx = len(in_specs)

    def body(*refs):
        return kernel_fn(*refs[:idx], *refs[idx + 1 :])

    return body, in_specs + [pl.BlockSpec(memory_space=pl.ANY)], args + [prev], {idx: 0}


def _ffn_kernel(*refs, mod_base, split_in, fused, ctx_tiles):
    n_x = 2 if split_in else 1
    x_refs, (mod_ref, g_ref, wgu_ref, wd_ref), rest = refs[:n_x], refs[n_x : n_x + 4], refs[n_x + 4 :]
    is_ctx = pl.program_id(0) < ctx_tiles
    x = jnp.where(is_ctx, x_refs[0][...], x_refs[1][...]) if split_in else x_refs[0][...]
    if fused:
        oa_ref, ob_ref, oc_ref, od_ref, wo_ref = rest[:5]
        mixed = jnp.concatenate([oa_ref[...], ob_ref[...], oc_ref[...], od_ref[...]], axis=-1)
        x = x + mod_ref[0, 5:6, :] * _dot(mixed, wo_ref[...])
    sh = mod_ref[0, mod_base : mod_base + 1, :]
    sc = mod_ref[0, mod_base + 1 : mod_base + 2, :]
    gate = mod_ref[0, mod_base + 2 : mod_base + 3, :]
    h = _adaln(x, g_ref[...], sc, sh).astype(BF16)
    au = _dot(h, wgu_ref[...])
    act = (_silu(au[:, :D_FF]) * au[:, D_FF:]).astype(BF16)
    rest[-1][...] = x + (0.5 * gate) * _dot(act, wd_ref[...])


def _resident(lead_index, shape):
    zeros = (0,) * len(shape)
    return pl.BlockSpec((None,) * len(lead_index) + tuple(shape), lambda i: tuple(lead_index) + zeros,
                        pipeline_mode=pl.Buffered(1))


def _ffn(tok, x, mod, layer, which, norm_g, w_gu, w_down, mixer_outs=None, w_out=None):
    tm = tok.tm
    split_in = isinstance(x, tuple)
    ctx_spec = pl.BlockSpec((tm, D_MODEL), lambda i: (jnp.minimum(i, tok.ctx_tiles - 1), 0))
    lat_spec = pl.BlockSpec((tm, D_MODEL), lambda i: (jnp.maximum(i - tok.ctx_tiles, 0), 0))
    in_specs = ([ctx_spec, lat_spec] if split_in else [_row_spec(tm, D_MODEL)]) + [
        _mod_spec(tok, layer),
        _const_spec((1, D_MODEL)),
        _resident((layer, which), (D_MODEL, 2 * D_FF)),
        _resident((layer, which), (D_FF, D_MODEL)),
    ]
    args = (list(x) if split_in else [x]) + [mod, norm_g.reshape(1, D_MODEL), w_gu, w_down]
    if mixer_outs is not None:
        in_specs += [_row_spec(tm, GROUP_W)] * 4
        in_specs.append(_resident((layer,), (D_MODEL, D_MODEL)))
        args += list(mixer_outs) + [w_out]
    return pl.pallas_call(
        functools.partial(_ffn_kernel, mod_base=6 * which, split_in=split_in, fused=mixer_outs is not None,
                          ctx_tiles=tok.ctx_tiles),
        grid=(tok.n_tiles,),
        in_specs=in_specs,
        out_specs=_row_spec(tm, D_MODEL),
        out_shape=jax.ShapeDtypeStruct((tok.t, D_MODEL), F32),
        compiler_params=_cparams(1),
        name="ffn",
    )(*args)


def _inproj_kernel(x_ref, mod_ref, g_ref, w_ref, pa_ref, pb_ref, pc_ref, pd_ref):
    h = _adaln(x_ref[...], g_ref[...], mod_ref[0, 4:5, :], mod_ref[0, 3:4, :]).astype(BF16)
    p = _dot(h, w_ref[...])
    pa_ref[...] = p[:, :PA_W]
    pb_ref[...] = p[:, PA_W : PA_W + PB_W]
    pc_ref[...] = p[:, PA_W + PB_W : PA_W + PB_W + PC_W]
    pd_ref[...] = p[:, PA_W + PB_W + PC_W :]


def _inproj(tok, x, mod, layer, norm_g, w_in_pad):
    tm = tok.tm
    widths = (PA_W, PB_W, PC_W, PD_W)
    return pl.pallas_call(
        _inproj_kernel,
        grid=(tok.n_tiles,),
        in_specs=[
            _row_spec(tm, D_MODEL),
            _mod_spec(tok, layer),
            _const_spec((1, D_MODEL)),
            _resident((layer,), (D_MODEL, IN_PAD_W)),
        ],
        out_specs=[_row_spec(tm, w) for w in widths],
        out_shape=[jax.ShapeDtypeStruct((tok.t, w), F32) for w in widths],
        compiler_params=_cparams(1),
        name="inproj",
    )(x, mod, norm_g.reshape(1, D_MODEL), w_in_pad)


def _rope_group_tables(n_tok):
    t = np.arange(n_tok)
    pos = np.stack([t // GRID_W, t % GRID_W], axis=1).astype(np.float32)
    inv = (ROPE_BASE ** (-np.arange(8, dtype=np.float32) / 8)).astype(np.float32)
    lane = np.arange(32)
    ang = (pos[:, lane // 16] * inv[lane % 8][None, :]).astype(np.float32)
    second = (lane % 16) >= 8
    cos, sin = np.cos(ang), np.sin(ang)
    return cos, np.where(second[None], sin, 0.0), np.where(second[None], 0.0, -sin)


def _rope_tables(tok, lane_groups, width):
    cos = np.ones((tok.tm + tok.sl, width), np.float32)
    s_hi = np.zeros_like(cos)
    s_lo = np.zeros_like(cos)
    c, a, b = _rope_group_tables(tok.sl)
    for g in lane_groups:
        cos[tok.tm :, g : g + 32] = c
        s_hi[tok.tm :, g : g + 32] = a
        s_lo[tok.tm :, g : g + 32] = b
    return jnp.asarray(cos), jnp.asarray(s_hi), jnp.asarray(s_lo)


def _mla_keys_values(kv_in, wk_ref, wv_ref, kn_ref, rope):
    kraw = _dot(kv_in.astype(BF16), wk_ref[...])
    v = _dot(kv_in[:, :KV_LORA].astype(BF16), wv_ref[...])
    ks = []
    for h in range(H_B):
        kh = kraw[:, h * HEAD_PAD : (h + 1) * HEAD_PAD]
        ss = jnp.sum(kh * kh, axis=-1, keepdims=True) * (1.0 / (NOPE_B + ROPE_B))
        kh = kh * lax.rsqrt(ss + EPS) * kn_ref[...]
        if rope is not None:
            kh = _rope_by_matmul(kh, *rope)
        ks.append(kh.astype(BF16))
    return ks, v.astype(BF16)


def _rope_by_matmul(x, cos, sin, perm):
    return x * cos + _dot(x.astype(BF16), perm) * sin


def _mla_prep_body(pb, gq_ref, gkv_ref, wuq_ref, wk_ref, wv_ref, qn_ref, kn_ref, cos_ref, shi_ref, slo_ref, perm_ref,
                   q_ref, k_ref, v_ref, cache_ref):
    rope = (cos_ref[...], shi_ref[...] + slo_ref[...], perm_ref[...])
    cq = pb[:, :Q_LORA]
    cq = cq * lax.rsqrt(jnp.mean(cq * cq, axis=-1, keepdims=True) + EPS) * gq_ref[...]
    qraw = _dot(cq.astype(BF16), wuq_ref[...])
    scale = (NOPE_B + ROPE_B) ** -0.5 * LOG2E
    for h in range(H_B):
        qh = qraw[:, h * HEAD_PAD : (h + 1) * HEAD_PAD]
        ss = jnp.sum(qh * qh, axis=-1, keepdims=True) * (1.0 / (NOPE_B + ROPE_B))
        qh = _rope_by_matmul(qh * lax.rsqrt(ss + EPS) * qn_ref[...], *rope)
        q_ref[:, h * HEAD_PAD : (h + 1) * HEAD_PAD] = (qh * scale).astype(BF16)
    ckv = pb[:, Q_LORA : Q_LORA + KV_LORA]
    ckv = ckv * lax.rsqrt(jnp.mean(ckv * ckv, axis=-1, keepdims=True) + EPS) * gkv_ref[...]
    kv_in = jnp.concatenate([ckv, pb[:, Q_LORA + KV_LORA :]], axis=-1)
    cache_ref[...] = kv_in
    ks, v = _mla_keys_values(kv_in, wk_ref, wv_ref, kn_ref, rope)
    for h in range(H_B):
        k_ref[:, h * HEAD_PAD : (h + 1) * HEAD_PAD] = ks[h]
    v_ref[...] = v


def _mla_cache_kernel(c_ref, wk_ref, wv_ref, kn_ref, k_ref, v_ref):
    ks, v = _mla_keys_values(c_ref[...], wk_ref, wv_ref, kn_ref, None)
    for h in range(H_B):
        k_ref[:, h * HEAD_PAD : (h + 1) * HEAD_PAD] = ks[h]
    v_ref[...] = v


def _mla_weights(w_uq, w_ukv, qk_norm):
    wuq = jnp.pad(w_uq.reshape(Q_LORA, H_B, NOPE_B + ROPE_B), ((0, 0), (0, 0), (0, HEAD_PAD - NOPE_B - ROPE_B)))
    wuq = wuq.reshape(Q_LORA, H_B * HEAD_PAD).astype(BF16)
    ukv = w_ukv.reshape(KV_LORA, H_B, NOPE_B + V_B)
    wk_nope = jnp.pad(ukv[:, :, :NOPE_B], ((0, 0), (0, 0), (0, HEAD_PAD - NOPE_B)))
    place = np.zeros((2 * LANES - KV_LORA, H_B, HEAD_PAD), np.float32)
    for h in range(H_B):
        place[np.arange(ROPE_B), h, NOPE_B + np.arange(ROPE_B)] = 1.0
    wk = jnp.concatenate([wk_nope, jnp.asarray(place)], axis=0).reshape(2 * LANES, H_B * HEAD_PAD).astype(BF16)
    wv = ukv[:, :, NOPE_B:].reshape(KV_LORA, H_B * V_B).astype(BF16)
    pad = lambda g: jnp.pad(g, (0, HEAD_PAD - NOPE_B - ROPE_B)).reshape(1, HEAD_PAD)
    return wuq, wk, wv, pad(qk_norm[0]), pad(qk_norm[1])


def _mla_prep_operands(tok, gq, gkv, weights, tables):
    wuq, wk, wv, qn, kn = weights
    tm = tok.tm
    rope_spec = pl.BlockSpec((tm, HEAD_PAD), lambda i: (tok.rope_block(i), 0))
    perm = np.zeros((HEAD_PAD, HEAD_PAD), np.float32)
    j = NOPE_B + np.arange(ROPE_B)
    perm[np.where((j - NOPE_B) % 16 < 8, j + 8, j - 8), j] = 1.0
    args = [gq.reshape(1, -1), gkv.reshape(1, -1), wuq, wk, wv, qn, kn, *tables, jnp.asarray(perm, BF16)]
    in_specs = [_const_spec((1, Q_LORA)), _const_spec((1, KV_LORA)), _const_spec(wuq.shape), _const_spec(wk.shape),
                _const_spec(wv.shape), _const_spec((1, HEAD_PAD)), _const_spec((1, HEAD_PAD))] + [rope_spec] * 3
    in_specs.append(_const_spec((HEAD_PAD, HEAD_PAD)))
    widths = [(H_B * HEAD_PAD, BF16), (H_B * HEAD_PAD, BF16), (H_B * V_B, BF16), (2 * LANES, F32)]
    return args, in_specs, widths


def _prep_call(tok, body, name, proj, operands):
    args, in_specs, outs = operands
    tm = tok.tm

    def kernel_fn(proj_ref, *refs):
        body(proj_ref[...], *refs)

    return pl.pallas_call(
        kernel_fn,
        grid=(tok.n_tiles,),
        in_specs=[_row_spec(tm, proj.shape[1])] + in_specs,
        out_specs=[_row_spec(tm, w) for w, _ in outs],
        out_shape=[jax.ShapeDtypeStruct((tok.t, w), dt) for w, dt in outs],
        compiler_params=_cparams(1),
        name=name,
    )(proj, *args)


def _mla_cache_prep(cache_rows, weights):
    _, wk, wv, _, kn = weights
    rows = cache_rows.shape[0]
    tm = min(TOKEN_TILE, rows)
    return pl.pallas_call(
        _mla_cache_kernel,
        grid=(rows // tm,),
        in_specs=[_row_spec(tm, 2 * LANES), _const_spec(wk.shape), _const_spec(wv.shape), _const_spec((1, HEAD_PAD))],
        out_specs=[_row_spec(tm, H_B * HEAD_PAD), _row_spec(tm, H_B * V_B)],
        out_shape=[jax.ShapeDtypeStruct((rows, H_B * HEAD_PAD), BF16), jax.ShapeDtypeStruct((rows, H_B * V_B), BF16)],
        compiler_params=_cparams(1),
        name="mla_cache_prep",
    )(cache_rows, wk, wv, kn)


LOG2E = math.log2(math.e)


def _softmax_parts(scores):
    m = functools.reduce(jnp.maximum, [jnp.max(s, axis=-1, keepdims=True) for s in scores])
    ps = [jnp.exp2(s - m) for s in scores]
    denom = functools.reduce(jnp.add, [jnp.sum(p, axis=-1, keepdims=True) for p in ps])
    return ps, denom


def _head_lane_mask(width, head, head_w):
    lane = lax.broadcasted_iota(jnp.int32, (1, width), 1)
    return (lane >= head * head_w) & (lane < (head + 1) * head_w)


def _mla_attn_kernel(*refs, n_kv):
    q_ref = refs[0]
    k_refs = refs[1 : 1 + n_kv]
    v_refs = refs[1 + n_kv : 1 + 2 * n_kv]
    o_ref = refs[1 + 2 * n_kv]
    tiles = [(k, v, j) for k, v in zip(k_refs, v_refs) for j in range(0, k.shape[0], KEY_TILE)]
    def tile_scores(unit):
        h, (k, _, j) = divmod(unit, len(tiles))[0], tiles[unit % len(tiles)]
        sl = slice(h * HEAD_PAD, (h + 1) * HEAD_PAD)
        return _nt_dot(q_ref[:, sl], k[j : min(j + KEY_TILE, k.shape[0]), sl])

    lookahead = 1
    n_units = H_B * len(tiles)
    pending = [tile_scores(u) for u in range(min(lookahead, n_units))]
    out = None
    for h in range(H_B):
        mask = _head_lane_mask(H_B * V_B, h, V_B)
        m = denom = acc = None
        for t, (k, v, j) in enumerate(tiles):
            rows = slice(j, min(j + KEY_TILE, k.shape[0]))
            s = pending.pop(0)
            nxt = h * len(tiles) + t + lookahead
            if nxt < n_units:
                pending.append(tile_scores(nxt))
            vm = jnp.where(mask, v[rows, :], jnp.zeros((), BF16))
            s_max = jnp.max(s, axis=-1, keepdims=True)
            if m is None:
                m = s_max
                p = jnp.exp2(s - m)
                denom = jnp.sum(p, axis=-1, keepdims=True)
                acc = _dot(p.astype(BF16), vm)
            else:
                m_new = jnp.maximum(m, s_max)
                alpha = jnp.exp2(m - m_new)
                p = jnp.exp2(s - m_new)
                denom = denom * alpha + jnp.sum(p, axis=-1, keepdims=True)
                acc = acc * alpha + _dot(p.astype(BF16), vm)
                m = m_new
        acc = acc * (1.0 / denom)
        out = acc if out is None else out + acc
    o_ref[...] = out.astype(BF16)


def _seq_specs(n_seq, seq_len, row0, tq, q_width, kv_widths, kv_new, kv_cache):
    q_blocks = seq_len // tq
    q_spec = lambda w: pl.BlockSpec((tq, w), lambda b, i: (row0 // tq + b * q_blocks + i, 0))
    new_spec = lambda w: pl.BlockSpec((seq_len, w), lambda b, i: (row0 // seq_len + b, 0))
    specs = [q_spec(q_width)]
    for w in kv_widths:
        if kv_cache is not None:
            specs.append(pl.BlockSpec((kv_cache, w), lambda b, i: (b, 0)))
        specs.append(new_spec(w))
    return specs, q_spec


def _mla_attention(n_seq, seq_len, row0, total_rows, q, k, v, k_cache=None, v_cache=None, prev=None):
    tq = min(Q_TILE, seq_len)
    has_cache = k_cache is not None
    cache_len = k_cache.shape[0] // n_seq if has_cache else None
    specs, q_spec = _seq_specs(n_seq, seq_len, row0, tq, H_B * HEAD_PAD, (H_B * HEAD_PAD, H_B * V_B), True, cache_len)
    args = [q] + ([k_cache, k] if has_cache else [k]) + ([v_cache, v] if has_cache else [v])
    body = functools.partial(_mla_attn_kernel, n_kv=2 if has_cache else 1)
    body, specs, args, aliases = _keep_rows_of(prev, body, specs, args)
    return pl.pallas_call(
        body,
        grid=(n_seq, seq_len // tq),
        in_specs=specs,
        out_specs=q_spec(H_B * V_B),
        out_shape=jax.ShapeDtypeStruct((total_rows, H_B * V_B), BF16),
        input_output_aliases=aliases,
        compiler_params=_cparams(2),
        name="mla_attention",
    )(*args)


DQK_W = H_D * 2 * DH_D


def _group_mean_matrix(width, group):
    g = np.arange(width)
    return jnp.asarray((g[:, None] // group == g[None, :] // group).astype(np.float32), BF16)


def _diff_prep_body(pd, gq_ref, gk_ref, gm_ref, cos_ref, shi_ref, slo_ref,
                    q_ref, k_ref, v_ref, kcache_ref, vcache_ref):
    rope = (cos_ref[...], shi_ref[...], slo_ref[...])
    w = H_D * 2 * DH_D

    def norm(x, g):
        ms = _dot_exact_rhs(x * x, gm_ref[...], passes=2) * (1.0 / DH_D)
        return x * lax.rsqrt(ms + EPS) * g

    qn = _rope(norm(pd[:, :w], gq_ref[...]), *rope) * (DH_D ** -0.5 * LOG2E)
    kn = norm(pd[:, w : 2 * w], gk_ref[...])
    kcache_ref[...] = kn
    q_ref[...] = qn.astype(BF16)
    k_ref[...] = _rope(kn, *rope).astype(BF16)
    vcache_ref[...] = pd[:, 2 * w :]
    v_ref[...] = pd[:, 2 * w :].astype(BF16)


def _diff_cache_kernel(k_in_ref, v_in_ref, k_ref, v_ref):
    k_ref[...] = k_in_ref[...].astype(BF16)
    v_ref[...] = v_in_ref[...].astype(BF16)


def _diff_prep_operands(tok, qk_norm, tables):
    tm = tok.tm
    w = H_D * 2 * DH_D
    rope_spec = pl.BlockSpec((tm, w), lambda i: (tok.rope_block(i), 0))
    tile_g = lambda g: jnp.tile(g, H_D * 2).reshape(1, w)
    args = [tile_g(qk_norm[0]), tile_g(qk_norm[1]), _group_mean_matrix(w, DH_D), *tables]
    in_specs = [_const_spec((1, w)), _const_spec((1, w)), _const_spec((w, w))] + [rope_spec] * 3
    widths = [(DQK_W, BF16), (DQK_W, BF16), (H_D * DV_D, BF16), (w, F32), (H_D * DV_D, F32)]
    return args, in_specs, widths


def _diff_cache_prep(k_rows, v_rows):
    rows = k_rows.shape[0]
    tm = min(TOKEN_TILE, rows)
    w = H_D * 2 * DH_D
    return pl.pallas_call(
        _diff_cache_kernel,
        grid=(rows // tm,),
        in_specs=[_row_spec(tm, w), _row_spec(tm, H_D * DV_D)],
        out_specs=[_row_spec(tm, DQK_W), _row_spec(tm, H_D * DV_D)],
        out_shape=[jax.ShapeDtypeStruct((rows, DQK_W), BF16), jax.ShapeDtypeStruct((rows, H_D * DV_D), BF16)],
        compiler_params=_cparams(1),
        name="diff_cache_prep",
    )(k_rows, v_rows)


def _diff_attn_kernel(*refs, n_kv, lam_init):
    q_ref, lam_ref, sub_ref = refs[0], refs[1], refs[2]
    k_refs = refs[3 : 3 + n_kv]
    v_refs = refs[3 + n_kv : 3 + 2 * n_kv]
    o_ref = refs[3 + 2 * n_kv]
    dl = lam_ref[...]
    lam = (jnp.exp(jnp.sum(dl[0:1] * dl[1:2], axis=-1, keepdims=True))
           - jnp.exp(jnp.sum(dl[2:3] * dl[3:4], axis=-1, keepdims=True)) + lam_init)
    groups_per_tile = LANES // DH_D
    lane = lax.broadcasted_iota(jnp.int32, (1, LANES), 1)

    def map_scores(g):
        sl = slice((g // groups_per_tile) * LANES, (g // groups_per_tile + 1) * LANES)
        first = (g % groups_per_tile) * DH_D
        qg = jnp.where((lane >= first) & (lane < first + DH_D), q_ref[:, sl], jnp.zeros((), BF16))
        return [_nt_dot(qg, k[:, sl]) for k in k_refs]

    n_maps = 2 * H_D
    pending = [map_scores(g) for g in range(min(ATTN_LOOKAHEAD, n_maps))]
    out = None
    for h in range(H_D):
        probs = []
        for m in range(2):
            scores = pending.pop(0)
            nxt = 2 * h + m + ATTN_LOOKAHEAD
            if nxt < n_maps:
                pending.append(map_scores(nxt))
            probs.append(_softmax_parts(scores))
        mask = _head_lane_mask(H_D * DV_D, h, DV_D)
        (ps0, den0), (ps1, den1) = probs
        ratio = lam * den0 / den1
        acc = None
        for p0, p1, v in zip(ps0, ps1, v_refs):
            term = _dot((p0 - p1 * ratio).astype(BF16), jnp.where(mask, v[...], jnp.zeros((), BF16)))
            acc = term if acc is None else acc + term
        acc = acc * (1.0 / den0)
        ms = jnp.sum(acc * acc, axis=-1, keepdims=True) * (1.0 / DV_D)
        acc = acc * lax.rsqrt(ms + EPS)
        out = acc if out is None else out + acc
    o_ref[...] = (out * sub_ref[...] * (1.0 - lam_init)).astype(BF16)


def _diff_attention(layer, n_seq, seq_len, row0, total_rows, lam_p, sub_g, q, k, v, k_cache=None, v_cache=None,
                    prev=None):
    tq = min(Q_TILE, seq_len)
    has_cache = k_cache is not None
    cache_len = k_cache.shape[0] // n_seq if has_cache else None
    specs, q_spec = _seq_specs(n_seq, seq_len, row0, tq, DQK_W, (DQK_W, H_D * DV_D), True, cache_len)
    specs = [specs[0], pl.BlockSpec((4, DH_D), lambda b, i: (0, 0)), pl.BlockSpec((1, H_D * DV_D), lambda b, i: (0, 0))
             ] + specs[1:]
    args = [q, lam_p, jnp.tile(sub_g, H_D).reshape(1, H_D * DV_D)]
    args += ([k_cache, k] if has_cache else [k]) + ([v_cache, v] if has_cache else [v])
    lam_init = 0.8 - 0.6 * math.exp(-0.3 * layer)
    body = functools.partial(_diff_attn_kernel, n_kv=2 if has_cache else 1, lam_init=lam_init)
    body, specs, args, aliases = _keep_rows_of(prev, body, specs, args)
    return pl.pallas_call(
        body,
        grid=(n_seq, seq_len // tq),
        in_specs=specs,
        out_specs=q_spec(H_D * DV_D),
        out_shape=jax.ShapeDtypeStruct((total_rows, H_D * DV_D), BF16),
        input_output_aliases=aliases,
        compiler_params=_cparams(2),
        name="diff_attention",
    )(*args)


A_W = H_A * DK_A


HGRN_LEVELS = (2, 4, 8, 16, 32, 64, 128)
PAIR_W = 2 * DK_A


def _hgrn_constants():
    t = np.arange(HGRN_BLOCK)
    lower = t[None, :] <= t[:, None]
    upper = t[None, :] >= t[:, None]
    masks = [t[:, None] == t[None, :]] + [(t[:, None] // b) == (t[None, :] // b) for b in HGRN_LEVELS[:-1]]
    masks = np.stack([np.tile(m, (1, 2)) for m in masks]).astype(np.float32)
    g = np.arange(A_W)
    heads = (g[:, None] // DK_A) == (g[None, :] // DK_A)
    as_bf16 = lambda m: jnp.asarray(m.astype(np.float32), BF16)
    pair_mask = heads[:PAIR_W, :PAIR_W].astype(np.float32)
    return as_bf16(lower), as_bf16(upper), as_bf16(heads), jnp.asarray(pair_mask), jnp.asarray(masks)


def _level_reference(cum, b, forward):
    off = b // 2 - 1 if forward else b // 2
    if b >= 8:
        c3 = cum.reshape(HGRN_BLOCK // b, b, A_W)
        return jnp.broadcast_to(c3[:, off : off + 1, :], c3.shape).reshape(HGRN_BLOCK, A_W)
    c3 = cum.reshape(HGRN_BLOCK // 8, 8, A_W)
    sub = lax.broadcasted_iota(jnp.int32, (1, 8, 1), 1)
    out = None
    for g in range(8 // b):
        cand = jnp.broadcast_to(c3[:, g * b + off : g * b + off + 1, :], c3.shape)
        out = cand if out is None else jnp.where(sub >= g * b, cand, out)
    return out.reshape(HGRN_BLOCK, A_W)


def _hgrn_kernel(*refs, layer, seq_len, has_state):
    if has_state:
        pa_ref, lbl_ref, og_ref, s0_ref, lower_ref, upper_ref, heads_ref, hmask_ref, lvl_ref = refs[:9]
        rest = refs[9:]
    else:
        pa_ref, lbl_ref, og_ref, lower_ref, upper_ref, heads_ref, hmask_ref, lvl_ref = refs[:8]
        s0_ref = None
        rest = refs[8:]
    o_ref, sout_ref, st_ref, oacc_ref = rest
    n_blocks = seq_len // HGRN_BLOCK
    row = lax.broadcasted_iota(jnp.int32, (HGRN_BLOCK, 1), 0)
    first_head = lax.broadcasted_iota(jnp.int32, (1, PAIR_W), 1) < DK_A
    zero_bf16 = jnp.zeros((), BF16)

    def per_head_rows(x):
        return jnp.concatenate([jnp.where(first_head, x, zero_bf16), jnp.where(first_head, zero_bf16, x)], axis=0)

    st_ref[...] = jnp.zeros(st_ref.shape, F32)
    lower_bounds = []
    for d in range(2):
        logits = lbl_ref[d]
        e = jnp.exp(logits - jnp.max(logits, axis=0, keepdims=True))
        p = e / jnp.sum(e, axis=0, keepdims=True)
        lb = jnp.zeros((1, A_W), F32)
        for j in range(1, layer + 1):
            lb = lb + p[j : j + 1, :]
        lower_bounds.append(lb)
        if has_state:
            for h in range(H_A):
                off = (h % 2) * DK_A
                st_ref[d, h // 2, off : off + DV_A, off : off + DK_A] = s0_ref[0, d, h].T

    def both_directions(i, carry):
        for d in range(2):
            lb = lower_bounds[d]
            tri_ref = lower_ref if d == 0 else upper_ref
            blk = i if d == 0 else n_blocks - 1 - i
            r0 = pl.multiple_of(blk * HGRN_BLOCK, HGRN_BLOCK)
            rows = pl.ds(r0, HGRN_BLOCK)
            q = _silu(pa_ref[rows, 0:A_W]) * DK_A ** -0.5
            v = pa_ref[rows, A_W : 2 * A_W]
            logit = pa_ref[rows, (2 + d) * A_W : (3 + d) * A_W]
            key = jnp.minimum((1.0 - lb) * jax.nn.sigmoid(-logit), MAX_INPUT_KEY)
            log_f = jnp.log1p(-key)
            cum = _dot_exact_lhs(tri_ref[...], log_f)
            tot = cum[HGRN_BLOCK - 1 : HGRN_BLOCK, :] if d == 0 else cum[0:1, :]
            v_b = v.astype(BF16)
            scores = [None] * (H_A // 2)

            def add_pairs(qd, kd, mask_index, scores=scores):
                qd_b, kd_b = qd.astype(BF16), kd.astype(BF16)
                for pi in range(H_A // 2):
                    lanes = slice(pi * PAIR_W, (pi + 1) * PAIR_W)
                    s = _nt_dot(qd_b[:, lanes], per_head_rows(kd_b[:, lanes]))
                    if mask_index is not None:
                        s = s * lvl_ref[mask_index]
                    scores[pi] = s if scores[pi] is None else scores[pi] + s

            add_pairs(q, key, 0)
            for li, b in enumerate(HGRN_LEVELS):
                later = ((row % b) >= b // 2) if d == 0 else ((row % b) < b // 2)
                if b == 2:
                    qd = jnp.where(later, q * jnp.exp(log_f), 0.0)
                    kd = jnp.where(later, 0.0, key)
                else:
                    ref = _level_reference(cum, b, d == 0)
                    qd = jnp.where(later, q * jnp.exp(cum - ref), 0.0)
                    kd = jnp.where(later, 0.0, key * jnp.exp(ref - cum))
                add_pairs(qd, kd, li + 1 if b < HGRN_BLOCK else None)
            q_dec = (q * jnp.exp(cum)).astype(BF16)
            k_dec = (key * jnp.exp(tot - cum)).astype(BF16)
            decay = jnp.exp(tot)
            parts = []
            for pi in range(H_A // 2):
                lanes = slice(pi * PAIR_W, (pi + 1) * PAIR_W)
                state = st_ref[d, pi]
                o_pair = _dot(scores[pi].astype(BF16), per_head_rows(v_b[:, lanes]))
                parts.append(o_pair + _nt_dot(q_dec[:, lanes], state.astype(BF16)))
                upd = lax.dot_general(v_b[:, lanes], k_dec[:, lanes], (((0,), (0,)), ((), ())),
                                      preferred_element_type=F32)
                st_ref[d, pi] = state * decay[:, lanes] + upd * hmask_ref[...]
            oacc_ref[d, rows, :] = jnp.concatenate(parts, axis=1)
        return carry

    lax.fori_loop(0, n_blocks, both_directions, 0, unroll=2)
    for d in range(2):
        for h in range(H_A):
            off = (h % 2) * DK_A
            sout_ref[0, d, h] = st_ref[d, h // 2, off : off + DV_A, off : off + DK_A].T

    o = oacc_ref[0] + oacc_ref[1]
    ms = _dot_exact_rhs(o * o, heads_ref[...], passes=2) * (1.0 / DV_A)
    o_ref[...] = (o * lax.rsqrt(ms + EPS) * og_ref[...] * _silu(pa_ref[:, 4 * A_W : 5 * A_W])).astype(BF16)


def _hgrn(layer, n_seq, seq_len, row0, total_rows, pa, lb_logits, onorm_g, s0, prev=None):
    has_state = s0 is not None
    consts = _hgrn_constants()
    seq_spec = lambda w: pl.BlockSpec((seq_len, w), lambda b: (row0 // seq_len + b, 0))
    state_spec = pl.BlockSpec((1, 2, H_A, DK_A, DV_A), lambda b: (b, 0, 0, 0, 0))
    in_specs = [seq_spec(PA_W), _const_spec(lb_logits.shape), _const_spec((1, A_W))]
    args = [pa, lb_logits, jnp.tile(onorm_g, H_A).reshape(1, A_W)]
    if has_state:
        in_specs.append(state_spec)
        args.append(s0)
    in_specs += [_const_spec(c.shape) for c in consts]
    args += list(consts)
    body = functools.partial(_hgrn_kernel, layer=layer, seq_len=seq_len, has_state=has_state)
    body, in_specs, args, aliases = _keep_rows_of(prev, body, in_specs, args)
    return pl.pallas_call(
        body,
        grid=(n_seq,),
        in_specs=in_specs,
        out_specs=[seq_spec(A_W), state_spec],
        out_shape=[
            jax.ShapeDtypeStruct((total_rows, A_W), BF16),
            jax.ShapeDtypeStruct((n_seq, 2, H_A, DK_A, DV_A), F32),
        ],
        input_output_aliases=aliases,
        scratch_shapes=[pltpu.VMEM((2, H_A // 2, 2 * DV_A, 2 * DK_A), F32), pltpu.VMEM((2, seq_len, A_W), F32)],
        compiler_params=_cparams(1),
        name="hgrn",
    )(*args)


HY_W = HY_ORDER * W_C
FFT_R = 64


def _hy_filter_kernel(w1_ref, b1_ref, w2_ref, b2_ref, w3_ref, ld_ref, hf_ref, hb_ref, *, n):
    f32dot = functools.partial(jnp.dot, preferred_element_type=F32, precision=HIGHEST)
    row = lax.broadcasted_iota(jnp.int32, (n, LANES), 0)
    lane = lax.broadcasted_iota(jnp.int32, (n, LANES), 1)
    tn = row.astype(F32) / n
    band = jnp.where(lane <= HY_BANDS, lane, lane - HY_BANDS).astype(F32)
    ang = (2.0 * math.pi) * tn * band
    feats = jnp.where(lane == 0, tn, jnp.where(lane <= HY_BANDS, jnp.cos(ang),
                                                jnp.where(lane <= 2 * HY_BANDS, jnp.sin(ang), 0.0)))
    h = jnp.sin(f32dot(feats, w1_ref[0]) + b1_ref[0])
    h = jnp.sin(f32dot(h, w2_ref[0]) + b2_ref[0])
    h = f32dot(h, w3_ref[0])
    h = h * jnp.exp(-jnp.exp(ld_ref[0]) * tn[:, 0:1])
    hf = h[:, :HY_W]
    hb = jnp.where(row[:, 0:1] == 0, 0.0, h[:, HY_W:])
    norm = jnp.sum(jnp.abs(hf), axis=0, keepdims=True) + jnp.sum(jnp.abs(hb), axis=0, keepdims=True) + EPS
    hf_ref[0] = hf / norm
    hb_ref[0] = hb / norm


def _hy_filters(n, w1, b1, w2, b2, w3, log_decay):
    w1p = jnp.pad(w1, ((0, 0), (0, LANES - HY_EMB), (0, 0)))
    lay = lambda shape: pl.BlockSpec((1,) + shape, lambda l: (l,) + (0,) * len(shape))
    return pl.pallas_call(
        functools.partial(_hy_filter_kernel, n=n),
        grid=(DEPTH,),
        in_specs=[lay((LANES, HY_FH)), lay((1, HY_FH)), lay((HY_FH, HY_FH)), lay((1, HY_FH)),
                  lay((HY_FH, 2 * HY_W)), lay((1, 2 * HY_W))],
        out_specs=[lay((n, HY_W)), lay((n, HY_W))],
        out_shape=[jax.ShapeDtypeStruct((DEPTH, n, HY_W), F32)] * 2,
        compiler_params=_cparams(1),
        name="hyena_filters",
    )(w1p, b1.reshape(DEPTH, 1, HY_FH), w2, b2.reshape(DEPTH, 1, HY_FH), w3, log_decay.reshape(DEPTH, 1, 2 * HY_W))


def _dft_tables_short(n):
    big = 2 * n
    k = np.arange(big)[:, None]
    t = np.arange(n)[None, :]
    ang = 2.0 * np.pi * ((k * t) % big) / big
    fwd = np.concatenate([np.cos(ang), -np.sin(ang)], axis=0)
    inv = np.concatenate([np.cos(ang).T, -np.sin(ang).T], axis=1) / big
    return fwd.astype(np.float32), inv.astype(np.float32)


def _hy_spec_short_kernel(hf_ref, hb_ref, fwd_ref, f_ref, *, n):
    f32dot = functools.partial(jnp.dot, preferred_element_type=F32, precision=HIGHEST)
    xf = f32dot(fwd_ref[...], hf_ref[0])
    xb = f32dot(fwd_ref[...], hb_ref[0])
    big = 2 * n
    f_ref[0, :big, :] = xf[:big] + xb[:big]
    f_ref[0, big:, :] = xf[big:] - xb[big:]


def _hy_spec_short(n, hf, hb):
    fwd, _ = _dft_tables_short(n)
    lay = lambda shape: pl.BlockSpec((1,) + shape, lambda l: (l,) + (0,) * len(shape))
    return pl.pallas_call(
        functools.partial(_hy_spec_short_kernel, n=n),
        grid=(DEPTH,),
        in_specs=[lay((n, HY_W)), lay((n, HY_W)), _const_spec(fwd.shape)],
        out_specs=lay((4 * n, HY_W)),
        out_shape=jax.ShapeDtypeStruct((DEPTH, 4 * n, HY_W), F32),
        compiler_params=_cparams(1),
        name="hyena_spectrum_short",
    )(hf, hb, jnp.asarray(fwd))


def _short_conv(x, w, n):
    row = lax.broadcasted_iota(jnp.int32, (n, 1), 0)
    prev = jnp.where(row == 0, 0.0, pltpu.roll(x, 1, 0))
    nxt = jnp.where(row == n - 1, 0.0, pltpu.roll(x, n - 1, 0))
    return prev * w[0:1, :] + x * w[1:2, :] + nxt * w[2:3, :]


def _hy_conv_short_kernel(pc_ref, ws_ref, bias_ref, f_ref, fwd_ref, inv_ref, o_ref, *, n):
    big = 2 * n
    u = _short_conv(pc_ref[...], ws_ref[...], n)
    v, x1, x2 = u[:, :W_C], u[:, W_C : 2 * W_C], u[:, 2 * W_C :]

    def conv(x, order):
        spec = _dot(fwd_ref[...], x.astype(BF16))
        fr = f_ref[:big, order * W_C : (order + 1) * W_C]
        fi = f_ref[big:, order * W_C : (order + 1) * W_C]
        zr = spec[:big] * fr - spec[big:] * fi
        zi = spec[:big] * fi + spec[big:] * fr
        return _dot(inv_ref[...], jnp.concatenate([zr, zi], axis=0).astype(BF16))

    z = x1 * (conv(v, 0) + v * bias_ref[0:1, :])
    z = x2 * (conv(z, 1) + z * bias_ref[1:2, :])
    o_ref[...] = z.astype(BF16)


def _hy_conv_short(n_seq, n, row0, total_rows, pc, w_short, bias, spec):
    fwd, inv = _dft_tables_short(n)
    seq_spec = lambda w: pl.BlockSpec((n, w), lambda b: (row0 // n + b, 0))
    return pl.pallas_call(
        functools.partial(_hy_conv_short_kernel, n=n),
        grid=(n_seq,),
        in_specs=[seq_spec(PC_W), _const_spec((SHORT_K, PC_W)), _const_spec((HY_ORDER, W_C)),
                  _const_spec((4 * n, HY_W)), _const_spec(fwd.shape), _const_spec(inv.shape)],
        out_specs=seq_spec(W_C),
        out_shape=jax.ShapeDtypeStruct((total_rows, W_C), BF16),
        compiler_params=_cparams(1),
        name="hyena_conv_short",
    )(pc, w_short, bias, spec, jnp.asarray(fwd, BF16), jnp.asarray(inv, BF16))


def _dft_tables_long():
    r = FFT_R
    big = r * r
    half = r // 2
    n2 = np.arange(r)[:, None, None]
    k1 = np.arange(r)[None, :, None]
    n1 = np.arange(half)[None, None, :]
    ang = 2.0 * np.pi * ((k1 * (r * n1 + n2)) % big) / big
    first = np.concatenate([np.cos(ang), -np.sin(ang)], axis=1)
    last = np.concatenate([np.cos(ang), -np.sin(ang)], axis=1).transpose(0, 2, 1) / big
    a = np.arange(r)
    ang_r = 2.0 * np.pi * ((a[:, None] * a[None, :]) % r) / r
    c, s = np.cos(ang_r), np.sin(ang_r)
    mid = np.block([[c, s], [-s, c]])
    mid_inv = np.block([[c, -s], [s, c]])
    f32 = lambda m: m.astype(np.float32)
    return f32(first), f32(mid), f32(mid_inv), f32(last)


FFT_K1 = FFT_R // 2 + 1
FFT_K1_PAD = 40
FFT_K1_UNROLL = 11


def _dft_tables_long_half():
    first, mid, mid_inv, last = _dft_tables_long()
    r = FFT_R
    keep = np.zeros((FFT_K1_PAD,), np.float32)
    keep[:FFT_K1] = 1.0
    weight = np.zeros((FFT_K1_PAD,), np.float32)
    weight[:FFT_K1] = 2.0
    weight[0] = weight[r // 2] = 1.0
    first_h = np.concatenate([first[:, :FFT_K1_PAD] * keep[None, :, None],
                              first[:, r : r + FFT_K1_PAD] * keep[None, :, None]], axis=1)
    last_h = np.concatenate([last[:, :, :FFT_K1_PAD] * weight, last[:, :, r : r + FFT_K1_PAD] * weight], axis=2)
    return first_h, mid, mid_inv, last_h


def _ld_rows(ref, rows):
    return jnp.concatenate([ref[j, rows, :] for j in range(ref.shape[0])], axis=-1)


def _st_rows(ref, rows, val):
    for j in range(ref.shape[0]):
        ref[j, rows, :] = val[:, j * LANES : (j + 1) * LANES]


def _st_transposed(ref, j, val, half):
    n = val.shape[0] // 2
    _st_rows(ref, pl.ds(j, n, stride=2 * half), val[:n])
    _st_rows(ref, pl.ds(half + j, n, stride=2 * half), val[n:])


def _lane_split_scratch(rows, width):
    return pltpu.VMEM((width // LANES, rows, LANES), F32)


FFT_UNROLL = 8


def _fft_long_forward(x_ref, y_ref, dot_first, dot_mid, out_fn, n_k1=FFT_R, unroll_k1=FFT_UNROLL):
    r = FFT_R

    def stage_a(n2, carry):
        res = dot_first(n2, _ld_rows(x_ref, pl.ds(n2, r // 2, stride=r)))
        _st_transposed(y_ref, n2, res, r)
        return carry

    lax.fori_loop(0, r, stage_a, 0, unroll=FFT_UNROLL)

    def stage_c(k1, carry):
        base = pl.multiple_of(k1 * 2 * r, 2 * r)
        out_fn(k1, base, dot_mid(_ld_rows(y_ref, pl.ds(base, 2 * r))))
        return carry

    lax.fori_loop(0, n_k1, stage_c, 0, unroll=unroll_k1)


def _dot_split(t_hi, t_lo, x):
    x_hi = x.astype(BF16)
    x_lo = (x - x_hi.astype(F32)).astype(BF16)
    return _dot(t_hi, x_hi) + _dot(t_hi, x_lo) + _dot(t_lo, x_hi)


def _split_table(m):
    hi = jnp.asarray(m, BF16)
    lo = (jnp.asarray(m) - hi.astype(F32)).astype(BF16)
    return hi, lo


def _hy_spec_long_kernel(hf_ref, hb_ref, first_hi_ref, first_lo_ref, mid_hi_ref, mid_lo_ref, f_ref,
                         x_ref, y_ref, tmp_ref):
    r = FFT_R
    dot_first = lambda n2, slab: _dot_split(first_hi_ref[n2], first_lo_ref[n2], slab)
    dot_mid = lambda block: _dot_split(mid_hi_ref[...], mid_lo_ref[...], block)

    def write_fwd(k1, base, spec):
        tmp_ref[pl.ds(base, 2 * r), :] = spec

    _st_rows(x_ref, slice(None), hf_ref[0])
    _fft_long_forward(x_ref, y_ref, dot_first, dot_mid, write_fwd, n_k1=FFT_K1, unroll_k1=FFT_K1_UNROLL)

    def write_sum(k1, base, spec):
        prev = tmp_ref[pl.ds(base, 2 * r), :]
        f_ref[0, pl.ds(base, r), :] = prev[:r] + spec[:r]
        f_ref[0, pl.ds(base + r, r), :] = prev[r:] - spec[r:]

    _st_rows(x_ref, slice(None), hb_ref[0])
    _fft_long_forward(x_ref, y_ref, dot_first, dot_mid, write_sum, n_k1=FFT_K1, unroll_k1=FFT_K1_UNROLL)


SPEC_ROWS = 2 * FFT_R * FFT_K1


def _hy_spec_long(n, hf, hb):
    first, mid, _, _ = _dft_tables_long_half()
    lay = lambda rows: pl.BlockSpec((1, rows, W_C), lambda l, o: (l, 0, o))
    return pl.pallas_call(
        _hy_spec_long_kernel,
        grid=(DEPTH, HY_ORDER),
        in_specs=[lay(n), lay(n)] + [_const_spec(first.shape)] * 2 + [_const_spec(mid.shape)] * 2,
        out_specs=lay(SPEC_ROWS),
        out_shape=jax.ShapeDtypeStruct((DEPTH, SPEC_ROWS, HY_W), F32),
        scratch_shapes=[_lane_split_scratch(n, W_C), _lane_split_scratch(2 * FFT_R * FFT_K1_PAD, W_C),
                        pltpu.VMEM((SPEC_ROWS, W_C), F32)],
        compiler_params=_cparams(2),
        name="hyena_spectrum_long",
    )(hf, hb, *_split_table(first), *_split_table(mid))


def _hy_conv_long_kernel(sig_ref, gate_ref, ws_ref, bias_ref, f_ref, first_ref, mid_ref, midinv_ref, last_ref, o_ref,
                         x_ref, y_ref, v_ref, out_ref, *, n, order):
    r = FFT_R
    to_bf16 = lambda x: x.astype(BF16)
    gate = _short_conv(gate_ref[...], ws_ref[:, (order + 1) * W_C : (order + 2) * W_C], n)
    sig = _short_conv(sig_ref[...], ws_ref[:, :W_C], n) if order == 0 else sig_ref[...]
    _st_rows(x_ref, slice(None), sig)
    @pl.when(pl.program_id(0) == 0)
    def _():
        v_ref[...] = jnp.zeros(v_ref.shape, F32)

    def filter_and_invert(k1, base, spec):
        fr = f_ref[pl.ds(base, r), :]
        fi = f_ref[pl.ds(base + r, r), :]
        zr = spec[:r] * fr - spec[r:] * fi
        zi = spec[:r] * fi + spec[r:] * fr
        res = _dot(midinv_ref[...], jnp.concatenate([zr, zi], axis=0).astype(BF16))
        _st_transposed(v_ref, k1, res, FFT_K1_PAD)

    dot_first = lambda n2, slab: _dot(first_ref[n2], to_bf16(slab))
    dot_mid = lambda block: _dot(mid_ref[...], to_bf16(block))
    _fft_long_forward(x_ref, y_ref, dot_first, dot_mid, filter_and_invert, n_k1=FFT_K1, unroll_k1=FFT_K1_UNROLL)

    def stage_last(n2, carry):
        base = pl.multiple_of(n2 * 2 * FFT_K1_PAD, 2 * FFT_K1_PAD)
        res = _dot(last_ref[n2], _ld_rows(v_ref, pl.ds(base, 2 * FFT_K1_PAD)).astype(BF16))
        _st_rows(out_ref, pl.ds(n2, r // 2, stride=r), res)
        return carry

    lax.fori_loop(0, r, stage_last, 0, unroll=FFT_UNROLL)
    z = gate * (_ld_rows(out_ref, slice(None)) + sig * bias_ref[order : order + 1, :])
    o_ref[...] = z.astype(o_ref.dtype)


def _hy_conv_long(n_seq, n, row0, total_rows, pc, w_short, bias, spec, prev=None):
    assert 2 * n == FFT_R * FFT_R and FFT_K1 % FFT_K1_UNROLL == 0
    first, mid, mid_inv, last = _dft_tables_long_half()
    big = 2 * n
    bf = lambda m: jnp.asarray(m, BF16)
    lane_block = lambda j: pl.BlockSpec((n, W_C), lambda b: (row0 // n + b, j))
    z = None
    for order in range(HY_ORDER):
        final = order + 1 == HY_ORDER
        in_specs = [lane_block(0), lane_block(order + 1), _const_spec((SHORT_K, PC_W)),
                    _const_spec((HY_ORDER, W_C)), pl.BlockSpec((SPEC_ROWS, W_C), lambda b, order=order: (0, order)),
                    _const_spec(first.shape), _const_spec(mid.shape), _const_spec(mid_inv.shape),
                    _const_spec(last.shape)]
        args = [pc if order == 0 else z, pc, w_short, bias, spec, bf(first), bf(mid), bf(mid_inv), bf(last)]
        body = functools.partial(_hy_conv_long_kernel, n=n, order=order)
        body, in_specs, args, aliases = _keep_rows_of(prev if final else None, body, in_specs, args)
        z = pl.pallas_call(
            body,
            grid=(n_seq,),
            in_specs=in_specs,
            out_specs=lane_block(0),
            out_shape=jax.ShapeDtypeStruct((total_rows, W_C), BF16 if final else F32),
            input_output_aliases=aliases,
            scratch_shapes=[_lane_split_scratch(n, W_C), _lane_split_scratch(2 * FFT_R * FFT_K1_PAD, W_C),
                            _lane_split_scratch(2 * FFT_K1_PAD * FFT_R, W_C), _lane_split_scratch(n, W_C)],
            compiler_params=_cparams(1),
            name="hyena_conv_long",
        )(*args)
    return z


def _pad_in_weights(w_in):
    gap = jnp.zeros(w_in.shape[:2] + (PB_W - MLA_IN,), BF16)
    w = w_in.astype(BF16)
    return jnp.concatenate([w[..., : PA_W + MLA_IN], gap, w[..., PA_W + MLA_IN :]], axis=-1)


def kernel(x_prompt, x_sample, c, cache_mla, cache_diff_k, cache_diff_v, state_hgrn, c_ctx, w_mod, b_mod, norm_g,
           ffn_w_gu, ffn_w_down, w_in, w_out, hgrn_lb_logits, hgrn_onorm, mla_q_norm, mla_kv_norm, mla_w_uq,
           mla_w_ukv, mla_qk_norm, hy_short, hy_w1, hy_b1, hy_w2, hy_b2, hy_w3, hy_log_decay, hy_bias,
           diff_qk_norm, diff_lambda, diff_subln):
    bc, sc, _ = x_prompt.shape
    bl, sl, _ = x_sample.shape
    past = cache_mla.shape[2]
    tok = _Tokens(bc, sc, bl, sl)
    assert bl + 1 <= MOD_ROWS and tok.tc % sl == 0

    x = (x_prompt.reshape(tok.tc, D_MODEL), x_sample.reshape(tok.tl, D_MODEL))
    cond = jnp.concatenate([c_ctx[None], c, jnp.zeros((MOD_ROWS - 1 - bl, D_MODEL), F32)], axis=0)
    mod = _modulation(cond, w_mod, b_mod)

    filters = {n: _hy_filters(n, hy_w1, hy_b1, hy_w2, hy_b2, hy_w3, hy_log_decay) for n in (sc, sl)}
    spec_ctx = _hy_spec_short(sc, *filters[sc])
    spec_lat = _hy_spec_long(sl, *filters[sl])

    mla_tables = _rope_tables(tok, [NOPE_B], HEAD_PAD)
    diff_tables = _rope_tables(tok, list(range(0, H_D * 2 * DH_D, DH_D)), H_D * 2 * DH_D)

    w_gu_b, w_down_b, w_out_b, w_in_b = (ffn_w_gu.astype(BF16), ffn_w_down.astype(BF16), w_out.astype(BF16),
                                         _pad_in_weights(w_in))

    new_mla, new_dk, new_dv, new_state = [], [], [], []
    for l in range(DEPTH):
        x = _ffn(tok, x, mod, l, 0, norm_g[l, 0], w_gu_b, w_down_b)
        pa, pb, pc, pd = _inproj(tok, x, mod, l, norm_g[l, 1], w_in_b)
        mla_w = _mla_weights(mla_w_uq[l], mla_w_ukv[l], mla_qk_norm[l])
        q_b, k_b, v_b, cache_b = _prep_call(
            tok, _mla_prep_body, "mla_prep", pb,
            _mla_prep_operands(tok, mla_q_norm[l], mla_kv_norm[l], mla_w, mla_tables))
        q_d, k_d, v_d, kcache_d, vcache_d = _prep_call(
            tok, _diff_prep_body, "diff_prep", pd, _diff_prep_operands(tok, diff_qk_norm[l], diff_tables))

        o_a, s_ctx = _hgrn(l, bc, sc, 0, tok.t, pa, hgrn_lb_logits, hgrn_onorm[l], None)
        o_a, _ = _hgrn(l, bl, sl, tok.tc, tok.t, pa, hgrn_lb_logits, hgrn_onorm[l], state_hgrn[:, l], prev=o_a)

        cache_rows = jnp.pad(cache_mla[:, l].reshape(bl * past, KV_LORA + ROPE_B),
                             ((0, 0), (0, 2 * LANES - KV_LORA - ROPE_B)))
        kc_b, vc_b = _mla_cache_prep(cache_rows, mla_w)
        o_b = _mla_attention(bc, sc, 0, tok.t, q_b, k_b, v_b)
        o_b = _mla_attention(bl, sl, tok.tc, tok.t, q_b, k_b, v_b, kc_b, vc_b, prev=o_b)

        o_c = _hy_conv_short(bc, sc, 0, tok.t, pc, hy_short[l], hy_bias[l], spec_ctx[l])
        o_c = _hy_conv_long(bl, sl, tok.tc, tok.t, pc, hy_short[l], hy_bias[l], spec_lat[l], prev=o_c)

        kc_d, vc_d = _diff_cache_prep(cache_diff_k[:, l].reshape(bl * past, H_D * 2 * DH_D),
                                      cache_diff_v[:, l].reshape(bl * past, H_D * DV_D))
        o_d = _diff_attention(l, bc, sc, 0, tok.t, diff_lambda[l], diff_subln[l], q_d, k_d, v_d)
        o_d = _diff_attention(l, bl, sl, tok.tc, tok.t, diff_lambda[l], diff_subln[l], q_d, k_d, v_d, kc_d, vc_d,
                              prev=o_d)

        x = _ffn(tok, x, mod, l, 1, norm_g[l, 2], w_gu_b, w_down_b, mixer_outs=[o_a, o_b, o_c, o_d], w_out=w_out_b)

        new_mla.append(cache_b[: tok.tc, : KV_LORA + ROPE_B].reshape(bc, sc, KV_LORA + ROPE_B))
        new_dk.append(kcache_d[: tok.tc].reshape(bc, sc, H_D, 2, DH_D))
        new_dv.append(vcache_d[: tok.tc].reshape(bc, sc, H_D, DV_D))
        new_state.append(s_ctx)

    y_prompt = x[: tok.tc].reshape(bc, sc, D_MODEL)
    y_sample = x[tok.tc :].reshape(bl, sl, D_MODEL)
    return (y_prompt, y_sample, jnp.stack(new_mla, axis=1), jnp.stack(new_dk, axis=1), jnp.stack(new_dv, axis=1),
            jnp.stack(new_state, axis=1))
```

```python
import functools
import math

import jax
import jax.numpy as jnp
import numpy as np
from jax import lax
from jax.experimental import pallas as pl
from jax.experimental.pallas import tpu as pltpu

F32 = jnp.float32
BF16 = jnp.bfloat16
HIGHEST = lax.Precision.HIGHEST

D_MODEL = 1024
DEPTH = 4
GRID_W = 64
N_MOD = 9
D_FF = 2816
EPS = 1e-6
ROPE_BASE = 10000.0
GROUP_W = 256
H_A, DK_A, DV_A = 4, 64, 64
MAX_INPUT_KEY = 1.0 - 1e-6
H_B, NOPE_B, ROPE_B, V_B = 4, 64, 32, 64
Q_LORA, KV_LORA = 256, 128
W_C, HY_ORDER, HY_BANDS, HY_FH, SHORT_K = 256, 2, 8, 64, 3
HY_EMB = 1 + 2 * HY_BANDS
H_D, DV_D, DH_D = 4, 64, 32

LANES = 128
MOD_ROWS = 16
TOKEN_TILE = 512
Q_TILE = 256
KEY_TILE = 512
ATTN_LOOKAHEAD = 3
HGRN_BLOCK = 128
VMEM_LIMIT = 56 * 1024 * 1024

PA_W, PB_W, PC_W, PD_W = 1280, 512, 768, 768
IN_PAD_W = PA_W + PB_W + PC_W + PD_W
MLA_IN = Q_LORA + KV_LORA + ROPE_B
HEAD_PAD = 128


def _cparams(n_axes):
    return pltpu.CompilerParams(dimension_semantics=("arbitrary",) * n_axes, vmem_limit_bytes=VMEM_LIMIT)


def _nt_dot(a, b):
    return lax.dot_general(a, b, (((1,), (1,)), ((), ())), preferred_element_type=F32)


def _dot(a, b):
    return jnp.dot(a, b, preferred_element_type=F32)


def _dot_exact_rhs(a, b_bf16, passes=3):
    out = None
    rem = a
    for _ in range(passes):
        piece = rem.astype(BF16)
        term = _dot(piece, b_bf16)
        out = term if out is None else out + term
        rem = rem - piece.astype(F32)
    return out


def _dot_exact_lhs(a_bf16, b, passes=3):
    out = None
    rem = b
    for _ in range(passes):
        piece = rem.astype(BF16)
        term = _dot(a_bf16, piece)
        out = term if out is None else out + term
        rem = rem - piece.astype(F32)
    return out


def _silu(x):
    return x * jax.nn.sigmoid(x)


def _adaln(x, g, sc, sh):
    y = x * lax.rsqrt(jnp.mean(x * x, axis=-1, keepdims=True) + EPS)
    return (y * g) * (1.0 + sc) + sh


def _rope(x, cos, sin_hi, sin_lo):
    w = x.shape[-1]
    return x * cos + pltpu.roll(x, 8, 1) * sin_hi + pltpu.roll(x, w - 8, 1) * sin_lo


def _mod_kernel(c_ref, w_ref, b_ref, o_ref):
    a = _silu(c_ref[...])
    o_ref[0] = jnp.dot(a, w_ref[0], preferred_element_type=F32, precision=HIGHEST) + b_ref[0]


def _modulation(cond, w_mod, b_mod):
    tn = D_MODEL
    n_col = N_MOD * D_MODEL
    out = pl.pallas_call(
        _mod_kernel,
        grid=(DEPTH, n_col // tn),
        in_specs=[
            pl.BlockSpec((MOD_ROWS, D_MODEL), lambda l, j: (0, 0)),
            pl.BlockSpec((1, D_MODEL, tn), lambda l, j: (l, 0, j)),
            pl.BlockSpec((1, 1, tn), lambda l, j: (l, 0, j)),
        ],
        out_specs=pl.BlockSpec((1, MOD_ROWS, tn), lambda l, j: (l, 0, j)),
        out_shape=jax.ShapeDtypeStruct((DEPTH, MOD_ROWS, n_col), F32),
        compiler_params=_cparams(2),
        name="modulation",
    )(cond, w_mod, b_mod.reshape(DEPTH, 1, n_col))
    return out.reshape(DEPTH * MOD_ROWS, N_MOD, D_MODEL)


class _Tokens:
    def __init__(self, n_ctx_seq, ctx_len, n_lat_seq, lat_len):
        self.bc, self.sc, self.bl, self.sl = n_ctx_seq, ctx_len, n_lat_seq, lat_len
        self.tc, self.tl = n_ctx_seq * ctx_len, n_lat_seq * lat_len
        self.t = self.tc + self.tl
        self.tm = min(TOKEN_TILE, lat_len)
        assert self.tc % self.tm == 0 and lat_len % self.tm == 0
        self.ctx_tiles = self.tc // self.tm
        self.lat_tiles_per_seq = lat_len // self.tm
        self.n_tiles = self.t // self.tm

    def mod_row(self, layer):
        def f(i):
            lat = 1 + (i - self.ctx_tiles) // self.lat_tiles_per_seq
            return layer * MOD_ROWS + jnp.where(i < self.ctx_tiles, 0, lat)

        return f

    def rope_block(self, i):
        return jnp.where(i < self.ctx_tiles, 0, 1 + (i - self.ctx_tiles) % self.lat_tiles_per_seq)


def _mod_spec(tok, layer):
    row = tok.mod_row(layer)
    return pl.BlockSpec((1, N_MOD, D_MODEL), lambda i: (row(i), 0, 0))


def _const_spec(shape):
    zeros = (0,) * len(shape)
    return pl.BlockSpec(shape, lambda *_: zeros)


def _row_spec(tm, width):
    return pl.BlockSpec((tm, width), lambda i: (i, 0))


def _keep_rows_of(prev, kernel_fn, in_specs, args):
    if prev is None:
        return kernel_fn, in_specs, args, {}
    idx = len(in_specs)

    def body(*refs):
        return kernel_fn(*refs[:idx], *refs[idx + 1 :])

    return body, in_specs + [pl.BlockSpec(memory_space=pl.ANY)], args + [prev], {idx: 0}


def _ffn_kernel(*refs, mod_base, split_in, fused, ctx_tiles):
    n_x = 2 if split_in else 1
    x_refs, (mod_ref, g_ref, wgu_ref, wd_ref), rest = refs[:n_x], refs[n_x : n_x + 4], refs[n_x + 4 :]
    is_ctx = pl.program_id(0) < ctx_tiles
    x = jnp.where(is_ctx, x_refs[0][...], x_refs[1][...]) if split_in else x_refs[0][...]
    if fused:
        oa_ref, ob_ref, oc_ref, od_ref, wo_ref = rest[:5]
        mixed = jnp.concatenate([oa_ref[...], ob_ref[...], oc_ref[...], od_ref[...]], axis=-1)
        x = x + mod_ref[0, 5:6, :] * _dot(mixed, wo_ref[...])
    sh = mod_ref[0, mod_base : mod_base + 1, :]
    sc = mod_ref[0, mod_base + 1 : mod_base + 2, :]
    gate = mod_ref[0, mod_base + 2 : mod_base + 3, :]
    h = _adaln(x, g_ref[...], sc, sh).astype(BF16)
    au = _dot(h, wgu_ref[...])
    act = (_silu(au[:, :D_FF]) * au[:, D_FF:]).astype(BF16)
    rest[-1][...] = x + (0.5 * gate) * _dot(act, wd_ref[...])


def _resident(lead_index, shape):
    zeros = (0,) * len(shape)
    return pl.BlockSpec((None,) * len(lead_index) + tuple(shape), lambda i: tuple(lead_index) + zeros,
                        pipeline_mode=pl.Buffered(1))


def _ffn(tok, x, mod, layer, which, norm_g, w_gu, w_down, mixer_outs=None, w_out=None, tiles=None):
    tm = tok.tm
    split_in = isinstance(x, tuple)
    first, count = (0, tok.n_tiles) if tiles is None else tiles
    assert not (split_in and first)
    rows = lambda w: pl.BlockSpec((tm, w), lambda i: (i + first, 0))
    mod_row = tok.mod_row(layer)
    ctx_spec = pl.BlockSpec((tm, D_MODEL), lambda i: (jnp.minimum(i, tok.ctx_tiles - 1), 0))
    lat_spec = pl.BlockSpec((tm, D_MODEL), lambda i: (jnp.maximum(i - tok.ctx_tiles, 0), 0))
    in_specs = ([ctx_spec, lat_spec] if split_in else [rows(D_MODEL)]) + [
        pl.BlockSpec((1, N_MOD, D_MODEL), lambda i: (mod_row(i + first), 0, 0)),
        _const_spec((1, D_MODEL)),
        _resident((layer, which), (D_MODEL, 2 * D_FF)),
        _resident((layer, which), (D_FF, D_MODEL)),
    ]
    args = (list(x) if split_in else [x]) + [mod, norm_g.reshape(1, D_MODEL), w_gu, w_down]
    if mixer_outs is not None:
        in_specs += [rows(GROUP_W)] * 4
        in_specs.append(_resident((layer,), (D_MODEL, D_MODEL)))
        args += list(mixer_outs) + [w_out]
    return pl.pallas_call(
        functools.partial(_ffn_kernel, mod_base=6 * which, split_in=split_in, fused=mixer_outs is not None,
                          ctx_tiles=tok.ctx_tiles),
        grid=(count,),
        in_specs=in_specs,
        out_specs=_row_spec(tm, D_MODEL),
        out_shape=jax.ShapeDtypeStruct((count * tm, D_MODEL), F32),
        compiler_params=_cparams(1),
        name="ffn",
    )(*args)


def _inproj_kernel(x_ref, mod_ref, g_ref, w_ref, pa_ref, pb_ref, pc_ref, pd_ref):
    h = _adaln(x_ref[...], g_ref[...], mod_ref[0, 4:5, :], mod_ref[0, 3:4, :]).astype(BF16)
    p = _dot(h, w_ref[...])
    pa_ref[...] = p[:, :PA_W]
    pb_ref[...] = p[:, PA_W : PA_W + PB_W]
    pc_ref[...] = p[:, PA_W + PB_W : PA_W + PB_W + PC_W]
    pd_ref[...] = p[:, PA_W + PB_W + PC_W :]


def _inproj(tok, x, mod, layer, norm_g, w_in_pad):
    tm = tok.tm
    widths = (PA_W, PB_W, PC_W, PD_W)
    return pl.pallas_call(
        _inproj_kernel,
        grid=(tok.n_tiles,),
        in_specs=[
            _row_spec(tm, D_MODEL),
            _mod_spec(tok, layer),
            _const_spec((1, D_MODEL)),
            _resident((layer,), (D_MODEL, IN_PAD_W)),
        ],
        out_specs=[_row_spec(tm, w) for w in widths],
        out_shape=[jax.ShapeDtypeStruct((tok.t, w), F32) for w in widths],
        compiler_params=_cparams(1),
        name="inproj",
    )(x, mod, norm_g.reshape(1, D_MODEL), w_in_pad)


def _rope_group_tables(n_tok):
    t = np.arange(n_tok)
    pos = np.stack([t // GRID_W, t % GRID_W], axis=1).astype(np.float32)
    inv = (ROPE_BASE ** (-np.arange(8, dtype=np.float32) / 8)).astype(np.float32)
    lane = np.arange(32)
    ang = (pos[:, lane // 16] * inv[lane % 8][None, :]).astype(np.float32)
    second = (lane % 16) >= 8
    cos, sin = np.cos(ang), np.sin(ang)
    return cos, np.where(second[None], sin, 0.0), np.where(second[None], 0.0, -sin)


def _rope_tables(tok, lane_groups, width):
    cos = np.ones((tok.tm + tok.sl, width), np.float32)
    s_hi = np.zeros_like(cos)
    s_lo = np.zeros_like(cos)
    c, a, b = _rope_group_tables(tok.sl)
    for g in lane_groups:
        cos[tok.tm :, g : g + 32] = c
        s_hi[tok.tm :, g : g + 32] = a
        s_lo[tok.tm :, g : g + 32] = b
    return jnp.asarray(cos), jnp.asarray(s_hi), jnp.asarray(s_lo)


def _mla_keys_values(kv_in, wk_ref, wv_ref, kn_ref, rope):
    kraw = _dot(kv_in.astype(BF16), wk_ref[...])
    v = _dot(kv_in[:, :KV_LORA].astype(BF16), wv_ref[...])
    ks = []
    for h in range(H_B):
        kh = kraw[:, h * HEAD_PAD : (h + 1) * HEAD_PAD]
        ss = jnp.sum(kh * kh, axis=-1, keepdims=True) * (1.0 / (NOPE_B + ROPE_B))
        kh = kh * lax.rsqrt(ss + EPS) * kn_ref[...]
        if rope is not None:
            kh = _rope_by_matmul(kh, *rope)
        ks.append(kh.astype(BF16))
    return ks, v.astype(BF16)


def _rope_by_matmul(x, cos, sin, perm):
    return x * cos + _dot(x.astype(BF16), perm) * sin


def _mla_prep_body(pb, gq_ref, gkv_ref, wuq_ref, wk_ref, wv_ref, qn_ref, kn_ref, cos_ref, shi_ref, slo_ref, perm_ref,
                   q_ref, k_ref, v_ref, cache_ref):
    rope = (cos_ref[...], shi_ref[...] + slo_ref[...], perm_ref[...])
    cq = pb[:, :Q_LORA]
    cq = cq * lax.rsqrt(jnp.mean(cq * cq, axis=-1, keepdims=True) + EPS) * gq_ref[...]
    qraw = _dot(cq.astype(BF16), wuq_ref[...])
    scale = (NOPE_B + ROPE_B) ** -0.5 * LOG2E
    for h in range(H_B):
        qh = qraw[:, h * HEAD_PAD : (h + 1) * HEAD_PAD]
        ss = jnp.sum(qh * qh, axis=-1, keepdims=True) * (1.0 / (NOPE_B + ROPE_B))
        qh = _rope_by_matmul(qh * lax.rsqrt(ss + EPS) * qn_ref[...], *rope)
        q_ref[:, h * HEAD_PAD : (h + 1) * HEAD_PAD] = (qh * scale).astype(BF16)
    ckv = pb[:, Q_LORA : Q_LORA + KV_LORA]
    ckv = ckv * lax.rsqrt(jnp.mean(ckv * ckv, axis=-1, keepdims=True) + EPS) * gkv_ref[...]
    kv_in = jnp.concatenate([ckv, pb[:, Q_LORA + KV_LORA :]], axis=-1)
    cache_ref[...] = kv_in
    ks, v = _mla_keys_values(kv_in, wk_ref, wv_ref, kn_ref, rope)
    for h in range(H_B):
        k_ref[:, h * HEAD_PAD : (h + 1) * HEAD_PAD] = ks[h]
    v_ref[...] = v


def _mla_cache_kernel(c_ref, wk_ref, wv_ref, kn_ref, k_ref, v_ref):
    ks, v = _mla_keys_values(c_ref[...], wk_ref, wv_ref, kn_ref, None)
    for h in range(H_B):
        k_ref[:, h * HEAD_PAD : (h + 1) * HEAD_PAD] = ks[h]
    v_ref[...] = v


def _mla_weights(w_uq, w_ukv, qk_norm):
    wuq = jnp.pad(w_uq.reshape(Q_LORA, H_B, NOPE_B + ROPE_B), ((0, 0), (0, 0), (0, HEAD_PAD - NOPE_B - ROPE_B)))
    wuq = wuq.reshape(Q_LORA, H_B * HEAD_PAD).astype(BF16)
    ukv = w_ukv.reshape(KV_LORA, H_B, NOPE_B + V_B)
    wk_nope = jnp.pad(ukv[:, :, :NOPE_B], ((0, 0), (0, 0), (0, HEAD_PAD - NOPE_B)))
    place = np.zeros((2 * LANES - KV_LORA, H_B, HEAD_PAD), np.float32)
    for h in range(H_B):
        place[np.arange(ROPE_B), h, NOPE_B + np.arange(ROPE_B)] = 1.0
    wk = jnp.concatenate([wk_nope, jnp.asarray(place)], axis=0).reshape(2 * LANES, H_B * HEAD_PAD).astype(BF16)
    wv = ukv[:, :, NOPE_B:].reshape(KV_LORA, H_B * V_B).astype(BF16)
    pad = lambda g: jnp.pad(g, (0, HEAD_PAD - NOPE_B - ROPE_B)).reshape(1, HEAD_PAD)
    return wuq, wk, wv, pad(qk_norm[0]), pad(qk_norm[1])


def _mla_prep_operands(tok, gq, gkv, weights, tables):
    wuq, wk, wv, qn, kn = weights
    tm = tok.tm
    rope_spec = pl.BlockSpec((tm, HEAD_PAD), lambda i: (tok.rope_block(i), 0))
    perm = np.zeros((HEAD_PAD, HEAD_PAD), np.float32)
    j = NOPE_B + np.arange(ROPE_B)
    perm[np.where((j - NOPE_B) % 16 < 8, j + 8, j - 8), j] = 1.0
    args = [gq.reshape(1, -1), gkv.reshape(1, -1), wuq, wk, wv, qn, kn, *tables, jnp.asarray(perm, BF16)]
    in_specs = [_const_spec((1, Q_LORA)), _const_spec((1, KV_LORA)), _const_spec(wuq.shape), _const_spec(wk.shape),
                _const_spec(wv.shape), _const_spec((1, HEAD_PAD)), _const_spec((1, HEAD_PAD))] + [rope_spec] * 3
    in_specs.append(_const_spec((HEAD_PAD, HEAD_PAD)))
    widths = [(H_B * HEAD_PAD, BF16), (H_B * HEAD_PAD, BF16), (H_B * V_B, BF16), (2 * LANES, F32)]
    return args, in_specs, widths


def _prep_call(tok, body, name, proj, operands):
    args, in_specs, outs = operands
    tm = tok.tm

    def kernel_fn(proj_ref, *refs):
        body(proj_ref[...], *refs)

    return pl.pallas_call(
        kernel_fn,
        grid=(tok.n_tiles,),
        in_specs=[_row_spec(tm, proj.shape[1])] + in_specs,
        out_specs=[_row_spec(tm, w) for w, _ in outs],
        out_shape=[jax.ShapeDtypeStruct((tok.t, w), dt) for w, dt in outs],
        compiler_params=_cparams(1),
        name=name,
    )(proj, *args)


def _mla_cache_prep(cache_rows, weights):
    _, wk, wv, _, kn = weights
    rows = cache_rows.shape[0]
    tm = min(TOKEN_TILE, rows)
    return pl.pallas_call(
        _mla_cache_kernel,
        grid=(rows // tm,),
        in_specs=[_row_spec(tm, 2 * LANES), _const_spec(wk.shape), _const_spec(wv.shape), _const_spec((1, HEAD_PAD))],
        out_specs=[_row_spec(tm, H_B * HEAD_PAD), _row_spec(tm, H_B * V_B)],
        out_shape=[jax.ShapeDtypeStruct((rows, H_B * HEAD_PAD), BF16), jax.ShapeDtypeStruct((rows, H_B * V_B), BF16)],
        compiler_params=_cparams(1),
        name="mla_cache_prep",
    )(cache_rows, wk, wv, kn)


LOG2E = math.log2(math.e)


def _softmax_parts(scores):
    m = functools.reduce(jnp.maximum, [jnp.max(s, axis=-1, keepdims=True) for s in scores])
    ps = [jnp.exp2(s - m) for s in scores]
    denom = functools.reduce(jnp.add, [jnp.sum(p, axis=-1, keepdims=True) for p in ps])
    return ps, denom


def _head_lane_mask(width, head, head_w):
    lane = lax.broadcasted_iota(jnp.int32, (1, width), 1)
    return (lane >= head * head_w) & (lane < (head + 1) * head_w)


def _mla_attn_kernel(*refs, n_kv):
    q_ref = refs[0]
    k_refs = refs[1 : 1 + n_kv]
    v_refs = refs[1 + n_kv : 1 + 2 * n_kv]
    o_ref = refs[1 + 2 * n_kv]
    def head_scores(h):
        sl = slice(h * HEAD_PAD, (h + 1) * HEAD_PAD)
        return [_nt_dot(q_ref[:, sl], k[:, sl]) for k in k_refs]

    pending = [head_scores(0)]
    out = None
    for h in range(H_B):
        scores = pending.pop(0)
        if h + 1 < H_B:
            pending.append(head_scores(h + 1))
        ps, denom = _softmax_parts(scores)
        mask = _head_lane_mask(H_B * V_B, h, V_B)
        acc = None
        for p, v in zip(ps, v_refs):
            term = _dot(p.astype(BF16), jnp.where(mask, v[...], jnp.zeros((), BF16)))
            acc = term if acc is None else acc + term
        acc = acc * (1.0 / denom)
        out = acc if out is None else out + acc
    o_ref[...] = out.astype(BF16)


def _seq_specs(n_seq, seq_len, row0, tq, q_width, kv_widths, kv_new, kv_cache):
    q_blocks = seq_len // tq
    q_spec = lambda w: pl.BlockSpec((tq, w), lambda b, i: (row0 // tq + b * q_blocks + i, 0))
    new_spec = lambda w: pl.BlockSpec((seq_len, w), lambda b, i: (row0 // seq_len + b, 0))
    specs = [q_spec(q_width)]
    for w in kv_widths:
        if kv_cache is not None:
            specs.append(pl.BlockSpec((kv_cache, w), lambda b, i: (b, 0)))
        specs.append(new_spec(w))
    return specs, q_spec


def _mla_attention(n_seq, seq_len, row0, total_rows, q, k, v, k_cache=None, v_cache=None, prev=None):
    tq = min(Q_TILE, seq_len)
    has_cache = k_cache is not None
    cache_len = k_cache.shape[0] // n_seq if has_cache else None
    specs, q_spec = _seq_specs(n_seq, seq_len, row0, tq, H_B * HEAD_PAD, (H_B * HEAD_PAD, H_B * V_B), True, cache_len)
    args = [q] + ([k_cache, k] if has_cache else [k]) + ([v_cache, v] if has_cache else [v])
    body = functools.partial(_mla_attn_kernel, n_kv=2 if has_cache else 1)
    body, specs, args, aliases = _keep_rows_of(prev, body, specs, args)
    return pl.pallas_call(
        body,
        grid=(n_seq, seq_len // tq),
        in_specs=specs,
        out_specs=q_spec(H_B * V_B),
        out_shape=jax.ShapeDtypeStruct((total_rows, H_B * V_B), BF16),
        input_output_aliases=aliases,
        compiler_params=_cparams(2),
        name="mla_attention",
    )(*args)


DQK_W = H_D * 2 * DH_D


def _group_mean_matrix(width, group):
    g = np.arange(width)
    return jnp.asarray((g[:, None] // group == g[None, :] // group).astype(np.float32), BF16)


def _diff_prep_body(pd, gq_ref, gk_ref, gm_ref, cos_ref, shi_ref, slo_ref,
                    q_ref, k_ref, v_ref, kcache_ref, vcache_ref):
    rope = (cos_ref[...], shi_ref[...], slo_ref[...])
    w = H_D * 2 * DH_D

    def norm(x, g):
        ms = _dot_exact_rhs(x * x, gm_ref[...], passes=2) * (1.0 / DH_D)
        return x * lax.rsqrt(ms + EPS) * g

    qn = _rope(norm(pd[:, :w], gq_ref[...]), *rope) * (DH_D ** -0.5 * LOG2E)
    kn = norm(pd[:, w : 2 * w], gk_ref[...])
    kcache_ref[...] = kn
    q_ref[...] = qn.astype(BF16)
    k_ref[...] = _rope(kn, *rope).astype(BF16)
    vcache_ref[...] = pd[:, 2 * w :]
    v_ref[...] = pd[:, 2 * w :].astype(BF16)


def _diff_cache_kernel(k_in_ref, v_in_ref, k_ref, v_ref):
    k_ref[...] = k_in_ref[...].astype(BF16)
    v_ref[...] = v_in_ref[...].astype(BF16)


def _diff_prep_operands(tok, qk_norm, tables):
    tm = tok.tm
    w = H_D * 2 * DH_D
    rope_spec = pl.BlockSpec((tm, w), lambda i: (tok.rope_block(i), 0))
    tile_g = lambda g: jnp.tile(g, H_D * 2).reshape(1, w)
    args = [tile_g(qk_norm[0]), tile_g(qk_norm[1]), _group_mean_matrix(w, DH_D), *tables]
    in_specs = [_const_spec((1, w)), _const_spec((1, w)), _const_spec((w, w))] + [rope_spec] * 3
    widths = [(DQK_W, BF16), (DQK_W, BF16), (H_D * DV_D, BF16), (w, F32), (H_D * DV_D, F32)]
    return args, in_specs, widths


def _diff_cache_prep(k_rows, v_rows):
    rows = k_rows.shape[0]
    tm = min(TOKEN_TILE, rows)
    w = H_D * 2 * DH_D
    return pl.pallas_call(
        _diff_cache_kernel,
        grid=(rows // tm,),
        in_specs=[_row_spec(tm, w), _row_spec(tm, H_D * DV_D)],
        out_specs=[_row_spec(tm, DQK_W), _row_spec(tm, H_D * DV_D)],
        out_shape=[jax.ShapeDtypeStruct((rows, DQK_W), BF16), jax.ShapeDtypeStruct((rows, H_D * DV_D), BF16)],
        compiler_params=_cparams(1),
        name="diff_cache_prep",
    )(k_rows, v_rows)


def _diff_attn_kernel(*refs, n_kv, lam_init):
    q_ref, lam_ref, sub_ref = refs[0], refs[1], refs[2]
    k_refs = refs[3 : 3 + n_kv]
    v_refs = refs[3 + n_kv : 3 + 2 * n_kv]
    o_ref = refs[3 + 2 * n_kv]
    dl = lam_ref[...]
    lam = (jnp.exp(jnp.sum(dl[0:1] * dl[1:2], axis=-1, keepdims=True))
           - jnp.exp(jnp.sum(dl[2:3] * dl[3:4], axis=-1, keepdims=True)) + lam_init)
    groups_per_tile = LANES // DH_D
    lane = lax.broadcasted_iota(jnp.int32, (1, LANES), 1)

    def map_scores(g):
        sl = slice((g // groups_per_tile) * LANES, (g // groups_per_tile + 1) * LANES)
        first = (g % groups_per_tile) * DH_D
        qg = jnp.where((lane >= first) & (lane < first + DH_D), q_ref[:, sl], jnp.zeros((), BF16))
        return [_nt_dot(qg, k[:, sl]) for k in k_refs]

    n_maps = 2 * H_D
    pending = [map_scores(g) for g in range(min(ATTN_LOOKAHEAD, n_maps))]
    out = None
    for h in range(H_D):
        probs = []
        for m in range(2):
            scores = pending.pop(0)
            nxt = 2 * h + m + ATTN_LOOKAHEAD
            if nxt < n_maps:
                pending.append(map_scores(nxt))
            probs.append(_softmax_parts(scores))
        mask = _head_lane_mask(H_D * DV_D, h, DV_D)
        (ps0, den0), (ps1, den1) = probs
        ratio = lam * den0 / den1
        acc = None
        for p0, p1, v in zip(ps0, ps1, v_refs):
            term = _dot((p0 - p1 * ratio).astype(BF16), jnp.where(mask, v[...], jnp.zeros((), BF16)))
            acc = term if acc is None else acc + term
        acc = acc * (1.0 / den0)
        ms = jnp.sum(acc * acc, axis=-1, keepdims=True) * (1.0 / DV_D)
        acc = acc * lax.rsqrt(ms + EPS)
        out = acc if out is None else out + acc
    o_ref[...] = (out * sub_ref[...] * (1.0 - lam_init)).astype(BF16)


def _diff_attention(layer, n_seq, seq_len, row0, total_rows, lam_p, sub_g, q, k, v, k_cache=None, v_cache=None,
                    prev=None):
    tq = min(Q_TILE, seq_len)
    has_cache = k_cache is not None
    cache_len = k_cache.shape[0] // n_seq if has_cache else None
    specs, q_spec = _seq_specs(n_seq, seq_len, row0, tq, DQK_W, (DQK_W, H_D * DV_D), True, cache_len)
    specs = [specs[0], pl.BlockSpec((4, DH_D), lambda b, i: (0, 0)), pl.BlockSpec((1, H_D * DV_D), lambda b, i: (0, 0))
             ] + specs[1:]
    args = [q, lam_p, jnp.tile(sub_g, H_D).reshape(1, H_D * DV_D)]
    args += ([k_cache, k] if has_cache else [k]) + ([v_cache, v] if has_cache else [v])
    lam_init = 0.8 - 0.6 * math.exp(-0.3 * layer)
    body = functools.partial(_diff_attn_kernel, n_kv=2 if has_cache else 1, lam_init=lam_init)
    body, specs, args, aliases = _keep_rows_of(prev, body, specs, args)
    return pl.pallas_call(
        body,
        grid=(n_seq, seq_len // tq),
        in_specs=specs,
        out_specs=q_spec(H_D * DV_D),
        out_shape=jax.ShapeDtypeStruct((total_rows, H_D * DV_D), BF16),
        input_output_aliases=aliases,
        compiler_params=_cparams(2),
        name="diff_attention",
    )(*args)


A_W = H_A * DK_A


HGRN_LEVELS = (2, 4, 8, 16, 32, 64, 128)
PAIR_W = 2 * DK_A


def _hgrn_constants():
    t = np.arange(HGRN_BLOCK)
    lower = t[None, :] <= t[:, None]
    upper = t[None, :] >= t[:, None]
    masks = [t[:, None] == t[None, :]] + [(t[:, None] // b) == (t[None, :] // b) for b in HGRN_LEVELS[:-1]]
    masks = np.stack([np.tile(m, (1, 2)) for m in masks]).astype(np.float32)
    g = np.arange(A_W)
    heads = (g[:, None] // DK_A) == (g[None, :] // DK_A)
    as_bf16 = lambda m: jnp.asarray(m.astype(np.float32), BF16)
    pair_mask = heads[:PAIR_W, :PAIR_W].astype(np.float32)
    return as_bf16(lower), as_bf16(upper), as_bf16(heads), jnp.asarray(pair_mask), jnp.asarray(masks)


def _level_reference(cum, b, forward):
    off = b // 2 - 1 if forward else b // 2
    if b >= 8:
        c3 = cum.reshape(HGRN_BLOCK // b, b, A_W)
        return jnp.broadcast_to(c3[:, off : off + 1, :], c3.shape).reshape(HGRN_BLOCK, A_W)
    c3 = cum.reshape(HGRN_BLOCK // 8, 8, A_W)
    sub = lax.broadcasted_iota(jnp.int32, (1, 8, 1), 1)
    out = None
    for g in range(8 // b):
        cand = jnp.broadcast_to(c3[:, g * b + off : g * b + off + 1, :], c3.shape)
        out = cand if out is None else jnp.where(sub >= g * b, cand, out)
    return out.reshape(HGRN_BLOCK, A_W)


def _hgrn_kernel(*refs, layer, seq_len, has_state):
    if has_state:
        pa_ref, lbl_ref, og_ref, s0_ref, lower_ref, upper_ref, heads_ref, hmask_ref, lvl_ref = refs[:9]
        rest = refs[9:]
    else:
        pa_ref, lbl_ref, og_ref, lower_ref, upper_ref, heads_ref, hmask_ref, lvl_ref = refs[:8]
        s0_ref = None
        rest = refs[8:]
    o_ref, sout_ref, st_ref, oacc_ref = rest if not has_state else (rest[0], None, rest[1], rest[2])
    n_blocks = seq_len // HGRN_BLOCK
    row = lax.broadcasted_iota(jnp.int32, (HGRN_BLOCK, 1), 0)
    first_head = lax.broadcasted_iota(jnp.int32, (1, PAIR_W), 1) < DK_A
    zero_bf16 = jnp.zeros((), BF16)

    def per_head_rows(x):
        return jnp.concatenate([jnp.where(first_head, x, zero_bf16), jnp.where(first_head, zero_bf16, x)], axis=0)

    st_ref[...] = jnp.zeros(st_ref.shape, F32)
    lower_bounds = []
    for d in range(2):
        logits = lbl_ref[d]
        e = jnp.exp(logits - jnp.max(logits, axis=0, keepdims=True))
        p = e / jnp.sum(e, axis=0, keepdims=True)
        lb = jnp.zeros((1, A_W), F32)
        for j in range(1, layer + 1):
            lb = lb + p[j : j + 1, :]
        lower_bounds.append(lb)
        if has_state:
            for h in range(H_A):
                off = (h % 2) * DK_A
                st_ref[d, h // 2, off : off + DV_A, off : off + DK_A] = s0_ref[0, d, h].T

    def both_directions(i, carry):
        for d in range(2):
            lb = lower_bounds[d]
            tri_ref = lower_ref if d == 0 else upper_ref
            blk = i if d == 0 else n_blocks - 1 - i
            r0 = pl.multiple_of(blk * HGRN_BLOCK, HGRN_BLOCK)
            rows = pl.ds(r0, HGRN_BLOCK)
            q = _silu(pa_ref[rows, 0:A_W]) * DK_A ** -0.5
            v = pa_ref[rows, A_W : 2 * A_W]
            logit = pa_ref[rows, (2 + d) * A_W : (3 + d) * A_W]
            key = jnp.minimum((1.0 - lb) * jax.nn.sigmoid(-logit), MAX_INPUT_KEY)
            log_f = jnp.log1p(-key)
            cum = _dot_exact_lhs(tri_ref[...], log_f)
            tot = cum[HGRN_BLOCK - 1 : HGRN_BLOCK, :] if d == 0 else cum[0:1, :]
            v_b = v.astype(BF16)
            scores = [None] * (H_A // 2)

            def add_pairs(qd, kd, mask_index, scores=scores):
                qd_b, kd_b = qd.astype(BF16), kd.astype(BF16)
                for pi in range(H_A // 2):
                    lanes = slice(pi * PAIR_W, (pi + 1) * PAIR_W)
                    s = _nt_dot(qd_b[:, lanes], per_head_rows(kd_b[:, lanes]))
                    if mask_index is not None:
                        s = s * lvl_ref[mask_index]
                    scores[pi] = s if scores[pi] is None else scores[pi] + s

            add_pairs(q, key, 0)
            for li, b in enumerate(HGRN_LEVELS):
                later = ((row % b) >= b // 2) if d == 0 else ((row % b) < b // 2)
                if b == 2:
                    qd = jnp.where(later, q * jnp.exp(log_f), 0.0)
                    kd = jnp.where(later, 0.0, key)
                else:
                    ref = _level_reference(cum, b, d == 0)
                    qd = jnp.where(later, q * jnp.exp(cum - ref), 0.0)
                    kd = jnp.where(later, 0.0, key * jnp.exp(ref - cum))
                add_pairs(qd, kd, li + 1 if b < HGRN_BLOCK else None)
            q_dec = (q * jnp.exp(cum)).astype(BF16)
            k_dec = (key * jnp.exp(tot - cum)).astype(BF16)
            decay = jnp.exp(tot)
            parts = []
            for pi in range(H_A // 2):
                lanes = slice(pi * PAIR_W, (pi + 1) * PAIR_W)
                state = st_ref[d, pi]
                o_pair = _dot(scores[pi].astype(BF16), per_head_rows(v_b[:, lanes]))
                parts.append(o_pair + _nt_dot(q_dec[:, lanes], state.astype(BF16)))
                upd = lax.dot_general(v_b[:, lanes], k_dec[:, lanes], (((0,), (0,)), ((), ())),
                                      preferred_element_type=F32)
                st_ref[d, pi] = state * decay[:, lanes] + upd * hmask_ref[...]
            oacc_ref[d, rows, :] = jnp.concatenate(parts, axis=1)
        return carry

    lax.fori_loop(0, n_blocks, both_directions, 0, unroll=2)
    if sout_ref is not None:
        for d in range(2):
            for h in range(H_A):
                off = (h % 2) * DK_A
                sout_ref[0, d, h] = st_ref[d, h // 2, off : off + DV_A, off : off + DK_A].T

    o = oacc_ref[0] + oacc_ref[1]
    ms = _dot_exact_rhs(o * o, heads_ref[...], passes=2) * (1.0 / DV_A)
    o_ref[...] = (o * lax.rsqrt(ms + EPS) * og_ref[...] * _silu(pa_ref[:, 4 * A_W : 5 * A_W])).astype(BF16)


def _hgrn(layer, n_seq, seq_len, row0, total_rows, pa, lb_logits, onorm_g, s0, prev=None):
    has_state = s0 is not None
    consts = _hgrn_constants()
    seq_spec = lambda w: pl.BlockSpec((seq_len, w), lambda b: (row0 // seq_len + b, 0))
    state_spec = pl.BlockSpec((1, 2, H_A, DK_A, DV_A), lambda b: (b, 0, 0, 0, 0))
    in_specs = [seq_spec(PA_W), _const_spec(lb_logits.shape), _const_spec((1, A_W))]
    args = [pa, lb_logits, jnp.tile(onorm_g, H_A).reshape(1, A_W)]
    if has_state:
        in_specs.append(state_spec)
        args.append(s0)
    in_specs += [_const_spec(c.shape) for c in consts]
    args += list(consts)
    body = functools.partial(_hgrn_kernel, layer=layer, seq_len=seq_len, has_state=has_state)
    body, in_specs, args, aliases = _keep_rows_of(prev, body, in_specs, args)
    out_specs = [seq_spec(A_W)]
    out_shape = [jax.ShapeDtypeStruct((total_rows, A_W), BF16)]
    if not has_state:
        out_specs.append(state_spec)
        out_shape.append(jax.ShapeDtypeStruct((n_seq, 2, H_A, DK_A, DV_A), F32))
    outs = pl.pallas_call(
        body,
        grid=(n_seq,),
        in_specs=in_specs,
        out_specs=out_specs,
        out_shape=out_shape,
        input_output_aliases=aliases,
        scratch_shapes=[pltpu.VMEM((2, H_A // 2, 2 * DV_A, 2 * DK_A), F32), pltpu.VMEM((2, seq_len, A_W), F32)],
        compiler_params=_cparams(1),
        name="hgrn",
    )(*args)
    return (outs[0], None) if has_state else tuple(outs)


HY_W = HY_ORDER * W_C
FFT_R = 64


def _hy_filter_kernel(w1_ref, b1_ref, w2_ref, b2_ref, w3_ref, ld_ref, hf_ref, hb_ref, *, n):
    f32dot = functools.partial(jnp.dot, preferred_element_type=F32, precision=HIGHEST)
    row = lax.broadcasted_iota(jnp.int32, (n, LANES), 0)
    lane = lax.broadcasted_iota(jnp.int32, (n, LANES), 1)
    tn = row.astype(F32) / n
    band = jnp.where(lane <= HY_BANDS, lane, lane - HY_BANDS).astype(F32)
    ang = (2.0 * math.pi) * tn * band
    feats = jnp.where(lane == 0, tn, jnp.where(lane <= HY_BANDS, jnp.cos(ang),
                                                jnp.where(lane <= 2 * HY_BANDS, jnp.sin(ang), 0.0)))
    h = jnp.sin(f32dot(feats, w1_ref[0]) + b1_ref[0])
    h = jnp.sin(f32dot(h, w2_ref[0]) + b2_ref[0])
    h = f32dot(h, w3_ref[0])
    h = h * jnp.exp(-jnp.exp(ld_ref[0]) * tn[:, 0:1])
    hf = h[:, :HY_W]
    hb = jnp.where(row[:, 0:1] == 0, 0.0, h[:, HY_W:])
    norm = jnp.sum(jnp.abs(hf), axis=0, keepdims=True) + jnp.sum(jnp.abs(hb), axis=0, keepdims=True) + EPS
    hf_ref[0] = hf / norm
    hb_ref[0] = hb / norm


def _hy_filters(n, w1, b1, w2, b2, w3, log_decay):
    w1p = jnp.pad(w1, ((0, 0), (0, LANES - HY_EMB), (0, 0)))
    lay = lambda shape: pl.BlockSpec((1,) + shape, lambda l: (l,) + (0,) * len(shape))
    return pl.pallas_call(
        functools.partial(_hy_filter_kernel, n=n),
        grid=(DEPTH,),
        in_specs=[lay((LANES, HY_FH)), lay((1, HY_FH)), lay((HY_FH, HY_FH)), lay((1, HY_FH)),
                  lay((HY_FH, 2 * HY_W)), lay((1, 2 * HY_W))],
        out_specs=[lay((n, HY_W)), lay((n, HY_W))],
        out_shape=[jax.ShapeDtypeStruct((DEPTH, n, HY_W), F32)] * 2,
        compiler_params=_cparams(1),
        name="hyena_filters",
    )(w1p, b1.reshape(DEPTH, 1, HY_FH), w2, b2.reshape(DEPTH, 1, HY_FH), w3, log_decay.reshape(DEPTH, 1, 2 * HY_W))


def _dft_tables_short(n):
    big = 2 * n
    k = np.arange(big)[:, None]
    t = np.arange(n)[None, :]
    ang = 2.0 * np.pi * ((k * t) % big) / big
    fwd = np.concatenate([np.cos(ang), -np.sin(ang)], axis=0)
    inv = np.concatenate([np.cos(ang).T, -np.sin(ang).T], axis=1) / big
    return fwd.astype(np.float32), inv.astype(np.float32)


def _hy_spec_short_kernel(hf_ref, hb_ref, fwd_ref, f_ref, *, n):
    f32dot = functools.partial(jnp.dot, preferred_element_type=F32, precision=HIGHEST)
    xf = f32dot(fwd_ref[...], hf_ref[0])
    xb = f32dot(fwd_ref[...], hb_ref[0])
    big = 2 * n
    f_ref[0, :big, :] = xf[:big] + xb[:big]
    f_ref[0, big:, :] = xf[big:] - xb[big:]


def _hy_spec_short(n, hf, hb):
    fwd, _ = _dft_tables_short(n)
    lay = lambda shape: pl.BlockSpec((1,) + shape, lambda l: (l,) + (0,) * len(shape))
    return pl.pallas_call(
        functools.partial(_hy_spec_short_kernel, n=n),
        grid=(DEPTH,),
        in_specs=[lay((n, HY_W)), lay((n, HY_W)), _const_spec(fwd.shape)],
        out_specs=lay((4 * n, HY_W)),
        out_shape=jax.ShapeDtypeStruct((DEPTH, 4 * n, HY_W), F32),
        compiler_params=_cparams(1),
        name="hyena_spectrum_short",
    )(hf, hb, jnp.asarray(fwd))


def _short_conv(x, w, n):
    row = lax.broadcasted_iota(jnp.int32, (n, 1), 0)
    prev = jnp.where(row == 0, 0.0, pltpu.roll(x, 1, 0))
    nxt = jnp.where(row == n - 1, 0.0, pltpu.roll(x, n - 1, 0))
    return prev * w[0:1, :] + x * w[1:2, :] + nxt * w[2:3, :]


def _hy_conv_short_kernel(pc_ref, ws_ref, bias_ref, f_ref, fwd_ref, inv_ref, o_ref, *, n):
    big = 2 * n
    u = _short_conv(pc_ref[...], ws_ref[...], n)
    v, x1, x2 = u[:, :W_C], u[:, W_C : 2 * W_C], u[:, 2 * W_C :]

    def conv(x, order):
        spec = _dot(fwd_ref[...], x.astype(BF16))
        fr = f_ref[:big, order * W_C : (order + 1) * W_C]
        fi = f_ref[big:, order * W_C : (order + 1) * W_C]
        zr = spec[:big] * fr - spec[big:] * fi
        zi = spec[:big] * fi + spec[big:] * fr
        return _dot(inv_ref[...], jnp.concatenate([zr, zi], axis=0).astype(BF16))

    z = x1 * (conv(v, 0) + v * bias_ref[0:1, :])
    z = x2 * (conv(z, 1) + z * bias_ref[1:2, :])
    o_ref[...] = z.astype(BF16)


def _hy_conv_short(n_seq, n, row0, total_rows, pc, w_short, bias, spec):
    fwd, inv = _dft_tables_short(n)
    seq_spec = lambda w: pl.BlockSpec((n, w), lambda b: (row0 // n + b, 0))
    return pl.pallas_call(
        functools.partial(_hy_conv_short_kernel, n=n),
        grid=(n_seq,),
        in_specs=[seq_spec(PC_W), _const_spec((SHORT_K, PC_W)), _const_spec((HY_ORDER, W_C)),
                  _const_spec((4 * n, HY_W)), _const_spec(fwd.shape), _const_spec(inv.shape)],
        out_specs=seq_spec(W_C),
        out_shape=jax.ShapeDtypeStruct((total_rows, W_C), BF16),
        compiler_params=_cparams(1),
        name="hyena_conv_short",
    )(pc, w_short, bias, spec, jnp.asarray(fwd, BF16), jnp.asarray(inv, BF16))


def _dft_tables_long():
    r = FFT_R
    big = r * r
    half = r // 2
    n2 = np.arange(r)[:, None, None]
    k1 = np.arange(r)[None, :, None]
    n1 = np.arange(half)[None, None, :]
    ang = 2.0 * np.pi * ((k1 * (r * n1 + n2)) % big) / big
    first = np.concatenate([np.cos(ang), -np.sin(ang)], axis=1)
    last = np.concatenate([np.cos(ang), -np.sin(ang)], axis=1).transpose(0, 2, 1) / big
    a = np.arange(r)
    ang_r = 2.0 * np.pi * ((a[:, None] * a[None, :]) % r) / r
    c, s = np.cos(ang_r), np.sin(ang_r)
    mid = np.block([[c, s], [-s, c]])
    mid_inv = np.block([[c, -s], [s, c]])
    f32 = lambda m: m.astype(np.float32)
    return f32(first), f32(mid), f32(mid_inv), f32(last)


FFT_K1 = FFT_R // 2 + 1
FFT_K1_PAD = 40
FFT_K1_UNROLL = 11


def _dft_tables_long_half():
    first, mid, mid_inv, last = _dft_tables_long()
    r = FFT_R
    keep = np.zeros((FFT_K1_PAD,), np.float32)
    keep[:FFT_K1] = 1.0
    weight = np.zeros((FFT_K1_PAD,), np.float32)
    weight[:FFT_K1] = 2.0
    weight[0] = weight[r // 2] = 1.0
    first_h = np.concatenate([first[:, :FFT_K1_PAD] * keep[None, :, None],
                              first[:, r : r + FFT_K1_PAD] * keep[None, :, None]], axis=1)
    last_h = np.concatenate([last[:, :, :FFT_K1_PAD] * weight, last[:, :, r : r + FFT_K1_PAD] * weight], axis=2)
    return first_h, mid, mid_inv, last_h


def _ld_rows(ref, rows):
    return jnp.concatenate([ref[j, rows, :] for j in range(ref.shape[0])], axis=-1)


def _st_rows(ref, rows, val):
    for j in range(ref.shape[0]):
        ref[j, rows, :] = val[:, j * LANES : (j + 1) * LANES]


def _st_transposed(ref, j, val, half):
    n = val.shape[0] // 2
    _st_rows(ref, pl.ds(j, n, stride=2 * half), val[:n])
    _st_rows(ref, pl.ds(half + j, n, stride=2 * half), val[n:])


def _lane_split_scratch(rows, width):
    return pltpu.VMEM((width // LANES, rows, LANES), F32)


FFT_UNROLL = 8


def _fft_long_forward(x_ref, y_ref, dot_first, dot_mid, out_fn, n_k1=FFT_R, unroll_k1=FFT_UNROLL):
    r = FFT_R

    def stage_a(n2, carry):
        res = dot_first(n2, _ld_rows(x_ref, pl.ds(n2, r // 2, stride=r)))
        _st_transposed(y_ref, n2, res, r)
        return carry

    lax.fori_loop(0, r, stage_a, 0, unroll=FFT_UNROLL)

    def stage_c(k1, carry):
        base = pl.multiple_of(k1 * 2 * r, 2 * r)
        out_fn(k1, base, dot_mid(_ld_rows(y_ref, pl.ds(base, 2 * r))))
        return carry

    lax.fori_loop(0, n_k1, stage_c, 0, unroll=unroll_k1)


def _dot_split(t_hi, t_lo, x):
    x_hi = x.astype(BF16)
    x_lo = (x - x_hi.astype(F32)).astype(BF16)
    return _dot(t_hi, x_hi) + _dot(t_hi, x_lo) + _dot(t_lo, x_hi)


def _split_table(m):
    hi = jnp.asarray(m, BF16)
    lo = (jnp.asarray(m) - hi.astype(F32)).astype(BF16)
    return hi, lo


def _hy_spec_long_kernel(hf_ref, hb_ref, first_hi_ref, first_lo_ref, mid_hi_ref, mid_lo_ref, f_ref,
                         x_ref, y_ref, tmp_ref):
    r = FFT_R
    dot_first = lambda n2, slab: _dot_split(first_hi_ref[n2], first_lo_ref[n2], slab)
    dot_mid = lambda block: _dot_split(mid_hi_ref[...], mid_lo_ref[...], block)

    def write_fwd(k1, base, spec):
        tmp_ref[pl.ds(base, 2 * r), :] = spec

    _st_rows(x_ref, slice(None), hf_ref[0])
    _fft_long_forward(x_ref, y_ref, dot_first, dot_mid, write_fwd, n_k1=FFT_K1, unroll_k1=FFT_K1_UNROLL)

    def write_sum(k1, base, spec):
        prev = tmp_ref[pl.ds(base, 2 * r), :]
        f_ref[0, pl.ds(base, r), :] = prev[:r] + spec[:r]
        f_ref[0, pl.ds(base + r, r), :] = prev[r:] - spec[r:]

    _st_rows(x_ref, slice(None), hb_ref[0])
    _fft_long_forward(x_ref, y_ref, dot_first, dot_mid, write_sum, n_k1=FFT_K1, unroll_k1=FFT_K1_UNROLL)


SPEC_ROWS = 2 * FFT_R * FFT_K1


def _hy_spec_long(n, hf, hb):
    first, mid, _, _ = _dft_tables_long_half()
    lay = lambda rows: pl.BlockSpec((1, rows, W_C), lambda l, o: (l, 0, o))
    return pl.pallas_call(
        _hy_spec_long_kernel,
        grid=(DEPTH, HY_ORDER),
        in_specs=[lay(n), lay(n)] + [_const_spec(first.shape)] * 2 + [_const_spec(mid.shape)] * 2,
        out_specs=lay(SPEC_ROWS),
        out_shape=jax.ShapeDtypeStruct((DEPTH, SPEC_ROWS, HY_W), F32),
        scratch_shapes=[_lane_split_scratch(n, W_C), _lane_split_scratch(2 * FFT_R * FFT_K1_PAD, W_C),
                        pltpu.VMEM((SPEC_ROWS, W_C), F32)],
        compiler_params=_cparams(2),
        name="hyena_spectrum_long",
    )(hf, hb, *_split_table(first), *_split_table(mid))


def _hy_conv_long_kernel(sig_ref, gate_ref, ws_ref, bias_ref, f_ref, first_ref, mid_ref, midinv_ref, last_ref, o_ref,
                         x_ref, y_ref, v_ref, out_ref, *, n, order):
    r = FFT_R
    to_bf16 = lambda x: x.astype(BF16)
    gate = _short_conv(gate_ref[...], ws_ref[:, (order + 1) * W_C : (order + 2) * W_C], n)
    sig = _short_conv(sig_ref[...], ws_ref[:, :W_C], n) if order == 0 else sig_ref[...]
    _st_rows(x_ref, slice(None), sig)
    @pl.when(pl.program_id(0) == 0)
    def _():
        v_ref[...] = jnp.zeros(v_ref.shape, F32)

    def filter_and_invert(k1, base, spec):
        fr = f_ref[pl.ds(base, r), :]
        fi = f_ref[pl.ds(base + r, r), :]
        zr = spec[:r] * fr - spec[r:] * fi
        zi = spec[:r] * fi + spec[r:] * fr
        res = _dot(midinv_ref[...], jnp.concatenate([zr, zi], axis=0).astype(BF16))
        _st_transposed(v_ref, k1, res, FFT_K1_PAD)

    dot_first = lambda n2, slab: _dot(first_ref[n2], to_bf16(slab))
    dot_mid = lambda block: _dot(mid_ref[...], to_bf16(block))
    _fft_long_forward(x_ref, y_ref, dot_first, dot_mid, filter_and_invert, n_k1=FFT_K1, unroll_k1=FFT_K1_UNROLL)

    def stage_last(n2, carry):
        base = pl.multiple_of(n2 * 2 * FFT_K1_PAD, 2 * FFT_K1_PAD)
        res = _dot(last_ref[n2], _ld_rows(v_ref, pl.ds(base, 2 * FFT_K1_PAD)).astype(BF16))
        _st_rows(out_ref, pl.ds(n2, r // 2, stride=r), res)
        return carry

    lax.fori_loop(0, r, stage_last, 0, unroll=FFT_UNROLL)
    z = gate * (_ld_rows(out_ref, slice(None)) + sig * bias_ref[order : order + 1, :])
    o_ref[...] = z.astype(o_ref.dtype)


def _hy_conv_long(n_seq, n, row0, total_rows, pc, w_short, bias, spec, prev=None):
    assert 2 * n == FFT_R * FFT_R and FFT_K1 % FFT_K1_UNROLL == 0
    first, mid, mid_inv, last = _dft_tables_long_half()
    big = 2 * n
    bf = lambda m: jnp.asarray(m, BF16)
    lane_block = lambda j: pl.BlockSpec((n, W_C), lambda b: (row0 // n + b, j))
    z = None
    for order in range(HY_ORDER):
        final = order + 1 == HY_ORDER
        in_specs = [lane_block(0), lane_block(order + 1), _const_spec((SHORT_K, PC_W)),
                    _const_spec((HY_ORDER, W_C)), pl.BlockSpec((SPEC_ROWS, W_C), lambda b, order=order: (0, order)),
                    _const_spec(first.shape), _const_spec(mid.shape), _const_spec(mid_inv.shape),
                    _const_spec(last.shape)]
        args = [pc if order == 0 else z, pc, w_short, bias, spec, bf(first), bf(mid), bf(mid_inv), bf(last)]
        body = functools.partial(_hy_conv_long_kernel, n=n, order=order)
        body, in_specs, args, aliases = _keep_rows_of(prev if final else None, body, in_specs, args)
        z = pl.pallas_call(
            body,
            grid=(n_seq,),
            in_specs=in_specs,
            out_specs=lane_block(0),
            out_shape=jax.ShapeDtypeStruct((total_rows, W_C), BF16 if final else F32),
            input_output_aliases=aliases,
            scratch_shapes=[_lane_split_scratch(n, W_C), _lane_split_scratch(2 * FFT_R * FFT_K1_PAD, W_C),
                            _lane_split_scratch(2 * FFT_K1_PAD * FFT_R, W_C), _lane_split_scratch(n, W_C)],
            compiler_params=_cparams(1),
            name="hyena_conv_long",
        )(*args)
    return z


def _pad_in_weights(w_in):
    gap = jnp.zeros(w_in.shape[:2] + (PB_W - MLA_IN,), BF16)
    w = w_in.astype(BF16)
    return jnp.concatenate([w[..., : PA_W + MLA_IN], gap, w[..., PA_W + MLA_IN :]], axis=-1)


def kernel(x_prompt, x_sample, c, cache_mla, cache_diff_k, cache_diff_v, state_hgrn, c_ctx, w_mod, b_mod, norm_g,
           ffn_w_gu, ffn_w_down, w_in, w_out, hgrn_lb_logits, hgrn_onorm, mla_q_norm, mla_kv_norm, mla_w_uq,
           mla_w_ukv, mla_qk_norm, hy_short, hy_w1, hy_b1, hy_w2, hy_b2, hy_w3, hy_log_decay, hy_bias,
           diff_qk_norm, diff_lambda, diff_subln):
    bc, sc, _ = x_prompt.shape
    bl, sl, _ = x_sample.shape
    past = cache_mla.shape[2]
    tok = _Tokens(bc, sc, bl, sl)
    assert bl + 1 <= MOD_ROWS and tok.tc % sl == 0

    x = (x_prompt.reshape(tok.tc, D_MODEL), x_sample.reshape(tok.tl, D_MODEL))
    cond = jnp.concatenate([c_ctx[None], c, jnp.zeros((MOD_ROWS - 1 - bl, D_MODEL), F32)], axis=0)
    mod = _modulation(cond, w_mod, b_mod)

    filters = {n: _hy_filters(n, hy_w1, hy_b1, hy_w2, hy_b2, hy_w3, hy_log_decay) for n in (sc, sl)}
    spec_ctx = _hy_spec_short(sc, *filters[sc])
    spec_lat = _hy_spec_long(sl, *filters[sl])

    mla_tables = _rope_tables(tok, [NOPE_B], HEAD_PAD)
    diff_tables = _rope_tables(tok, list(range(0, H_D * 2 * DH_D, DH_D)), H_D * 2 * DH_D)

    w_gu_b, w_down_b, w_out_b, w_in_b = (ffn_w_gu.astype(BF16), ffn_w_down.astype(BF16), w_out.astype(BF16),
                                         _pad_in_weights(w_in))

    new_mla, new_dk, new_dv, new_state = [], [], [], []
    for l in range(DEPTH):
        x = _ffn(tok, x, mod, l, 0, norm_g[l, 0], w_gu_b, w_down_b)
        pa, pb, pc, pd = _inproj(tok, x, mod, l, norm_g[l, 1], w_in_b)
        mla_w = _mla_weights(mla_w_uq[l], mla_w_ukv[l], mla_qk_norm[l])
        q_b, k_b, v_b, cache_b = _prep_call(
            tok, _mla_prep_body, "mla_prep", pb,
            _mla_prep_operands(tok, mla_q_norm[l], mla_kv_norm[l], mla_w, mla_tables))
        q_d, k_d, v_d, kcache_d, vcache_d = _prep_call(
            tok, _diff_prep_body, "diff_prep", pd, _diff_prep_operands(tok, diff_qk_norm[l], diff_tables))

        o_a, s_ctx = _hgrn(l, bc, sc, 0, tok.t, pa, hgrn_lb_logits, hgrn_onorm[l], None)
        o_a, _ = _hgrn(l, bl, sl, tok.tc, tok.t, pa, hgrn_lb_logits, hgrn_onorm[l], state_hgrn[:, l], prev=o_a)

        cache_rows = jnp.pad(cache_mla[:, l].reshape(bl * past, KV_LORA + ROPE_B),
                             ((0, 0), (0, 2 * LANES - KV_LORA - ROPE_B)))
        kc_b, vc_b = _mla_cache_prep(cache_rows, mla_w)
        o_b = _mla_attention(bc, sc, 0, tok.t, q_b, k_b, v_b)
        o_b = _mla_attention(bl, sl, tok.tc, tok.t, q_b, k_b, v_b, kc_b, vc_b, prev=o_b)

        o_c = _hy_conv_short(bc, sc, 0, tok.t, pc, hy_short[l], hy_bias[l], spec_ctx[l])
        o_c = _hy_conv_long(bl, sl, tok.tc, tok.t, pc, hy_short[l], hy_bias[l], spec_lat[l], prev=o_c)

        kc_d, vc_d = _diff_cache_prep(cache_diff_k[:, l].reshape(bl * past, H_D * 2 * DH_D),
                                      cache_diff_v[:, l].reshape(bl * past, H_D * DV_D))
        o_d = _diff_attention(l, bc, sc, 0, tok.t, diff_lambda[l], diff_subln[l], q_d, k_d, v_d)
        o_d = _diff_attention(l, bl, sl, tok.tc, tok.t, diff_lambda[l], diff_subln[l], q_d, k_d, v_d, kc_d, vc_d,
                              prev=o_d)

        last_ffn = functools.partial(_ffn, tok, x, mod, l, 1, norm_g[l, 2], w_gu_b, w_down_b,
                                     mixer_outs=[o_a, o_b, o_c, o_d], w_out=w_out_b)
        if l + 1 < DEPTH:
            x = last_ffn()
        else:
            y_ctx = last_ffn(tiles=(0, tok.ctx_tiles))
            y_lat = last_ffn(tiles=(tok.ctx_tiles, tok.n_tiles - tok.ctx_tiles))

        new_mla.append(cache_b[: tok.tc, : KV_LORA + ROPE_B].reshape(bc, sc, KV_LORA + ROPE_B))
        new_dk.append(kcache_d[: tok.tc].reshape(bc, sc, H_D, 2, DH_D))
        new_dv.append(vcache_d[: tok.tc].reshape(bc, sc, H_D, DV_D))
        new_state.append(s_ctx)

    y_prompt = y_ctx.reshape(bc, sc, D_MODEL)
    y_sample = y_lat.reshape(bl, sl, D_MODEL)
    return (y_prompt, y_sample, jnp.stack(new_mla, axis=1), jnp.stack(new_dk, axis=1), jnp.stack(new_dv, axis=1),
            jnp.stack(new_state, axis=1))
```

```python
import functools
import math

import jax
import jax.numpy as jnp
import numpy as np
from jax import lax
from jax.experimental import pallas as pl
from jax.experimental.pallas import tpu as pltpu

F32 = jnp.float32
BF16 = jnp.bfloat16
HIGHEST = lax.Precision.HIGHEST

D_MODEL = 1024
DEPTH = 4
GRID_W = 64
N_MOD = 9
D_FF = 2816
EPS = 1e-6
ROPE_BASE = 10000.0
GROUP_W = 256
H_A, DK_A, DV_A = 4, 64, 64
MAX_INPUT_KEY = 1.0 - 1e-6
H_B, NOPE_B, ROPE_B, V_B = 4, 64, 32, 64
Q_LORA, KV_LORA = 256, 128
W_C, HY_ORDER, HY_BANDS, HY_FH, SHORT_K = 256, 2, 8, 64, 3
HY_EMB = 1 + 2 * HY_BANDS
H_D, DV_D, DH_D = 4, 64, 32

LANES = 128
MOD_ROWS = 16
TOKEN_TILE = 512
Q_TILE = 256
KEY_TILE = 512
ATTN_LOOKAHEAD = 3
HGRN_BLOCK = 128
SUBLANES = 8
VMEM_LIMIT = 56 * 1024 * 1024
ROPE_HALF = ROPE_B // 4

PA_W, PB_W, PC_W, PD_W = 1280, 512, 768, 768
IN_PAD_W = PA_W + PB_W + PC_W + PD_W
MLA_IN = Q_LORA + KV_LORA + ROPE_B
HEAD_PAD = 128


def _cparams(n_axes):
    return pltpu.CompilerParams(dimension_semantics=("arbitrary",) * n_axes, vmem_limit_bytes=VMEM_LIMIT)


def _nt_dot(a, b):
    return lax.dot_general(a, b, (((1,), (1,)), ((), ())), preferred_element_type=F32)


def _dot(a, b):
    return jnp.dot(a, b, preferred_element_type=F32)


def _dot_exact_rhs(a, b_bf16, passes=3):
    out = None
    rem = a
    for _ in range(passes):
        piece = rem.astype(BF16)
        term = _dot(piece, b_bf16)
        out = term if out is None else out + term
        rem = rem - piece.astype(F32)
    return out


def _dot_exact_lhs(a_bf16, b, passes=3):
    out = None
    rem = b
    for _ in range(passes):
        piece = rem.astype(BF16)
        term = _dot(a_bf16, piece)
        out = term if out is None else out + term
        rem = rem - piece.astype(F32)
    return out


def _silu(x):
    return x * jax.nn.sigmoid(x)


def _adaln(x, g, sc, sh):
    y = x * lax.rsqrt(jnp.mean(x * x, axis=-1, keepdims=True) + EPS)
    return (y * g) * (1.0 + sc) + sh


def _rope(x, cos, sin_hi, sin_lo):
    w = x.shape[-1]
    return x * cos + pltpu.roll(x, ROPE_HALF, 1) * sin_hi + pltpu.roll(x, w - ROPE_HALF, 1) * sin_lo


def _mod_kernel(c_ref, w_ref, b_ref, o_ref):
    a = _silu(c_ref[...])
    o_ref[0] = jnp.dot(a, w_ref[0], preferred_element_type=F32, precision=HIGHEST) + b_ref[0]


def _modulation(cond, w_mod, b_mod):
    tn = D_MODEL
    n_col = N_MOD * D_MODEL
    out = pl.pallas_call(
        _mod_kernel,
        grid=(DEPTH, n_col // tn),
        in_specs=[
            pl.BlockSpec((MOD_ROWS, D_MODEL), lambda l, j: (0, 0)),
            pl.BlockSpec((1, D_MODEL, tn), lambda l, j: (l, 0, j)),
            pl.BlockSpec((1, 1, tn), lambda l, j: (l, 0, j)),
        ],
        out_specs=pl.BlockSpec((1, MOD_ROWS, tn), lambda l, j: (l, 0, j)),
        out_shape=jax.ShapeDtypeStruct((DEPTH, MOD_ROWS, n_col), F32),
        compiler_params=_cparams(2),
        name="modulation",
    )(cond, w_mod, b_mod.reshape(DEPTH, 1, n_col))
    return out.reshape(DEPTH * MOD_ROWS, N_MOD, D_MODEL)


class _Tokens:
    def __init__(self, n_ctx_seq, ctx_len, n_lat_seq, lat_len):
        self.bc, self.sc, self.bl, self.sl = n_ctx_seq, ctx_len, n_lat_seq, lat_len
        self.tc, self.tl = n_ctx_seq * ctx_len, n_lat_seq * lat_len
        self.t = self.tc + self.tl
        self.tm = min(TOKEN_TILE, lat_len)
        assert self.tc % self.tm == 0 and lat_len % self.tm == 0
        self.ctx_tiles = self.tc // self.tm
        self.lat_tiles_per_seq = lat_len // self.tm
        self.n_tiles = self.t // self.tm

    def mod_row(self, layer):
        def f(i):
            lat = 1 + (i - self.ctx_tiles) // self.lat_tiles_per_seq
            return layer * MOD_ROWS + jnp.where(i < self.ctx_tiles, 0, lat)

        return f

    def rope_block(self, i):
        return jnp.where(i < self.ctx_tiles, 0, 1 + (i - self.ctx_tiles) % self.lat_tiles_per_seq)


def _mod_spec(tok, layer):
    row = tok.mod_row(layer)
    return pl.BlockSpec((1, N_MOD, D_MODEL), lambda i: (row(i), 0, 0))


def _const_spec(shape):
    zeros = (0,) * len(shape)
    return pl.BlockSpec(shape, lambda *_: zeros)


def _row_spec(tm, width):
    return pl.BlockSpec((tm, width), lambda i: (i, 0))


def _keep_rows_of(prev, kernel_fn, in_specs, args):
    if prev is None:
        return kernel_fn, in_specs, args, {}
    idx = len(in_specs)

    def body(*refs):
        return kernel_fn(*refs[:idx], *refs[idx + 1 :])

    return body, in_specs + [pl.BlockSpec(memory_space=pl.ANY)], args + [prev], {idx: 0}


def _ffn_kernel(*refs, mod_base, split_in, fused, ctx_tiles):
    n_x = 2 if split_in else 1
    x_refs, (mod_ref, g_ref, wgu_ref, wd_ref), rest = refs[:n_x], refs[n_x : n_x + 4], refs[n_x + 4 :]
    is_ctx = pl.program_id(0) < ctx_tiles
    x = jnp.where(is_ctx, x_refs[0][...], x_refs[1][...]) if split_in else x_refs[0][...]
    if fused:
        oa_ref, ob_ref, oc_ref, od_ref, wo_ref = rest[:5]
        mixed = jnp.concatenate([oa_ref[...], ob_ref[...], oc_ref[...], od_ref[...]], axis=-1)
        x = x + mod_ref[0, 5:6, :] * _dot(mixed, wo_ref[...])
    sh = mod_ref[0, mod_base : mod_base + 1, :]
    sc = mod_ref[0, mod_base + 1 : mod_base + 2, :]
    gate = mod_ref[0, mod_base + 2 : mod_base + 3, :]
    h = _adaln(x, g_ref[...], sc, sh).astype(BF16)
    au = _dot(h, wgu_ref[...])
    act = (_silu(au[:, :D_FF]) * au[:, D_FF:]).astype(BF16)
    rest[-1][...] = x + (0.5 * gate) * _dot(act, wd_ref[...])


def _resident(lead_index, shape):
    zeros = (0,) * len(shape)
    return pl.BlockSpec((None,) * len(lead_index) + tuple(shape), lambda i: tuple(lead_index) + zeros,
                        pipeline_mode=pl.Buffered(1))


def _ffn(tok, x, mod, layer, which, norm_g, w_gu, w_down, mixer_outs=None, w_out=None, tiles=None):
    tm = tok.tm
    split_in = isinstance(x, tuple)
    first, count = (0, tok.n_tiles) if tiles is None else tiles
    assert not (split_in and first)
    rows = lambda w: pl.BlockSpec((tm, w), lambda i: (i + first, 0))
    mod_row = tok.mod_row(layer)
    ctx_spec = pl.BlockSpec((tm, D_MODEL), lambda i: (jnp.minimum(i, tok.ctx_tiles - 1), 0))
    lat_spec = pl.BlockSpec((tm, D_MODEL), lambda i: (jnp.maximum(i - tok.ctx_tiles, 0), 0))
    in_specs = ([ctx_spec, lat_spec] if split_in else [rows(D_MODEL)]) + [
        pl.BlockSpec((1, N_MOD, D_MODEL), lambda i: (mod_row(i + first), 0, 0)),
        _const_spec((1, D_MODEL)),
        _resident((layer, which), (D_MODEL, 2 * D_FF)),
        _resident((layer, which), (D_FF, D_MODEL)),
    ]
    args = (list(x) if split_in else [x]) + [mod, norm_g.reshape(1, D_MODEL), w_gu, w_down]
    if mixer_outs is not None:
        in_specs += [rows(GROUP_W)] * 4
        in_specs.append(_resident((layer,), (D_MODEL, D_MODEL)))
        args += list(mixer_outs) + [w_out]
    return pl.pallas_call(
        functools.partial(_ffn_kernel, mod_base=6 * which, split_in=split_in, fused=mixer_outs is not None,
                          ctx_tiles=tok.ctx_tiles),
        grid=(count,),
        in_specs=in_specs,
        out_specs=_row_spec(tm, D_MODEL),
        out_shape=jax.ShapeDtypeStruct((count * tm, D_MODEL), F32),
        compiler_params=_cparams(1),
        name="ffn",
    )(*args)


def _inproj_kernel(x_ref, mod_ref, g_ref, w_ref, pa_ref, pb_ref, pc_ref, pd_ref):
    h = _adaln(x_ref[...], g_ref[...], mod_ref[0, 4:5, :], mod_ref[0, 3:4, :]).astype(BF16)
    p = _dot(h, w_ref[...])
    pa_ref[...] = p[:, :PA_W]
    pb_ref[...] = p[:, PA_W : PA_W + PB_W]
    pc_ref[...] = p[:, PA_W + PB_W : PA_W + PB_W + PC_W]
    pd_ref[...] = p[:, PA_W + PB_W + PC_W :]


def _inproj(tok, x, mod, layer, norm_g, w_in_pad):
    tm = tok.tm
    widths = (PA_W, PB_W, PC_W, PD_W)
    return pl.pallas_call(
        _inproj_kernel,
        grid=(tok.n_tiles,),
        in_specs=[
            _row_spec(tm, D_MODEL),
            _mod_spec(tok, layer),
            _const_spec((1, D_MODEL)),
            _resident((layer,), (D_MODEL, IN_PAD_W)),
        ],
        out_specs=[_row_spec(tm, w) for w in widths],
        out_shape=[jax.ShapeDtypeStruct((tok.t, w), F32) for w in widths],
        compiler_params=_cparams(1),
        name="inproj",
    )(x, mod, norm_g.reshape(1, D_MODEL), w_in_pad)


def _rope_group_tables(n_tok):
    t = np.arange(n_tok)
    pos = np.stack([t // GRID_W, t % GRID_W], axis=1).astype(np.float32)
    inv = (ROPE_BASE ** (-np.arange(ROPE_HALF, dtype=np.float32) / ROPE_HALF)).astype(np.float32)
    lane = np.arange(4 * ROPE_HALF)
    ang = (pos[:, lane // (2 * ROPE_HALF)] * inv[lane % ROPE_HALF][None, :]).astype(np.float32)
    second = (lane % (2 * ROPE_HALF)) >= ROPE_HALF
    cos, sin = np.cos(ang), np.sin(ang)
    return cos, np.where(second[None], sin, 0.0), np.where(second[None], 0.0, -sin)


def _rope_tables(tok, lane_groups, width):
    cos = np.ones((tok.tm + tok.sl, width), np.float32)
    s_hi = np.zeros_like(cos)
    s_lo = np.zeros_like(cos)
    c, a, b = _rope_group_tables(tok.sl)
    for g in lane_groups:
        cos[tok.tm :, g : g + ROPE_B] = c
        s_hi[tok.tm :, g : g + ROPE_B] = a
        s_lo[tok.tm :, g : g + ROPE_B] = b
    return jnp.asarray(cos), jnp.asarray(s_hi), jnp.asarray(s_lo)


def _mla_keys_values(kv_in, wk_ref, wv_ref, kn_ref, rope):
    kraw = _dot(kv_in.astype(BF16), wk_ref[...])
    v = _dot(kv_in[:, :KV_LORA].astype(BF16), wv_ref[...])
    ks = []
    for h in range(H_B):
        kh = kraw[:, h * HEAD_PAD : (h + 1) * HEAD_PAD]
        ss = jnp.sum(kh * kh, axis=-1, keepdims=True) * (1.0 / (NOPE_B + ROPE_B))
        kh = kh * lax.rsqrt(ss + EPS) * kn_ref[...]
        if rope is not None:
            kh = _rope_by_matmul(kh, *rope)
        ks.append(kh.astype(BF16))
    return ks, v.astype(BF16)


def _rope_by_matmul(x, cos, sin, perm):
    return x * cos + _dot(x.astype(BF16), perm) * sin


def _mla_prep_body(pb, gq_ref, gkv_ref, wuq_ref, wk_ref, wv_ref, qn_ref, kn_ref, cos_ref, shi_ref, slo_ref, perm_ref,
                   q_ref, k_ref, v_ref, cache_ref):
    rope = (cos_ref[...], shi_ref[...] + slo_ref[...], perm_ref[...])
    cq = pb[:, :Q_LORA]
    cq = cq * lax.rsqrt(jnp.mean(cq * cq, axis=-1, keepdims=True) + EPS) * gq_ref[...]
    qraw = _dot(cq.astype(BF16), wuq_ref[...])
    scale = (NOPE_B + ROPE_B) ** -0.5 * LOG2E
    for h in range(H_B):
        qh = qraw[:, h * HEAD_PAD : (h + 1) * HEAD_PAD]
        ss = jnp.sum(qh * qh, axis=-1, keepdims=True) * (1.0 / (NOPE_B + ROPE_B))
        qh = _rope_by_matmul(qh * lax.rsqrt(ss + EPS) * qn_ref[...], *rope)
        q_ref[:, h * HEAD_PAD : (h + 1) * HEAD_PAD] = (qh * scale).astype(BF16)
    ckv = pb[:, Q_LORA : Q_LORA + KV_LORA]
    ckv = ckv * lax.rsqrt(jnp.mean(ckv * ckv, axis=-1, keepdims=True) + EPS) * gkv_ref[...]
    kv_in = jnp.concatenate([ckv, pb[:, Q_LORA + KV_LORA :]], axis=-1)
    cache_ref[...] = kv_in
    ks, v = _mla_keys_values(kv_in, wk_ref, wv_ref, kn_ref, rope)
    for h in range(H_B):
        k_ref[:, h * HEAD_PAD : (h + 1) * HEAD_PAD] = ks[h]
    v_ref[...] = v


def _mla_cache_kernel(c_ref, wk_ref, wv_ref, kn_ref, k_ref, v_ref):
    ks, v = _mla_keys_values(c_ref[...], wk_ref, wv_ref, kn_ref, None)
    for h in range(H_B):
        k_ref[:, h * HEAD_PAD : (h + 1) * HEAD_PAD] = ks[h]
    v_ref[...] = v


def _mla_weights(w_uq, w_ukv, qk_norm):
    wuq = jnp.pad(w_uq.reshape(Q_LORA, H_B, NOPE_B + ROPE_B), ((0, 0), (0, 0), (0, HEAD_PAD - NOPE_B - ROPE_B)))
    wuq = wuq.reshape(Q_LORA, H_B * HEAD_PAD).astype(BF16)
    ukv = w_ukv.reshape(KV_LORA, H_B, NOPE_B + V_B)
    wk_nope = jnp.pad(ukv[:, :, :NOPE_B], ((0, 0), (0, 0), (0, HEAD_PAD - NOPE_B)))
    place = np.zeros((2 * LANES - KV_LORA, H_B, HEAD_PAD), np.float32)
    for h in range(H_B):
        place[np.arange(ROPE_B), h, NOPE_B + np.arange(ROPE_B)] = 1.0
    wk = jnp.concatenate([wk_nope, jnp.asarray(place)], axis=0).reshape(2 * LANES, H_B * HEAD_PAD).astype(BF16)
    wv = ukv[:, :, NOPE_B:].reshape(KV_LORA, H_B * V_B).astype(BF16)
    pad = lambda g: jnp.pad(g, (0, HEAD_PAD - NOPE_B - ROPE_B)).reshape(1, HEAD_PAD)
    return wuq, wk, wv, pad(qk_norm[0]), pad(qk_norm[1])


def _mla_prep_operands(tok, gq, gkv, weights, tables):
    wuq, wk, wv, qn, kn = weights
    tm = tok.tm
    rope_spec = pl.BlockSpec((tm, HEAD_PAD), lambda i: (tok.rope_block(i), 0))
    perm = np.zeros((HEAD_PAD, HEAD_PAD), np.float32)
    j = NOPE_B + np.arange(ROPE_B)
    perm[np.where((j - NOPE_B) % (2 * ROPE_HALF) < ROPE_HALF, j + ROPE_HALF, j - ROPE_HALF), j] = 1.0
    args = [gq.reshape(1, -1), gkv.reshape(1, -1), wuq, wk, wv, qn, kn, *tables, jnp.asarray(perm, BF16)]
    in_specs = [_const_spec((1, Q_LORA)), _const_spec((1, KV_LORA)), _const_spec(wuq.shape), _const_spec(wk.shape),
                _const_spec(wv.shape), _const_spec((1, HEAD_PAD)), _const_spec((1, HEAD_PAD))] + [rope_spec] * 3
    in_specs.append(_const_spec((HEAD_PAD, HEAD_PAD)))
    widths = [(H_B * HEAD_PAD, BF16), (H_B * HEAD_PAD, BF16), (H_B * V_B, BF16), (2 * LANES, F32)]
    return args, in_specs, widths


def _prep_call(tok, body, name, proj, operands):
    args, in_specs, outs = operands
    tm = tok.tm

    def kernel_fn(proj_ref, *refs):
        body(proj_ref[...], *refs)

    return pl.pallas_call(
        kernel_fn,
        grid=(tok.n_tiles,),
        in_specs=[_row_spec(tm, proj.shape[1])] + in_specs,
        out_specs=[_row_spec(tm, w) for w, _ in outs],
        out_shape=[jax.ShapeDtypeStruct((tok.t, w), dt) for w, dt in outs],
        compiler_params=_cparams(1),
        name=name,
    )(proj, *args)


def _mla_cache_prep(cache_rows, weights):
    _, wk, wv, _, kn = weights
    rows = cache_rows.shape[0]
    tm = min(TOKEN_TILE, rows)
    return pl.pallas_call(
        _mla_cache_kernel,
        grid=(rows // tm,),
        in_specs=[_row_spec(tm, 2 * LANES), _const_spec(wk.shape), _const_spec(wv.shape), _const_spec((1, HEAD_PAD))],
        out_specs=[_row_spec(tm, H_B * HEAD_PAD), _row_spec(tm, H_B * V_B)],
        out_shape=[jax.ShapeDtypeStruct((rows, H_B * HEAD_PAD), BF16), jax.ShapeDtypeStruct((rows, H_B * V_B), BF16)],
        compiler_params=_cparams(1),
        name="mla_cache_prep",
    )(cache_rows, wk, wv, kn)


LOG2E = math.log2(math.e)


def _softmax_parts(scores):
    m = functools.reduce(jnp.maximum, [jnp.max(s, axis=-1, keepdims=True) for s in scores])
    ps = [jnp.exp2(s - m) for s in scores]
    denom = functools.reduce(jnp.add, [jnp.sum(p, axis=-1, keepdims=True) for p in ps])
    return ps, denom


def _head_lane_mask(width, head, head_w):
    lane = lax.broadcasted_iota(jnp.int32, (1, width), 1)
    return (lane >= head * head_w) & (lane < (head + 1) * head_w)


def _mla_attn_kernel(*refs, n_kv):
    q_ref = refs[0]
    k_refs = refs[1 : 1 + n_kv]
    v_refs = refs[1 + n_kv : 1 + 2 * n_kv]
    o_ref = refs[1 + 2 * n_kv]
    def head_scores(h):
        sl = slice(h * HEAD_PAD, (h + 1) * HEAD_PAD)
        return [_nt_dot(q_ref[:, sl], k[:, sl]) for k in k_refs]

    pending = [head_scores(0)]
    out = None
    for h in range(H_B):
        scores = pending.pop(0)
        if h + 1 < H_B:
            pending.append(head_scores(h + 1))
        ps, denom = _softmax_parts(scores)
        mask = _head_lane_mask(H_B * V_B, h, V_B)
        acc = None
        for p, v in zip(ps, v_refs):
            term = _dot(p.astype(BF16), jnp.where(mask, v[...], jnp.zeros((), BF16)))
            acc = term if acc is None else acc + term
        acc = acc * (1.0 / denom)
        out = acc if out is None else out + acc
    o_ref[...] = out.astype(BF16)


def _seq_specs(n_seq, seq_len, row0, tq, q_width, kv_widths, kv_cache):
    q_blocks = seq_len // tq
    q_spec = lambda w: pl.BlockSpec((tq, w), lambda b, i: (row0 // tq + b * q_blocks + i, 0))
    new_spec = lambda w: pl.BlockSpec((seq_len, w), lambda b, i: (row0 // seq_len + b, 0))
    specs = [q_spec(q_width)]
    for w in kv_widths:
        if kv_cache is not None:
            specs.append(pl.BlockSpec((kv_cache, w), lambda b, i: (b, 0)))
        specs.append(new_spec(w))
    return specs, q_spec


def _mla_attention(n_seq, seq_len, row0, total_rows, q, k, v, k_cache=None, v_cache=None, prev=None):
    tq = min(Q_TILE, seq_len)
    has_cache = k_cache is not None
    cache_len = k_cache.shape[0] // n_seq if has_cache else None
    specs, q_spec = _seq_specs(n_seq, seq_len, row0, tq, H_B * HEAD_PAD, (H_B * HEAD_PAD, H_B * V_B), cache_len)
    args = [q] + ([k_cache, k] if has_cache else [k]) + ([v_cache, v] if has_cache else [v])
    body = functools.partial(_mla_attn_kernel, n_kv=2 if has_cache else 1)
    body, specs, args, aliases = _keep_rows_of(prev, body, specs, args)
    return pl.pallas_call(
        body,
        grid=(n_seq, seq_len // tq),
        in_specs=specs,
        out_specs=q_spec(H_B * V_B),
        out_shape=jax.ShapeDtypeStruct((total_rows, H_B * V_B), BF16),
        input_output_aliases=aliases,
        compiler_params=_cparams(2),
        name="mla_attention",
    )(*args)


DQK_W = H_D * 2 * DH_D


def _group_mean_matrix(width, group):
    g = np.arange(width)
    return jnp.asarray((g[:, None] // group == g[None, :] // group).astype(np.float32), BF16)


def _diff_prep_body(pd, gq_ref, gk_ref, gm_ref, cos_ref, shi_ref, slo_ref,
                    q_ref, k_ref, v_ref, kcache_ref, vcache_ref):
    rope = (cos_ref[...], shi_ref[...], slo_ref[...])
    w = H_D * 2 * DH_D

    def norm(x, g):
        ms = _dot_exact_rhs(x * x, gm_ref[...], passes=2) * (1.0 / DH_D)
        return x * lax.rsqrt(ms + EPS) * g

    qn = _rope(norm(pd[:, :w], gq_ref[...]), *rope) * (DH_D ** -0.5 * LOG2E)
    kn = norm(pd[:, w : 2 * w], gk_ref[...])
    kcache_ref[...] = kn
    q_ref[...] = qn.astype(BF16)
    k_ref[...] = _rope(kn, *rope).astype(BF16)
    vcache_ref[...] = pd[:, 2 * w :]
    v_ref[...] = pd[:, 2 * w :].astype(BF16)


def _diff_cache_kernel(k_in_ref, v_in_ref, k_ref, v_ref):
    k_ref[...] = k_in_ref[...].astype(BF16)
    v_ref[...] = v_in_ref[...].astype(BF16)


def _diff_prep_operands(tok, qk_norm, tables):
    tm = tok.tm
    w = H_D * 2 * DH_D
    rope_spec = pl.BlockSpec((tm, w), lambda i: (tok.rope_block(i), 0))
    tile_g = lambda g: jnp.tile(g, H_D * 2).reshape(1, w)
    args = [tile_g(qk_norm[0]), tile_g(qk_norm[1]), _group_mean_matrix(w, DH_D), *tables]
    in_specs = [_const_spec((1, w)), _const_spec((1, w)), _const_spec((w, w))] + [rope_spec] * 3
    widths = [(DQK_W, BF16), (DQK_W, BF16), (H_D * DV_D, BF16), (w, F32), (H_D * DV_D, F32)]
    return args, in_specs, widths


def _diff_cache_prep(k_rows, v_rows):
    rows = k_rows.shape[0]
    tm = min(TOKEN_TILE, rows)
    w = H_D * 2 * DH_D
    return pl.pallas_call(
        _diff_cache_kernel,
        grid=(rows // tm,),
        in_specs=[_row_spec(tm, w), _row_spec(tm, H_D * DV_D)],
        out_specs=[_row_spec(tm, DQK_W), _row_spec(tm, H_D * DV_D)],
        out_shape=[jax.ShapeDtypeStruct((rows, DQK_W), BF16), jax.ShapeDtypeStruct((rows, H_D * DV_D), BF16)],
        compiler_params=_cparams(1),
        name="diff_cache_prep",
    )(k_rows, v_rows)


def _diff_attn_kernel(*refs, n_kv, lam_init):
    q_ref, lam_ref, sub_ref = refs[0], refs[1], refs[2]
    k_refs = refs[3 : 3 + n_kv]
    v_refs = refs[3 + n_kv : 3 + 2 * n_kv]
    o_ref = refs[3 + 2 * n_kv]
    dl = lam_ref[...]
    lam = (jnp.exp(jnp.sum(dl[0:1] * dl[1:2], axis=-1, keepdims=True))
           - jnp.exp(jnp.sum(dl[2:3] * dl[3:4], axis=-1, keepdims=True)) + lam_init)
    groups_per_tile = LANES // DH_D
    lane = lax.broadcasted_iota(jnp.int32, (1, LANES), 1)

    def map_scores(g):
        sl = slice((g // groups_per_tile) * LANES, (g // groups_per_tile + 1) * LANES)
        first = (g % groups_per_tile) * DH_D
        qg = jnp.where((lane >= first) & (lane < first + DH_D), q_ref[:, sl], jnp.zeros((), BF16))
        return [_nt_dot(qg, k[:, sl]) for k in k_refs]

    n_maps = 2 * H_D
    pending = [map_scores(g) for g in range(min(ATTN_LOOKAHEAD, n_maps))]
    out = None
    for h in range(H_D):
        probs = []
        for m in range(2):
            scores = pending.pop(0)
            nxt = 2 * h + m + ATTN_LOOKAHEAD
            if nxt < n_maps:
                pending.append(map_scores(nxt))
            probs.append(_softmax_parts(scores))
        mask = _head_lane_mask(H_D * DV_D, h, DV_D)
        (ps0, den0), (ps1, den1) = probs
        ratio = lam * den0 / den1
        acc = None
        for p0, p1, v in zip(ps0, ps1, v_refs):
            term = _dot((p0 - p1 * ratio).astype(BF16), jnp.where(mask, v[...], jnp.zeros((), BF16)))
            acc = term if acc is None else acc + term
        acc = acc * (1.0 / den0)
        ms = jnp.sum(acc * acc, axis=-1, keepdims=True) * (1.0 / DV_D)
        acc = acc * lax.rsqrt(ms + EPS)
        out = acc if out is None else out + acc
    o_ref[...] = (out * sub_ref[...] * (1.0 - lam_init)).astype(BF16)


def _diff_attention(layer, n_seq, seq_len, row0, total_rows, lam_p, sub_g, q, k, v, k_cache=None, v_cache=None,
                    prev=None):
    tq = min(Q_TILE, seq_len)
    has_cache = k_cache is not None
    cache_len = k_cache.shape[0] // n_seq if has_cache else None
    specs, q_spec = _seq_specs(n_seq, seq_len, row0, tq, DQK_W, (DQK_W, H_D * DV_D), cache_len)
    specs = [specs[0], pl.BlockSpec((4, DH_D), lambda b, i: (0, 0)), pl.BlockSpec((1, H_D * DV_D), lambda b, i: (0, 0))
             ] + specs[1:]
    args = [q, lam_p, jnp.tile(sub_g, H_D).reshape(1, H_D * DV_D)]
    args += ([k_cache, k] if has_cache else [k]) + ([v_cache, v] if has_cache else [v])
    lam_init = 0.8 - 0.6 * math.exp(-0.3 * layer)
    body = functools.partial(_diff_attn_kernel, n_kv=2 if has_cache else 1, lam_init=lam_init)
    body, specs, args, aliases = _keep_rows_of(prev, body, specs, args)
    return pl.pallas_call(
        body,
        grid=(n_seq, seq_len // tq),
        in_specs=specs,
        out_specs=q_spec(H_D * DV_D),
        out_shape=jax.ShapeDtypeStruct((total_rows, H_D * DV_D), BF16),
        input_output_aliases=aliases,
        compiler_params=_cparams(2),
        name="diff_attention",
    )(*args)


A_W = H_A * DK_A


HGRN_LEVELS = (2, 4, 8, 16, 32, 64, 128)
PAIR_W = 2 * DK_A


def _hgrn_constants():
    t = np.arange(HGRN_BLOCK)
    lower = t[None, :] <= t[:, None]
    upper = t[None, :] >= t[:, None]
    masks = [t[:, None] == t[None, :]] + [(t[:, None] // b) == (t[None, :] // b) for b in HGRN_LEVELS[:-1]]
    masks = np.stack([np.tile(m, (1, 2)) for m in masks]).astype(np.float32)
    g = np.arange(A_W)
    heads = (g[:, None] // DK_A) == (g[None, :] // DK_A)
    as_bf16 = lambda m: jnp.asarray(m.astype(np.float32), BF16)
    pair_mask = heads[:PAIR_W, :PAIR_W].astype(np.float32)
    return as_bf16(lower), as_bf16(upper), as_bf16(heads), jnp.asarray(pair_mask), jnp.asarray(masks)


def _level_reference(cum, b, forward):
    off = b // 2 - 1 if forward else b // 2
    if b >= SUBLANES:
        c3 = cum.reshape(HGRN_BLOCK // b, b, A_W)
        return jnp.broadcast_to(c3[:, off : off + 1, :], c3.shape).reshape(HGRN_BLOCK, A_W)
    c3 = cum.reshape(HGRN_BLOCK // SUBLANES, SUBLANES, A_W)
    sub = lax.broadcasted_iota(jnp.int32, (1, SUBLANES, 1), 1)
    out = None
    for g in range(SUBLANES // b):
        cand = jnp.broadcast_to(c3[:, g * b + off : g * b + off + 1, :], c3.shape)
        out = cand if out is None else jnp.where(sub >= g * b, cand, out)
    return out.reshape(HGRN_BLOCK, A_W)


def _hgrn_kernel(*refs, layer, seq_len, has_state):
    if has_state:
        pa_ref, lbl_ref, og_ref, s0_ref, lower_ref, upper_ref, heads_ref, hmask_ref, lvl_ref = refs[:9]
        rest = refs[9:]
    else:
        pa_ref, lbl_ref, og_ref, lower_ref, upper_ref, heads_ref, hmask_ref, lvl_ref = refs[:8]
        s0_ref = None
        rest = refs[8:]
    o_ref, sout_ref, st_ref, oacc_ref = rest if not has_state else (rest[0], None, rest[1], rest[2])
    n_blocks = seq_len // HGRN_BLOCK
    row = lax.broadcasted_iota(jnp.int32, (HGRN_BLOCK, 1), 0)
    first_head = lax.broadcasted_iota(jnp.int32, (1, PAIR_W), 1) < DK_A
    zero_bf16 = jnp.zeros((), BF16)

    def per_head_rows(x):
        return jnp.concatenate([jnp.where(first_head, x, zero_bf16), jnp.where(first_head, zero_bf16, x)], axis=0)

    st_ref[...] = jnp.zeros(st_ref.shape, F32)
    lower_bounds = []
    for d in range(2):
        logits = lbl_ref[d]
        e = jnp.exp(logits - jnp.max(logits, axis=0, keepdims=True))
        p = e / jnp.sum(e, axis=0, keepdims=True)
        lb = jnp.zeros((1, A_W), F32)
        for j in range(1, layer + 1):
            lb = lb + p[j : j + 1, :]
        lower_bounds.append(lb)
        if has_state:
            for h in range(H_A):
                off = (h % 2) * DK_A
                st_ref[d, h // 2, off : off + DV_A, off : off + DK_A] = s0_ref[0, d, h].T

    def both_directions(i, carry):
        for d in range(2):
            lb = lower_bounds[d]
            tri_ref = lower_ref if d == 0 else upper_ref
            blk = i if d == 0 else n_blocks - 1 - i
            r0 = pl.multiple_of(blk * HGRN_BLOCK, HGRN_BLOCK)
            rows = pl.ds(r0, HGRN_BLOCK)
            q = _silu(pa_ref[rows, 0:A_W]) * DK_A ** -0.5
            v = pa_ref[rows, A_W : 2 * A_W]
            logit = pa_ref[rows, (2 + d) * A_W : (3 + d) * A_W]
            key = jnp.minimum((1.0 - lb) * jax.nn.sigmoid(-logit), MAX_INPUT_KEY)
            log_f = jnp.log1p(-key)
            cum = _dot_exact_lhs(tri_ref[...], log_f)
            tot = cum[HGRN_BLOCK - 1 : HGRN_BLOCK, :] if d == 0 else cum[0:1, :]
            v_b = v.astype(BF16)
            scores = [None] * (H_A // 2)

            def add_pairs(qd, kd, mask_index, scores=scores):
                qd_b, kd_b = qd.astype(BF16), kd.astype(BF16)
                for pi in range(H_A // 2):
                    lanes = slice(pi * PAIR_W, (pi + 1) * PAIR_W)
                    s = _nt_dot(qd_b[:, lanes], per_head_rows(kd_b[:, lanes]))
                    if mask_index is not None:
                        s = s * lvl_ref[mask_index]
                    scores[pi] = s if scores[pi] is None else scores[pi] + s

            add_pairs(q, key, 0)
            for li, b in enumerate(HGRN_LEVELS):
                later = ((row % b) >= b // 2) if d == 0 else ((row % b) < b // 2)
                if b == 2:
                    qd = jnp.where(later, q * jnp.exp(log_f), 0.0)
                    kd = jnp.where(later, 0.0, key)
                else:
                    ref = _level_reference(cum, b, d == 0)
                    qd = jnp.where(later, q * jnp.exp(cum - ref), 0.0)
                    kd = jnp.where(later, 0.0, key * jnp.exp(ref - cum))
                add_pairs(qd, kd, li + 1 if b < HGRN_BLOCK else None)
            q_dec = (q * jnp.exp(cum)).astype(BF16)
            k_dec = (key * jnp.exp(tot - cum)).astype(BF16)
            decay = jnp.exp(tot)
            parts = []
            for pi in range(H_A // 2):
                lanes = slice(pi * PAIR_W, (pi + 1) * PAIR_W)
                state = st_ref[d, pi]
                o_pair = _dot(scores[pi].astype(BF16), per_head_rows(v_b[:, lanes]))
                parts.append(o_pair + _nt_dot(q_dec[:, lanes], state.astype(BF16)))
                upd = lax.dot_general(v_b[:, lanes], k_dec[:, lanes], (((0,), (0,)), ((), ())),
                                      preferred_element_type=F32)
                st_ref[d, pi] = state * decay[:, lanes] + upd * hmask_ref[...]
            oacc_ref[d, rows, :] = jnp.concatenate(parts, axis=1)
        return carry

    lax.fori_loop(0, n_blocks, both_directions, 0, unroll=2)
    if sout_ref is not None:
        for d in range(2):
            for h in range(H_A):
                off = (h % 2) * DK_A
                sout_ref[0, d, h] = st_ref[d, h // 2, off : off + DV_A, off : off + DK_A].T

    o = oacc_ref[0] + oacc_ref[1]
    ms = _dot_exact_rhs(o * o, heads_ref[...], passes=2) * (1.0 / DV_A)
    o_ref[...] = (o * lax.rsqrt(ms + EPS) * og_ref[...] * _silu(pa_ref[:, 4 * A_W : 5 * A_W])).astype(BF16)


def _hgrn(layer, n_seq, seq_len, row0, total_rows, pa, lb_logits, onorm_g, s0, prev=None):
    has_state = s0 is not None
    consts = _hgrn_constants()
    seq_spec = lambda w: pl.BlockSpec((seq_len, w), lambda b: (row0 // seq_len + b, 0))
    state_spec = pl.BlockSpec((1, 2, H_A, DK_A, DV_A), lambda b: (b, 0, 0, 0, 0))
    in_specs = [seq_spec(PA_W), _const_spec(lb_logits.shape), _const_spec((1, A_W))]
    args = [pa, lb_logits, jnp.tile(onorm_g, H_A).reshape(1, A_W)]
    if has_state:
        in_specs.append(state_spec)
        args.append(s0)
    in_specs += [_const_spec(c.shape) for c in consts]
    args += list(consts)
    body = functools.partial(_hgrn_kernel, layer=layer, seq_len=seq_len, has_state=has_state)
    body, in_specs, args, aliases = _keep_rows_of(prev, body, in_specs, args)
    out_specs = [seq_spec(A_W)]
    out_shape = [jax.ShapeDtypeStruct((total_rows, A_W), BF16)]
    if not has_state:
        out_specs.append(state_spec)
        out_shape.append(jax.ShapeDtypeStruct((n_seq, 2, H_A, DK_A, DV_A), F32))
    outs = pl.pallas_call(
        body,
        grid=(n_seq,),
        in_specs=in_specs,
        out_specs=out_specs,
        out_shape=out_shape,
        input_output_aliases=aliases,
        scratch_shapes=[pltpu.VMEM((2, H_A // 2, 2 * DV_A, 2 * DK_A), F32), pltpu.VMEM((2, seq_len, A_W), F32)],
        compiler_params=_cparams(1),
        name="hgrn",
    )(*args)
    return (outs[0], None) if has_state else tuple(outs)


HY_W = HY_ORDER * W_C
FFT_R = 64


def _hy_filter_kernel(w1_ref, b1_ref, w2_ref, b2_ref, w3_ref, ld_ref, hf_ref, hb_ref, *, n):
    f32dot = functools.partial(jnp.dot, preferred_element_type=F32, precision=HIGHEST)
    row = lax.broadcasted_iota(jnp.int32, (n, LANES), 0)
    lane = lax.broadcasted_iota(jnp.int32, (n, LANES), 1)
    tn = row.astype(F32) / n
    band = jnp.where(lane <= HY_BANDS, lane, lane - HY_BANDS).astype(F32)
    ang = (2.0 * math.pi) * tn * band
    feats = jnp.where(lane == 0, tn, jnp.where(lane <= HY_BANDS, jnp.cos(ang),
                                                jnp.where(lane <= 2 * HY_BANDS, jnp.sin(ang), 0.0)))
    h = jnp.sin(f32dot(feats, w1_ref[0]) + b1_ref[0])
    h = jnp.sin(f32dot(h, w2_ref[0]) + b2_ref[0])
    h = f32dot(h, w3_ref[0])
    h = h * jnp.exp(-jnp.exp(ld_ref[0]) * tn[:, 0:1])
    hf = h[:, :HY_W]
    hb = jnp.where(row[:, 0:1] == 0, 0.0, h[:, HY_W:])
    norm = jnp.sum(jnp.abs(hf), axis=0, keepdims=True) + jnp.sum(jnp.abs(hb), axis=0, keepdims=True) + EPS
    hf_ref[0] = hf / norm
    hb_ref[0] = hb / norm


def _hy_filters(n, w1, b1, w2, b2, w3, log_decay):
    w1p = jnp.pad(w1, ((0, 0), (0, LANES - HY_EMB), (0, 0)))
    lay = lambda shape: pl.BlockSpec((1,) + shape, lambda l: (l,) + (0,) * len(shape))
    return pl.pallas_call(
        functools.partial(_hy_filter_kernel, n=n),
        grid=(DEPTH,),
        in_specs=[lay((LANES, HY_FH)), lay((1, HY_FH)), lay((HY_FH, HY_FH)), lay((1, HY_FH)),
                  lay((HY_FH, 2 * HY_W)), lay((1, 2 * HY_W))],
        out_specs=[lay((n, HY_W)), lay((n, HY_W))],
        out_shape=[jax.ShapeDtypeStruct((DEPTH, n, HY_W), F32)] * 2,
        compiler_params=_cparams(1),
        name="hyena_filters",
    )(w1p, b1.reshape(DEPTH, 1, HY_FH), w2, b2.reshape(DEPTH, 1, HY_FH), w3, log_decay.reshape(DEPTH, 1, 2 * HY_W))


def _dft_tables_short(n):
    big = 2 * n
    k = np.arange(big)[:, None]
    t = np.arange(n)[None, :]
    ang = 2.0 * np.pi * ((k * t) % big) / big
    fwd = np.concatenate([np.cos(ang), -np.sin(ang)], axis=0)
    inv = np.concatenate([np.cos(ang).T, -np.sin(ang).T], axis=1) / big
    return fwd.astype(np.float32), inv.astype(np.float32)


def _hy_spec_short_kernel(hf_ref, hb_ref, fwd_ref, f_ref, *, n):
    f32dot = functools.partial(jnp.dot, preferred_element_type=F32, precision=HIGHEST)
    xf = f32dot(fwd_ref[...], hf_ref[0])
    xb = f32dot(fwd_ref[...], hb_ref[0])
    big = 2 * n
    f_ref[0, :big, :] = xf[:big] + xb[:big]
    f_ref[0, big:, :] = xf[big:] - xb[big:]


def _hy_spec_short(n, hf, hb):
    fwd, _ = _dft_tables_short(n)
    lay = lambda shape: pl.BlockSpec((1,) + shape, lambda l: (l,) + (0,) * len(shape))
    return pl.pallas_call(
        functools.partial(_hy_spec_short_kernel, n=n),
        grid=(DEPTH,),
        in_specs=[lay((n, HY_W)), lay((n, HY_W)), _const_spec(fwd.shape)],
        out_specs=lay((4 * n, HY_W)),
        out_shape=jax.ShapeDtypeStruct((DEPTH, 4 * n, HY_W), F32),
        compiler_params=_cparams(1),
        name="hyena_spectrum_short",
    )(hf, hb, jnp.asarray(fwd))


def _short_conv(x, w, n):
    row = lax.broadcasted_iota(jnp.int32, (n, 1), 0)
    prev = jnp.where(row == 0, 0.0, pltpu.roll(x, 1, 0))
    nxt = jnp.where(row == n - 1, 0.0, pltpu.roll(x, n - 1, 0))
    return prev * w[0:1, :] + x * w[1:2, :] + nxt * w[2:3, :]


def _hy_conv_short_kernel(pc_ref, ws_ref, bias_ref, f_ref, fwd_ref, inv_ref, o_ref, *, n):
    big = 2 * n
    u = _short_conv(pc_ref[...], ws_ref[...], n)
    v, x1, x2 = u[:, :W_C], u[:, W_C : 2 * W_C], u[:, 2 * W_C :]

    def conv(x, order):
        spec = _dot(fwd_ref[...], x.astype(BF16))
        fr = f_ref[:big, order * W_C : (order + 1) * W_C]
        fi = f_ref[big:, order * W_C : (order + 1) * W_C]
        zr = spec[:big] * fr - spec[big:] * fi
        zi = spec[:big] * fi + spec[big:] * fr
        return _dot(inv_ref[...], jnp.concatenate([zr, zi], axis=0).astype(BF16))

    z = x1 * (conv(v, 0) + v * bias_ref[0:1, :])
    z = x2 * (conv(z, 1) + z * bias_ref[1:2, :])
    o_ref[...] = z.astype(BF16)


def _hy_conv_short(n_seq, n, row0, total_rows, pc, w_short, bias, spec):
    fwd, inv = _dft_tables_short(n)
    seq_spec = lambda w: pl.BlockSpec((n, w), lambda b: (row0 // n + b, 0))
    return pl.pallas_call(
        functools.partial(_hy_conv_short_kernel, n=n),
        grid=(n_seq,),
        in_specs=[seq_spec(PC_W), _const_spec((SHORT_K, PC_W)), _const_spec((HY_ORDER, W_C)),
                  _const_spec((4 * n, HY_W)), _const_spec(fwd.shape), _const_spec(inv.shape)],
        out_specs=seq_spec(W_C),
        out_shape=jax.ShapeDtypeStruct((total_rows, W_C), BF16),
        compiler_params=_cparams(1),
        name="hyena_conv_short",
    )(pc, w_short, bias, spec, jnp.asarray(fwd, BF16), jnp.asarray(inv, BF16))


def _dft_tables_long():
    r = FFT_R
    big = r * r
    half = r // 2
    n2 = np.arange(r)[:, None, None]
    k1 = np.arange(r)[None, :, None]
    n1 = np.arange(half)[None, None, :]
    ang = 2.0 * np.pi * ((k1 * (r * n1 + n2)) % big) / big
    first = np.concatenate([np.cos(ang), -np.sin(ang)], axis=1)
    last = np.concatenate([np.cos(ang), -np.sin(ang)], axis=1).transpose(0, 2, 1) / big
    a = np.arange(r)
    ang_r = 2.0 * np.pi * ((a[:, None] * a[None, :]) % r) / r
    c, s = np.cos(ang_r), np.sin(ang_r)
    mid = np.block([[c, s], [-s, c]])
    mid_inv = np.block([[c, -s], [s, c]])
    f32 = lambda m: m.astype(np.float32)
    return f32(first), f32(mid), f32(mid_inv), f32(last)


FFT_K1 = FFT_R // 2 + 1
FFT_K1_PAD = -(-FFT_K1 // SUBLANES) * SUBLANES
FFT_K1_UNROLL = 11


def _dft_tables_long_half():
    first, mid, mid_inv, last = _dft_tables_long()
    r = FFT_R
    keep = np.zeros((FFT_K1_PAD,), np.float32)
    keep[:FFT_K1] = 1.0
    weight = np.zeros((FFT_K1_PAD,), np.float32)
    weight[:FFT_K1] = 2.0
    weight[0] = weight[r // 2] = 1.0
    first_h = np.concatenate([first[:, :FFT_K1_PAD] * keep[None, :, None],
                              first[:, r : r + FFT_K1_PAD] * keep[None, :, None]], axis=1)
    last_h = np.concatenate([last[:, :, :FFT_K1_PAD] * weight, last[:, :, r : r + FFT_K1_PAD] * weight], axis=2)
    return first_h, mid, mid_inv, last_h


def _ld_rows(ref, rows):
    return jnp.concatenate([ref[j, rows, :] for j in range(ref.shape[0])], axis=-1)


def _st_rows(ref, rows, val):
    for j in range(ref.shape[0]):
        ref[j, rows, :] = val[:, j * LANES : (j + 1) * LANES]


def _st_transposed(ref, j, val, half):
    n = val.shape[0] // 2
    _st_rows(ref, pl.ds(j, n, stride=2 * half), val[:n])
    _st_rows(ref, pl.ds(half + j, n, stride=2 * half), val[n:])


def _lane_split_scratch(rows, width):
    return pltpu.VMEM((width // LANES, rows, LANES), F32)


FFT_UNROLL = 16


def _fft_long_forward(x_ref, y_ref, dot_first, dot_mid, out_fn, n_k1=FFT_R, unroll_k1=FFT_UNROLL):
    r = FFT_R

    def stage_a(n2, carry):
        res = dot_first(n2, _ld_rows(x_ref, pl.ds(n2, r // 2, stride=r)))
        _st_transposed(y_ref, n2, res, r)
        return carry

    lax.fori_loop(0, r, stage_a, 0, unroll=FFT_UNROLL)

    def stage_c(k1, carry):
        base = pl.multiple_of(k1 * 2 * r, 2 * r)
        out_fn(k1, base, dot_mid(_ld_rows(y_ref, pl.ds(base, 2 * r))))
        return carry

    lax.fori_loop(0, n_k1, stage_c, 0, unroll=unroll_k1)


def _dot_split(t_hi, t_lo, x):
    x_hi = x.astype(BF16)
    x_lo = (x - x_hi.astype(F32)).astype(BF16)
    return _dot(t_hi, x_hi) + _dot(t_hi, x_lo) + _dot(t_lo, x_hi)


def _split_table(m):
    hi = jnp.asarray(m, BF16)
    lo = (jnp.asarray(m) - hi.astype(F32)).astype(BF16)
    return hi, lo


def _hy_spec_long_kernel(hf_ref, hb_ref, first_hi_ref, first_lo_ref, mid_hi_ref, mid_lo_ref, f_ref,
                         x_ref, y_ref, tmp_ref):
    r = FFT_R
    dot_first = lambda n2, slab: _dot_split(first_hi_ref[n2], first_lo_ref[n2], slab)
    dot_mid = lambda block: _dot_split(mid_hi_ref[...], mid_lo_ref[...], block)

    def write_fwd(k1, base, spec):
        tmp_ref[pl.ds(base, 2 * r), :] = spec

    _st_rows(x_ref, slice(None), hf_ref[0])
    _fft_long_forward(x_ref, y_ref, dot_first, dot_mid, write_fwd, n_k1=FFT_K1, unroll_k1=FFT_K1_UNROLL)

    def write_sum(k1, base, spec):
        prev = tmp_ref[pl.ds(base, 2 * r), :]
        f_ref[0, pl.ds(base, r), :] = prev[:r] + spec[:r]
        f_ref[0, pl.ds(base + r, r), :] = prev[r:] - spec[r:]

    _st_rows(x_ref, slice(None), hb_ref[0])
    _fft_long_forward(x_ref, y_ref, dot_first, dot_mid, write_sum, n_k1=FFT_K1, unroll_k1=FFT_K1_UNROLL)


SPEC_ROWS = 2 * FFT_R * FFT_K1


def _hy_spec_long(n, hf, hb):
    first, mid, _, _ = _dft_tables_long_half()
    lay = lambda rows: pl.BlockSpec((1, rows, W_C), lambda l, o: (l, 0, o))
    return pl.pallas_call(
        _hy_spec_long_kernel,
        grid=(DEPTH, HY_ORDER),
        in_specs=[lay(n), lay(n)] + [_const_spec(first.shape)] * 2 + [_const_spec(mid.shape)] * 2,
        out_specs=lay(SPEC_ROWS),
        out_shape=jax.ShapeDtypeStruct((DEPTH, SPEC_ROWS, HY_W), F32),
        scratch_shapes=[_lane_split_scratch(n, W_C), _lane_split_scratch(2 * FFT_R * FFT_K1_PAD, W_C),
                        pltpu.VMEM((SPEC_ROWS, W_C), F32)],
        compiler_params=_cparams(2),
        name="hyena_spectrum_long",
    )(hf, hb, *_split_table(first), *_split_table(mid))


def _hy_conv_long_kernel(sig_ref, gate_ref, ws_ref, bias_ref, f_ref, first_ref, mid_ref, midinv_ref, last_ref, o_ref,
                         x_ref, y_ref, v_ref, out_ref, *, n, order):
    r = FFT_R
    to_bf16 = lambda x: x.astype(BF16)
    gate = _short_conv(gate_ref[...], ws_ref[:, (order + 1) * W_C : (order + 2) * W_C], n)
    sig = _short_conv(sig_ref[...], ws_ref[:, :W_C], n) if order == 0 else sig_ref[...]
    _st_rows(x_ref, slice(None), sig)
    @pl.when(pl.program_id(0) == 0)
    def _():
        v_ref[...] = jnp.zeros(v_ref.shape, F32)

    def filter_and_invert(k1, base, spec):
        fr = f_ref[pl.ds(base, r), :]
        fi = f_ref[pl.ds(base + r, r), :]
        zr = spec[:r] * fr - spec[r:] * fi
        zi = spec[:r] * fi + spec[r:] * fr
        res = _dot(midinv_ref[...], jnp.concatenate([zr, zi], axis=0).astype(BF16))
        _st_transposed(v_ref, k1, res, FFT_K1_PAD)

    dot_first = lambda n2, slab: _dot(first_ref[n2], to_bf16(slab))
    dot_mid = lambda block: _dot(mid_ref[...], to_bf16(block))
    _fft_long_forward(x_ref, y_ref, dot_first, dot_mid, filter_and_invert, n_k1=FFT_K1, unroll_k1=FFT_K1_UNROLL)

    def stage_last(n2, carry):
        base = pl.multiple_of(n2 * 2 * FFT_K1_PAD, 2 * FFT_K1_PAD)
        res = _dot(last_ref[n2], _ld_rows(v_ref, pl.ds(base, 2 * FFT_K1_PAD)).astype(BF16))
        _st_rows(out_ref, pl.ds(n2, r // 2, stride=r), res)
        return carry

    lax.fori_loop(0, r, stage_last, 0, unroll=FFT_UNROLL)
    z = gate * (_ld_rows(out_ref, slice(None)) + sig * bias_ref[order : order + 1, :])
    o_ref[...] = z.astype(o_ref.dtype)


def _hy_conv_long(n_seq, n, row0, total_rows, pc, w_short, bias, spec, prev=None):
    assert 2 * n == FFT_R * FFT_R and FFT_K1 % FFT_K1_UNROLL == 0
    first, mid, mid_inv, last = _dft_tables_long_half()
    bf = lambda m: jnp.asarray(m, BF16)
    lane_block = lambda j: pl.BlockSpec((n, W_C), lambda b: (row0 // n + b, j))
    z = None
    for order in range(HY_ORDER):
        final = order + 1 == HY_ORDER
        in_specs = [lane_block(0), lane_block(order + 1), _const_spec((SHORT_K, PC_W)),
                    _const_spec((HY_ORDER, W_C)), pl.BlockSpec((SPEC_ROWS, W_C), lambda b, order=order: (0, order)),
                    _const_spec(first.shape), _const_spec(mid.shape), _const_spec(mid_inv.shape),
                    _const_spec(last.shape)]
        args = [pc if order == 0 else z, pc, w_short, bias, spec, bf(first), bf(mid), bf(mid_inv), bf(last)]
        body = functools.partial(_hy_conv_long_kernel, n=n, order=order)
        body, in_specs, args, aliases = _keep_rows_of(prev if final else None, body, in_specs, args)
        z = pl.pallas_call(
            body,
            grid=(n_seq,),
            in_specs=in_specs,
            out_specs=lane_block(0),
            out_shape=jax.ShapeDtypeStruct((total_rows, W_C), BF16 if final else F32),
            input_output_aliases=aliases,
            scratch_shapes=[_lane_split_scratch(n, W_C), _lane_split_scratch(2 * FFT_R * FFT_K1_PAD, W_C),
                            _lane_split_scratch(2 * FFT_K1_PAD * FFT_R, W_C), _lane_split_scratch(n, W_C)],
            compiler_params=_cparams(1),
            name="hyena_conv_long",
        )(*args)
    return z


def _pad_in_weights(w_in):
    gap = jnp.zeros(w_in.shape[:2] + (PB_W - MLA_IN,), BF16)
    w = w_in.astype(BF16)
    return jnp.concatenate([w[..., : PA_W + MLA_IN], gap, w[..., PA_W + MLA_IN :]], axis=-1)


def kernel(x_prompt, x_sample, c, cache_mla, cache_diff_k, cache_diff_v, state_hgrn, c_ctx, w_mod, b_mod, norm_g,
           ffn_w_gu, ffn_w_down, w_in, w_out, hgrn_lb_logits, hgrn_onorm, mla_q_norm, mla_kv_norm, mla_w_uq,
           mla_w_ukv, mla_qk_norm, hy_short, hy_w1, hy_b1, hy_w2, hy_b2, hy_w3, hy_log_decay, hy_bias,
           diff_qk_norm, diff_lambda, diff_subln):
    bc, sc, _ = x_prompt.shape
    bl, sl, _ = x_sample.shape
    past = cache_mla.shape[2]
    tok = _Tokens(bc, sc, bl, sl)
    assert bl + 1 <= MOD_ROWS and tok.tc % sl == 0

    x = (x_prompt.reshape(tok.tc, D_MODEL), x_sample.reshape(tok.tl, D_MODEL))
    cond = jnp.concatenate([c_ctx[None], c, jnp.zeros((MOD_ROWS - 1 - bl, D_MODEL), F32)], axis=0)
    mod = _modulation(cond, w_mod, b_mod)

    filters = {n: _hy_filters(n, hy_w1, hy_b1, hy_w2, hy_b2, hy_w3, hy_log_decay) for n in (sc, sl)}
    spec_ctx = _hy_spec_short(sc, *filters[sc])
    spec_lat = _hy_spec_long(sl, *filters[sl])

    mla_tables = _rope_tables(tok, [NOPE_B], HEAD_PAD)
    diff_tables = _rope_tables(tok, list(range(0, H_D * 2 * DH_D, DH_D)), H_D * 2 * DH_D)

    w_gu_b, w_down_b, w_out_b, w_in_b = (ffn_w_gu.astype(BF16), ffn_w_down.astype(BF16), w_out.astype(BF16),
                                         _pad_in_weights(w_in))

    new_mla, new_dk, new_dv, new_state = [], [], [], []
    for l in range(DEPTH):
        x = _ffn(tok, x, mod, l, 0, norm_g[l, 0], w_gu_b, w_down_b)
        pa, pb, pc, pd = _inproj(tok, x, mod, l, norm_g[l, 1], w_in_b)
        mla_w = _mla_weights(mla_w_uq[l], mla_w_ukv[l], mla_qk_norm[l])
        q_b, k_b, v_b, cache_b = _prep_call(
            tok, _mla_prep_body, "mla_prep", pb,
            _mla_prep_operands(tok, mla_q_norm[l], mla_kv_norm[l], mla_w, mla_tables))
        q_d, k_d, v_d, kcache_d, vcache_d = _prep_call(
            tok, _diff_prep_body, "diff_prep", pd, _diff_prep_operands(tok, diff_qk_norm[l], diff_tables))

        o_a, s_ctx = _hgrn(l, bc, sc, 0, tok.t, pa, hgrn_lb_logits, hgrn_onorm[l], None)
        o_a, _ = _hgrn(l, bl, sl, tok.tc, tok.t, pa, hgrn_lb_logits, hgrn_onorm[l], state_hgrn[:, l], prev=o_a)

        cache_rows = jnp.pad(cache_mla[:, l].reshape(bl * past, KV_LORA + ROPE_B),
                             ((0, 0), (0, 2 * LANES - KV_LORA - ROPE_B)))
        kc_b, vc_b = _mla_cache_prep(cache_rows, mla_w)
        o_b = _mla_attention(bc, sc, 0, tok.t, q_b, k_b, v_b)
        o_b = _mla_attention(bl, sl, tok.tc, tok.t, q_b, k_b, v_b, kc_b, vc_b, prev=o_b)

        o_c = _hy_conv_short(bc, sc, 0, tok.t, pc, hy_short[l], hy_bias[l], spec_ctx[l])
        o_c = _hy_conv_long(bl, sl, tok.tc, tok.t, pc, hy_short[l], hy_bias[l], spec_lat[l], prev=o_c)

        kc_d, vc_d = _diff_cache_prep(cache_diff_k[:, l].reshape(bl * past, H_D * 2 * DH_D),
                                      cache_diff_v[:, l].reshape(bl * past, H_D * DV_D))
        o_d = _diff_attention(l, bc, sc, 0, tok.t, diff_lambda[l], diff_subln[l], q_d, k_d, v_d)
        o_d = _diff_attention(l, bl, sl, tok.tc, tok.t, diff_lambda[l], diff_subln[l], q_d, k_d, v_d, kc_d, vc_d,
                              prev=o_d)

        last_ffn = functools.partial(_ffn, tok, x, mod, l, 1, norm_g[l, 2], w_gu_b, w_down_b,
                                     mixer_outs=[o_a, o_b, o_c, o_d], w_out=w_out_b)
        if l + 1 < DEPTH:
            x = last_ffn()
        else:
            y_ctx = last_ffn(tiles=(0, tok.ctx_tiles))
            y_lat = last_ffn(tiles=(tok.ctx_tiles, tok.n_tiles - tok.ctx_tiles))

        new_mla.append(cache_b[: tok.tc, : KV_LORA + ROPE_B].reshape(bc, sc, KV_LORA + ROPE_B))
        new_dk.append(kcache_d[: tok.tc].reshape(bc, sc, H_D, 2, DH_D))
        new_dv.append(vcache_d[: tok.tc].reshape(bc, sc, H_D, DV_D))
        new_state.append(s_ctx)

    y_prompt = y_ctx.reshape(bc, sc, D_MODEL)
    y_sample = y_lat.reshape(bl, sl, D_MODEL)
    return (y_prompt, y_sample, jnp.stack(new_mla, axis=1), jnp.stack(new_dk, axis=1), jnp.stack(new_dv, axis=1),
            jnp.stack(new_state, axis=1))
```

```python
import functools
import math

import jax
import jax.numpy as jnp
import numpy as np
from jax import lax
from jax.experimental import pallas as pl
from jax.experimental.pallas import tpu as pltpu

F32 = jnp.float32
BF16 = jnp.bfloat16
HIGHEST = lax.Precision.HIGHEST

D_MODEL = 1024
DEPTH = 4
GRID_W = 64
N_MOD = 9
D_FF = 2816
EPS = 1e-6
ROPE_BASE = 10000.0
GROUP_W = 256
H_A, DK_A, DV_A = 4, 64, 64
MAX_INPUT_KEY = 1.0 - 1e-6
H_B, NOPE_B, ROPE_B, V_B = 4, 64, 32, 64
Q_LORA, KV_LORA = 256, 128
W_C, HY_ORDER, HY_BANDS, HY_FH, SHORT_K = 256, 2, 8, 64, 3
HY_EMB = 1 + 2 * HY_BANDS
H_D, DV_D, DH_D = 4, 64, 32

LANES = 128
MOD_ROWS = 16
TOKEN_TILE = 512
Q_TILE = 256
ATTN_LOOKAHEAD = 3
HGRN_BLOCK = 128
SUBLANES = 8
VMEM_LIMIT = 56 * 1024 * 1024
ROPE_HALF = ROPE_B // 4

PA_W, PB_W, PC_W, PD_W = 1280, 512, 768, 768
IN_PAD_W = PA_W + PB_W + PC_W + PD_W
MLA_IN = Q_LORA + KV_LORA + ROPE_B
HEAD_PAD = 128


def _cparams(n_axes):
    return pltpu.CompilerParams(dimension_semantics=("arbitrary",) * n_axes, vmem_limit_bytes=VMEM_LIMIT)


def _nt_dot(a, b):
    return lax.dot_general(a, b, (((1,), (1,)), ((), ())), preferred_element_type=F32)


def _dot(a, b):
    return jnp.dot(a, b, preferred_element_type=F32)


def _dot_exact_rhs(a, b_bf16, passes=3):
    out = None
    rem = a
    for _ in range(passes):
        piece = rem.astype(BF16)
        term = _dot(piece, b_bf16)
        out = term if out is None else out + term
        rem = rem - piece.astype(F32)
    return out


def _dot_exact_lhs(a_bf16, b, passes=3):
    out = None
    rem = b
    for _ in range(passes):
        piece = rem.astype(BF16)
        term = _dot(a_bf16, piece)
        out = term if out is None else out + term
        rem = rem - piece.astype(F32)
    return out


def _silu(x):
    return x * jax.nn.sigmoid(x)


def _adaln(x, g, sc, sh):
    y = x * lax.rsqrt(jnp.mean(x * x, axis=-1, keepdims=True) + EPS)
    return (y * g) * (1.0 + sc) + sh


def _rope(x, cos, sin_hi, sin_lo):
    w = x.shape[-1]
    return x * cos + pltpu.roll(x, ROPE_HALF, 1) * sin_hi + pltpu.roll(x, w - ROPE_HALF, 1) * sin_lo


def _mod_kernel(c_ref, w_ref, b_ref, o_ref):
    a = _silu(c_ref[...])
    o_ref[0] = jnp.dot(a, w_ref[0], preferred_element_type=F32, precision=HIGHEST) + b_ref[0]


def _modulation(cond, w_mod, b_mod):
    tn = D_MODEL
    n_col = N_MOD * D_MODEL
    out = pl.pallas_call(
        _mod_kernel,
        grid=(DEPTH, n_col // tn),
        in_specs=[
            pl.BlockSpec((MOD_ROWS, D_MODEL), lambda l, j: (0, 0)),
            pl.BlockSpec((1, D_MODEL, tn), lambda l, j: (l, 0, j)),
            pl.BlockSpec((1, 1, tn), lambda l, j: (l, 0, j)),
        ],
        out_specs=pl.BlockSpec((1, MOD_ROWS, tn), lambda l, j: (l, 0, j)),
        out_shape=jax.ShapeDtypeStruct((DEPTH, MOD_ROWS, n_col), F32),
        compiler_params=_cparams(2),
        name="modulation",
    )(cond, w_mod, b_mod.reshape(DEPTH, 1, n_col))
    return out.reshape(DEPTH * MOD_ROWS, N_MOD, D_MODEL)


class _Tokens:
    def __init__(self, n_ctx_seq, ctx_len, n_lat_seq, lat_len):
        self.bc, self.sc, self.bl, self.sl = n_ctx_seq, ctx_len, n_lat_seq, lat_len
        self.tc, self.tl = n_ctx_seq * ctx_len, n_lat_seq * lat_len
        self.t = self.tc + self.tl
        self.tm = min(TOKEN_TILE, lat_len)
        assert self.tc % self.tm == 0 and lat_len % self.tm == 0
        self.ctx_tiles = self.tc // self.tm
        self.lat_tiles_per_seq = lat_len // self.tm
        self.n_tiles = self.t // self.tm

    def mod_row(self, layer):
        def f(i):
            lat = 1 + (i - self.ctx_tiles) // self.lat_tiles_per_seq
            return layer * MOD_ROWS + jnp.where(i < self.ctx_tiles, 0, lat)

        return f

    def rope_block(self, i):
        return jnp.where(i < self.ctx_tiles, 0, 1 + (i - self.ctx_tiles) % self.lat_tiles_per_seq)


def _mod_spec(tok, layer):
    row = tok.mod_row(layer)
    return pl.BlockSpec((1, N_MOD, D_MODEL), lambda i: (row(i), 0, 0))


def _const_spec(shape):
    zeros = (0,) * len(shape)
    return pl.BlockSpec(shape, lambda *_: zeros)


def _row_spec(tm, width):
    return pl.BlockSpec((tm, width), lambda i: (i, 0))


def _keep_rows_of(prev, kernel_fn, in_specs, args):
    if prev is None:
        return kernel_fn, in_specs, args, {}
    idx = len(in_specs)

    def body(*refs):
        return kernel_fn(*refs[:idx], *refs[idx + 1 :])

    return body, in_specs + [pl.BlockSpec(memory_space=pl.ANY)], args + [prev], {idx: 0}


def _ffn_kernel(*refs, mod_base, split_in, fused, ctx_tiles):
    n_x = 2 if split_in else 1
    x_refs, (mod_ref, g_ref, wgu_ref, wd_ref), rest = refs[:n_x], refs[n_x : n_x + 4], refs[n_x + 4 :]
    is_ctx = pl.program_id(0) < ctx_tiles
    x = jnp.where(is_ctx, x_refs[0][...], x_refs[1][...]) if split_in else x_refs[0][...]
    if fused:
        oa_ref, ob_ref, oc_ref, od_ref, wo_ref = rest[:5]
        mixed = jnp.concatenate([oa_ref[...], ob_ref[...], oc_ref[...], od_ref[...]], axis=-1)
        x = x + mod_ref[0, 5:6, :] * _dot(mixed, wo_ref[...])
    sh = mod_ref[0, mod_base : mod_base + 1, :]
    sc = mod_ref[0, mod_base + 1 : mod_base + 2, :]
    gate = mod_ref[0, mod_base + 2 : mod_base + 3, :]
    h = _adaln(x, g_ref[...], sc, sh).astype(BF16)
    au = _dot(h, wgu_ref[...])
    act = (_silu(au[:, :D_FF]) * au[:, D_FF:]).astype(BF16)
    rest[-1][...] = x + (0.5 * gate) * _dot(act, wd_ref[...])


def _resident(lead_index, shape):
    zeros = (0,) * len(shape)
    return pl.BlockSpec((None,) * len(lead_index) + tuple(shape), lambda i: tuple(lead_index) + zeros,
                        pipeline_mode=pl.Buffered(1))


def _ffn(tok, x, mod, layer, which, norm_g, w_gu, w_down, mixer_outs=None, w_out=None, tiles=None):
    tm = tok.tm
    split_in = isinstance(x, tuple)
    first, count = (0, tok.n_tiles) if tiles is None else tiles
    assert not (split_in and first)
    rows = lambda w: pl.BlockSpec((tm, w), lambda i: (i + first, 0))
    mod_row = tok.mod_row(layer)
    ctx_spec = pl.BlockSpec((tm, D_MODEL), lambda i: (jnp.minimum(i, tok.ctx_tiles - 1), 0))
    lat_spec = pl.BlockSpec((tm, D_MODEL), lambda i: (jnp.maximum(i - tok.ctx_tiles, 0), 0))
    in_specs = ([ctx_spec, lat_spec] if split_in else [rows(D_MODEL)]) + [
        pl.BlockSpec((1, N_MOD, D_MODEL), lambda i: (mod_row(i + first), 0, 0)),
        _const_spec((1, D_MODEL)),
        _resident((layer, which), (D_MODEL, 2 * D_FF)),
        _resident((layer, which), (D_FF, D_MODEL)),
    ]
    args = (list(x) if split_in else [x]) + [mod, norm_g.reshape(1, D_MODEL), w_gu, w_down]
    if mixer_outs is not None:
        in_specs += [rows(GROUP_W)] * 4
        in_specs.append(_resident((layer,), (D_MODEL, D_MODEL)))
        args += list(mixer_outs) + [w_out]
    return pl.pallas_call(
        functools.partial(_ffn_kernel, mod_base=6 * which, split_in=split_in, fused=mixer_outs is not None,
                          ctx_tiles=tok.ctx_tiles),
        grid=(count,),
        in_specs=in_specs,
        out_specs=_row_spec(tm, D_MODEL),
        out_shape=jax.ShapeDtypeStruct((count * tm, D_MODEL), F32),
        compiler_params=_cparams(1),
        name="ffn",
    )(*args)


def _inproj_kernel(x_ref, mod_ref, g_ref, w_ref, pa_ref, pb_ref, pc_ref, pd_ref):
    h = _adaln(x_ref[...], g_ref[...], mod_ref[0, 4:5, :], mod_ref[0, 3:4, :]).astype(BF16)
    p = _dot(h, w_ref[...])
    pa_ref[...] = p[:, :PA_W]
    pb_ref[...] = p[:, PA_W : PA_W + PB_W]
    pc_ref[...] = p[:, PA_W + PB_W : PA_W + PB_W + PC_W]
    pd_ref[...] = p[:, PA_W + PB_W + PC_W :]


def _inproj(tok, x, mod, layer, norm_g, w_in_pad):
    tm = tok.tm
    widths = (PA_W, PB_W, PC_W, PD_W)
    return pl.pallas_call(
        _inproj_kernel,
        grid=(tok.n_tiles,),
        in_specs=[
            _row_spec(tm, D_MODEL),
            _mod_spec(tok, layer),
            _const_spec((1, D_MODEL)),
            _resident((layer,), (D_MODEL, IN_PAD_W)),
        ],
        out_specs=[_row_spec(tm, w) for w in widths],
        out_shape=[jax.ShapeDtypeStruct((tok.t, w), F32) for w in widths],
        compiler_params=_cparams(1),
        name="inproj",
    )(x, mod, norm_g.reshape(1, D_MODEL), w_in_pad)


def _rope_group_tables(n_tok):
    t = np.arange(n_tok)
    pos = np.stack([t // GRID_W, t % GRID_W], axis=1).astype(np.float32)
    inv = (ROPE_BASE ** (-np.arange(ROPE_HALF, dtype=np.float32) / ROPE_HALF)).astype(np.float32)
    lane = np.arange(4 * ROPE_HALF)
    ang = (pos[:, lane // (2 * ROPE_HALF)] * inv[lane % ROPE_HALF][None, :]).astype(np.float32)
    second = (lane % (2 * ROPE_HALF)) >= ROPE_HALF
    cos, sin = np.cos(ang), np.sin(ang)
    return cos, np.where(second[None], sin, 0.0), np.where(second[None], 0.0, -sin)


def _rope_tables(tok, lane_groups, width):
    cos = np.ones((tok.tm + tok.sl, width), np.float32)
    s_hi = np.zeros_like(cos)
    s_lo = np.zeros_like(cos)
    c, a, b = _rope_group_tables(tok.sl)
    for g in lane_groups:
        cos[tok.tm :, g : g + ROPE_B] = c
        s_hi[tok.tm :, g : g + ROPE_B] = a
        s_lo[tok.tm :, g : g + ROPE_B] = b
    return jnp.asarray(cos), jnp.asarray(s_hi), jnp.asarray(s_lo)


def _mla_keys_values(kv_in, wk_ref, wv_ref, kn_ref, rope):
    kraw = _dot(kv_in.astype(BF16), wk_ref[...])
    v = _dot(kv_in[:, :KV_LORA].astype(BF16), wv_ref[...])
    ks = []
    for h in range(H_B):
        kh = kraw[:, h * HEAD_PAD : (h + 1) * HEAD_PAD]
        ss = jnp.sum(kh * kh, axis=-1, keepdims=True) * (1.0 / (NOPE_B + ROPE_B))
        kh = kh * lax.rsqrt(ss + EPS) * kn_ref[...]
        if rope is not None:
            kh = _rope_by_matmul(kh, *rope)
        ks.append(kh.astype(BF16))
    return ks, v.astype(BF16)


def _rope_by_matmul(x, cos, sin, perm):
    return x * cos + _dot(x.astype(BF16), perm) * sin


def _mla_prep_body(pb, gq_ref, gkv_ref, wuq_ref, wk_ref, wv_ref, qn_ref, kn_ref, cos_ref, shi_ref, slo_ref, perm_ref,
                   q_ref, k_ref, v_ref, cache_ref):
    rope = (cos_ref[...], shi_ref[...] + slo_ref[...], perm_ref[...])
    cq = pb[:, :Q_LORA]
    cq = cq * lax.rsqrt(jnp.mean(cq * cq, axis=-1, keepdims=True) + EPS) * gq_ref[...]
    qraw = _dot(cq.astype(BF16), wuq_ref[...])
    scale = (NOPE_B + ROPE_B) ** -0.5 * LOG2E
    for h in range(H_B):
        qh = qraw[:, h * HEAD_PAD : (h + 1) * HEAD_PAD]
        ss = jnp.sum(qh * qh, axis=-1, keepdims=True) * (1.0 / (NOPE_B + ROPE_B))
        qh = _rope_by_matmul(qh * lax.rsqrt(ss + EPS) * qn_ref[...], *rope)
        q_ref[:, h * HEAD_PAD : (h + 1) * HEAD_PAD] = (qh * scale).astype(BF16)
    ckv = pb[:, Q_LORA : Q_LORA + KV_LORA]
    ckv = ckv * lax.rsqrt(jnp.mean(ckv * ckv, axis=-1, keepdims=True) + EPS) * gkv_ref[...]
    kv_in = jnp.concatenate([ckv, pb[:, Q_LORA + KV_LORA :]], axis=-1)
    cache_ref[...] = kv_in
    ks, v = _mla_keys_values(kv_in, wk_ref, wv_ref, kn_ref, rope)
    for h in range(H_B):
        k_ref[:, h * HEAD_PAD : (h + 1) * HEAD_PAD] = ks[h]
    v_ref[...] = v


def _mla_cache_kernel(c_ref, wk_ref, wv_ref, kn_ref, k_ref, v_ref):
    ks, v = _mla_keys_values(c_ref[...], wk_ref, wv_ref, kn_ref, None)
    for h in range(H_B):
        k_ref[:, h * HEAD_PAD : (h + 1) * HEAD_PAD] = ks[h]
    v_ref[...] = v


def _mla_weights(w_uq, w_ukv, qk_norm):
    wuq = jnp.pad(w_uq.reshape(Q_LORA, H_B, NOPE_B + ROPE_B), ((0, 0), (0, 0), (0, HEAD_PAD - NOPE_B - ROPE_B)))
    wuq = wuq.reshape(Q_LORA, H_B * HEAD_PAD).astype(BF16)
    ukv = w_ukv.reshape(KV_LORA, H_B, NOPE_B + V_B)
    wk_nope = jnp.pad(ukv[:, :, :NOPE_B], ((0, 0), (0, 0), (0, HEAD_PAD - NOPE_B)))
    place = np.zeros((2 * LANES - KV_LORA, H_B, HEAD_PAD), np.float32)
    for h in range(H_B):
        place[np.arange(ROPE_B), h, NOPE_B + np.arange(ROPE_B)] = 1.0
    wk = jnp.concatenate([wk_nope, jnp.asarray(place)], axis=0).reshape(2 * LANES, H_B * HEAD_PAD).astype(BF16)
    wv = ukv[:, :, NOPE_B:].reshape(KV_LORA, H_B * V_B).astype(BF16)
    pad = lambda g: jnp.pad(g, (0, HEAD_PAD - NOPE_B - ROPE_B)).reshape(1, HEAD_PAD)
    return wuq, wk, wv, pad(qk_norm[0]), pad(qk_norm[1])


def _mla_prep_operands(tok, gq, gkv, weights, tables):
    wuq, wk, wv, qn, kn = weights
    tm = tok.tm
    rope_spec = pl.BlockSpec((tm, HEAD_PAD), lambda i: (tok.rope_block(i), 0))
    perm = np.zeros((HEAD_PAD, HEAD_PAD), np.float32)
    j = NOPE_B + np.arange(ROPE_B)
    perm[np.where((j - NOPE_B) % (2 * ROPE_HALF) < ROPE_HALF, j + ROPE_HALF, j - ROPE_HALF), j] = 1.0
    args = [gq.reshape(1, -1), gkv.reshape(1, -1), wuq, wk, wv, qn, kn, *tables, jnp.asarray(perm, BF16)]
    in_specs = [_const_spec((1, Q_LORA)), _const_spec((1, KV_LORA)), _const_spec(wuq.shape), _const_spec(wk.shape),
                _const_spec(wv.shape), _const_spec((1, HEAD_PAD)), _const_spec((1, HEAD_PAD))] + [rope_spec] * 3
    in_specs.append(_const_spec((HEAD_PAD, HEAD_PAD)))
    widths = [(H_B * HEAD_PAD, BF16), (H_B * HEAD_PAD, BF16), (H_B * V_B, BF16), (2 * LANES, F32)]
    return args, in_specs, widths


def _prep_call(tok, body, name, proj, operands):
    args, in_specs, outs = operands
    tm = tok.tm

    def kernel_fn(proj_ref, *refs):
        body(proj_ref[...], *refs)

    return pl.pallas_call(
        kernel_fn,
        grid=(tok.n_tiles,),
        in_specs=[_row_spec(tm, proj.shape[1])] + in_specs,
        out_specs=[_row_spec(tm, w) for w, _ in outs],
        out_shape=[jax.ShapeDtypeStruct((tok.t, w), dt) for w, dt in outs],
        compiler_params=_cparams(1),
        name=name,
    )(proj, *args)


def _mla_cache_prep(cache_rows, weights):
    _, wk, wv, _, kn = weights
    rows = cache_rows.shape[0]
    tm = min(TOKEN_TILE, rows)
    return pl.pallas_call(
        _mla_cache_kernel,
        grid=(rows // tm,),
        in_specs=[_row_spec(tm, 2 * LANES), _const_spec(wk.shape), _const_spec(wv.shape), _const_spec((1, HEAD_PAD))],
        out_specs=[_row_spec(tm, H_B * HEAD_PAD), _row_spec(tm, H_B * V_B)],
        out_shape=[jax.ShapeDtypeStruct((rows, H_B * HEAD_PAD), BF16), jax.ShapeDtypeStruct((rows, H_B * V_B), BF16)],
        compiler_params=_cparams(1),
        name="mla_cache_prep",
    )(cache_rows, wk, wv, kn)


LOG2E = math.log2(math.e)


def _softmax_parts(scores):
    m = functools.reduce(jnp.maximum, [jnp.max(s, axis=-1, keepdims=True) for s in scores])
    ps = [jnp.exp2(s - m) for s in scores]
    denom = functools.reduce(jnp.add, [jnp.sum(p, axis=-1, keepdims=True) for p in ps])
    return ps, denom


def _head_lane_mask(width, head, head_w):
    lane = lax.broadcasted_iota(jnp.int32, (1, width), 1)
    return (lane >= head * head_w) & (lane < (head + 1) * head_w)


def _mla_attn_kernel(*refs, n_kv):
    q_ref = refs[0]
    k_refs = refs[1 : 1 + n_kv]
    v_refs = refs[1 + n_kv : 1 + 2 * n_kv]
    o_ref = refs[1 + 2 * n_kv]
    def head_scores(h):
        sl = slice(h * HEAD_PAD, (h + 1) * HEAD_PAD)
        return [_nt_dot(q_ref[:, sl], k[:, sl]) for k in k_refs]

    pending = [head_scores(0)]
    out = None
    for h in range(H_B):
        scores = pending.pop(0)
        if h + 1 < H_B:
            pending.append(head_scores(h + 1))
        ps, denom = _softmax_parts(scores)
        mask = _head_lane_mask(H_B * V_B, h, V_B)
        acc = None
        for p, v in zip(ps, v_refs):
            term = _dot(p.astype(BF16), jnp.where(mask, v[...], jnp.zeros((), BF16)))
            acc = term if acc is None else acc + term
        acc = acc * (1.0 / denom)
        out = acc if out is None else out + acc
    o_ref[...] = out.astype(BF16)


def _seq_specs(n_seq, seq_len, row0, tq, q_width, kv_widths, kv_cache):
    q_blocks = seq_len // tq
    q_spec = lambda w: pl.BlockSpec((tq, w), lambda b, i: (row0 // tq + b * q_blocks + i, 0))
    new_spec = lambda w: pl.BlockSpec((seq_len, w), lambda b, i: (row0 // seq_len + b, 0))
    specs = [q_spec(q_width)]
    for w in kv_widths:
        if kv_cache is not None:
            specs.append(pl.BlockSpec((kv_cache, w), lambda b, i: (b, 0)))
        specs.append(new_spec(w))
    return specs, q_spec


def _mla_attention(n_seq, seq_len, row0, total_rows, q, k, v, k_cache=None, v_cache=None, prev=None):
    tq = min(Q_TILE, seq_len)
    has_cache = k_cache is not None
    cache_len = k_cache.shape[0] // n_seq if has_cache else None
    specs, q_spec = _seq_specs(n_seq, seq_len, row0, tq, H_B * HEAD_PAD, (H_B * HEAD_PAD, H_B * V_B), cache_len)
    args = [q] + ([k_cache, k] if has_cache else [k]) + ([v_cache, v] if has_cache else [v])
    body = functools.partial(_mla_attn_kernel, n_kv=2 if has_cache else 1)
    body, specs, args, aliases = _keep_rows_of(prev, body, specs, args)
    return pl.pallas_call(
        body,
        grid=(n_seq, seq_len // tq),
        in_specs=specs,
        out_specs=q_spec(H_B * V_B),
        out_shape=jax.ShapeDtypeStruct((total_rows, H_B * V_B), BF16),
        input_output_aliases=aliases,
        compiler_params=_cparams(2),
        name="mla_attention",
    )(*args)


DQK_W = H_D * 2 * DH_D


def _group_mean_matrix(width, group):
    g = np.arange(width)
    return jnp.asarray((g[:, None] // group == g[None, :] // group).astype(np.float32), BF16)


def _diff_prep_body(pd, gq_ref, gk_ref, gm_ref, cos_ref, shi_ref, slo_ref,
                    q_ref, k_ref, v_ref, kcache_ref, vcache_ref):
    rope = (cos_ref[...], shi_ref[...], slo_ref[...])
    w = H_D * 2 * DH_D

    def norm(x, g):
        ms = _dot_exact_rhs(x * x, gm_ref[...], passes=2) * (1.0 / DH_D)
        return x * lax.rsqrt(ms + EPS) * g

    qn = _rope(norm(pd[:, :w], gq_ref[...]), *rope) * (DH_D ** -0.5 * LOG2E)
    kn = norm(pd[:, w : 2 * w], gk_ref[...])
    kcache_ref[...] = kn
    q_ref[...] = qn.astype(BF16)
    k_ref[...] = _rope(kn, *rope).astype(BF16)
    vcache_ref[...] = pd[:, 2 * w :]
    v_ref[...] = pd[:, 2 * w :].astype(BF16)


def _diff_cache_kernel(k_in_ref, v_in_ref, k_ref, v_ref):
    k_ref[...] = k_in_ref[...].astype(BF16)
    v_ref[...] = v_in_ref[...].astype(BF16)


def _diff_prep_operands(tok, qk_norm, tables):
    tm = tok.tm
    w = H_D * 2 * DH_D
    rope_spec = pl.BlockSpec((tm, w), lambda i: (tok.rope_block(i), 0))
    tile_g = lambda g: jnp.tile(g, H_D * 2).reshape(1, w)
    args = [tile_g(qk_norm[0]), tile_g(qk_norm[1]), _group_mean_matrix(w, DH_D), *tables]
    in_specs = [_const_spec((1, w)), _const_spec((1, w)), _const_spec((w, w))] + [rope_spec] * 3
    widths = [(DQK_W, BF16), (DQK_W, BF16), (H_D * DV_D, BF16), (w, F32), (H_D * DV_D, F32)]
    return args, in_specs, widths


def _diff_cache_prep(k_rows, v_rows):
    rows = k_rows.shape[0]
    tm = min(TOKEN_TILE, rows)
    w = H_D * 2 * DH_D
    return pl.pallas_call(
        _diff_cache_kernel,
        grid=(rows // tm,),
        in_specs=[_row_spec(tm, w), _row_spec(tm, H_D * DV_D)],
        out_specs=[_row_spec(tm, DQK_W), _row_spec(tm, H_D * DV_D)],
        out_shape=[jax.ShapeDtypeStruct((rows, DQK_W), BF16), jax.ShapeDtypeStruct((rows, H_D * DV_D), BF16)],
        compiler_params=_cparams(1),
        name="diff_cache_prep",
    )(k_rows, v_rows)


def _diff_attn_kernel(*refs, n_kv, lam_init):
    q_ref, lam_ref, sub_ref = refs[0], refs[1], refs[2]
    k_refs = refs[3 : 3 + n_kv]
    v_refs = refs[3 + n_kv : 3 + 2 * n_kv]
    o_ref = refs[3 + 2 * n_kv]
    dl = lam_ref[...]
    lam = (jnp.exp(jnp.sum(dl[0:1] * dl[1:2], axis=-1, keepdims=True))
           - jnp.exp(jnp.sum(dl[2:3] * dl[3:4], axis=-1, keepdims=True)) + lam_init)
    groups_per_tile = LANES // DH_D
    lane = lax.broadcasted_iota(jnp.int32, (1, LANES), 1)

    def map_scores(g):
        sl = slice((g // groups_per_tile) * LANES, (g // groups_per_tile + 1) * LANES)
        first = (g % groups_per_tile) * DH_D
        qg = jnp.where((lane >= first) & (lane < first + DH_D), q_ref[:, sl], jnp.zeros((), BF16))
        return [_nt_dot(qg, k[:, sl]) for k in k_refs]

    n_maps = 2 * H_D
    pending = [map_scores(g) for g in range(min(ATTN_LOOKAHEAD, n_maps))]
    out = None
    for h in range(H_D):
        probs = []
        for m in range(2):
            scores = pending.pop(0)
            nxt = 2 * h + m + ATTN_LOOKAHEAD
            if nxt < n_maps:
                pending.append(map_scores(nxt))
            probs.append(_softmax_parts(scores))
        mask = _head_lane_mask(H_D * DV_D, h, DV_D)
        (ps0, den0), (ps1, den1) = probs
        ratio = lam * den0 / den1
        acc = None
        for p0, p1, v in zip(ps0, ps1, v_refs):
            term = _dot((p0 - p1 * ratio).astype(BF16), jnp.where(mask, v[...], jnp.zeros((), BF16)))
            acc = term if acc is None else acc + term
        acc = acc * (1.0 / den0)
        ms = jnp.sum(acc * acc, axis=-1, keepdims=True) * (1.0 / DV_D)
        acc = acc * lax.rsqrt(ms + EPS)
        out = acc if out is None else out + acc
    o_ref[...] = (out * sub_ref[...] * (1.0 - lam_init)).astype(BF16)


def _diff_attention(layer, n_seq, seq_len, row0, total_rows, lam_p, sub_g, q, k, v, k_cache=None, v_cache=None,
                    prev=None):
    tq = min(Q_TILE, seq_len)
    has_cache = k_cache is not None
    cache_len = k_cache.shape[0] // n_seq if has_cache else None
    specs, q_spec = _seq_specs(n_seq, seq_len, row0, tq, DQK_W, (DQK_W, H_D * DV_D), cache_len)
    specs = [specs[0], pl.BlockSpec((4, DH_D), lambda b, i: (0, 0)), pl.BlockSpec((1, H_D * DV_D), lambda b, i: (0, 0))
             ] + specs[1:]
    args = [q, lam_p, jnp.tile(sub_g, H_D).reshape(1, H_D * DV_D)]
    args += ([k_cache, k] if has_cache else [k]) + ([v_cache, v] if has_cache else [v])
    lam_init = 0.8 - 0.6 * math.exp(-0.3 * layer)
    body = functools.partial(_diff_attn_kernel, n_kv=2 if has_cache else 1, lam_init=lam_init)
    body, specs, args, aliases = _keep_rows_of(prev, body, specs, args)
    return pl.pallas_call(
        body,
        grid=(n_seq, seq_len // tq),
        in_specs=specs,
        out_specs=q_spec(H_D * DV_D),
        out_shape=jax.ShapeDtypeStruct((total_rows, H_D * DV_D), BF16),
        input_output_aliases=aliases,
        compiler_params=_cparams(2),
        name="diff_attention",
    )(*args)


A_W = H_A * DK_A


HGRN_LEVELS = (2, 4, 8, 16, 32, 64, 128)
PAIR_W = 2 * DK_A


def _hgrn_constants():
    t = np.arange(HGRN_BLOCK)
    lower = t[None, :] <= t[:, None]
    upper = t[None, :] >= t[:, None]
    masks = [t[:, None] == t[None, :]] + [(t[:, None] // b) == (t[None, :] // b) for b in HGRN_LEVELS[:-1]]
    masks = np.stack([np.tile(m, (1, 2)) for m in masks]).astype(np.float32)
    g = np.arange(A_W)
    heads = (g[:, None] // DK_A) == (g[None, :] // DK_A)
    as_bf16 = lambda m: jnp.asarray(m.astype(np.float32), BF16)
    pair_mask = heads[:PAIR_W, :PAIR_W].astype(np.float32)
    return as_bf16(lower), as_bf16(upper), as_bf16(heads), jnp.asarray(pair_mask), jnp.asarray(masks)


def _level_reference(cum, b, forward):
    off = b // 2 - 1 if forward else b // 2
    if b >= SUBLANES:
        c3 = cum.reshape(HGRN_BLOCK // b, b, A_W)
        return jnp.broadcast_to(c3[:, off : off + 1, :], c3.shape).reshape(HGRN_BLOCK, A_W)
    c3 = cum.reshape(HGRN_BLOCK // SUBLANES, SUBLANES, A_W)
    sub = lax.broadcasted_iota(jnp.int32, (1, SUBLANES, 1), 1)
    out = None
    for g in range(SUBLANES // b):
        cand = jnp.broadcast_to(c3[:, g * b + off : g * b + off + 1, :], c3.shape)
        out = cand if out is None else jnp.where(sub >= g * b, cand, out)
    return out.reshape(HGRN_BLOCK, A_W)


def _hgrn_kernel(*refs, layer, seq_len, has_state):
    if has_state:
        pa_ref, lbl_ref, og_ref, s0_ref, lower_ref, upper_ref, heads_ref, hmask_ref, lvl_ref = refs[:9]
        rest = refs[9:]
    else:
        pa_ref, lbl_ref, og_ref, lower_ref, upper_ref, heads_ref, hmask_ref, lvl_ref = refs[:8]
        s0_ref = None
        rest = refs[8:]
    o_ref, sout_ref, st_ref, oacc_ref = rest if not has_state else (rest[0], None, rest[1], rest[2])
    n_blocks = seq_len // HGRN_BLOCK
    row = lax.broadcasted_iota(jnp.int32, (HGRN_BLOCK, 1), 0)
    first_head = lax.broadcasted_iota(jnp.int32, (1, PAIR_W), 1) < DK_A
    zero_bf16 = jnp.zeros((), BF16)

    def per_head_rows(x):
        return jnp.concatenate([jnp.where(first_head, x, zero_bf16), jnp.where(first_head, zero_bf16, x)], axis=0)

    st_ref[...] = jnp.zeros(st_ref.shape, F32)
    lower_bounds = []
    for d in range(2):
        logits = lbl_ref[d]
        e = jnp.exp(logits - jnp.max(logits, axis=0, keepdims=True))
        p = e / jnp.sum(e, axis=0, keepdims=True)
        lb = jnp.zeros((1, A_W), F32)
        for j in range(1, layer + 1):
            lb = lb + p[j : j + 1, :]
        lower_bounds.append(lb)
        if has_state:
            for h in range(H_A):
                off = (h % 2) * DK_A
                st_ref[d, h // 2, off : off + DV_A, off : off + DK_A] = s0_ref[0, d, h].T

    def both_directions(i, carry):
        for d in range(2):
            lb = lower_bounds[d]
            tri_ref = lower_ref if d == 0 else upper_ref
            blk = i if d == 0 else n_blocks - 1 - i
            r0 = pl.multiple_of(blk * HGRN_BLOCK, HGRN_BLOCK)
            rows = pl.ds(r0, HGRN_BLOCK)
            q = _silu(pa_ref[rows, 0:A_W]) * DK_A ** -0.5
            v = pa_ref[rows, A_W : 2 * A_W]
            logit = pa_ref[rows, (2 + d) * A_W : (3 + d) * A_W]
            key = jnp.minimum((1.0 - lb) * jax.nn.sigmoid(-logit), MAX_INPUT_KEY)
            log_f = jnp.log1p(-key)
            cum = _dot_exact_lhs(tri_ref[...], log_f)
            tot = cum[HGRN_BLOCK - 1 : HGRN_BLOCK, :] if d == 0 else cum[0:1, :]
            v_b = v.astype(BF16)
            scores = [None] * (H_A // 2)

            def add_pairs(qd, kd, mask_index, scores=scores):
                qd_b, kd_b = qd.astype(BF16), kd.astype(BF16)
                for pi in range(H_A // 2):
                    lanes = slice(pi * PAIR_W, (pi + 1) * PAIR_W)
                    s = _nt_dot(qd_b[:, lanes], per_head_rows(kd_b[:, lanes]))
                    if mask_index is not None:
                        s = s * lvl_ref[mask_index]
                    scores[pi] = s if scores[pi] is None else scores[pi] + s

            add_pairs(q, key, 0)
            for li, b in enumerate(HGRN_LEVELS):
                later = ((row % b) >= b // 2) if d == 0 else ((row % b) < b // 2)
                if b == 2:
                    qd = jnp.where(later, q * jnp.exp(log_f), 0.0)
                    kd = jnp.where(later, 0.0, key)
                else:
                    ref = _level_reference(cum, b, d == 0)
                    qd = jnp.where(later, q * jnp.exp(cum - ref), 0.0)
                    kd = jnp.where(later, 0.0, key * jnp.exp(ref - cum))
                add_pairs(qd, kd, li + 1 if b < HGRN_BLOCK else None)
            q_dec = (q * jnp.exp(cum)).astype(BF16)
            k_dec = (key * jnp.exp(tot - cum)).astype(BF16)
            decay = jnp.exp(tot)
            parts = []
            for pi in range(H_A // 2):
                lanes = slice(pi * PAIR_W, (pi + 1) * PAIR_W)
                state = st_ref[d, pi]
                o_pair = _dot(scores[pi].astype(BF16), per_head_rows(v_b[:, lanes]))
                parts.append(o_pair + _nt_dot(q_dec[:, lanes], state.astype(BF16)))
                upd = lax.dot_general(v_b[:, lanes], k_dec[:, lanes], (((0,), (0,)), ((), ())),
                                      preferred_element_type=F32)
                st_ref[d, pi] = state * decay[:, lanes] + upd * hmask_ref[...]
            oacc_ref[d, rows, :] = jnp.concatenate(parts, axis=1)
        return carry

    lax.fori_loop(0, n_blocks, both_directions, 0, unroll=min(4, n_blocks))
    if sout_ref is not None:
        for d in range(2):
            for h in range(H_A):
                off = (h % 2) * DK_A
                sout_ref[0, d, h] = st_ref[d, h // 2, off : off + DV_A, off : off + DK_A].T

    o = oacc_ref[0] + oacc_ref[1]
    ms = _dot_exact_rhs(o * o, heads_ref[...], passes=2) * (1.0 / DV_A)
    o_ref[...] = (o * lax.rsqrt(ms + EPS) * og_ref[...] * _silu(pa_ref[:, 4 * A_W : 5 * A_W])).astype(BF16)


def _hgrn(layer, n_seq, seq_len, row0, total_rows, pa, lb_logits, onorm_g, s0, prev=None):
    has_state = s0 is not None
    consts = _hgrn_constants()
    seq_spec = lambda w: pl.BlockSpec((seq_len, w), lambda b: (row0 // seq_len + b, 0))
    state_spec = pl.BlockSpec((1, 2, H_A, DK_A, DV_A), lambda b: (b, 0, 0, 0, 0))
    in_specs = [seq_spec(PA_W), _const_spec(lb_logits.shape), _const_spec((1, A_W))]
    args = [pa, lb_logits, jnp.tile(onorm_g, H_A).reshape(1, A_W)]
    if has_state:
        in_specs.append(state_spec)
        args.append(s0)
    in_specs += [_const_spec(c.shape) for c in consts]
    args += list(consts)
    body = functools.partial(_hgrn_kernel, layer=layer, seq_len=seq_len, has_state=has_state)
    body, in_specs, args, aliases = _keep_rows_of(prev, body, in_specs, args)
    out_specs = [seq_spec(A_W)]
    out_shape = [jax.ShapeDtypeStruct((total_rows, A_W), BF16)]
    if not has_state:
        out_specs.append(state_spec)
        out_shape.append(jax.ShapeDtypeStruct((n_seq, 2, H_A, DK_A, DV_A), F32))
    outs = pl.pallas_call(
        body,
        grid=(n_seq,),
        in_specs=in_specs,
        out_specs=out_specs,
        out_shape=out_shape,
        input_output_aliases=aliases,
        scratch_shapes=[pltpu.VMEM((2, H_A // 2, 2 * DV_A, 2 * DK_A), F32), pltpu.VMEM((2, seq_len, A_W), F32)],
        compiler_params=_cparams(1),
        name="hgrn",
    )(*args)
    return (outs[0], None) if has_state else tuple(outs)


HY_W = HY_ORDER * W_C
FFT_R = 64


def _hy_filter_kernel(w1_ref, b1_ref, w2_ref, b2_ref, w3_ref, ld_ref, hf_ref, hb_ref, *, n):
    f32dot = functools.partial(jnp.dot, preferred_element_type=F32, precision=HIGHEST)
    row = lax.broadcasted_iota(jnp.int32, (n, LANES), 0)
    lane = lax.broadcasted_iota(jnp.int32, (n, LANES), 1)
    tn = row.astype(F32) / n
    band = jnp.where(lane <= HY_BANDS, lane, lane - HY_BANDS).astype(F32)
    ang = (2.0 * math.pi) * tn * band
    feats = jnp.where(lane == 0, tn, jnp.where(lane <= HY_BANDS, jnp.cos(ang),
                                                jnp.where(lane <= 2 * HY_BANDS, jnp.sin(ang), 0.0)))
    h = jnp.sin(f32dot(feats, w1_ref[0]) + b1_ref[0])
    h = jnp.sin(f32dot(h, w2_ref[0]) + b2_ref[0])
    h = f32dot(h, w3_ref[0])
    h = h * jnp.exp(-jnp.exp(ld_ref[0]) * tn[:, 0:1])
    hf = h[:, :HY_W]
    hb = jnp.where(row[:, 0:1] == 0, 0.0, h[:, HY_W:])
    norm = jnp.sum(jnp.abs(hf), axis=0, keepdims=True) + jnp.sum(jnp.abs(hb), axis=0, keepdims=True) + EPS
    hf_ref[0] = hf / norm
    hb_ref[0] = hb / norm


def _hy_filters(n, w1, b1, w2, b2, w3, log_decay):
    w1p = jnp.pad(w1, ((0, 0), (0, LANES - HY_EMB), (0, 0)))
    lay = lambda shape: pl.BlockSpec((1,) + shape, lambda l: (l,) + (0,) * len(shape))
    return pl.pallas_call(
        functools.partial(_hy_filter_kernel, n=n),
        grid=(DEPTH,),
        in_specs=[lay((LANES, HY_FH)), lay((1, HY_FH)), lay((HY_FH, HY_FH)), lay((1, HY_FH)),
                  lay((HY_FH, 2 * HY_W)), lay((1, 2 * HY_W))],
        out_specs=[lay((n, HY_W)), lay((n, HY_W))],
        out_shape=[jax.ShapeDtypeStruct((DEPTH, n, HY_W), F32)] * 2,
        compiler_params=_cparams(1),
        name="hyena_filters",
    )(w1p, b1.reshape(DEPTH, 1, HY_FH), w2, b2.reshape(DEPTH, 1, HY_FH), w3, log_decay.reshape(DEPTH, 1, 2 * HY_W))


def _dft_tables_short(n):
    big = 2 * n
    kept = -(-(n + 1) // SUBLANES) * SUBLANES
    k = np.arange(kept)[:, None]
    t = np.arange(n)[None, :]
    ang = 2.0 * np.pi * ((k * t) % big) / big
    keep = (k <= n).astype(np.float64)
    weight = keep * np.where((k == 0) | (k == n), 1.0, 2.0)
    fwd = np.concatenate([np.cos(ang) * keep, -np.sin(ang) * keep], axis=0)
    inv = np.concatenate([(np.cos(ang) * weight).T, (-np.sin(ang) * weight).T], axis=1) / big
    return fwd.astype(np.float32), inv.astype(np.float32)


def _hy_spec_short_kernel(hf_ref, hb_ref, fwd_ref, f_ref, *, n):
    f32dot = functools.partial(jnp.dot, preferred_element_type=F32, precision=HIGHEST)
    xf = f32dot(fwd_ref[...], hf_ref[0])
    xb = f32dot(fwd_ref[...], hb_ref[0])
    kept = fwd_ref.shape[0] // 2
    f_ref[0, :kept, :] = xf[:kept] + xb[:kept]
    f_ref[0, kept:, :] = xf[kept:] - xb[kept:]


def _hy_spec_short(n, hf, hb):
    fwd, _ = _dft_tables_short(n)
    lay = lambda shape: pl.BlockSpec((1,) + shape, lambda l: (l,) + (0,) * len(shape))
    return pl.pallas_call(
        functools.partial(_hy_spec_short_kernel, n=n),
        grid=(DEPTH,),
        in_specs=[lay((n, HY_W)), lay((n, HY_W)), _const_spec(fwd.shape)],
        out_specs=lay((fwd.shape[0], HY_W)),
        out_shape=jax.ShapeDtypeStruct((DEPTH, fwd.shape[0], HY_W), F32),
        compiler_params=_cparams(1),
        name="hyena_spectrum_short",
    )(hf, hb, jnp.asarray(fwd))


def _short_conv(x, w, n):
    row = lax.broadcasted_iota(jnp.int32, (n, 1), 0)
    prev = jnp.where(row == 0, 0.0, pltpu.roll(x, 1, 0))
    nxt = jnp.where(row == n - 1, 0.0, pltpu.roll(x, n - 1, 0))
    return prev * w[0:1, :] + x * w[1:2, :] + nxt * w[2:3, :]


def _hy_conv_short_kernel(pc_ref, ws_ref, bias_ref, f_ref, fwd_ref, inv_ref, o_ref, *, n):
    kept = fwd_ref.shape[0] // 2
    u = _short_conv(pc_ref[...], ws_ref[...], n)
    v, x1, x2 = u[:, :W_C], u[:, W_C : 2 * W_C], u[:, 2 * W_C :]

    def conv(x, order):
        spec = _dot(fwd_ref[...], x.astype(BF16))
        fr = f_ref[:kept, order * W_C : (order + 1) * W_C]
        fi = f_ref[kept:, order * W_C : (order + 1) * W_C]
        zr = spec[:kept] * fr - spec[kept:] * fi
        zi = spec[:kept] * fi + spec[kept:] * fr
        return _dot(inv_ref[...], jnp.concatenate([zr, zi], axis=0).astype(BF16))

    z = x1 * (conv(v, 0) + v * bias_ref[0:1, :])
    z = x2 * (conv(z, 1) + z * bias_ref[1:2, :])
    o_ref[...] = z.astype(BF16)


def _hy_conv_short(n_seq, n, row0, total_rows, pc, w_short, bias, spec):
    fwd, inv = _dft_tables_short(n)
    seq_spec = lambda w: pl.BlockSpec((n, w), lambda b: (row0 // n + b, 0))
    return pl.pallas_call(
        functools.partial(_hy_conv_short_kernel, n=n),
        grid=(n_seq,),
        in_specs=[seq_spec(PC_W), _const_spec((SHORT_K, PC_W)), _const_spec((HY_ORDER, W_C)),
                  _const_spec((fwd.shape[0], HY_W)), _const_spec(fwd.shape), _const_spec(inv.shape)],
        out_specs=seq_spec(W_C),
        out_shape=jax.ShapeDtypeStruct((total_rows, W_C), BF16),
        compiler_params=_cparams(1),
        name="hyena_conv_short",
    )(pc, w_short, bias, spec, jnp.asarray(fwd, BF16), jnp.asarray(inv, BF16))


def _dft_tables_long():
    r = FFT_R
    big = r * r
    half = r // 2
    n2 = np.arange(r)[:, None, None]
    k1 = np.arange(r)[None, :, None]
    n1 = np.arange(half)[None, None, :]
    ang = 2.0 * np.pi * ((k1 * (r * n1 + n2)) % big) / big
    first = np.concatenate([np.cos(ang), -np.sin(ang)], axis=1)
    last = np.concatenate([np.cos(ang), -np.sin(ang)], axis=1).transpose(0, 2, 1) / big
    a = np.arange(r)
    ang_r = 2.0 * np.pi * ((a[:, None] * a[None, :]) % r) / r
    c, s = np.cos(ang_r), np.sin(ang_r)
    mid = np.block([[c, s], [-s, c]])
    mid_inv = np.block([[c, -s], [s, c]])
    f32 = lambda m: m.astype(np.float32)
    return f32(first), f32(mid), f32(mid_inv), f32(last)


FFT_K1 = FFT_R // 2 + 1
FFT_K1_PAD = -(-FFT_K1 // SUBLANES) * SUBLANES
FFT_K1_UNROLL = 11


def _dft_tables_long_half():
    first, mid, mid_inv, last = _dft_tables_long()
    r = FFT_R
    keep = np.zeros((FFT_K1_PAD,), np.float32)
    keep[:FFT_K1] = 1.0
    weight = np.zeros((FFT_K1_PAD,), np.float32)
    weight[:FFT_K1] = 2.0
    weight[0] = weight[r // 2] = 1.0
    first_h = np.concatenate([first[:, :FFT_K1_PAD] * keep[None, :, None],
                              first[:, r : r + FFT_K1_PAD] * keep[None, :, None]], axis=1)
    last_h = np.concatenate([last[:, :, :FFT_K1_PAD] * weight, last[:, :, r : r + FFT_K1_PAD] * weight], axis=2)
    return first_h, mid, mid_inv, last_h


def _ld_rows(ref, rows):
    return jnp.concatenate([ref[j, rows, :] for j in range(ref.shape[0])], axis=-1)


def _st_rows(ref, rows, val):
    for j in range(ref.shape[0]):
        ref[j, rows, :] = val[:, j * LANES : (j + 1) * LANES]


def _st_transposed(ref, j, val, half):
    n = val.shape[0] // 2
    _st_rows(ref, pl.ds(j, n, stride=2 * half), val[:n])
    _st_rows(ref, pl.ds(half + j, n, stride=2 * half), val[n:])


def _lane_split_scratch(rows, width):
    return pltpu.VMEM((width // LANES, rows, LANES), F32)


FFT_UNROLL = 16


def _fft_long_forward(x_ref, y_ref, dot_first, dot_mid, out_fn, n_k1=FFT_R, unroll_k1=FFT_UNROLL):
    r = FFT_R

    def stage_a(n2, carry):
        res = dot_first(n2, _ld_rows(x_ref, pl.ds(n2, r // 2, stride=r)))
        _st_transposed(y_ref, n2, res, r)
        return carry

    lax.fori_loop(0, r, stage_a, 0, unroll=FFT_UNROLL)

    def stage_c(k1, carry):
        base = pl.multiple_of(k1 * 2 * r, 2 * r)
        out_fn(k1, base, dot_mid(_ld_rows(y_ref, pl.ds(base, 2 * r))))
        return carry

    lax.fori_loop(0, n_k1, stage_c, 0, unroll=unroll_k1)


def _dot_split(t_hi, t_lo, x):
    x_hi = x.astype(BF16)
    x_lo = (x - x_hi.astype(F32)).astype(BF16)
    return _dot(t_hi, x_hi) + _dot(t_hi, x_lo) + _dot(t_lo, x_hi)


def _split_table(m):
    hi = jnp.asarray(m, BF16)
    lo = (jnp.asarray(m) - hi.astype(F32)).astype(BF16)
    return hi, lo


def _hy_spec_long_kernel(hf_ref, hb_ref, first_hi_ref, first_lo_ref, mid_hi_ref, mid_lo_ref, f_ref,
                         x_ref, y_ref, tmp_ref):
    r = FFT_R
    dot_first = lambda n2, slab: _dot_split(first_hi_ref[n2], first_lo_ref[n2], slab)
    dot_mid = lambda block: _dot_split(mid_hi_ref[...], mid_lo_ref[...], block)

    def write_fwd(k1, base, spec):
        tmp_ref[pl.ds(base, 2 * r), :] = spec

    _st_rows(x_ref, slice(None), hf_ref[0])
    _fft_long_forward(x_ref, y_ref, dot_first, dot_mid, write_fwd, n_k1=FFT_K1, unroll_k1=FFT_K1_UNROLL)

    def write_sum(k1, base, spec):
        prev = tmp_ref[pl.ds(base, 2 * r), :]
        f_ref[0, pl.ds(base, r), :] = prev[:r] + spec[:r]
        f_ref[0, pl.ds(base + r, r), :] = prev[r:] - spec[r:]

    _st_rows(x_ref, slice(None), hb_ref[0])
    _fft_long_forward(x_ref, y_ref, dot_first, dot_mid, write_sum, n_k1=FFT_K1, unroll_k1=FFT_K1_UNROLL)


SPEC_ROWS = 2 * FFT_R * FFT_K1


def _hy_spec_long(n, hf, hb):
    first, mid, _, _ = _dft_tables_long_half()
    lay = lambda rows: pl.BlockSpec((1, rows, W_C), lambda l, o: (l, 0, o))
    return pl.pallas_call(
        _hy_spec_long_kernel,
        grid=(DEPTH, HY_ORDER),
        in_specs=[lay(n), lay(n)] + [_const_spec(first.shape)] * 2 + [_const_spec(mid.shape)] * 2,
        out_specs=lay(SPEC_ROWS),
        out_shape=jax.ShapeDtypeStruct((DEPTH, SPEC_ROWS, HY_W), F32),
        scratch_shapes=[_lane_split_scratch(n, W_C), _lane_split_scratch(2 * FFT_R * FFT_K1_PAD, W_C),
                        pltpu.VMEM((SPEC_ROWS, W_C), F32)],
        compiler_params=_cparams(2),
        name="hyena_spectrum_long",
    )(hf, hb, *_split_table(first), *_split_table(mid))


def _hy_conv_long_kernel(sig_ref, gate_ref, ws_ref, bias_ref, f_ref, first_ref, mid_ref, midinv_ref, last_ref, o_ref,
                         x_ref, y_ref, v_ref, out_ref, *, n, order):
    r = FFT_R
    to_bf16 = lambda x: x.astype(BF16)
    gate = _short_conv(gate_ref[...], ws_ref[:, (order + 1) * W_C : (order + 2) * W_C], n)
    sig = _short_conv(sig_ref[...], ws_ref[:, :W_C], n) if order == 0 else sig_ref[...]
    _st_rows(x_ref, slice(None), sig)
    @pl.when(pl.program_id(0) == 0)
    def _():
        v_ref[...] = jnp.zeros(v_ref.shape, F32)

    def filter_and_invert(k1, base, spec):
        fr = f_ref[pl.ds(base, r), :]
        fi = f_ref[pl.ds(base + r, r), :]
        zr = spec[:r] * fr - spec[r:] * fi
        zi = spec[:r] * fi + spec[r:] * fr
        res = _dot(midinv_ref[...], jnp.concatenate([zr, zi], axis=0).astype(BF16))
        _st_transposed(v_ref, k1, res, FFT_K1_PAD)

    dot_first = lambda n2, slab: _dot(first_ref[n2], to_bf16(slab))
    dot_mid = lambda block: _dot(mid_ref[...], to_bf16(block))
    _fft_long_forward(x_ref, y_ref, dot_first, dot_mid, filter_and_invert, n_k1=FFT_K1, unroll_k1=FFT_K1_UNROLL)

    def stage_last(n2, carry):
        base = pl.multiple_of(n2 * 2 * FFT_K1_PAD, 2 * FFT_K1_PAD)
        res = _dot(last_ref[n2], _ld_rows(v_ref, pl.ds(base, 2 * FFT_K1_PAD)).astype(BF16))
        _st_rows(out_ref, pl.ds(n2, r // 2, stride=r), res)
        return carry

    lax.fori_loop(0, r, stage_last, 0, unroll=FFT_UNROLL)
    z = gate * (_ld_rows(out_ref, slice(None)) + sig * bias_ref[order : order + 1, :])
    o_ref[...] = z.astype(o_ref.dtype)


def _hy_conv_long(n_seq, n, row0, total_rows, pc, w_short, bias, spec, prev=None):
    assert 2 * n == FFT_R * FFT_R and FFT_K1 % FFT_K1_UNROLL == 0
    first, mid, mid_inv, last = _dft_tables_long_half()
    bf = lambda m: jnp.asarray(m, BF16)
    lane_block = lambda j: pl.BlockSpec((n, W_C), lambda b: (row0 // n + b, j))
    z = None
    for order in range(HY_ORDER):
        final = order + 1 == HY_ORDER
        in_specs = [lane_block(0), lane_block(order + 1), _const_spec((SHORT_K, PC_W)),
                    _const_spec((HY_ORDER, W_C)), pl.BlockSpec((SPEC_ROWS, W_C), lambda b, order=order: (0, order)),
                    _const_spec(first.shape), _const_spec(mid.shape), _const_spec(mid_inv.shape),
                    _const_spec(last.shape)]
        args = [pc if order == 0 else z, pc, w_short, bias, spec, bf(first), bf(mid), bf(mid_inv), bf(last)]
        body = functools.partial(_hy_conv_long_kernel, n=n, order=order)
        body, in_specs, args, aliases = _keep_rows_of(prev if final else None, body, in_specs, args)
        z = pl.pallas_call(
            body,
            grid=(n_seq,),
            in_specs=in_specs,
            out_specs=lane_block(0),
            out_shape=jax.ShapeDtypeStruct((total_rows, W_C), BF16 if final else F32),
            input_output_aliases=aliases,
            scratch_shapes=[_lane_split_scratch(n, W_C), _lane_split_scratch(2 * FFT_R * FFT_K1_PAD, W_C),
                            _lane_split_scratch(2 * FFT_K1_PAD * FFT_R, W_C), _lane_split_scratch(n, W_C)],
            compiler_params=_cparams(1),
            name="hyena_conv_long",
        )(*args)
    return z


def _pad_in_weights(w_in):
    gap = jnp.zeros(w_in.shape[:2] + (PB_W - MLA_IN,), BF16)
    w = w_in.astype(BF16)
    return jnp.concatenate([w[..., : PA_W + MLA_IN], gap, w[..., PA_W + MLA_IN :]], axis=-1)


def kernel(x_prompt, x_sample, c, cache_mla, cache_diff_k, cache_diff_v, state_hgrn, c_ctx, w_mod, b_mod, norm_g,
           ffn_w_gu, ffn_w_down, w_in, w_out, hgrn_lb_logits, hgrn_onorm, mla_q_norm, mla_kv_norm, mla_w_uq,
           mla_w_ukv, mla_qk_norm, hy_short, hy_w1, hy_b1, hy_w2, hy_b2, hy_w3, hy_log_decay, hy_bias,
           diff_qk_norm, diff_lambda, diff_subln):
    bc, sc, _ = x_prompt.shape
    bl, sl, _ = x_sample.shape
    past = cache_mla.shape[2]
    tok = _Tokens(bc, sc, bl, sl)
    assert bl + 1 <= MOD_ROWS and tok.tc % sl == 0

    x = (x_prompt.reshape(tok.tc, D_MODEL), x_sample.reshape(tok.tl, D_MODEL))
    cond = jnp.concatenate([c_ctx[None], c, jnp.zeros((MOD_ROWS - 1 - bl, D_MODEL), F32)], axis=0)
    mod = _modulation(cond, w_mod, b_mod)

    filters = {n: _hy_filters(n, hy_w1, hy_b1, hy_w2, hy_b2, hy_w3, hy_log_decay) for n in (sc, sl)}
    spec_ctx = _hy_spec_short(sc, *filters[sc])
    spec_lat = _hy_spec_long(sl, *filters[sl])

    mla_tables = _rope_tables(tok, [NOPE_B], HEAD_PAD)
    diff_tables = _rope_tables(tok, list(range(0, H_D * 2 * DH_D, DH_D)), H_D * 2 * DH_D)

    w_gu_b, w_down_b, w_out_b, w_in_b = (ffn_w_gu.astype(BF16), ffn_w_down.astype(BF16), w_out.astype(BF16),
                                         _pad_in_weights(w_in))

    new_mla, new_dk, new_dv, new_state = [], [], [], []
    for l in range(DEPTH):
        x = _ffn(tok, x, mod, l, 0, norm_g[l, 0], w_gu_b, w_down_b)
        pa, pb, pc, pd = _inproj(tok, x, mod, l, norm_g[l, 1], w_in_b)
        mla_w = _mla_weights(mla_w_uq[l], mla_w_ukv[l], mla_qk_norm[l])
        q_b, k_b, v_b, cache_b = _prep_call(
            tok, _mla_prep_body, "mla_prep", pb,
            _mla_prep_operands(tok, mla_q_norm[l], mla_kv_norm[l], mla_w, mla_tables))
        q_d, k_d, v_d, kcache_d, vcache_d = _prep_call(
            tok, _diff_prep_body, "diff_prep", pd, _diff_prep_operands(tok, diff_qk_norm[l], diff_tables))

        o_a, s_ctx = _hgrn(l, bc, sc, 0, tok.t, pa, hgrn_lb_logits, hgrn_onorm[l], None)
        o_a, _ = _hgrn(l, bl, sl, tok.tc, tok.t, pa, hgrn_lb_logits, hgrn_onorm[l], state_hgrn[:, l], prev=o_a)

        cache_rows = jnp.pad(cache_mla[:, l].reshape(bl * past, KV_LORA + ROPE_B),
                             ((0, 0), (0, 2 * LANES - KV_LORA - ROPE_B)))
        kc_b, vc_b = _mla_cache_prep(cache_rows, mla_w)
        o_b = _mla_attention(bc, sc, 0, tok.t, q_b, k_b, v_b)
        o_b = _mla_attention(bl, sl, tok.tc, tok.t, q_b, k_b, v_b, kc_b, vc_b, prev=o_b)

        o_c = _hy_conv_short(bc, sc, 0, tok.t, pc, hy_short[l], hy_bias[l], spec_ctx[l])
        o_c = _hy_conv_long(bl, sl, tok.tc, tok.t, pc, hy_short[l], hy_bias[l], spec_lat[l], prev=o_c)

        kc_d, vc_d = _diff_cache_prep(cache_diff_k[:, l].reshape(bl * past, H_D * 2 * DH_D),
                                      cache_diff_v[:, l].reshape(bl * past, H_D * DV_D))
        o_d = _diff_attention(l, bc, sc, 0, tok.t, diff_lambda[l], diff_subln[l], q_d, k_d, v_d)
        o_d = _diff_attention(l, bl, sl, tok.tc, tok.t, diff_lambda[l], diff_subln[l], q_d, k_d, v_d, kc_d, vc_d,
                              prev=o_d)

        last_ffn = functools.partial(_ffn, tok, x, mod, l, 1, norm_g[l, 2], w_gu_b, w_down_b,
                                     mixer_outs=[o_a, o_b, o_c, o_d], w_out=w_out_b)
        if l + 1 < DEPTH:
            x = last_ffn()
        else:
            y_ctx = last_ffn(tiles=(0, tok.ctx_tiles))
            y_lat = last_ffn(tiles=(tok.ctx_tiles, tok.n_tiles - tok.ctx_tiles))

        new_mla.append(cache_b[: tok.tc, : KV_LORA + ROPE_B].reshape(bc, sc, KV_LORA + ROPE_B))
        new_dk.append(kcache_d[: tok.tc].reshape(bc, sc, H_D, 2, DH_D))
        new_dv.append(vcache_d[: tok.tc].reshape(bc, sc, H_D, DV_D))
        new_state.append(s_ctx)

    y_prompt = y_ctx.reshape(bc, sc, D_MODEL)
    y_sample = y_lat.reshape(bl, sl, D_MODEL)
    return (y_prompt, y_sample, jnp.stack(new_mla, axis=1), jnp.stack(new_dk, axis=1), jnp.stack(new_dv, axis=1),
            jnp.stack(new_state, axis=1))
```
